```python
import math
import jax, jax.numpy as jnp
from jax import lax
import numpy as np

D_MODEL = 1024
BATCH = 8
SEQ = 2048
DEPTH = 1
DEC_BATCH = 128
DEC_SEQ = 4
PAST_LEN = 16384
PAGE_SIZE = 128

D_RG = D_MODEL
RG_BLOCKS = 8
RG_BLOCK_W = D_RG // RG_BLOCKS
RG_C = 8.0
CONV_W = 4
D_SSD = D_MODEL
SSD_HEAD_DIM = 64
SSD_HEADS = D_SSD // SSD_HEAD_DIM
SSD_GROUPS = 2
SSD_STATE = 128
SSD_CHUNK = 128
D_SSD_CONV = D_SSD + 2 * SSD_GROUPS * SSD_STATE
D_XA = D_MODEL
MEM_HEADS = 4
MEM_HEAD_DIM = D_XA // MEM_HEADS
N_MEM = 256
D_MIX = D_RG + D_SSD + D_XA
IN_SIZES = (D_RG, D_RG, D_SSD_CONV, D_SSD, SSD_HEADS, D_XA, D_XA)
D_IN = sum(IN_SIZES)
DEEPNORM_ALPHA = (2 * DEPTH) ** 0.25
DEEPNORM_BETA = (8 * DEPTH) ** -0.25
LN_EPS = 1e-5
RMS_EPS = 1e-5

kernel_name = 'hymba_style_rglru_ssd_memxattn_step'


def _causal_conv(u, buf, w, b):
    seqlen = u.shape[1]
    up = jnp.concatenate([buf.astype(jnp.float32), u], axis=1)
    w = w.astype(jnp.float32)
    y = up[:, 0:seqlen] * w[0]
    for k in range(1, CONV_W):
        y = y + up[:, k:k + seqlen] * w[k]
    return y + b.astype(jnp.float32), up[:, -(CONV_W - 1):]


def _rg_lru(u, h0, w_a, b_a, w_i, b_i, lam):
    bsz, seqlen, _ = u.shape
    f32 = jnp.float32
    ub = u.reshape(bsz, seqlen, RG_BLOCKS, RG_BLOCK_W)
    r = jax.nn.sigmoid(jnp.einsum('blki,kij->blkj', ub, w_a.astype(f32)).reshape(bsz, seqlen, D_RG) + b_a.astype(f32))
    i = jax.nn.sigmoid(jnp.einsum('blki,kij->blkj', ub, w_i.astype(f32)).reshape(bsz, seqlen, D_RG) + b_i.astype(f32))
    log_a = -RG_C * r * jax.nn.softplus(-lam.astype(f32))
    a = jnp.exp(log_a)
    bterm = jnp.sqrt(-jnp.expm1(2.0 * log_a)) * (i * u)
    bterm = bterm.at[:, 0].add(a[:, 0] * h0.astype(f32))

    def combine(left, right):
        a1, b1 = left
        a2, b2 = right
        return a1 * a2, a2 * b1 + b2

    _, h = lax.associative_scan(combine, (a, bterm), axis=1)
    return h, h[:, -1]


def _ssd_chunked(x, dt, a, bmat, cmat, h0):
    bsz, seqlen = x.shape[:2]
    q = SSD_CHUNK if seqlen % SSD_CHUNK == 0 else seqlen
    c = seqlen // q
    hg = SSD_HEADS // SSD_GROUPS
    xs = (x * dt[..., None]).reshape(bsz, c, q, SSD_GROUPS, hg, SSD_HEAD_DIM)
    da = (dt * a).reshape(bsz, c, q, SSD_GROUPS, hg)
    bc = bmat.reshape(bsz, c, q, SSD_GROUPS, SSD_STATE)
    cc = cmat.reshape(bsz, c, q, SSD_GROUPS, SSD_STATE)
    acum = jnp.cumsum(da, axis=2)
    causal = jnp.tril(jnp.ones((q, q), dtype=bool))
    seg = acum[:, :, :, None] - acum[:, :, None, :]
    decay = jnp.exp(jnp.where(causal[None, None, :, :, None, None], seg, -jnp.inf))
    cb = jnp.einsum('bcign,bcjgn->bcijg', cc, bc)
    y_diag = jnp.einsum('bcijg,bcijgh,bcjghp->bcighp', cb, decay, xs)
    decay_to_end = jnp.exp(acum[:, :, -1:] - acum)
    chunk_states = jnp.einsum('bcjgn,bcjgh,bcjghp->bcghpn', bc, decay_to_end, xs)
    chunk_decay = jnp.exp(acum[:, :, -1])

    def step(h, inp):
        st, dec = inp
        return h * dec[..., None, None] + st, h

    h0g = h0.reshape(bsz, SSD_GROUPS, hg, SSD_HEAD_DIM, SSD_STATE)
    h_last, h_in = lax.scan(step, h0g, (jnp.moveaxis(chunk_states, 1, 0), jnp.moveaxis(chunk_decay, 1, 0)))
    h_in = jnp.moveaxis(h_in, 0, 1)
    y_off = jnp.einsum('bcign,bcghpn,bcigh->bcighp', cc, h_in, jnp.exp(acum))
    y = (y_diag + y_off).reshape(bsz, seqlen, SSD_HEADS, SSD_HEAD_DIM)
    return y, h_last.reshape(bsz, SSD_HEADS, SSD_HEAD_DIM, SSD_STATE)


def _layer(x, mem_k, mem_v, rg_buf, rg_h0, ssd_buf, ssd_h0, wl):
    (w_in, rg_conv_w, rg_conv_b, w_rg_a, b_rg_a, w_rg_i, b_rg_i, rg_lambda,
     ssd_conv_w, ssd_conv_b, ssd_dt_bias, ssd_a_log, ssd_d, ssd_norm_g,
     w_out, ln_g, ln_b) = wl
    f32 = jnp.float32
    bsz, seqlen, _ = x.shape
    xf = x.astype(f32)
    proj = xf @ w_in.astype(f32)
    offs = np.cumsum(IN_SIZES)[:-1].tolist()
    rg_x, rg_gate, ssd_xbc, ssd_z, ssd_dt, xa_q, xa_gate = jnp.split(proj, offs, axis=-1)

    rg_u, rg_buf_new = _causal_conv(rg_x, rg_buf, rg_conv_w, rg_conv_b)
    rg_h, rg_h_last = _rg_lru(rg_u, rg_h0, w_rg_a, b_rg_a, w_rg_i, b_rg_i, rg_lambda)
    rg_out = rg_h * jax.nn.silu(rg_gate)

    xbc, ssd_buf_new = _causal_conv(ssd_xbc, ssd_buf, ssd_conv_w, ssd_conv_b)
    xbc = jax.nn.silu(xbc)
    gn = SSD_GROUPS * SSD_STATE
    s_x = xbc[..., :D_SSD].reshape(bsz, seqlen, SSD_HEADS, SSD_HEAD_DIM)
    s_b = xbc[..., D_SSD:D_SSD + gn].reshape(bsz, seqlen, SSD_GROUPS, SSD_STATE)
    s_c = xbc[..., D_SSD + gn:].reshape(bsz, seqlen, SSD_GROUPS, SSD_STATE)
    dt = jax.nn.softplus(ssd_dt + ssd_dt_bias.astype(f32))
    a = -jnp.exp(ssd_a_log.astype(f32))
    s_y, ssd_h_last = _ssd_chunked(s_x, dt, a, s_b, s_c, ssd_h0.astype(f32))
    s_y = s_y + ssd_d.astype(f32)[:, None] * s_x
    s_y = s_y.reshape(bsz, seqlen, D_SSD) * jax.nn.silu(ssd_z)
    s_y = s_y.reshape(bsz, seqlen, SSD_GROUPS, D_SSD // SSD_GROUPS)
    s_y = s_y * lax.rsqrt(jnp.mean(s_y * s_y, axis=-1, keepdims=True) + RMS_EPS)
    ssd_out = s_y.reshape(bsz, seqlen, D_SSD) * ssd_norm_g.astype(f32)

    q = xa_q.reshape(bsz, seqlen, MEM_HEADS, MEM_HEAD_DIM)
    s = jnp.einsum('blhd,bmhd->bhlm', q, mem_k.astype(f32)) * (MEM_HEAD_DIM ** -0.5)
    pr = jax.nn.softmax(s, axis=-1)
    o = jnp.einsum('bhlm,bmhd->blhd', pr, mem_v.astype(f32)).reshape(bsz, seqlen, D_XA)
    xa_out = o * jax.nn.silu(xa_gate)

    mix = jnp.concatenate([rg_out, ssd_out, xa_out], axis=-1) @ w_out.astype(f32)
    res = DEEPNORM_ALPHA * xf + mix
    mu = jnp.mean(res, axis=-1, keepdims=True)
    var = jnp.mean(jnp.square(res - mu), axis=-1, keepdims=True)
    y = (res - mu) * lax.rsqrt(var + LN_EPS) * ln_g.astype(f32) + ln_b.astype(f32)
    od = x.dtype
    return y.astype(od), rg_buf_new.astype(od), rg_h_last.astype(od), ssd_buf_new.astype(od), ssd_h_last.astype(od)


def setup_inputs(seed: int = 0) -> dict:
    key = jax.random.key(seed)
    ks = jax.random.split(key, 32)
    f32 = jnp.float32

    def nrm(k, shape, scale=1.0):
        return jax.random.normal(k, shape, f32) * scale

    x_prompt = nrm(ks[0], (BATCH, SEQ, D_MODEL))
    x_sample = nrm(ks[1], (DEC_BATCH, DEC_SEQ, D_MODEL))
    mem_prompt = nrm(ks[2], (BATCH, N_MEM, D_MODEL))
    state_rg_conv = nrm(ks[3], (DEPTH, DEC_BATCH, CONV_W - 1, D_RG))
    state_rg_h = nrm(ks[4], (DEPTH, DEC_BATCH, D_RG), 0.5)
    state_ssd_conv = nrm(ks[5], (DEPTH, DEC_BATCH, CONV_W - 1, D_SSD_CONV))
    state_ssd_h = nrm(ks[6], (DEPTH, DEC_BATCH, SSD_HEADS, SSD_HEAD_DIM, SSD_STATE), 0.1)
    cache_mem_k = nrm(ks[7], (DEPTH, DEC_BATCH, N_MEM, MEM_HEADS, MEM_HEAD_DIM))
    cache_mem_v = nrm(ks[8], (DEPTH, DEC_BATCH, N_MEM, MEM_HEADS, MEM_HEAD_DIM), DEEPNORM_BETA)
    w_in = nrm(ks[9], (DEPTH, D_MODEL, D_IN), D_MODEL ** -0.5)
    rg_conv_w = nrm(ks[10], (DEPTH, CONV_W, D_RG), CONV_W ** -0.5)
    rg_conv_b = nrm(ks[11], (DEPTH, D_RG), 0.01)
    w_rg_a = nrm(ks[12], (DEPTH, RG_BLOCKS, RG_BLOCK_W, RG_BLOCK_W), RG_BLOCK_W ** -0.5)
    b_rg_a = nrm(ks[13], (DEPTH, D_RG), 0.01)
    w_rg_i = nrm(ks[14], (DEPTH, RG_BLOCKS, RG_BLOCK_W, RG_BLOCK_W), RG_BLOCK_W ** -0.5)
    b_rg_i = nrm(ks[15], (DEPTH, D_RG), 0.01)
    a0 = jax.random.uniform(ks[16], (DEPTH, D_RG), f32, 0.9, 0.999)
    rg_lambda = jnp.log(a0) - jnp.log1p(-a0)
    ssd_conv_w = nrm(ks[17], (DEPTH, CONV_W, D_SSD_CONV), CONV_W ** -0.5)
    ssd_conv_b = nrm(ks[18], (DEPTH, D_SSD_CONV), 0.01)
    dt0 = jnp.exp(jax.random.uniform(ks[19], (DEPTH, SSD_HEADS), f32, math.log(1e-3), math.log(1e-1)))
    ssd_dt_bias = dt0 + jnp.log(-jnp.expm1(-dt0))
    ssd_a_log = jnp.log(jax.random.uniform(ks[20], (DEPTH, SSD_HEADS), f32, 1.0, 16.0))
    ssd_d = 1.0 + nrm(ks[21], (DEPTH, SSD_HEADS), 0.1)
    ssd_norm_g = 1.0 + nrm(ks[22], (DEPTH, D_SSD), 0.1)
    w_mem_k = nrm(ks[23], (DEPTH, D_MODEL, D_XA), D_MODEL ** -0.5)
    w_mem_v = nrm(ks[24], (DEPTH, D_MODEL, D_XA), D_MODEL ** -0.5 * DEEPNORM_BETA)
    w_out = nrm(ks[25], (DEPTH, D_MIX, D_MODEL), D_MIX ** -0.5 * DEEPNORM_BETA)
    ln_g = 1.0 + nrm(ks[26], (DEPTH, D_MODEL), 0.1)
    ln_b = nrm(ks[27], (DEPTH, D_MODEL), 0.01)
    return {
        'x_prompt': x_prompt, 'x_sample': x_sample, 'mem_prompt': mem_prompt,
        'state_rg_conv': state_rg_conv, 'state_rg_h': state_rg_h,
        'state_ssd_conv': state_ssd_conv, 'state_ssd_h': state_ssd_h,
        'cache_mem_k': cache_mem_k, 'cache_mem_v': cache_mem_v,
        'w_in': w_in, 'rg_conv_w': rg_conv_w, 'rg_conv_b': rg_conv_b,
        'w_rg_a': w_rg_a, 'b_rg_a': b_rg_a, 'w_rg_i': w_rg_i, 'b_rg_i': b_rg_i,
        'rg_lambda': rg_lambda, 'ssd_conv_w': ssd_conv_w, 'ssd_conv_b': ssd_conv_b,
        'ssd_dt_bias': ssd_dt_bias, 'ssd_a_log': ssd_a_log, 'ssd_d': ssd_d,
        'ssd_norm_g': ssd_norm_g, 'w_mem_k': w_mem_k, 'w_mem_v': w_mem_v,
        'w_out': w_out, 'ln_g': ln_g, 'ln_b': ln_b,
    }


def reference(x_prompt, x_sample, mem_prompt, state_rg_conv, state_rg_h, state_ssd_conv,
              state_ssd_h, cache_mem_k, cache_mem_v, w_in, rg_conv_w, rg_conv_b, w_rg_a,
              b_rg_a, w_rg_i, b_rg_i, rg_lambda, ssd_conv_w, ssd_conv_b, ssd_dt_bias,
              ssd_a_log, ssd_d, ssd_norm_g, w_mem_k, w_mem_v, w_out, ln_g, ln_b):
    f32 = jnp.float32
    bp = x_prompt.shape[0]
    yp, ys = x_prompt, x_sample
    rgc_p, rgh_p, sc_p, sh_p, mk_p, mv_p = [], [], [], [], [], []
    rgc_s, rgh_s, sc_s, sh_s = [], [], [], []
    memf = mem_prompt.astype(f32)
    for l in range(DEPTH):
        wl = (w_in[l], rg_conv_w[l], rg_conv_b[l], w_rg_a[l], b_rg_a[l], w_rg_i[l], b_rg_i[l],
              rg_lambda[l], ssd_conv_w[l], ssd_conv_b[l], ssd_dt_bias[l], ssd_a_log[l], ssd_d[l],
              ssd_norm_g[l], w_out[l], ln_g[l], ln_b[l])
        mk = (memf @ w_mem_k[l].astype(f32)).reshape(bp, N_MEM, MEM_HEADS, MEM_HEAD_DIM)
        mv = (memf @ w_mem_v[l].astype(f32)).reshape(bp, N_MEM, MEM_HEADS, MEM_HEAD_DIM)
        yp, a1, a2, a3, a4 = _layer(
            yp, mk, mv,
            jnp.zeros((bp, CONV_W - 1, D_RG), f32), jnp.zeros((bp, D_RG), f32),
            jnp.zeros((bp, CONV_W - 1, D_SSD_CONV), f32),
            jnp.zeros((bp, SSD_HEADS, SSD_HEAD_DIM, SSD_STATE), f32), wl)
        rgc_p.append(a1); rgh_p.append(a2); sc_p.append(a3); sh_p.append(a4)
        mk_p.append(mk.astype(x_prompt.dtype)); mv_p.append(mv.astype(x_prompt.dtype))
        ys, b1, b2, b3, b4 = _layer(
            ys, cache_mem_k[l], cache_mem_v[l], state_rg_conv[l], state_rg_h[l],
            state_ssd_conv[l], state_ssd_h[l], wl)
        rgc_s.append(b1); rgh_s.append(b2); sc_s.append(b3); sh_s.append(b4)
    return (yp, ys,
            jnp.stack(rgc_p), jnp.stack(rgh_p), jnp.stack(sc_p), jnp.stack(sh_p),
            jnp.stack(mk_p), jnp.stack(mv_p),
            jnp.stack(rgc_s), jnp.stack(rgh_s), jnp.stack(sc_s), jnp.stack(sh_s))
```

```python
import functools

import jax
import jax.numpy as jnp
from jax import lax
from jax.experimental import pallas as pl
from jax.experimental.pallas import tpu as pltpu

F32 = jnp.float32
BF16 = jnp.bfloat16

SUBLANES = 8
LANES = 128
VMEM_LIMIT_BYTES = 56 * 1024 * 1024

RG_C = 8.0
CONV_W = 4
RG_BLOCKS = 8
SSD_HEAD_DIM = 64
SSD_GROUPS = 2
SSD_STATE = 128
SSD_CHUNK = 128
MEM_HEADS = 4
LN_EPS = 1e-5
RMS_EPS = 1e-5
DEPTH = 1
DEEPNORM_ALPHA = (2 * DEPTH) ** 0.25
SAMPLE_PAD = SUBLANES


def _cparams(*sem):
    return pltpu.CompilerParams(dimension_semantics=sem, vmem_limit_bytes=VMEM_LIMIT_BYTES)


def _mm(a, b):
    return jnp.dot(a.astype(BF16), b.astype(BF16), preferred_element_type=F32)


def _mm_nt(a, b):
    return lax.dot_general(a.astype(BF16), b.astype(BF16), (((1,), (1,)), ((), ())),
                           preferred_element_type=F32)


def _mm_tn(a, b):
    return lax.dot_general(a.astype(BF16), b.astype(BF16), (((0,), (0,)), ((), ())),
                           preferred_element_type=F32)


def _sigmoid(x):
    return 1.0 / (1.0 + jnp.exp(-x))


def _silu(x):
    return x * _sigmoid(x)


def _softplus(x):
    return jnp.maximum(x, 0.0) + jnp.log(1.0 + jnp.exp(-jnp.abs(x)))


def _causal_conv_tiles(x3, p3, w, b):
    row = lax.broadcasted_iota(jnp.int32, x3.shape, 1)
    y = x3 * w[CONV_W - 1:CONV_W][None]
    for s in range(1, CONV_W):
        shifted = jnp.where(row >= s, pltpu.roll(x3, s, axis=1), pltpu.roll(p3, s, axis=1))
        y = y + shifted * w[CONV_W - 1 - s:CONV_W - s][None]
    return y + b[None]


def _scan_in_tiles(a3, b3):
    row = lax.broadcasted_iota(jnp.int32, a3.shape, 1)
    s = 1
    while s < SUBLANES:
        keep = row >= s
        a_sh = jnp.where(keep, pltpu.roll(a3, s, axis=1), 1.0)
        b_sh = jnp.where(keep, pltpu.roll(b3, s, axis=1), 0.0)
        b3 = a3 * b_sh + b3
        a3 = a3 * a_sh
        s *= 2
    return a3, b3


def _rg_gates(u, wgate, ba, bi, lam):
    pre = _mm(u, wgate)
    r = _sigmoid(pre[:, :LANES] + ba)
    i = _sigmoid(pre[:, LANES:] + bi)
    log_a = (-RG_C) * r * _softplus(-lam)
    a = jnp.exp(log_a)
    mult = jnp.sqrt(-jnp.tanh(log_a) * (1.0 + a * a))
    return a, mult * (i * u)


def _rg_prompt_kernel(x_ref, wx_ref, wg_ref, cw_ref, cb_ref, wgate_ref, ba_ref, bi_ref, lam_ref,
                      out_ref, conv_ref, hlast_ref, tail_scr, h_scr):
    t = pl.program_id(1)

    @pl.when(t == 0)
    def _():
        tail_scr[...] = jnp.zeros_like(tail_scr)
        h_scr[...] = jnp.zeros_like(h_scr)

    rows = x_ref.shape[1]
    g_tiles = rows // SUBLANES
    xb = x_ref[0].astype(BF16)
    rgx = jnp.dot(xb, wx_ref[...], preferred_element_type=F32)
    gate = jnp.dot(xb, wg_ref[...], preferred_element_type=F32)
    x3 = rgx.reshape(g_tiles, SUBLANES, rgx.shape[1])
    p3 = jnp.concatenate([tail_scr[...][None], x3[:-1]], axis=0)
    tail_scr[...] = x3[g_tiles - 1]
    conv_ref[0] = x3[g_tiles - 1][SUBLANES - (CONV_W - 1):]

    for k in range(RG_BLOCKS):
        ks = slice(k * LANES, (k + 1) * LANES)
        u3 = _causal_conv_tiles(x3[:, :, ks], p3[:, :, ks], cw_ref[:, ks], cb_ref[:, ks])
        u = u3.reshape(rows, LANES)
        a, b = _rg_gates(u, wgate_ref[k], ba_ref[:, ks], bi_ref[:, ks], lam_ref[:, ks])
        a3, b3 = _scan_in_tiles(a.reshape(g_tiles, SUBLANES, LANES), b.reshape(g_tiles, SUBLANES, LANES))
        carry = h_scr[:, ks]
        hs = []
        for g in range(g_tiles):
            hg = a3[g] * carry + b3[g]
            hs.append(hg)
            carry = hg[SUBLANES - 1:]
        h_scr[:, ks] = carry
        hlast_ref[0, :, ks] = carry
        h = jnp.concatenate(hs, axis=0)
        out_ref[0, :, ks] = (h * _silu(gate[:, ks])).astype(out_ref.dtype)


def _rg_sample_kernel(x_ref, wx_ref, wg_ref, cw_ref, cb_ref, wgate_ref, ba_ref, bi_ref, lam_ref,
                      cstate_ref, h0_ref, out_ref, conv_ref, hlast_ref, *, valid):
    seqs = x_ref.shape[0]
    rows = seqs * SUBLANES
    xb = x_ref[...].reshape(rows, x_ref.shape[2]).astype(BF16)
    rgx = jnp.dot(xb, wx_ref[...], preferred_element_type=F32)
    gate = jnp.dot(xb, wg_ref[...], preferred_element_type=F32)
    x3 = rgx.reshape(seqs, SUBLANES, rgx.shape[1])
    conv_ref[...] = x3[:, valid - (CONV_W - 1):valid, :]
    row = lax.broadcasted_iota(jnp.int32, (seqs, SUBLANES, LANES), 1)
    for k in range(RG_BLOCKS):
        ks = slice(k * LANES, (k + 1) * LANES)
        u3 = _causal_conv_tiles(x3[:, :, ks], cstate_ref[:, :, ks], cw_ref[:, ks], cb_ref[:, ks])
        u = u3.reshape(rows, LANES)
        a, b = _rg_gates(u, wgate_ref[k], ba_ref[:, ks], bi_ref[:, ks], lam_ref[:, ks])
        a3 = a.reshape(seqs, SUBLANES, LANES)
        b3 = b.reshape(seqs, SUBLANES, LANES)
        b3 = b3 + jnp.where(row == 0, a3 * h0_ref[:, :, ks], 0.0)
        _, h3 = _scan_in_tiles(a3, b3)
        hlast_ref[:, :, ks] = h3[:, valid - 1:valid, :]
        g3 = gate[:, ks].reshape(seqs, SUBLANES, LANES)
        out_ref[:, :, ks] = (h3 * _silu(g3)).astype(out_ref.dtype)


def _cumsum_rows(x):
    rows = x.shape[0]
    row = lax.broadcasted_iota(jnp.int32, x.shape, 0)
    s = 1
    while s < rows:
        x = x + jnp.where(row >= s, pltpu.roll(x, s, axis=0), 0.0)
        s *= 2
    return x


def _group_rmsnorm(y, gain):
    width = y.shape[-1] // SSD_GROUPS
    parts = []
    for g in range(SSD_GROUPS):
        yg = y[..., g * width:(g + 1) * width]
        ms = jnp.sum(yg * yg, axis=-1, keepdims=True) * (1.0 / width)
        parts.append(yg * lax.rsqrt(ms + RMS_EPS))
    return jnp.concatenate(parts, axis=-1) * gain


def _ssd_prompt_kernel(x_ref, wxbc_ref, wz_ref, wdt_ref, cw_ref, cb_ref, dtb_ref, alog_ref, dexp_ref,
                       ng_ref, out_ref, conv_ref, h_ref, tail_scr):
    t = pl.program_id(1)

    @pl.when(t == 0)
    def _():
        tail_scr[...] = jnp.zeros_like(tail_scr)
        h_ref[...] = jnp.zeros_like(h_ref)

    q = x_ref.shape[1]
    g_tiles = q // SUBLANES
    d_ssd = wz_ref.shape[1]
    gn = SSD_GROUPS * SSD_STATE
    xb = x_ref[0].astype(BF16)
    xbc_raw = jnp.dot(xb, wxbc_ref[...], preferred_element_type=F32)
    z = jnp.dot(xb, wz_ref[...], preferred_element_type=F32)
    dtr = jnp.dot(xb, wdt_ref[...], preferred_element_type=F32)

    x3 = xbc_raw.reshape(g_tiles, SUBLANES, xbc_raw.shape[1])
    p3 = jnp.concatenate([tail_scr[...][None], x3[:-1]], axis=0)
    tail_scr[...] = x3[g_tiles - 1]
    conv_ref[0] = x3[g_tiles - 1][SUBLANES - (CONV_W - 1):]
    xbc = _silu(_causal_conv_tiles(x3, p3, cw_ref[...], cb_ref[...])).reshape(q, xbc_raw.shape[1])
    sx = xbc[:, :d_ssd]
    bm = xbc[:, d_ssd:d_ssd + gn]
    cm = xbc[:, d_ssd + gn:]

    dt = _softplus(dtr + dtb_ref[...])
    da = dt * (-jnp.exp(alog_ref[...]))
    acum = _cumsum_rows(da)
    alast = acum[q - 1:q, :]
    wgt = dt * jnp.exp(alast - acum)
    tot = jnp.exp(alast)
    acum_t = acum.T
    dt_t = dt.T

    ii = lax.broadcasted_iota(jnp.int32, (q, q), 0)
    jj = lax.broadcasted_iota(jnp.int32, (q, q), 1)
    causal = ii >= jj
    lane = lax.broadcasted_iota(jnp.int32, (q, LANES), 1)
    lo = lane < SSD_HEAD_DIM
    srow = lax.broadcasted_iota(jnp.int32, (LANES, SSD_STATE), 0) < SSD_HEAD_DIM

    cb = [_mm_nt(cm[:, g * SSD_STATE:(g + 1) * SSD_STATE], bm[:, g * SSD_STATE:(g + 1) * SSD_STATE])
          for g in range(SSD_GROUPS)]
    heads = d_ssd // SSD_HEAD_DIM
    pairs = heads // 2
    y_parts = []
    for pq in range(pairs):
        g = (2 * pq) // (heads // SSD_GROUPS)
        ps = slice(pq * LANES, (pq + 1) * LANES)
        xq = sx[:, ps]
        ms, es, ws, ts = [], [], [], []
        for h in (2 * pq, 2 * pq + 1):
            acol = jnp.broadcast_to(acum[:, h:h + 1], (q, q))
            arow = jnp.broadcast_to(acum_t[h:h + 1, :], (q, q))
            decay = jnp.exp(jnp.where(causal, acol - arow, -jnp.inf))
            ms.append((cb[g] * decay * jnp.broadcast_to(dt_t[h:h + 1, :], (q, q))).astype(BF16))
            es.append(jnp.exp(jnp.broadcast_to(acum[:, h:h + 1], (q, LANES))))
            ws.append(jnp.broadcast_to(wgt[:, h:h + 1], (q, LANES)))
            ts.append(jnp.broadcast_to(tot[:, h:h + 1], (LANES, SSD_STATE)))
        lhs = jnp.concatenate(ms, axis=1)
        rhs = jnp.concatenate([jnp.where(lo, xq, 0.0), jnp.where(lo, 0.0, xq)], axis=0)
        y_diag = _mm(lhs, rhs)
        hq = h_ref[0, ps, :]
        y_off = _mm_nt(cm[:, g * SSD_STATE:(g + 1) * SSD_STATE], hq) * jnp.where(lo, es[0], es[1])
        xw = xq * jnp.where(lo, ws[0], ws[1])
        h_ref[0, ps, :] = hq * jnp.where(srow, ts[0], ts[1]) + _mm_tn(xw, bm[:, g * SSD_STATE:(g + 1) * SSD_STATE])
        y_parts.append(y_diag + y_off + dexp_ref[:, ps] * xq)
    y = jnp.concatenate(y_parts, axis=1) * _silu(z)
    out_ref[0] = _group_rmsnorm(y, ng_ref[...]).astype(out_ref.dtype)


def _ssd_sample_kernel(x_ref, wxbc_ref, wz_ref, wdt_ref, cw_ref, cb_ref, dtb_ref, alog_ref, dexp_ref,
                       ng_ref, expand_ref, cstate_ref, h0_ref, out_ref, conv_ref, h_ref,
                       c_scr, b_scr, xw_scr, tot_scr, yoff_scr, *, valid):
    seqs = x_ref.shape[0]
    rows = seqs * SUBLANES
    d_ssd = wz_ref.shape[1]
    gn = SSD_GROUPS * SSD_STATE
    heads = d_ssd // SSD_HEAD_DIM
    hg = heads // SSD_GROUPS
    xb = x_ref[...].reshape(rows, x_ref.shape[2]).astype(BF16)
    xbc_raw = jnp.dot(xb, wxbc_ref[...], preferred_element_type=F32)
    z = jnp.dot(xb, wz_ref[...], preferred_element_type=F32)
    dtr = jnp.dot(xb, wdt_ref[...], preferred_element_type=F32)

    x3 = xbc_raw.reshape(seqs, SUBLANES, xbc_raw.shape[1])
    conv_ref[...] = x3[:, valid - (CONV_W - 1):valid, :]
    xbc3 = _silu(_causal_conv_tiles(x3, cstate_ref[...], cw_ref[...], cb_ref[...]))
    sx3 = xbc3[:, :, :d_ssd]
    b3 = xbc3[:, :, d_ssd:d_ssd + gn]
    c3 = xbc3[:, :, d_ssd + gn:]

    row = lax.broadcasted_iota(jnp.int32, (seqs, SUBLANES, LANES), 1)
    lane = lax.broadcasted_iota(jnp.int32, (seqs, SUBLANES, LANES), 2)
    dt3 = jnp.where(row < valid, _softplus(dtr + dtb_ref[...]).reshape(seqs, SUBLANES, LANES), 0.0)
    da3 = dt3 * (-jnp.exp(alog_ref[...]))[None]
    ones = jnp.ones_like(da3)
    _, acum3 = _scan_in_tiles(ones, da3)
    alast = acum3[:, SUBLANES - 1:, :]
    wgt3 = dt3 * jnp.exp(alast - acum3)
    tot_scr[...] = jnp.exp(alast)
    e3 = jnp.exp(acum3)

    def expand(v3):
        flat = jnp.dot(v3.reshape(rows, LANES), expand_ref[...], precision=lax.Precision.HIGHEST,
                       preferred_element_type=F32)
        return flat.reshape(seqs, SUBLANES, d_ssd)

    y_diag = jnp.zeros((seqs, SUBLANES, d_ssd), F32)
    for u in range(valid):
        prod = c3 * b3[:, u:u + 1, :]
        cbu = [jnp.sum(prod[:, :, g * SSD_STATE:(g + 1) * SSD_STATE], axis=-1, keepdims=True)
               for g in range(SSD_GROUPS)]
        cb_heads = jnp.where(lane < hg, cbu[0], cbu[1])
        coef = jnp.where(row >= u, cb_heads * jnp.exp(acum3 - acum3[:, u:u + 1, :]) * dt3[:, u:u + 1, :], 0.0)
        y_diag = y_diag + expand(coef) * sx3[:, u:u + 1, :]

    c_scr[...] = c3
    b_scr[...] = b3
    xw_scr[...] = sx3 * expand(wgt3)
    srow = lax.broadcasted_iota(jnp.int32, (2 * SSD_HEAD_DIM, SSD_STATE), 0) < SSD_HEAD_DIM

    def per_seq(s, carry):
        cs = c_scr[s]
        bs = b_scr[s]
        xws = xw_scr[s]
        tots = tot_scr[s]
        for g in range(SSD_GROUPS):
            gs = slice(g * SSD_STATE, (g + 1) * SSD_STATE)
            width = hg * SSD_HEAD_DIM
            cols = slice(g * width, (g + 1) * width)
            hin = h0_ref[s, cols, :]
            yoff_scr[s, :, cols] = _mm_nt(cs[:, gs], hin)
            upd = _mm_tn(xws[:, cols], bs[:, gs])
            for pq in range(hg // 2):
                h = g * hg + 2 * pq
                rs = slice(pq * LANES, (pq + 1) * LANES)
                t0 = jnp.broadcast_to(tots[0:1, h:h + 1], (LANES, SSD_STATE))
                t1 = jnp.broadcast_to(tots[0:1, h + 1:h + 2], (LANES, SSD_STATE))
                h_ref[s, g * width + pq * LANES:g * width + (pq + 1) * LANES, :] = (
                    hin[rs] * jnp.where(srow, t0, t1) + upd[rs])
        return carry

    lax.fori_loop(0, seqs, per_seq, 0)

    y = y_diag + yoff_scr[...] * expand(e3) + dexp_ref[...][None] * sx3
    y = y * _silu(z.reshape(seqs, SUBLANES, d_ssd))
    out_ref[...] = _group_rmsnorm(y, ng_ref[...][None]).astype(out_ref.dtype)


def _memkv_kernel(mem_ref, wk_ref, wv_ref, k_ref, v_ref):
    mb = mem_ref[0].astype(BF16)
    k_ref[0] = jnp.dot(mb, wk_ref[...], preferred_element_type=F32)
    v_ref[0] = jnp.dot(mb, wv_ref[...], preferred_element_type=F32)


def _xattn_kernel(x_ref, wq_ref, wg_ref, k_ref, v_ref, out_ref):
    xb = x_ref[0].astype(BF16)
    q = jnp.dot(xb, wq_ref[...], preferred_element_type=F32)
    gate = jnp.dot(xb, wg_ref[...], preferred_element_type=F32)
    d_head = q.shape[1] // MEM_HEADS
    scale = d_head ** -0.5
    for h in range(MEM_HEADS):
        hs = slice(h * d_head, (h + 1) * d_head)
        s = _mm_nt(q[:, hs], k_ref[0, :, hs]) * scale
        m = jnp.max(s, axis=-1, keepdims=True)
        p = jnp.exp(s - m)
        l = jnp.sum(p, axis=-1, keepdims=True)
        o = _mm(p, v_ref[0, :, hs]) * (1.0 / l)
        out_ref[0, :, hs] = (o * _silu(gate[:, hs])).astype(out_ref.dtype)


def _merge_kernel(rg_ref, ssd_ref, xa_ref, x_ref, w_ref, g_ref, b_ref, y_ref):
    d = rg_ref.shape[1]
    mix = (jnp.dot(rg_ref[...].astype(BF16), w_ref[0:d, :], preferred_element_type=F32)
           + jnp.dot(ssd_ref[...].astype(BF16), w_ref[d:2 * d, :], preferred_element_type=F32)
           + jnp.dot(xa_ref[...].astype(BF16), w_ref[2 * d:3 * d, :], preferred_element_type=F32))
    res = DEEPNORM_ALPHA * x_ref[...] + mix
    mu = jnp.mean(res, axis=-1, keepdims=True)
    cen = res - mu
    var = jnp.mean(cen * cen, axis=-1, keepdims=True)
    y_ref[...] = cen * lax.rsqrt(var + LN_EPS) * g_ref[...] + b_ref[...]


def _full(shape):
    return pl.BlockSpec(shape, lambda *_: (0,) * len(shape))


def _rg_prompt(x, p, tile):
    b, l, d = x.shape
    c = p["wx"].shape[1]
    row_spec = pl.BlockSpec((1, tile, d), lambda i, j: (i, j, 0))
    return pl.pallas_call(
        _rg_prompt_kernel,
        grid=(b, l // tile),
        in_specs=[row_spec, _full(p["wx"].shape), _full(p["wg"].shape), _full(p["cw"].shape),
                  _full(p["cb"].shape), _full(p["wgate"].shape), _full(p["ba"].shape),
                  _full(p["bi"].shape), _full(p["lam"].shape)],
        out_specs=[pl.BlockSpec((1, tile, c), lambda i, j: (i, j, 0)),
                   pl.BlockSpec((1, CONV_W - 1, c), lambda i, j: (i, 0, 0)),
                   pl.BlockSpec((1, 1, c), lambda i, j: (i, 0, 0))],
        out_shape=[jax.ShapeDtypeStruct((b, l, c), BF16),
                   jax.ShapeDtypeStruct((b, CONV_W - 1, c), F32),
                   jax.ShapeDtypeStruct((b, 1, c), F32)],
        scratch_shapes=[pltpu.VMEM((SUBLANES, c), F32), pltpu.VMEM((1, c), F32)],
        compiler_params=_cparams("parallel", "arbitrary"),
        name="rg_prompt",
    )(x, p["wx"], p["wg"], p["cw"], p["cb"], p["wgate"], p["ba"], p["bi"], p["lam"])


def _rg_sample(xpad, p, cstate, h0, seqs, valid):
    n, _, d = xpad.shape
    c = p["wx"].shape[1]
    blk = lambda w: pl.BlockSpec((seqs, w[0], w[1]), lambda i: (i, 0, 0))
    return pl.pallas_call(
        functools.partial(_rg_sample_kernel, valid=valid),
        grid=(n // seqs,),
        in_specs=[blk((SUBLANES, d)), _full(p["wx"].shape), _full(p["wg"].shape), _full(p["cw"].shape),
                  _full(p["cb"].shape), _full(p["wgate"].shape), _full(p["ba"].shape),
                  _full(p["bi"].shape), _full(p["lam"].shape), blk((SUBLANES, c)), blk((1, c))],
        out_specs=[blk((SUBLANES, c)), blk((CONV_W - 1, c)), blk((1, c))],
        out_shape=[jax.ShapeDtypeStruct((n, SUBLANES, c), F32),
                   jax.ShapeDtypeStruct((n, CONV_W - 1, c), F32),
                   jax.ShapeDtypeStruct((n, 1, c), F32)],
        compiler_params=_cparams("parallel"),
        name="rg_sample",
    )(xpad, p["wx"], p["wg"], p["cw"], p["cb"], p["wgate"], p["ba"], p["bi"], p["lam"], cstate, h0)


def _ssd_param_specs(p):
    return [_full(p[k].shape) for k in ("wxbc", "wz", "wdt", "cw", "cb", "dtb", "alog", "dexp", "ng")]


def _ssd_param_args(p):
    return [p[k] for k in ("wxbc", "wz", "wdt", "cw", "cb", "dtb", "alog", "dexp", "ng")]


def _ssd_prompt(x, p):
    b, l, d = x.shape
    cc = p["wxbc"].shape[1]
    c = p["wz"].shape[1]
    return pl.pallas_call(
        _ssd_prompt_kernel,
        grid=(b, l // SSD_CHUNK),
        in_specs=[pl.BlockSpec((1, SSD_CHUNK, d), lambda i, j: (i, j, 0))] + _ssd_param_specs(p),
        out_specs=[pl.BlockSpec((1, SSD_CHUNK, c), lambda i, j: (i, j, 0)),
                   pl.BlockSpec((1, CONV_W - 1, cc), lambda i, j: (i, 0, 0)),
                   pl.BlockSpec((1, c, SSD_STATE), lambda i, j: (i, 0, 0))],
        out_shape=[jax.ShapeDtypeStruct((b, l, c), BF16),
                   jax.ShapeDtypeStruct((b, CONV_W - 1, cc), F32),
                   jax.ShapeDtypeStruct((b, c, SSD_STATE), F32)],
        scratch_shapes=[pltpu.VMEM((SUBLANES, cc), F32)],
        compiler_params=_cparams("parallel", "arbitrary"),
        name="ssd_prompt",
    )(x, *_ssd_param_args(p))


def _ssd_sample(xpad, p, cstate, h0, seqs, valid):
    n, _, d = xpad.shape
    cc = p["wxbc"].shape[1]
    c = p["wz"].shape[1]
    gn = SSD_GROUPS * SSD_STATE
    blk = lambda w: pl.BlockSpec((seqs, w[0], w[1]), lambda i: (i, 0, 0))
    return pl.pallas_call(
        functools.partial(_ssd_sample_kernel, valid=valid),
        grid=(n // seqs,),
        in_specs=[blk((SUBLANES, d))] + _ssd_param_specs(p)
        + [_full(p["expand"].shape), blk((SUBLANES, cc)), blk((c, SSD_STATE))],
        out_specs=[blk((SUBLANES, c)), blk((CONV_W - 1, cc)), blk((c, SSD_STATE))],
        out_shape=[jax.ShapeDtypeStruct((n, SUBLANES, c), F32),
                   jax.ShapeDtypeStruct((n, CONV_W - 1, cc), F32),
                   jax.ShapeDtypeStruct((n, c, SSD_STATE), F32)],
        scratch_shapes=[pltpu.VMEM((seqs, SUBLANES, gn), F32), pltpu.VMEM((seqs, SUBLANES, gn), F32),
                        pltpu.VMEM((seqs, SUBLANES, c), F32), pltpu.VMEM((seqs, 1, LANES), F32),
                        pltpu.VMEM((seqs, SUBLANES, c), F32)],
        compiler_params=_cparams("parallel"),
        name="ssd_sample",
    )(xpad, *_ssd_param_args(p), p["expand"], cstate, h0)


def _memkv(mem, wk, wv):
    b, m, d = mem.shape
    c = wk.shape[1]
    spec = pl.BlockSpec((1, m, c), lambda i: (i, 0, 0))
    return pl.pallas_call(
        _memkv_kernel,
        grid=(b,),
        in_specs=[pl.BlockSpec((1, m, d), lambda i: (i, 0, 0)), _full(wk.shape), _full(wv.shape)],
        out_specs=[spec, spec],
        out_shape=[jax.ShapeDtypeStruct((b, m, c), F32)] * 2,
        compiler_params=_cparams("parallel"),
        name="mem_kv",
    )(mem, wk, wv)


def _xattn(x, wq, wg, k, v, tile, out_dtype, name):
    b, l, d = x.shape
    c = wq.shape[1]
    m = k.shape[1]
    kv_spec = pl.BlockSpec((1, m, c), lambda i, j: (i, 0, 0))
    return pl.pallas_call(
        _xattn_kernel,
        grid=(b, l // tile),
        in_specs=[pl.BlockSpec((1, tile, d), lambda i, j: (i, j, 0)), _full(wq.shape), _full(wg.shape),
                  kv_spec, kv_spec],
        out_specs=pl.BlockSpec((1, tile, c), lambda i, j: (i, j, 0)),
        out_shape=jax.ShapeDtypeStruct((b, l, c), out_dtype),
        compiler_params=_cparams("parallel", "parallel"),
        name=name,
    )(x, wq, wg, k, v)


def _merge(rg, ssd, xa, x, w_out, ln_g, ln_b, tile, name):
    n, d = x.shape
    c = rg.shape[1]
    tile = min(tile, n)
    row = lambda w: pl.BlockSpec((tile, w), lambda i: (i, 0))
    return pl.pallas_call(
        _merge_kernel,
        grid=(n // tile,),
        in_specs=[row(c), row(c), row(c), row(d), _full(w_out.shape), _full(ln_g.shape), _full(ln_b.shape)],
        out_specs=row(d),
        out_shape=jax.ShapeDtypeStruct((n, d), F32),
        compiler_params=_cparams("parallel"),
        name=name,
    )(rg, ssd, xa, x, w_out, ln_g, ln_b)


def _layer_params(w_in, rg_conv_w, rg_conv_b, w_rg_a, b_rg_a, w_rg_i, b_rg_i, rg_lambda, ssd_conv_w,
                  ssd_conv_b, ssd_dt_bias, ssd_a_log, ssd_d, ssd_norm_g, w_out, ln_g, ln_b):
    d_rg = rg_conv_w.shape[1]
    d_conv = ssd_conv_w.shape[1]
    d_ssd = ssd_norm_g.shape[0]
    heads = ssd_d.shape[0]
    sizes = (d_rg, d_rg, d_conv, d_ssd, heads)
    offs = [0]
    for s in sizes:
        offs.append(offs[-1] + s)
    d_xa = (w_in.shape[1] - offs[-1]) // 2
    wb = w_in.astype(BF16)
    row = lambda v: v.reshape(1, -1).astype(F32)
    pad_lanes = lambda v: jnp.pad(v, ((0, 0), (0, LANES - v.shape[1])))
    rg = dict(wx=wb[:, offs[0]:offs[1]], wg=wb[:, offs[1]:offs[2]], cw=rg_conv_w, cb=row(rg_conv_b),
              wgate=jnp.concatenate([w_rg_a, w_rg_i], axis=2).astype(BF16),
              ba=row(b_rg_a), bi=row(b_rg_i), lam=row(rg_lambda))
    head_of_channel = jnp.arange(d_ssd) // SSD_HEAD_DIM
    ssd = dict(wxbc=wb[:, offs[2]:offs[3]], wz=wb[:, offs[3]:offs[4]], wdt=pad_lanes(wb[:, offs[4]:offs[5]]),
               cw=ssd_conv_w, cb=row(ssd_conv_b), dtb=pad_lanes(row(ssd_dt_bias)),
               alog=pad_lanes(row(ssd_a_log)), dexp=row(ssd_d[head_of_channel]), ng=row(ssd_norm_g),
               expand=(jnp.arange(LANES)[:, None] == head_of_channel[None, :]).astype(F32))
    xa = dict(wq=wb[:, offs[5]:offs[5] + d_xa], wg=wb[:, offs[5] + d_xa:offs[5] + 2 * d_xa])
    merge = dict(w=w_out.astype(BF16), g=row(ln_g), b=row(ln_b))
    return rg, ssd, xa, merge


PROMPT_RG_TILE = 256
PROMPT_XA_TILE = 256
MERGE_TILE = 512
SAMPLE_RG_SEQS = 16
SAMPLE_SSD_SEQS = 8


def kernel(x_prompt, x_sample, mem_prompt, state_rg_conv, state_rg_h, state_ssd_conv, state_ssd_h,
           cache_mem_k, cache_mem_v, w_in, rg_conv_w, rg_conv_b, w_rg_a, b_rg_a, w_rg_i, b_rg_i,
           rg_lambda, ssd_conv_w, ssd_conv_b, ssd_dt_bias, ssd_a_log, ssd_d, ssd_norm_g, w_mem_k,
           w_mem_v, w_out, ln_g, ln_b):
    assert w_in.shape[0] == DEPTH
    bp, lp, d = x_prompt.shape
    bs, ls, _ = x_sample.shape
    heads = ssd_d.shape[1]
    outs = {k: [] for k in ("rgc_p", "rgh_p", "sc_p", "sh_p", "mk_p", "mv_p", "rgc_s", "rgh_s", "sc_s", "sh_s")}
    yp, ys = x_prompt, x_sample
    pad_rows = lambda v, before, after: jnp.pad(v, ((0, 0), (before, after), (0, 0)))
    for l in range(DEPTH):
        rg, ssd, xa, merge = _layer_params(
            w_in[l], rg_conv_w[l], rg_conv_b[l], w_rg_a[l], b_rg_a[l], w_rg_i[l], b_rg_i[l], rg_lambda[l],
            ssd_conv_w[l], ssd_conv_b[l], ssd_dt_bias[l], ssd_a_log[l], ssd_d[l], ssd_norm_g[l],
            w_out[l], ln_g[l], ln_b[l])
        d_xa = xa["wq"].shape[1]
        mk, mv = _memkv(mem_prompt, w_mem_k[l].astype(BF16), w_mem_v[l].astype(BF16))
        rg_o, rgc, rgh = _rg_prompt(yp, rg, PROMPT_RG_TILE)
        ssd_o, sc, sh = _ssd_prompt(yp, ssd)
        xa_o = _xattn(yp, xa["wq"], xa["wg"], mk, mv, PROMPT_XA_TILE, BF16, "xattn_prompt")
        flat = lambda v: v.reshape(bp * lp, v.shape[-1])
        yp = _merge(flat(rg_o), flat(ssd_o), flat(xa_o), flat(yp), merge["w"], merge["g"], merge["b"],
                    MERGE_TILE, "merge_prompt").reshape(bp, lp, d)
        outs["rgc_p"].append(rgc)
        outs["rgh_p"].append(rgh.reshape(bp, -1))
        outs["sc_p"].append(sc)
        outs["sh_p"].append(sh.reshape(bp, heads, SSD_HEAD_DIM, SSD_STATE))
        outs["mk_p"].append(mk.reshape(bp, -1, MEM_HEADS, d_xa // MEM_HEADS))
        outs["mv_p"].append(mv.reshape(bp, -1, MEM_HEADS, d_xa // MEM_HEADS))
        xs_pad = pad_rows(ys, 0, SAMPLE_PAD - ls)
        tail = SAMPLE_PAD - (CONV_W - 1)
        rg_o, rgc, rgh = _rg_sample(xs_pad, rg, pad_rows(state_rg_conv[l], tail, 0),
                                    state_rg_h[l][:, None, :], SAMPLE_RG_SEQS, ls)
        ssd_o, sc, sh = _ssd_sample(xs_pad, ssd, pad_rows(state_ssd_conv[l], tail, 0),
                                    state_ssd_h[l].reshape(bs, heads * SSD_HEAD_DIM, SSD_STATE),
                                    SAMPLE_SSD_SEQS, ls)
        xa_o = _xattn(xs_pad, xa["wq"], xa["wg"], cache_mem_k[l].reshape(bs, -1, d_xa),
                      cache_mem_v[l].reshape(bs, -1, d_xa), SAMPLE_PAD, F32, "xattn_sample")
        flat = lambda v: v.reshape(bs * SAMPLE_PAD, v.shape[-1])
        ys_pad = _merge(flat(rg_o), flat(ssd_o), flat(xa_o), flat(xs_pad), merge["w"], merge["g"], merge["b"],
                        MERGE_TILE, "merge_sample").reshape(bs, SAMPLE_PAD, d)
        ys = ys_pad[:, :ls, :]
        outs["rgc_s"].append(rgc)
        outs["rgh_s"].append(rgh.reshape(bs, -1))
        outs["sc_s"].append(sc)
        outs["sh_s"].append(sh.reshape(bs, heads, SSD_HEAD_DIM, SSD_STATE))
    st = lambda k: jnp.stack(outs[k])
    return (yp, ys, st("rgc_p"), st("rgh_p"), st("sc_p"), st("sh_p"), st("mk_p"), st("mv_p"),
            st("rgc_s"), st("rgh_s"), st("sc_s"), st("sh_s"))
```

```python
import functools

import jax
import jax.numpy as jnp
from jax import lax
from jax.experimental import pallas as pl
from jax.experimental.pallas import tpu as pltpu

F32 = jnp.float32
BF16 = jnp.bfloat16

SUBLANES = 8
LANES = 128
VMEM_LIMIT_BYTES = 56 * 1024 * 1024

RG_C = 8.0
CONV_W = 4
RG_BLOCKS = 8
SSD_HEAD_DIM = 64
SSD_GROUPS = 2
SSD_STATE = 128
SSD_CHUNK = 128
MEM_HEADS = 4
LN_EPS = 1e-5
RMS_EPS = 1e-5
DEPTH = 1
DEEPNORM_ALPHA = (2 * DEPTH) ** 0.25
SAMPLE_PAD = SUBLANES


def _cparams(*sem):
    return pltpu.CompilerParams(dimension_semantics=sem, vmem_limit_bytes=VMEM_LIMIT_BYTES)


def _mm(a, b):
    return jnp.dot(a.astype(BF16), b.astype(BF16), preferred_element_type=F32)


def _mm_nt(a, b):
    return lax.dot_general(a.astype(BF16), b.astype(BF16), (((1,), (1,)), ((), ())),
                           preferred_element_type=F32)


def _mm_tn(a, b):
    return lax.dot_general(a.astype(BF16), b.astype(BF16), (((0,), (0,)), ((), ())),
                           preferred_element_type=F32)


def _sigmoid(x):
    return 1.0 / (1.0 + jnp.exp(-x))


def _silu(x):
    return x * _sigmoid(x)


def _softplus(x):
    return jnp.maximum(x, 0.0) + jnp.log(1.0 + jnp.exp(-jnp.abs(x)))


def _causal_conv_tiles(x3, p3, w, b):
    row = lax.broadcasted_iota(jnp.int32, x3.shape, 1)
    y = x3 * w[CONV_W - 1:CONV_W][None]
    for s in range(1, CONV_W):
        shifted = jnp.where(row >= s, pltpu.roll(x3, s, axis=1), pltpu.roll(p3, s, axis=1))
        y = y + shifted * w[CONV_W - 1 - s:CONV_W - s][None]
    return y + b[None]


def _scan_in_tiles(a3, b3):
    row = lax.broadcasted_iota(jnp.int32, a3.shape, 1)
    s = 1
    while s < SUBLANES:
        keep = row >= s
        a_sh = jnp.where(keep, pltpu.roll(a3, s, axis=1), 1.0)
        b_sh = jnp.where(keep, pltpu.roll(b3, s, axis=1), 0.0)
        b3 = a3 * b_sh + b3
        a3 = a3 * a_sh
        s *= 2
    return a3, b3


def _rg_gates(u, wgate, ba, bi, lam):
    pre = _mm(u, wgate)
    r = _sigmoid(pre[:, :LANES] + ba)
    i = _sigmoid(pre[:, LANES:] + bi)
    log_a = (-RG_C) * r * _softplus(-lam)
    a = jnp.exp(log_a)
    mult = jnp.sqrt(-jnp.tanh(log_a) * (1.0 + a * a))
    return a, mult * (i * u)


def _rg_prompt_kernel(x_ref, wx_ref, wg_ref, cw_ref, cb_ref, wgate_ref, ba_ref, bi_ref, lam_ref,
                      out_ref, conv_ref, hlast_ref, tail_scr, h_scr):
    t = pl.program_id(1)

    @pl.when(t == 0)
    def _():
        tail_scr[...] = jnp.zeros_like(tail_scr)
        h_scr[...] = jnp.zeros_like(h_scr)

    rows = x_ref.shape[1]
    g_tiles = rows // SUBLANES
    xb = x_ref[0].astype(BF16)
    rgx = jnp.dot(xb, wx_ref[...], preferred_element_type=F32)
    gate = jnp.dot(xb, wg_ref[...], preferred_element_type=F32)
    x3 = rgx.reshape(g_tiles, SUBLANES, rgx.shape[1])
    p3 = jnp.concatenate([tail_scr[...][None], x3[:-1]], axis=0)
    tail_scr[...] = x3[g_tiles - 1]
    conv_ref[0] = x3[g_tiles - 1][SUBLANES - (CONV_W - 1):]

    for k in range(RG_BLOCKS):
        ks = slice(k * LANES, (k + 1) * LANES)
        u3 = _causal_conv_tiles(x3[:, :, ks], p3[:, :, ks], cw_ref[:, ks], cb_ref[:, ks])
        u = u3.reshape(rows, LANES)
        a, b = _rg_gates(u, wgate_ref[k], ba_ref[:, ks], bi_ref[:, ks], lam_ref[:, ks])
        a3, b3 = _scan_in_tiles(a.reshape(g_tiles, SUBLANES, LANES), b.reshape(g_tiles, SUBLANES, LANES))
        carry = h_scr[:, ks]
        hs = []
        for g in range(g_tiles):
            hg = a3[g] * carry + b3[g]
            hs.append(hg)
            carry = hg[SUBLANES - 1:]
        h_scr[:, ks] = carry
        hlast_ref[0, :, ks] = carry
        h = jnp.concatenate(hs, axis=0)
        out_ref[0, :, ks] = (h * _silu(gate[:, ks])).astype(out_ref.dtype)


def _rg_sample_kernel(x_ref, wx_ref, wg_ref, cw_ref, cb_ref, wgate_ref, ba_ref, bi_ref, lam_ref,
                      cstate_ref, h0_ref, out_ref, conv_ref, hlast_ref, *, valid):
    seqs = x_ref.shape[0]
    rows = seqs * SUBLANES
    xb = x_ref[...].reshape(rows, x_ref.shape[2]).astype(BF16)
    rgx = jnp.dot(xb, wx_ref[...], preferred_element_type=F32)
    gate = jnp.dot(xb, wg_ref[...], preferred_element_type=F32)
    x3 = rgx.reshape(seqs, SUBLANES, rgx.shape[1])
    conv_ref[...] = x3[:, valid - (CONV_W - 1):valid, :]
    row = lax.broadcasted_iota(jnp.int32, (seqs, SUBLANES, LANES), 1)
    for k in range(RG_BLOCKS):
        ks = slice(k * LANES, (k + 1) * LANES)
        u3 = _causal_conv_tiles(x3[:, :, ks], cstate_ref[:, :, ks], cw_ref[:, ks], cb_ref[:, ks])
        u = u3.reshape(rows, LANES)
        a, b = _rg_gates(u, wgate_ref[k], ba_ref[:, ks], bi_ref[:, ks], lam_ref[:, ks])
        a3 = a.reshape(seqs, SUBLANES, LANES)
        b3 = b.reshape(seqs, SUBLANES, LANES)
        b3 = b3 + jnp.where(row == 0, a3 * h0_ref[:, :, ks], 0.0)
        _, h3 = _scan_in_tiles(a3, b3)
        hlast_ref[:, :, ks] = h3[:, valid - 1:valid, :]
        g3 = gate[:, ks].reshape(seqs, SUBLANES, LANES)
        out_ref[:, :, ks] = (h3 * _silu(g3)).astype(out_ref.dtype)


def _cumsum_rows(x):
    rows = x.shape[0]
    row = lax.broadcasted_iota(jnp.int32, x.shape, 0)
    s = 1
    while s < rows:
        x = x + jnp.where(row >= s, pltpu.roll(x, s, axis=0), 0.0)
        s *= 2
    return x


def _group_rmsnorm(y, gain):
    width = y.shape[-1] // SSD_GROUPS
    parts = []
    for g in range(SSD_GROUPS):
        yg = y[..., g * width:(g + 1) * width]
        ms = jnp.sum(yg * yg, axis=-1, keepdims=True) * (1.0 / width)
        parts.append(yg * lax.rsqrt(ms + RMS_EPS))
    return jnp.concatenate(parts, axis=-1) * gain


def _ssd_prompt_kernel(x_ref, wxbc_ref, wz_ref, wdt_ref, cw_ref, cb_ref, dtb_ref, alog_ref, dexp_ref,
                       ng_ref, out_ref, conv_ref, h_ref, tail_scr):
    t = pl.program_id(1)

    @pl.when(t == 0)
    def _():
        tail_scr[...] = jnp.zeros_like(tail_scr)
        h_ref[...] = jnp.zeros_like(h_ref)

    q = x_ref.shape[1]
    g_tiles = q // SUBLANES
    d_ssd = wz_ref.shape[1]
    gn = SSD_GROUPS * SSD_STATE
    xb = x_ref[0].astype(BF16)
    xbc_raw = jnp.dot(xb, wxbc_ref[...], preferred_element_type=F32)
    z = jnp.dot(xb, wz_ref[...], preferred_element_type=F32)
    dtr = jnp.dot(xb, wdt_ref[...], preferred_element_type=F32)

    x3 = xbc_raw.reshape(g_tiles, SUBLANES, xbc_raw.shape[1])
    p3 = jnp.concatenate([tail_scr[...][None], x3[:-1]], axis=0)
    tail_scr[...] = x3[g_tiles - 1]
    conv_ref[0] = x3[g_tiles - 1][SUBLANES - (CONV_W - 1):]
    xbc = _silu(_causal_conv_tiles(x3, p3, cw_ref[...], cb_ref[...])).reshape(q, xbc_raw.shape[1])
    sx = xbc[:, :d_ssd]
    bm = xbc[:, d_ssd:d_ssd + gn]
    cm = xbc[:, d_ssd + gn:]

    dt = _softplus(dtr + dtb_ref[...])
    da = dt * (-jnp.exp(alog_ref[...]))
    acum = _cumsum_rows(da)
    alast = acum[q - 1:q, :]
    wgt = dt * jnp.exp(alast - acum)
    tot = jnp.exp(alast)
    acum_t = acum.T
    dt_t = dt.T

    ii = lax.broadcasted_iota(jnp.int32, (q, q), 0)
    jj = lax.broadcasted_iota(jnp.int32, (q, q), 1)
    causal = ii >= jj
    lane = lax.broadcasted_iota(jnp.int32, (q, LANES), 1)
    lo = lane < SSD_HEAD_DIM
    srow = lax.broadcasted_iota(jnp.int32, (LANES, SSD_STATE), 0) < SSD_HEAD_DIM

    cb = [_mm_nt(cm[:, g * SSD_STATE:(g + 1) * SSD_STATE], bm[:, g * SSD_STATE:(g + 1) * SSD_STATE])
          for g in range(SSD_GROUPS)]
    heads = d_ssd // SSD_HEAD_DIM
    pairs = heads // 2
    y_parts = []
    for pq in range(pairs):
        g = (2 * pq) // (heads // SSD_GROUPS)
        ps = slice(pq * LANES, (pq + 1) * LANES)
        xq = sx[:, ps]
        ms, es, ws, ts = [], [], [], []
        for h in (2 * pq, 2 * pq + 1):
            acol = jnp.broadcast_to(acum[:, h:h + 1], (q, q))
            arow = jnp.broadcast_to(acum_t[h:h + 1, :], (q, q))
            decay = jnp.exp(jnp.where(causal, acol - arow, -jnp.inf))
            ms.append((cb[g] * decay * jnp.broadcast_to(dt_t[h:h + 1, :], (q, q))).astype(BF16))
            es.append(jnp.exp(jnp.broadcast_to(acum[:, h:h + 1], (q, LANES))))
            ws.append(jnp.broadcast_to(wgt[:, h:h + 1], (q, LANES)))
            ts.append(jnp.broadcast_to(tot[:, h:h + 1], (LANES, SSD_STATE)))
        lhs = jnp.concatenate(ms, axis=1)
        rhs = jnp.concatenate([jnp.where(lo, xq, 0.0), jnp.where(lo, 0.0, xq)], axis=0)
        y_diag = _mm(lhs, rhs)
        hq = h_ref[0, ps, :]
        y_off = _mm_nt(cm[:, g * SSD_STATE:(g + 1) * SSD_STATE], hq) * jnp.where(lo, es[0], es[1])
        xw = xq * jnp.where(lo, ws[0], ws[1])
        h_ref[0, ps, :] = hq * jnp.where(srow, ts[0], ts[1]) + _mm_tn(xw, bm[:, g * SSD_STATE:(g + 1) * SSD_STATE])
        y_parts.append(y_diag + y_off + dexp_ref[:, ps] * xq)
    y = jnp.concatenate(y_parts, axis=1) * _silu(z)
    out_ref[0] = _group_rmsnorm(y, ng_ref[...]).astype(out_ref.dtype)


def _ssd_sample_kernel(x_ref, wxbc_ref, wz_ref, wdt_ref, cw_ref, cb_ref, dtb_ref, alog_ref, dexp_ref,
                       ng_ref, expand_ref, cstate_ref, h0_ref, out_ref, conv_ref, h_ref,
                       c_scr, b_scr, xw_scr, tot_scr, yoff_scr, *, valid):
    seqs = x_ref.shape[0]
    rows = seqs * SUBLANES
    d_ssd = wz_ref.shape[1]
    gn = SSD_GROUPS * SSD_STATE
    heads = d_ssd // SSD_HEAD_DIM
    hg = heads // SSD_GROUPS
    xb = x_ref[...].reshape(rows, x_ref.shape[2]).astype(BF16)
    xbc_raw = jnp.dot(xb, wxbc_ref[...], preferred_element_type=F32)
    z = jnp.dot(xb, wz_ref[...], preferred_element_type=F32)
    dtr = jnp.dot(xb, wdt_ref[...], preferred_element_type=F32)

    x3 = xbc_raw.reshape(seqs, SUBLANES, xbc_raw.shape[1])
    conv_ref[...] = x3[:, valid - (CONV_W - 1):valid, :]
    xbc3 = _silu(_causal_conv_tiles(x3, cstate_ref[...], cw_ref[...], cb_ref[...]))
    sx3 = xbc3[:, :, :d_ssd]
    b3 = xbc3[:, :, d_ssd:d_ssd + gn]
    c3 = xbc3[:, :, d_ssd + gn:]

    row = lax.broadcasted_iota(jnp.int32, (seqs, SUBLANES, LANES), 1)
    lane = lax.broadcasted_iota(jnp.int32, (seqs, SUBLANES, LANES), 2)
    dt3 = jnp.where(row < valid, _softplus(dtr + dtb_ref[...]).reshape(seqs, SUBLANES, LANES), 0.0)
    da3 = dt3 * (-jnp.exp(alog_ref[...]))[None]
    ones = jnp.ones_like(da3)
    _, acum3 = _scan_in_tiles(ones, da3)
    alast = acum3[:, SUBLANES - 1:, :]
    wgt3 = dt3 * jnp.exp(alast - acum3)
    tot_scr[...] = jnp.exp(alast)
    e3 = jnp.exp(acum3)

    def expand(v3):
        flat = jnp.dot(v3.reshape(rows, LANES), expand_ref[...], precision=lax.Precision.HIGHEST,
                       preferred_element_type=F32)
        return flat.reshape(seqs, SUBLANES, d_ssd)

    y_diag = jnp.zeros((seqs, SUBLANES, d_ssd), F32)
    for u in range(valid):
        prod = c3 * b3[:, u:u + 1, :]
        cbu = [jnp.sum(prod[:, :, g * SSD_STATE:(g + 1) * SSD_STATE], axis=-1, keepdims=True)
               for g in range(SSD_GROUPS)]
        cb_heads = jnp.where(lane < hg, cbu[0], cbu[1])
        coef = jnp.where(row >= u, cb_heads * jnp.exp(acum3 - acum3[:, u:u + 1, :]) * dt3[:, u:u + 1, :], 0.0)
        y_diag = y_diag + expand(coef) * sx3[:, u:u + 1, :]

    c_scr[...] = c3
    b_scr[...] = b3
    xw_scr[...] = sx3 * expand(wgt3)
    srow = lax.broadcasted_iota(jnp.int32, (2 * SSD_HEAD_DIM, SSD_STATE), 0) < SSD_HEAD_DIM

    def per_seq(s, carry):
        cs = c_scr[s]
        bs = b_scr[s]
        xws = xw_scr[s]
        tots = tot_scr[s]
        for g in range(SSD_GROUPS):
            gs = slice(g * SSD_STATE, (g + 1) * SSD_STATE)
            width = hg * SSD_HEAD_DIM
            cols = slice(g * width, (g + 1) * width)
            hin = h0_ref[s, cols, :]
            yoff_scr[s, :, cols] = _mm_nt(cs[:, gs], hin)
            upd = _mm_tn(xws[:, cols], bs[:, gs])
            for pq in range(hg // 2):
                h = g * hg + 2 * pq
                rs = slice(pq * LANES, (pq + 1) * LANES)
                t0 = jnp.broadcast_to(tots[0:1, h:h + 1], (LANES, SSD_STATE))
                t1 = jnp.broadcast_to(tots[0:1, h + 1:h + 2], (LANES, SSD_STATE))
                h_ref[s, g * width + pq * LANES:g * width + (pq + 1) * LANES, :] = (
                    hin[rs] * jnp.where(srow, t0, t1) + upd[rs])
        return carry

    lax.fori_loop(0, seqs, per_seq, 0)

    y = y_diag + yoff_scr[...] * expand(e3) + dexp_ref[...][None] * sx3
    y = y * _silu(z.reshape(seqs, SUBLANES, d_ssd))
    out_ref[...] = _group_rmsnorm(y, ng_ref[...][None]).astype(out_ref.dtype)


def _memkv_kernel(mem_ref, wk_ref, wv_ref, k_ref, v_ref, kb_ref, vb_ref):
    mb = mem_ref[0].astype(BF16)
    for w_ref, o_ref, ob_ref in ((wk_ref, k_ref, kb_ref), (wv_ref, v_ref, vb_ref)):
        proj = jnp.dot(mb, w_ref[...], preferred_element_type=F32)
        o_ref[0] = proj.reshape(o_ref.shape[1:])
        ob_ref[0] = proj.astype(BF16)


def _xattn_kernel(x_ref, wq_ref, wg_ref, k_ref, v_ref, out_ref):
    xb = x_ref[0].astype(BF16)
    q = jnp.dot(xb, wq_ref[...], preferred_element_type=F32)
    gate = jnp.dot(xb, wg_ref[...], preferred_element_type=F32)
    d_head = q.shape[1] // MEM_HEADS
    scale = d_head ** -0.5
    for h in range(MEM_HEADS):
        hs = slice(h * d_head, (h + 1) * d_head)
        s = _mm_nt(q[:, hs], k_ref[0, :, hs]) * scale
        m = jnp.max(s, axis=-1, keepdims=True)
        p = jnp.exp(s - m)
        l = jnp.sum(p, axis=-1, keepdims=True)
        o = _mm(p, v_ref[0, :, hs]) * (1.0 / l)
        out_ref[0, :, hs] = (o * _silu(gate[:, hs])).astype(out_ref.dtype)


def _xattn_sample_kernel(x_ref, wq_ref, wg_ref, k_ref, v_ref, out_ref):
    seqs, _, d = x_ref.shape
    n_mem, heads, d_head = k_ref.shape[1:]
    xb = x_ref[...].reshape(seqs * SUBLANES, d).astype(BF16)
    q = jnp.dot(xb, wq_ref[...], preferred_element_type=F32)
    gate = jnp.dot(xb, wg_ref[...], preferred_element_type=F32)
    scale = d_head ** -0.5
    shape = (heads * SUBLANES, n_mem * heads)
    same_head = (lax.broadcasted_iota(jnp.int32, shape, 0) // SUBLANES
                 == lax.broadcasted_iota(jnp.int32, shape, 1) % heads)
    for s in range(seqs):
        rs = slice(s * SUBLANES, (s + 1) * SUBLANES)
        qh = jnp.concatenate([q[rs, h * d_head:(h + 1) * d_head] for h in range(heads)], axis=0)
        sc = jnp.where(same_head, _mm_nt(qh, k_ref[s].reshape(n_mem * heads, d_head)) * scale, -jnp.inf)
        p = jnp.exp(sc - jnp.max(sc, axis=-1, keepdims=True))
        l = jnp.sum(p, axis=-1, keepdims=True)
        o = _mm(p, v_ref[s].reshape(n_mem * heads, d_head)) * (1.0 / l)
        o = jnp.concatenate([o[h * SUBLANES:(h + 1) * SUBLANES] for h in range(heads)], axis=1)
        out_ref[s] = (o * _silu(gate[rs])).astype(out_ref.dtype)


def _merge_kernel(rg_ref, ssd_ref, xa_ref, x_ref, w_ref, g_ref, b_ref, y_ref):
    d = rg_ref.shape[1]
    mix = (jnp.dot(rg_ref[...].astype(BF16), w_ref[0:d, :], preferred_element_type=F32)
           + jnp.dot(ssd_ref[...].astype(BF16), w_ref[d:2 * d, :], preferred_element_type=F32)
           + jnp.dot(xa_ref[...].astype(BF16), w_ref[2 * d:3 * d, :], preferred_element_type=F32))
    res = DEEPNORM_ALPHA * x_ref[...] + mix
    mu = jnp.mean(res, axis=-1, keepdims=True)
    cen = res - mu
    var = jnp.mean(cen * cen, axis=-1, keepdims=True)
    y_ref[...] = cen * lax.rsqrt(var + LN_EPS) * g_ref[...] + b_ref[...]


def _full(shape):
    return pl.BlockSpec(shape, lambda *_: (0,) * len(shape))


def _rg_prompt(x, p, tile):
    b, l, d = x.shape
    c = p["wx"].shape[1]
    row_spec = pl.BlockSpec((1, tile, d), lambda i, j: (i, j, 0))
    return pl.pallas_call(
        _rg_prompt_kernel,
        grid=(b, l // tile),
        in_specs=[row_spec, _full(p["wx"].shape), _full(p["wg"].shape), _full(p["cw"].shape),
                  _full(p["cb"].shape), _full(p["wgate"].shape), _full(p["ba"].shape),
                  _full(p["bi"].shape), _full(p["lam"].shape)],
        out_specs=[pl.BlockSpec((1, tile, c), lambda i, j: (i, j, 0)),
                   pl.BlockSpec((1, CONV_W - 1, c), lambda i, j: (i, 0, 0)),
                   pl.BlockSpec((1, 1, c), lambda i, j: (i, 0, 0))],
        out_shape=[jax.ShapeDtypeStruct((b, l, c), BF16),
                   jax.ShapeDtypeStruct((b, CONV_W - 1, c), F32),
                   jax.ShapeDtypeStruct((b, 1, c), F32)],
        scratch_shapes=[pltpu.VMEM((SUBLANES, c), F32), pltpu.VMEM((1, c), F32)],
        compiler_params=_cparams("parallel", "arbitrary"),
        name="rg_prompt",
    )(x, p["wx"], p["wg"], p["cw"], p["cb"], p["wgate"], p["ba"], p["bi"], p["lam"])


def _rg_sample(xpad, p, cstate, h0, seqs, valid):
    n, _, d = xpad.shape
    c = p["wx"].shape[1]
    blk = lambda w: pl.BlockSpec((seqs, w[0], w[1]), lambda i: (i, 0, 0))
    return pl.pallas_call(
        functools.partial(_rg_sample_kernel, valid=valid),
        grid=(n // seqs,),
        in_specs=[blk((SUBLANES, d)), _full(p["wx"].shape), _full(p["wg"].shape), _full(p["cw"].shape),
                  _full(p["cb"].shape), _full(p["wgate"].shape), _full(p["ba"].shape),
                  _full(p["bi"].shape), _full(p["lam"].shape), blk((SUBLANES, c)), blk((1, c))],
        out_specs=[blk((SUBLANES, c)), blk((CONV_W - 1, c)), blk((1, c))],
        out_shape=[jax.ShapeDtypeStruct((n, SUBLANES, c), F32),
                   jax.ShapeDtypeStruct((n, CONV_W - 1, c), F32),
                   jax.ShapeDtypeStruct((n, 1, c), F32)],
        compiler_params=_cparams("parallel"),
        name="rg_sample",
    )(xpad, p["wx"], p["wg"], p["cw"], p["cb"], p["wgate"], p["ba"], p["bi"], p["lam"], cstate, h0)


def _ssd_param_specs(p):
    return [_full(p[k].shape) for k in ("wxbc", "wz", "wdt", "cw", "cb", "dtb", "alog", "dexp", "ng")]


def _ssd_param_args(p):
    return [p[k] for k in ("wxbc", "wz", "wdt", "cw", "cb", "dtb", "alog", "dexp", "ng")]


def _ssd_prompt(x, p):
    b, l, d = x.shape
    cc = p["wxbc"].shape[1]
    c = p["wz"].shape[1]
    return pl.pallas_call(
        _ssd_prompt_kernel,
        grid=(b, l // SSD_CHUNK),
        in_specs=[pl.BlockSpec((1, SSD_CHUNK, d), lambda i, j: (i, j, 0))] + _ssd_param_specs(p),
        out_specs=[pl.BlockSpec((1, SSD_CHUNK, c), lambda i, j: (i, j, 0)),
                   pl.BlockSpec((1, CONV_W - 1, cc), lambda i, j: (i, 0, 0)),
                   pl.BlockSpec((1, c, SSD_STATE), lambda i, j: (i, 0, 0))],
        out_shape=[jax.ShapeDtypeStruct((b, l, c), BF16),
                   jax.ShapeDtypeStruct((b, CONV_W - 1, cc), F32),
                   jax.ShapeDtypeStruct((b, c, SSD_STATE), F32)],
        scratch_shapes=[pltpu.VMEM((SUBLANES, cc), F32)],
        compiler_params=_cparams("parallel", "arbitrary"),
        name="ssd_prompt",
    )(x, *_ssd_param_args(p))


def _ssd_sample(xpad, p, cstate, h0, seqs, valid):
    n, _, d = xpad.shape
    cc = p["wxbc"].shape[1]
    c = p["wz"].shape[1]
    gn = SSD_GROUPS * SSD_STATE
    blk = lambda w: pl.BlockSpec((seqs, w[0], w[1]), lambda i: (i, 0, 0))
    return pl.pallas_call(
        functools.partial(_ssd_sample_kernel, valid=valid),
        grid=(n // seqs,),
        in_specs=[blk((SUBLANES, d))] + _ssd_param_specs(p)
        + [_full(p["expand"].shape), blk((SUBLANES, cc)), blk((c, SSD_STATE))],
        out_specs=[blk((SUBLANES, c)), blk((CONV_W - 1, cc)), blk((c, SSD_STATE))],
        out_shape=[jax.ShapeDtypeStruct((n, SUBLANES, c), F32),
                   jax.ShapeDtypeStruct((n, CONV_W - 1, cc), F32),
                   jax.ShapeDtypeStruct((n, c, SSD_STATE), F32)],
        scratch_shapes=[pltpu.VMEM((seqs, SUBLANES, gn), F32), pltpu.VMEM((seqs, SUBLANES, gn), F32),
                        pltpu.VMEM((seqs, SUBLANES, c), F32), pltpu.VMEM((seqs, 1, LANES), F32),
                        pltpu.VMEM((seqs, SUBLANES, c), F32)],
        compiler_params=_cparams("parallel"),
        name="ssd_sample",
    )(xpad, *_ssd_param_args(p), p["expand"], cstate, h0)


def _memkv(mem, wk, wv):
    b, m, d = mem.shape
    c = wk.shape[1]
    spec = pl.BlockSpec((1, m, c), lambda i: (i, 0, 0))
    spec4 = pl.BlockSpec((1, m, MEM_HEADS, c // MEM_HEADS), lambda i: (i, 0, 0, 0))
    return pl.pallas_call(
        _memkv_kernel,
        grid=(b,),
        in_specs=[pl.BlockSpec((1, m, d), lambda i: (i, 0, 0)), _full(wk.shape), _full(wv.shape)],
        out_specs=[spec4, spec4, spec, spec],
        out_shape=[jax.ShapeDtypeStruct((b, m, MEM_HEADS, c // MEM_HEADS), F32)] * 2
        + [jax.ShapeDtypeStruct((b, m, c), BF16)] * 2,
        compiler_params=_cparams("parallel"),
        name="mem_kv",
    )(mem, wk, wv)


def _xattn(x, wq, wg, k, v, tile, out_dtype, name):
    b, l, d = x.shape
    c = wq.shape[1]
    m = k.shape[1]
    kv_spec = pl.BlockSpec((1, m, c), lambda i, j: (i, 0, 0))
    return pl.pallas_call(
        _xattn_kernel,
        grid=(b, l // tile),
        in_specs=[pl.BlockSpec((1, tile, d), lambda i, j: (i, j, 0)), _full(wq.shape), _full(wg.shape),
                  kv_spec, kv_spec],
        out_specs=pl.BlockSpec((1, tile, c), lambda i, j: (i, j, 0)),
        out_shape=jax.ShapeDtypeStruct((b, l, c), out_dtype),
        compiler_params=_cparams("parallel", "parallel"),
        name=name,
    )(x, wq, wg, k, v)


def _xattn_sample(xpad, wq, wg, k, v, seqs):
    n, _, d = xpad.shape
    c = wq.shape[1]
    kv_spec = pl.BlockSpec((seqs,) + k.shape[1:], lambda i: (i, 0, 0, 0))
    row_spec = lambda w: pl.BlockSpec((seqs, SUBLANES, w), lambda i: (i, 0, 0))
    return pl.pallas_call(
        _xattn_sample_kernel,
        grid=(n // seqs,),
        in_specs=[row_spec(d), _full(wq.shape), _full(wg.shape), kv_spec, kv_spec],
        out_specs=row_spec(c),
        out_shape=jax.ShapeDtypeStruct((n, SUBLANES, c), F32),
        compiler_params=_cparams("parallel"),
        name="xattn_sample",
    )(xpad, wq, wg, k, v)


def _merge(rg, ssd, xa, x, w_out, ln_g, ln_b, tile, name):
    n, d = x.shape
    c = rg.shape[1]
    tile = min(tile, n)
    row = lambda w: pl.BlockSpec((tile, w), lambda i: (i, 0))
    return pl.pallas_call(
        _merge_kernel,
        grid=(n // tile,),
        in_specs=[row(c), row(c), row(c), row(d), _full(w_out.shape), _full(ln_g.shape), _full(ln_b.shape)],
        out_specs=row(d),
        out_shape=jax.ShapeDtypeStruct((n, d), F32),
        compiler_params=_cparams("parallel"),
        name=name,
    )(rg, ssd, xa, x, w_out, ln_g, ln_b)


def _layer_params(w_in, rg_conv_w, rg_conv_b, w_rg_a, b_rg_a, w_rg_i, b_rg_i, rg_lambda, ssd_conv_w,
                  ssd_conv_b, ssd_dt_bias, ssd_a_log, ssd_d, ssd_norm_g, w_out, ln_g, ln_b):
    d_rg = rg_conv_w.shape[1]
    d_conv = ssd_conv_w.shape[1]
    d_ssd = ssd_norm_g.shape[0]
    heads = ssd_d.shape[0]
    sizes = (d_rg, d_rg, d_conv, d_ssd, heads)
    offs = [0]
    for s in sizes:
        offs.append(offs[-1] + s)
    d_xa = (w_in.shape[1] - offs[-1]) // 2
    wb = w_in.astype(BF16)
    row = lambda v: v.reshape(1, -1).astype(F32)
    pad_lanes = lambda v: jnp.pad(v, ((0, 0), (0, LANES - v.shape[1])))
    rg = dict(wx=wb[:, offs[0]:offs[1]], wg=wb[:, offs[1]:offs[2]], cw=rg_conv_w, cb=row(rg_conv_b),
              wgate=jnp.concatenate([w_rg_a, w_rg_i], axis=2).astype(BF16),
              ba=row(b_rg_a), bi=row(b_rg_i), lam=row(rg_lambda))
    head_of_channel = jnp.arange(d_ssd) // SSD_HEAD_DIM
    ssd = dict(wxbc=wb[:, offs[2]:offs[3]], wz=wb[:, offs[3]:offs[4]], wdt=pad_lanes(wb[:, offs[4]:offs[5]]),
               cw=ssd_conv_w, cb=row(ssd_conv_b), dtb=pad_lanes(row(ssd_dt_bias)),
               alog=pad_lanes(row(ssd_a_log)), dexp=row(ssd_d[head_of_channel]), ng=row(ssd_norm_g),
               expand=(jnp.arange(LANES)[:, None] == head_of_channel[None, :]).astype(F32))
    xa = dict(wq=wb[:, offs[5]:offs[5] + d_xa], wg=wb[:, offs[5] + d_xa:offs[5] + 2 * d_xa])
    merge = dict(w=w_out.astype(BF16), g=row(ln_g), b=row(ln_b))
    return rg, ssd, xa, merge


PROMPT_RG_TILE = 256
PROMPT_XA_TILE = 256
MERGE_TILE = 512
SAMPLE_RG_SEQS = 16
SAMPLE_SSD_SEQS = 8
SAMPLE_XA_SEQS = 4


def kernel(x_prompt, x_sample, mem_prompt, state_rg_conv, state_rg_h, state_ssd_conv, state_ssd_h,
           cache_mem_k, cache_mem_v, w_in, rg_conv_w, rg_conv_b, w_rg_a, b_rg_a, w_rg_i, b_rg_i,
           rg_lambda, ssd_conv_w, ssd_conv_b, ssd_dt_bias, ssd_a_log, ssd_d, ssd_norm_g, w_mem_k,
           w_mem_v, w_out, ln_g, ln_b):
    assert w_in.shape[0] == DEPTH
    bp, lp, d = x_prompt.shape
    bs, ls, _ = x_sample.shape
    heads = ssd_d.shape[1]
    outs = {k: [] for k in ("rgc_p", "rgh_p", "sc_p", "sh_p", "mk_p", "mv_p", "rgc_s", "rgh_s", "sc_s", "sh_s")}
    yp, ys = x_prompt, x_sample
    pad_rows = lambda v, before, after: jnp.pad(v, ((0, 0), (before, after), (0, 0)))
    for l in range(DEPTH):
        rg, ssd, xa, merge = _layer_params(
            w_in[l], rg_conv_w[l], rg_conv_b[l], w_rg_a[l], b_rg_a[l], w_rg_i[l], b_rg_i[l], rg_lambda[l],
            ssd_conv_w[l], ssd_conv_b[l], ssd_dt_bias[l], ssd_a_log[l], ssd_d[l], ssd_norm_g[l],
            w_out[l], ln_g[l], ln_b[l])
        mk, mv, mkb, mvb = _memkv(mem_prompt, w_mem_k[l].astype(BF16), w_mem_v[l].astype(BF16))
        rg_o, rgc, rgh = _rg_prompt(yp, rg, PROMPT_RG_TILE)
        ssd_o, sc, sh = _ssd_prompt(yp, ssd)
        xa_o = _xattn(yp, xa["wq"], xa["wg"], mkb, mvb, PROMPT_XA_TILE, BF16, "xattn_prompt")
        flat = lambda v: v.reshape(bp * lp, v.shape[-1])
        yp = _merge(flat(rg_o), flat(ssd_o), flat(xa_o), flat(yp), merge["w"], merge["g"], merge["b"],
                    MERGE_TILE, "merge_prompt").reshape(bp, lp, d)
        outs["rgc_p"].append(rgc)
        outs["rgh_p"].append(rgh.reshape(bp, -1))
        outs["sc_p"].append(sc)
        outs["sh_p"].append(sh.reshape(bp, heads, SSD_HEAD_DIM, SSD_STATE))
        outs["mk_p"].append(mk)
        outs["mv_p"].append(mv)
        xs_pad = pad_rows(ys, 0, SAMPLE_PAD - ls)
        tail = SAMPLE_PAD - (CONV_W - 1)
        rg_o, rgc, rgh = _rg_sample(xs_pad, rg, pad_rows(state_rg_conv[l], tail, 0),
                                    state_rg_h[l][:, None, :], SAMPLE_RG_SEQS, ls)
        ssd_o, sc, sh = _ssd_sample(xs_pad, ssd, pad_rows(state_ssd_conv[l], tail, 0),
                                    state_ssd_h[l].reshape(bs, heads * SSD_HEAD_DIM, SSD_STATE),
                                    SAMPLE_SSD_SEQS, ls)
        xa_o = _xattn_sample(xs_pad, xa["wq"], xa["wg"], cache_mem_k[l], cache_mem_v[l], SAMPLE_XA_SEQS)
        flat = lambda v: v.reshape(bs * SAMPLE_PAD, v.shape[-1])
        ys_pad = _merge(flat(rg_o), flat(ssd_o), flat(xa_o), flat(xs_pad), merge["w"], merge["g"], merge["b"],
                        MERGE_TILE, "merge_sample").reshape(bs, SAMPLE_PAD, d)
        ys = ys_pad[:, :ls, :]
        outs["rgc_s"].append(rgc)
        outs["rgh_s"].append(rgh.reshape(bs, -1))
        outs["sc_s"].append(sc)
        outs["sh_s"].append(sh.reshape(bs, heads, SSD_HEAD_DIM, SSD_STATE))
    st = lambda k: jnp.stack(outs[k])
    return (yp, ys, st("rgc_p"), st("rgh_p"), st("sc_p"), st("sh_p"), st("mk_p"), st("mv_p"),
            st("rgc_s"), st("rgh_s"), st("sc_s"), st("sh_s"))
```

```python
import functools

import jax
import jax.numpy as jnp
from jax import lax
from jax.experimental import pallas as pl
from jax.experimental.pallas import tpu as pltpu

F32 = jnp.float32
BF16 = jnp.bfloat16

SUBLANES = 8
LANES = 128
MXU_WIDTH = 256
VMEM_LIMIT_BYTES = 56 * 1024 * 1024

RG_C = 8.0
CONV_W = 4
RG_BLOCKS = 8
SSD_HEAD_DIM = 64
SSD_GROUPS = 2
SSD_STATE = 128
SSD_CHUNK = 128
MEM_HEADS = 4
LN_EPS = 1e-5
RMS_EPS = 1e-5
DEPTH = 1
DEEPNORM_ALPHA = (2 * DEPTH) ** 0.25
SAMPLE_PAD = SUBLANES


def _cparams(*sem):
    return pltpu.CompilerParams(dimension_semantics=sem, vmem_limit_bytes=VMEM_LIMIT_BYTES)


def _mm(a, b):
    return jnp.dot(a.astype(BF16), b.astype(BF16), preferred_element_type=F32)


def _mm_nt(a, b):
    return lax.dot_general(a.astype(BF16), b.astype(BF16), (((1,), (1,)), ((), ())),
                           preferred_element_type=F32)


def _mm_tn(a, b):
    return lax.dot_general(a.astype(BF16), b.astype(BF16), (((0,), (0,)), ((), ())),
                           preferred_element_type=F32)


def _sigmoid(x):
    return 1.0 / (1.0 + jnp.exp(-x))


def _silu(x):
    return x * _sigmoid(x)


def _softplus(x):
    return jnp.maximum(x, 0.0) + jnp.log(1.0 + jnp.exp(-jnp.abs(x)))


def _causal_conv_tiles(x3, p3, w, b):
    row = lax.broadcasted_iota(jnp.int32, x3.shape, 1)
    y = x3 * w[CONV_W - 1:CONV_W][None]
    for s in range(1, CONV_W):
        shifted = jnp.where(row >= s, pltpu.roll(x3, s, axis=1), pltpu.roll(p3, s, axis=1))
        y = y + shifted * w[CONV_W - 1 - s:CONV_W - s][None]
    return y + b[None]


def _scan_in_tiles(a3, b3):
    row = lax.broadcasted_iota(jnp.int32, a3.shape, 1)
    s = 1
    while s < SUBLANES:
        keep = row >= s
        a_sh = jnp.where(keep, pltpu.roll(a3, s, axis=1), 1.0)
        b_sh = jnp.where(keep, pltpu.roll(b3, s, axis=1), 0.0)
        b3 = a3 * b_sh + b3
        a3 = a3 * a_sh
        s *= 2
    return a3, b3


def _rg_gates(u, wgate, ba, bi, lam):
    pre = _mm(u, wgate)
    r = _sigmoid(pre[:, :LANES] + ba)
    i = _sigmoid(pre[:, LANES:] + bi)
    log_a = (-RG_C) * r * _softplus(-lam)
    a = jnp.exp(log_a)
    mult = jnp.sqrt(-jnp.tanh(log_a) * (1.0 + a * a))
    return a, mult * (i * u)


def _time_strided_conv(x3, tail3, w, b):
    slabs = x3.shape[0]
    row = lax.broadcasted_iota(jnp.int32, tail3.shape, 1)
    wrapped = jnp.where(row >= 1, pltpu.roll(x3[slabs - (CONV_W - 1):], 1, axis=1), pltpu.roll(tail3, 1, axis=1))
    y = x3 * w[CONV_W - 1:CONV_W][None]
    for s in range(1, CONV_W):
        shifted = jnp.concatenate([wrapped[CONV_W - 1 - s:], x3[:slabs - s]], axis=0)
        y = y + shifted * w[CONV_W - 1 - s:CONV_W - s][None]
    return y + b[None]


def _rg_prompt_kernel(x_ref, perm_ref, permt_ref, wx_ref, wg_ref, cw_ref, cb_ref, wgate_ref, ba_ref,
                      bi_ref, lam_ref, out_ref, conv_ref, hlast_ref, tail_scr, h_scr, proj_scr, outp_scr, *,
                      tiles_per_seq):
    s = pl.program_id(0)

    @pl.when(s == 0)
    def _():
        proj_scr[...] = jnp.zeros_like(proj_scr)

    @pl.when(jnp.logical_or(s == 0, s % tiles_per_seq == 1 % tiles_per_seq))
    def _():
        tail_scr[...] = jnp.zeros_like(tail_scr)
        h_scr[...] = jnp.zeros_like(h_scr)

    rows = x_ref.shape[0]
    slabs = rows // SUBLANES
    width = wx_ref.shape[1]
    per_group = MXU_WIDTH // LANES
    row = lax.broadcasted_iota(jnp.int32, (SUBLANES, LANES), 0)
    xb = jnp.dot(perm_ref[...], x_ref[...].astype(BF16), preferred_element_type=F32).astype(BF16)

    for cg in range(width // MXU_WIDTH):
        gs = slice(cg * MXU_WIDTH, (cg + 1) * MXU_WIDTH)
        for kk in range(per_group):
            k = cg * per_group + kk
            ks = slice(k * LANES, (k + 1) * LANES)
            x3 = proj_scr[:, ks].reshape(slabs, SUBLANES, LANES)
            tail3 = tail_scr[:, :, ks]
            last3 = x3[slabs - (CONV_W - 1):]
            tail_scr[:, :, ks] = last3
            for i in range(CONV_W - 1):
                conv_ref[0, i:i + 1, ks] = last3[i, SUBLANES - 1:, :]
            u3 = _time_strided_conv(x3, tail3, cw_ref[:, ks], cb_ref[:, ks])
            a, b = _rg_gates(u3.reshape(rows, LANES), wgate_ref[k], ba_ref[:, ks], bi_ref[:, ks], lam_ref[:, ks])
            a3 = a.reshape(slabs, SUBLANES, LANES)
            b3 = b.reshape(slabs, SUBLANES, LANES)
            h_loc, a_cum = [b3[0]], [a3[0]]
            for j in range(1, slabs):
                h_loc.append(a3[j] * h_loc[j - 1] + b3[j])
                a_cum.append(a3[j] * a_cum[j - 1])
            a_run, h_run = _scan_in_tiles(a_cum[-1][None], h_loc[-1][None])
            h_prev = h_scr[:, ks]
            h_end = h_run[0] + a_run[0] * h_prev
            h_in = jnp.where(row >= 1, pltpu.roll(h_end, 1, axis=0), h_prev)
            h_scr[:, ks] = h_end[SUBLANES - 1:]
            hlast_ref[0, :, ks] = h_end[SUBLANES - 1:]
            h = jnp.concatenate([h_loc[j] + a_cum[j] * h_in for j in range(slabs)], axis=0)
            gate = proj_scr[:, width + k * LANES:width + (k + 1) * LANES]
            outp_scr[:, ks] = (h * _silu(gate)).astype(BF16)
        proj_scr[:, gs] = jnp.dot(xb, wx_ref[:, gs], preferred_element_type=F32)
        proj_scr[:, width + cg * MXU_WIDTH:width + (cg + 1) * MXU_WIDTH] = jnp.dot(
            xb, wg_ref[:, gs], preferred_element_type=F32)

    out_ref[...] = jnp.dot(permt_ref[...], outp_scr[...], preferred_element_type=F32).astype(out_ref.dtype)


def _rg_sample_kernel(x_ref, wx_ref, wg_ref, cw_ref, cb_ref, wgate_ref, ba_ref, bi_ref, lam_ref,
                      cstate_ref, h0_ref, out_ref, conv_ref, hlast_ref, *, valid):
    seqs = x_ref.shape[0]
    rows = seqs * SUBLANES
    xb = x_ref[...].reshape(rows, x_ref.shape[2]).astype(BF16)
    rgx = jnp.dot(xb, wx_ref[...], preferred_element_type=F32)
    gate = jnp.dot(xb, wg_ref[...], preferred_element_type=F32)
    x3 = rgx.reshape(seqs, SUBLANES, rgx.shape[1])
    conv_ref[...] = x3[:, valid - (CONV_W - 1):valid, :]
    row = lax.broadcasted_iota(jnp.int32, (seqs, SUBLANES, LANES), 1)
    for k in range(RG_BLOCKS):
        ks = slice(k * LANES, (k + 1) * LANES)
        u3 = _causal_conv_tiles(x3[:, :, ks], cstate_ref[:, :, ks], cw_ref[:, ks], cb_ref[:, ks])
        u = u3.reshape(rows, LANES)
        a, b = _rg_gates(u, wgate_ref[k], ba_ref[:, ks], bi_ref[:, ks], lam_ref[:, ks])
        a3 = a.reshape(seqs, SUBLANES, LANES)
        b3 = b.reshape(seqs, SUBLANES, LANES)
        b3 = b3 + jnp.where(row == 0, a3 * h0_ref[:, :, ks], 0.0)
        _, h3 = _scan_in_tiles(a3, b3)
        hlast_ref[:, :, ks] = h3[:, valid - 1:valid, :]
        g3 = gate[:, ks].reshape(seqs, SUBLANES, LANES)
        out_ref[:, :, ks] = (h3 * _silu(g3)).astype(out_ref.dtype)


def _cumsum_rows(x):
    rows = x.shape[0]
    row = lax.broadcasted_iota(jnp.int32, x.shape, 0)
    s = 1
    while s < rows:
        x = x + jnp.where(row >= s, pltpu.roll(x, s, axis=0), 0.0)
        s *= 2
    return x


def _group_rmsnorm(y, gain):
    width = y.shape[-1] // SSD_GROUPS
    parts = []
    for g in range(SSD_GROUPS):
        yg = y[..., g * width:(g + 1) * width]
        ms = jnp.sum(yg * yg, axis=-1, keepdims=True) * (1.0 / width)
        parts.append(yg * lax.rsqrt(ms + RMS_EPS))
    return jnp.concatenate(parts, axis=-1) * gain


def _time_strided_cumsum(x):
    slabs = x.shape[0] // SUBLANES
    x3 = x.reshape(slabs, SUBLANES, x.shape[1])
    acc = [x3[0]]
    for j in range(1, slabs):
        acc.append(acc[j - 1] + x3[j])
    _, run = _scan_in_tiles(jnp.ones_like(acc[-1])[None], acc[-1][None])
    row = lax.broadcasted_iota(jnp.int32, run[0].shape, 0)
    before = jnp.where(row >= 1, pltpu.roll(run[0], 1, axis=0), 0.0)
    return jnp.concatenate([a + before for a in acc], axis=0)


def _ssd_prompt_kernel(x_ref, perm_ref, permt_ref, wxbc_ref, wz_ref, wdt_ref, cw_ref, cb_ref, dtb_ref,
                       alog_ref, dexp_ref, ng_ref, out_ref, conv_ref, h_ref, tail_scr, proj_scr, xb_scr, *,
                       chunks_per_seq):
    s = pl.program_id(0)

    @pl.when(s == 0)
    def _():
        proj_scr[...] = jnp.zeros_like(proj_scr)
        xb_scr[...] = jnp.zeros_like(xb_scr)

    @pl.when(jnp.logical_or(s == 0, s % chunks_per_seq == 1 % chunks_per_seq))
    def _():
        tail_scr[...] = jnp.zeros_like(tail_scr)
        h_ref[...] = jnp.zeros_like(h_ref)

    q = x_ref.shape[0]
    slabs = q // SUBLANES
    d_ssd = wz_ref.shape[1]
    d_conv = wxbc_ref.shape[1]
    gn = SSD_GROUPS * SSD_STATE
    xb_new = jnp.dot(perm_ref[...], x_ref[...].astype(BF16), preferred_element_type=F32).astype(BF16)
    z = jnp.dot(xb_scr[...], wz_ref[...], preferred_element_type=F32)
    xb_scr[...] = xb_new

    x3 = proj_scr[:, :d_conv].reshape(slabs, SUBLANES, d_conv)
    dtr = proj_scr[:, d_conv:]
    tail3 = tail_scr[...]
    last3 = x3[slabs - (CONV_W - 1):]
    tail_scr[...] = last3
    for i in range(CONV_W - 1):
        conv_ref[0, i:i + 1, :] = last3[i, SUBLANES - 1:, :]
    xbc = _silu(_time_strided_conv(x3, tail3, cw_ref[...], cb_ref[...])).reshape(q, d_conv)
    sx = xbc[:, :d_ssd]
    bm = xbc[:, d_ssd:d_ssd + gn]
    cm = xbc[:, d_ssd + gn:]

    dt = _softplus(dtr + dtb_ref[...])
    da = dt * (-jnp.exp(alog_ref[...]))
    acum = _time_strided_cumsum(da)
    alast = acum[q - 1:q, :]
    wgt = dt * jnp.exp(alast - acum)
    tot = jnp.exp(alast)
    acum_t = acum.T
    dt_t = dt.T

    ii = lax.broadcasted_iota(jnp.int32, (q, q), 0)
    jj = lax.broadcasted_iota(jnp.int32, (q, q), 1)
    time_of = lambda r: (r % SUBLANES) * slabs + r // SUBLANES
    causal = time_of(ii) >= time_of(jj)
    lane = lax.broadcasted_iota(jnp.int32, (q, LANES), 1)
    lo = lane < SSD_HEAD_DIM
    srow = lax.broadcasted_iota(jnp.int32, (LANES, SSD_STATE), 0) < SSD_HEAD_DIM

    cb = [_mm_nt(cm[:, g * SSD_STATE:(g + 1) * SSD_STATE], bm[:, g * SSD_STATE:(g + 1) * SSD_STATE])
          for g in range(SSD_GROUPS)]
    heads = d_ssd // SSD_HEAD_DIM
    pairs = heads // 2
    conv_groups = d_conv // MXU_WIDTH
    assert conv_groups < pairs
    y_parts = []
    for pq in range(pairs):
        g = (2 * pq) // (heads // SSD_GROUPS)
        ps = slice(pq * LANES, (pq + 1) * LANES)
        xq = sx[:, ps]
        ms, es, ws, ts = [], [], [], []
        for h in (2 * pq, 2 * pq + 1):
            acol = jnp.broadcast_to(acum[:, h:h + 1], (q, q))
            arow = jnp.broadcast_to(acum_t[h:h + 1, :], (q, q))
            decay = jnp.exp(jnp.where(causal, acol - arow, -jnp.inf))
            ms.append((cb[g] * decay * jnp.broadcast_to(dt_t[h:h + 1, :], (q, q))).astype(BF16))
            es.append(jnp.exp(jnp.broadcast_to(acum[:, h:h + 1], (q, LANES))))
            ws.append(jnp.broadcast_to(wgt[:, h:h + 1], (q, LANES)))
            ts.append(jnp.broadcast_to(tot[:, h:h + 1], (LANES, SSD_STATE)))
        lhs = jnp.concatenate(ms, axis=1)
        rhs = jnp.concatenate([jnp.where(lo, xq, 0.0), jnp.where(lo, 0.0, xq)], axis=0)
        y_diag = _mm(lhs, rhs)
        hq = h_ref[0, ps, :]
        y_off = _mm_nt(cm[:, g * SSD_STATE:(g + 1) * SSD_STATE], hq) * jnp.where(lo, es[0], es[1])
        xw = xq * jnp.where(lo, ws[0], ws[1])
        h_ref[0, ps, :] = hq * jnp.where(srow, ts[0], ts[1]) + _mm_tn(xw, bm[:, g * SSD_STATE:(g + 1) * SSD_STATE])
        y_parts.append(y_diag + y_off + dexp_ref[:, ps] * xq)
        if pq < conv_groups:
            gs = slice(pq * MXU_WIDTH, (pq + 1) * MXU_WIDTH)
            proj_scr[:, gs] = jnp.dot(xb_new, wxbc_ref[:, gs], preferred_element_type=F32)
        elif pq == conv_groups:
            proj_scr[:, d_conv:] = jnp.dot(xb_new, wdt_ref[...], preferred_element_type=F32)
    y = jnp.concatenate(y_parts, axis=1) * _silu(z)
    y = _group_rmsnorm(y, ng_ref[...]).astype(BF16)
    out_ref[...] = jnp.dot(permt_ref[...], y, preferred_element_type=F32).astype(out_ref.dtype)


def _ssd_sample_kernel(x_ref, wxbc_ref, wz_ref, wdt_ref, cw_ref, cb_ref, dtb_ref, alog_ref, dexp_ref,
                       ng_ref, expand_ref, cstate_ref, h0_ref, out_ref, conv_ref, h_ref,
                       c_scr, b_scr, xw_scr, tot_scr, yoff_scr, *, valid):
    seqs = x_ref.shape[0]
    rows = seqs * SUBLANES
    d_ssd = wz_ref.shape[1]
    gn = SSD_GROUPS * SSD_STATE
    heads = d_ssd // SSD_HEAD_DIM
    hg = heads // SSD_GROUPS
    xb = x_ref[...].reshape(rows, x_ref.shape[2]).astype(BF16)
    xbc_raw = jnp.dot(xb, wxbc_ref[...], preferred_element_type=F32)
    z = jnp.dot(xb, wz_ref[...], preferred_element_type=F32)
    dtr = jnp.dot(xb, wdt_ref[...], preferred_element_type=F32)

    x3 = xbc_raw.reshape(seqs, SUBLANES, xbc_raw.shape[1])
    conv_ref[...] = x3[:, valid - (CONV_W - 1):valid, :]
    xbc3 = _silu(_causal_conv_tiles(x3, cstate_ref[...], cw_ref[...], cb_ref[...]))
    sx3 = xbc3[:, :, :d_ssd]
    b3 = xbc3[:, :, d_ssd:d_ssd + gn]
    c3 = xbc3[:, :, d_ssd + gn:]

    row = lax.broadcasted_iota(jnp.int32, (seqs, SUBLANES, LANES), 1)
    lane = lax.broadcasted_iota(jnp.int32, (seqs, SUBLANES, LANES), 2)
    dt3 = jnp.where(row < valid, _softplus(dtr + dtb_ref[...]).reshape(seqs, SUBLANES, LANES), 0.0)
    da3 = dt3 * (-jnp.exp(alog_ref[...]))[None]
    ones = jnp.ones_like(da3)
    _, acum3 = _scan_in_tiles(ones, da3)
    alast = acum3[:, SUBLANES - 1:, :]
    wgt3 = dt3 * jnp.exp(alast - acum3)
    tot_scr[...] = jnp.exp(alast)
    e3 = jnp.exp(acum3)

    def expand(v3):
        flat = jnp.dot(v3.reshape(rows, LANES), expand_ref[...], precision=lax.Precision.HIGHEST,
                       preferred_element_type=F32)
        return flat.reshape(seqs, SUBLANES, d_ssd)

    y_diag = jnp.zeros((seqs, SUBLANES, d_ssd), F32)
    for u in range(valid):
        prod = c3 * b3[:, u:u + 1, :]
        cbu = [jnp.sum(prod[:, :, g * SSD_STATE:(g + 1) * SSD_STATE], axis=-1, keepdims=True)
               for g in range(SSD_GROUPS)]
        cb_heads = jnp.where(lane < hg, cbu[0], cbu[1])
        coef = jnp.where(row >= u, cb_heads * jnp.exp(acum3 - acum3[:, u:u + 1, :]) * dt3[:, u:u + 1, :], 0.0)
        y_diag = y_diag + expand(coef) * sx3[:, u:u + 1, :]

    c_scr[...] = c3
    b_scr[...] = b3
    xw_scr[...] = sx3 * expand(wgt3)
    srow = lax.broadcasted_iota(jnp.int32, (2 * SSD_HEAD_DIM, SSD_STATE), 0) < SSD_HEAD_DIM

    def per_seq(s, carry):
        cs = c_scr[s]
        bs = b_scr[s]
        xws = xw_scr[s]
        tots = tot_scr[s]
        for g in range(SSD_GROUPS):
            gs = slice(g * SSD_STATE, (g + 1) * SSD_STATE)
            width = hg * SSD_HEAD_DIM
            cols = slice(g * width, (g + 1) * width)
            hin = h0_ref[s, cols, :]
            yoff_scr[s, :, cols] = _mm_nt(cs[:, gs], hin)
            upd = _mm_tn(xws[:, cols], bs[:, gs])
            for pq in range(hg // 2):
                h = g * hg + 2 * pq
                rs = slice(pq * LANES, (pq + 1) * LANES)
                t0 = jnp.broadcast_to(tots[0:1, h:h + 1], (LANES, SSD_STATE))
                t1 = jnp.broadcast_to(tots[0:1, h + 1:h + 2], (LANES, SSD_STATE))
                h_ref[s, g * width + pq * LANES:g * width + (pq + 1) * LANES, :] = (
                    hin[rs] * jnp.where(srow, t0, t1) + upd[rs])
        return carry

    lax.fori_loop(0, seqs, per_seq, 0)

    y = y_diag + yoff_scr[...] * expand(e3) + dexp_ref[...][None] * sx3
    y = y * _silu(z.reshape(seqs, SUBLANES, d_ssd))
    out_ref[...] = _group_rmsnorm(y, ng_ref[...][None]).astype(out_ref.dtype)


def _memkv_kernel(mem_ref, wk_ref, wv_ref, k_ref, v_ref, kb_ref, vb_ref):
    mb = mem_ref[0].astype(BF16)
    for w_ref, o_ref, ob_ref in ((wk_ref, k_ref, kb_ref), (wv_ref, v_ref, vb_ref)):
        proj = jnp.dot(mb, w_ref[...], preferred_element_type=F32)
        o_ref[0] = proj.reshape(o_ref.shape[1:])
        ob_ref[0] = proj.astype(BF16)


def _xattn_kernel(x_ref, wq_ref, wg_ref, k_ref, v_ref, out_ref):
    xb = x_ref[0].astype(BF16)
    q = jnp.dot(xb, wq_ref[...], preferred_element_type=F32)
    gate = jnp.dot(xb, wg_ref[...], preferred_element_type=F32)
    d_head = q.shape[1] // MEM_HEADS
    scale = d_head ** -0.5
    for h in range(MEM_HEADS):
        hs = slice(h * d_head, (h + 1) * d_head)
        s = _mm_nt(q[:, hs], k_ref[0, :, hs]) * scale
        m = jnp.max(s, axis=-1, keepdims=True)
        p = jnp.exp(s - m)
        l = jnp.sum(p, axis=-1, keepdims=True)
        o = _mm(p, v_ref[0, :, hs]) * (1.0 / l)
        out_ref[0, :, hs] = (o * _silu(gate[:, hs])).astype(out_ref.dtype)


def _xattn_sample_kernel(x_ref, wq_ref, wg_ref, k_ref, v_ref, out_ref):
    seqs, _, d = x_ref.shape
    n_mem, heads, d_head = k_ref.shape[1:]
    xb = x_ref[...].reshape(seqs * SUBLANES, d).astype(BF16)
    q = jnp.dot(xb, wq_ref[...], preferred_element_type=F32)
    gate = jnp.dot(xb, wg_ref[...], preferred_element_type=F32)
    scale = d_head ** -0.5
    shape = (heads * SUBLANES, n_mem * heads)
    same_head = (lax.broadcasted_iota(jnp.int32, shape, 0) // SUBLANES
                 == lax.broadcasted_iota(jnp.int32, shape, 1) % heads)
    for s in range(seqs):
        rs = slice(s * SUBLANES, (s + 1) * SUBLANES)
        qh = jnp.concatenate([q[rs, h * d_head:(h + 1) * d_head] for h in range(heads)], axis=0)
        sc = jnp.where(same_head, _mm_nt(qh, k_ref[s].reshape(n_mem * heads, d_head)) * scale, -jnp.inf)
        p = jnp.exp(sc - jnp.max(sc, axis=-1, keepdims=True))
        l = jnp.sum(p, axis=-1, keepdims=True)
        o = _mm(p, v_ref[s].reshape(n_mem * heads, d_head)) * (1.0 / l)
        o = jnp.concatenate([o[h * SUBLANES:(h + 1) * SUBLANES] for h in range(heads)], axis=1)
        out_ref[s] = (o * _silu(gate[rs])).astype(out_ref.dtype)


def _merge_kernel(rg_ref, ssd_ref, xa_ref, x_ref, w_ref, g_ref, b_ref, y_ref):
    d = rg_ref.shape[1]
    mix = (jnp.dot(rg_ref[...].astype(BF16), w_ref[0:d, :], preferred_element_type=F32)
           + jnp.dot(ssd_ref[...].astype(BF16), w_ref[d:2 * d, :], preferred_element_type=F32)
           + jnp.dot(xa_ref[...].astype(BF16), w_ref[2 * d:3 * d, :], preferred_element_type=F32))
    res = DEEPNORM_ALPHA * x_ref[...] + mix
    mu = jnp.mean(res, axis=-1, keepdims=True)
    cen = res - mu
    var = jnp.mean(cen * cen, axis=-1, keepdims=True)
    y_ref[...] = cen * lax.rsqrt(var + LN_EPS) * g_ref[...] + b_ref[...]


def _full(shape):
    return pl.BlockSpec(shape, lambda *_: (0,) * len(shape))


def _time_stride_perm(rows):
    p = jnp.arange(rows)
    t = (p % SUBLANES) * (rows // SUBLANES) + p // SUBLANES
    perm = (t[:, None] == jnp.arange(rows)[None, :]).astype(BF16)
    return perm, perm.T


def _rg_prompt(x, p, tile):
    b, l, d = x.shape
    c = p["wx"].shape[1]
    tiles_per_seq = l // tile
    n_tiles = b * tiles_per_seq
    perm, perm_t = _time_stride_perm(tile)
    done = lambda s: jnp.maximum(s - 1, 0)
    out, conv, hlast = pl.pallas_call(
        functools.partial(_rg_prompt_kernel, tiles_per_seq=tiles_per_seq),
        grid=(n_tiles + 1,),
        in_specs=[pl.BlockSpec((tile, d), lambda s: (jnp.minimum(s, n_tiles - 1), 0)),
                  _full(perm.shape), _full(perm.shape), _full(p["wx"].shape), _full(p["wg"].shape),
                  _full(p["cw"].shape), _full(p["cb"].shape), _full(p["wgate"].shape), _full(p["ba"].shape),
                  _full(p["bi"].shape), _full(p["lam"].shape)],
        out_specs=[pl.BlockSpec((tile, c), lambda s: (done(s), 0)),
                   pl.BlockSpec((1, CONV_W - 1, c), lambda s: (done(s) // tiles_per_seq, 0, 0)),
                   pl.BlockSpec((1, 1, c), lambda s: (done(s) // tiles_per_seq, 0, 0))],
        out_shape=[jax.ShapeDtypeStruct((b * l, c), BF16),
                   jax.ShapeDtypeStruct((b, CONV_W - 1, c), F32),
                   jax.ShapeDtypeStruct((b, 1, c), F32)],
        scratch_shapes=[pltpu.VMEM((CONV_W - 1, SUBLANES, c), F32), pltpu.VMEM((1, c), F32),
                        pltpu.VMEM((tile, 2 * c), F32), pltpu.VMEM((tile, c), BF16)],
        compiler_params=_cparams("arbitrary"),
        name="rg_prompt",
    )(x.reshape(b * l, d), perm, perm_t, p["wx"], p["wg"], p["cw"], p["cb"], p["wgate"], p["ba"], p["bi"],
      p["lam"])
    return out.reshape(b, l, c), conv, hlast


def _rg_sample(xpad, p, cstate, h0, seqs, valid):
    n, _, d = xpad.shape
    c = p["wx"].shape[1]
    blk = lambda w: pl.BlockSpec((seqs, w[0], w[1]), lambda i: (i, 0, 0))
    return pl.pallas_call(
        functools.partial(_rg_sample_kernel, valid=valid),
        grid=(n // seqs,),
        in_specs=[blk((SUBLANES, d)), _full(p["wx"].shape), _full(p["wg"].shape), _full(p["cw"].shape),
                  _full(p["cb"].shape), _full(p["wgate"].shape), _full(p["ba"].shape),
                  _full(p["bi"].shape), _full(p["lam"].shape), blk((SUBLANES, c)), blk((1, c))],
        out_specs=[blk((SUBLANES, c)), blk((CONV_W - 1, c)), blk((1, c))],
        out_shape=[jax.ShapeDtypeStruct((n, SUBLANES, c), F32),
                   jax.ShapeDtypeStruct((n, CONV_W - 1, c), F32),
                   jax.ShapeDtypeStruct((n, 1, c), F32)],
        compiler_params=_cparams("parallel"),
        name="rg_sample",
    )(xpad, p["wx"], p["wg"], p["cw"], p["cb"], p["wgate"], p["ba"], p["bi"], p["lam"], cstate, h0)


def _ssd_param_specs(p):
    return [_full(p[k].shape) for k in ("wxbc", "wz", "wdt", "cw", "cb", "dtb", "alog", "dexp", "ng")]


def _ssd_param_args(p):
    return [p[k] for k in ("wxbc", "wz", "wdt", "cw", "cb", "dtb", "alog", "dexp", "ng")]


def _ssd_prompt(x, p):
    b, l, d = x.shape
    cc = p["wxbc"].shape[1]
    c = p["wz"].shape[1]
    chunks_per_seq = l // SSD_CHUNK
    n_chunks = b * chunks_per_seq
    perm, perm_t = _time_stride_perm(SSD_CHUNK)
    done = lambda s: jnp.maximum(s - 1, 0)
    out, conv, hstate = pl.pallas_call(
        functools.partial(_ssd_prompt_kernel, chunks_per_seq=chunks_per_seq),
        grid=(n_chunks + 1,),
        in_specs=[pl.BlockSpec((SSD_CHUNK, d), lambda s: (jnp.minimum(s, n_chunks - 1), 0)),
                  _full(perm.shape), _full(perm.shape)] + _ssd_param_specs(p),
        out_specs=[pl.BlockSpec((SSD_CHUNK, c), lambda s: (done(s), 0)),
                   pl.BlockSpec((1, CONV_W - 1, cc), lambda s: (done(s) // chunks_per_seq, 0, 0)),
                   pl.BlockSpec((1, c, SSD_STATE), lambda s: (done(s) // chunks_per_seq, 0, 0))],
        out_shape=[jax.ShapeDtypeStruct((b * l, c), BF16),
                   jax.ShapeDtypeStruct((b, CONV_W - 1, cc), F32),
                   jax.ShapeDtypeStruct((b, c, SSD_STATE), F32)],
        scratch_shapes=[pltpu.VMEM((CONV_W - 1, SUBLANES, cc), F32),
                        pltpu.VMEM((SSD_CHUNK, cc + LANES), F32), pltpu.VMEM((SSD_CHUNK, d), BF16)],
        compiler_params=_cparams("arbitrary"),
        name="ssd_prompt",
    )(x.reshape(b * l, d), perm, perm_t, *_ssd_param_args(p))
    return out.reshape(b, l, c), conv, hstate


def _ssd_sample(xpad, p, cstate, h0, seqs, valid):
    n, _, d = xpad.shape
    cc = p["wxbc"].shape[1]
    c = p["wz"].shape[1]
    gn = SSD_GROUPS * SSD_STATE
    blk = lambda w: pl.BlockSpec((seqs, w[0], w[1]), lambda i: (i, 0, 0))
    return pl.pallas_call(
        functools.partial(_ssd_sample_kernel, valid=valid),
        grid=(n // seqs,),
        in_specs=[blk((SUBLANES, d))] + _ssd_param_specs(p)
        + [_full(p["expand"].shape), blk((SUBLANES, cc)), blk((c, SSD_STATE))],
        out_specs=[blk((SUBLANES, c)), blk((CONV_W - 1, cc)), blk((c, SSD_STATE))],
        out_shape=[jax.ShapeDtypeStruct((n, SUBLANES, c), F32),
                   jax.ShapeDtypeStruct((n, CONV_W - 1, cc), F32),
                   jax.ShapeDtypeStruct((n, c, SSD_STATE), F32)],
        scratch_shapes=[pltpu.VMEM((seqs, SUBLANES, gn), F32), pltpu.VMEM((seqs, SUBLANES, gn), F32),
                        pltpu.VMEM((seqs, SUBLANES, c), F32), pltpu.VMEM((seqs, 1, LANES), F32),
                        pltpu.VMEM((seqs, SUBLANES, c), F32)],
        compiler_params=_cparams("parallel"),
        name="ssd_sample",
    )(xpad, *_ssd_param_args(p), p["expand"], cstate, h0)


def _memkv(mem, wk, wv):
    b, m, d = mem.shape
    c = wk.shape[1]
    spec = pl.BlockSpec((1, m, c), lambda i: (i, 0, 0))
    spec4 = pl.BlockSpec((1, m, MEM_HEADS, c // MEM_HEADS), lambda i: (i, 0, 0, 0))
    return pl.pallas_call(
        _memkv_kernel,
        grid=(b,),
        in_specs=[pl.BlockSpec((1, m, d), lambda i: (i, 0, 0)), _full(wk.shape), _full(wv.shape)],
        out_specs=[spec4, spec4, spec, spec],
        out_shape=[jax.ShapeDtypeStruct((b, m, MEM_HEADS, c // MEM_HEADS), F32)] * 2
        + [jax.ShapeDtypeStruct((b, m, c), BF16)] * 2,
        compiler_params=_cparams("parallel"),
        name="mem_kv",
    )(mem, wk, wv)


def _xattn(x, wq, wg, k, v, tile, out_dtype, name):
    b, l, d = x.shape
    c = wq.shape[1]
    m = k.shape[1]
    kv_spec = pl.BlockSpec((1, m, c), lambda i, j: (i, 0, 0))
    return pl.pallas_call(
        _xattn_kernel,
        grid=(b, l // tile),
        in_specs=[pl.BlockSpec((1, tile, d), lambda i, j: (i, j, 0)), _full(wq.shape), _full(wg.shape),
                  kv_spec, kv_spec],
        out_specs=pl.BlockSpec((1, tile, c), lambda i, j: (i, j, 0)),
        out_shape=jax.ShapeDtypeStruct((b, l, c), out_dtype),
        compiler_params=_cparams("parallel", "parallel"),
        name=name,
    )(x, wq, wg, k, v)


def _xattn_sample(xpad, wq, wg, k, v, seqs):
    n, _, d = xpad.shape
    c = wq.shape[1]
    kv_spec = pl.BlockSpec((seqs,) + k.shape[1:], lambda i: (i, 0, 0, 0))
    row_spec = lambda w: pl.BlockSpec((seqs, SUBLANES, w), lambda i: (i, 0, 0))
    return pl.pallas_call(
        _xattn_sample_kernel,
        grid=(n // seqs,),
        in_specs=[row_spec(d), _full(wq.shape), _full(wg.shape), kv_spec, kv_spec],
        out_specs=row_spec(c),
        out_shape=jax.ShapeDtypeStruct((n, SUBLANES, c), F32),
        compiler_params=_cparams("parallel"),
        name="xattn_sample",
    )(xpad, wq, wg, k, v)


def _merge(rg, ssd, xa, x, w_out, ln_g, ln_b, tile, name):
    n, d = x.shape
    c = rg.shape[1]
    tile = min(tile, n)
    row = lambda w: pl.BlockSpec((tile, w), lambda i: (i, 0))
    return pl.pallas_call(
        _merge_kernel,
        grid=(n // tile,),
        in_specs=[row(c), row(c), row(c), row(d), _full(w_out.shape), _full(ln_g.shape), _full(ln_b.shape)],
        out_specs=row(d),
        out_shape=jax.ShapeDtypeStruct((n, d), F32),
        compiler_params=_cparams("parallel"),
        name=name,
    )(rg, ssd, xa, x, w_out, ln_g, ln_b)


def _layer_params(w_in, rg_conv_w, rg_conv_b, w_rg_a, b_rg_a, w_rg_i, b_rg_i, rg_lambda, ssd_conv_w,
                  ssd_conv_b, ssd_dt_bias, ssd_a_log, ssd_d, ssd_norm_g, w_out, ln_g, ln_b):
    d_rg = rg_conv_w.shape[1]
    d_conv = ssd_conv_w.shape[1]
    d_ssd = ssd_norm_g.shape[0]
    heads = ssd_d.shape[0]
    sizes = (d_rg, d_rg, d_conv, d_ssd, heads)
    offs = [0]
    for s in sizes:
        offs.append(offs[-1] + s)
    d_xa = (w_in.shape[1] - offs[-1]) // 2
    wb = w_in.astype(BF16)
    row = lambda v: v.reshape(1, -1).astype(F32)
    pad_lanes = lambda v: jnp.pad(v, ((0, 0), (0, LANES - v.shape[1])))
    rg = dict(wx=wb[:, offs[0]:offs[1]], wg=wb[:, offs[1]:offs[2]], cw=rg_conv_w, cb=row(rg_conv_b),
              wgate=jnp.concatenate([w_rg_a, w_rg_i], axis=2).astype(BF16),
              ba=row(b_rg_a), bi=row(b_rg_i), lam=row(rg_lambda))
    head_of_channel = jnp.arange(d_ssd) // SSD_HEAD_DIM
    ssd = dict(wxbc=wb[:, offs[2]:offs[3]], wz=wb[:, offs[3]:offs[4]], wdt=pad_lanes(wb[:, offs[4]:offs[5]]),
               cw=ssd_conv_w, cb=row(ssd_conv_b), dtb=pad_lanes(row(ssd_dt_bias)),
               alog=pad_lanes(row(ssd_a_log)), dexp=row(ssd_d[head_of_channel]), ng=row(ssd_norm_g),
               expand=(jnp.arange(LANES)[:, None] == head_of_channel[None, :]).astype(F32))
    xa = dict(wq=wb[:, offs[5]:offs[5] + d_xa], wg=wb[:, offs[5] + d_xa:offs[5] + 2 * d_xa])
    merge = dict(w=w_out.astype(BF16), g=row(ln_g), b=row(ln_b))
    return rg, ssd, xa, merge


PROMPT_RG_TILE = 256
PROMPT_XA_TILE = 256
MERGE_TILE = 512
SAMPLE_RG_SEQS = 16
SAMPLE_SSD_SEQS = 8
SAMPLE_XA_SEQS = 4


def kernel(x_prompt, x_sample, mem_prompt, state_rg_conv, state_rg_h, state_ssd_conv, state_ssd_h,
           cache_mem_k, cache_mem_v, w_in, rg_conv_w, rg_conv_b, w_rg_a, b_rg_a, w_rg_i, b_rg_i,
           rg_lambda, ssd_conv_w, ssd_conv_b, ssd_dt_bias, ssd_a_log, ssd_d, ssd_norm_g, w_mem_k,
           w_mem_v, w_out, ln_g, ln_b):
    assert w_in.shape[0] == DEPTH
    bp, lp, d = x_prompt.shape
    bs, ls, _ = x_sample.shape
    heads = ssd_d.shape[1]
    outs = {k: [] for k in ("rgc_p", "rgh_p", "sc_p", "sh_p", "mk_p", "mv_p", "rgc_s", "rgh_s", "sc_s", "sh_s")}
    yp, ys = x_prompt, x_sample
    pad_rows = lambda v, before, after: jnp.pad(v, ((0, 0), (before, after), (0, 0)))
    for l in range(DEPTH):
        rg, ssd, xa, merge = _layer_params(
            w_in[l], rg_conv_w[l], rg_conv_b[l], w_rg_a[l], b_rg_a[l], w_rg_i[l], b_rg_i[l], rg_lambda[l],
            ssd_conv_w[l], ssd_conv_b[l], ssd_dt_bias[l], ssd_a_log[l], ssd_d[l], ssd_norm_g[l],
            w_out[l], ln_g[l], ln_b[l])
        mk, mv, mkb, mvb = _memkv(mem_prompt, w_mem_k[l].astype(BF16), w_mem_v[l].astype(BF16))
        rg_o, rgc, rgh = _rg_prompt(yp, rg, PROMPT_RG_TILE)
        ssd_o, sc, sh = _ssd_prompt(yp, ssd)
        xa_o = _xattn(yp, xa["wq"], xa["wg"], mkb, mvb, PROMPT_XA_TILE, BF16, "xattn_prompt")
        flat = lambda v: v.reshape(bp * lp, v.shape[-1])
        yp = _merge(flat(rg_o), flat(ssd_o), flat(xa_o), flat(yp), merge["w"], merge["g"], merge["b"],
                    MERGE_TILE, "merge_prompt").reshape(bp, lp, d)
        outs["rgc_p"].append(rgc)
        outs["rgh_p"].append(rgh.reshape(bp, -1))
        outs["sc_p"].append(sc)
        outs["sh_p"].append(sh.reshape(bp, heads, SSD_HEAD_DIM, SSD_STATE))
        outs["mk_p"].append(mk)
        outs["mv_p"].append(mv)
        xs_pad = pad_rows(ys, 0, SAMPLE_PAD - ls)
        tail = SAMPLE_PAD - (CONV_W - 1)
        rg_o, rgc, rgh = _rg_sample(xs_pad, rg, pad_rows(state_rg_conv[l], tail, 0),
                                    state_rg_h[l][:, None, :], SAMPLE_RG_SEQS, ls)
        ssd_o, sc, sh = _ssd_sample(xs_pad, ssd, pad_rows(state_ssd_conv[l], tail, 0),
                                    state_ssd_h[l].reshape(bs, heads * SSD_HEAD_DIM, SSD_STATE),
                                    SAMPLE_SSD_SEQS, ls)
        xa_o = _xattn_sample(xs_pad, xa["wq"], xa["wg"], cache_mem_k[l], cache_mem_v[l], SAMPLE_XA_SEQS)
        flat = lambda v: v.reshape(bs * SAMPLE_PAD, v.shape[-1])
        ys_pad = _merge(flat(rg_o), flat(ssd_o), flat(xa_o), flat(xs_pad), merge["w"], merge["g"], merge["b"],
                        MERGE_TILE, "merge_sample").reshape(bs, SAMPLE_PAD, d)
        ys = ys_pad[:, :ls, :]
        outs["rgc_s"].append(rgc)
        outs["rgh_s"].append(rgh.reshape(bs, -1))
        outs["sc_s"].append(sc)
        outs["sh_s"].append(sh.reshape(bs, heads, SSD_HEAD_DIM, SSD_STATE))
    st = lambda k: jnp.stack(outs[k])
    return (yp, ys, st("rgc_p"), st("rgh_p"), st("sc_p"), st("sh_p"), st("mk_p"), st("mv_p"),
            st("rgc_s"), st("rgh_s"), st("sc_s"), st("sh_s"))
```

```python
import functools

import jax
import jax.numpy as jnp
import numpy as np
from jax import lax
from jax.experimental import pallas as pl
from jax.experimental.pallas import tpu as pltpu

F32 = jnp.float32
BF16 = jnp.bfloat16

SUBLANES = 8
LANES = 128
MXU_WIDTH = 256
VMEM_LIMIT_BYTES = 56 * 1024 * 1024

RG_C = 8.0
CONV_W = 4
RG_BLOCKS = 8
SSD_HEAD_DIM = 64
SSD_GROUPS = 2
SSD_STATE = 128
SSD_CHUNK = 128
MEM_HEADS = 4
LN_EPS = 1e-5
RMS_EPS = 1e-5
DEPTH = 1
DEEPNORM_ALPHA = (2 * DEPTH) ** 0.25
SAMPLE_PAD = SUBLANES


def _cparams(*sem):
    return pltpu.CompilerParams(dimension_semantics=sem, vmem_limit_bytes=VMEM_LIMIT_BYTES)


def _mm(a, b):
    return jnp.dot(a.astype(BF16), b.astype(BF16), preferred_element_type=F32)


def _mm_nt(a, b):
    return lax.dot_general(a.astype(BF16), b.astype(BF16), (((1,), (1,)), ((), ())),
                           preferred_element_type=F32)


def _mm_tn(a, b):
    return lax.dot_general(a.astype(BF16), b.astype(BF16), (((0,), (0,)), ((), ())),
                           preferred_element_type=F32)


def _sigmoid(x):
    return 1.0 / (1.0 + jnp.exp(-x))


def _silu(x):
    return x * _sigmoid(x)


def _softplus(x):
    return jnp.maximum(x, 0.0) + jnp.log(1.0 + jnp.exp(-jnp.abs(x)))


def _causal_conv_tiles(x3, p3, w, b):
    row = lax.broadcasted_iota(jnp.int32, x3.shape, 1)
    y = x3 * w[CONV_W - 1:CONV_W][None]
    for s in range(1, CONV_W):
        shifted = jnp.where(row >= s, pltpu.roll(x3, s, axis=1), pltpu.roll(p3, s, axis=1))
        y = y + shifted * w[CONV_W - 1 - s:CONV_W - s][None]
    return y + b[None]


def _scan_in_tiles(a3, b3):
    row = lax.broadcasted_iota(jnp.int32, a3.shape, 1)
    s = 1
    while s < SUBLANES:
        keep = row >= s
        a_sh = jnp.where(keep, pltpu.roll(a3, s, axis=1), 1.0)
        b_sh = jnp.where(keep, pltpu.roll(b3, s, axis=1), 0.0)
        b3 = a3 * b_sh + b3
        a3 = a3 * a_sh
        s *= 2
    return a3, b3


def _rg_gates(u, wgate, ba, bi, lam):
    pre = _mm(u, wgate)
    r = _sigmoid(pre[:, :LANES] + ba)
    i = _sigmoid(pre[:, LANES:] + bi)
    log_a = (-RG_C) * r * _softplus(-lam)
    a = jnp.exp(log_a)
    mult = jnp.sqrt(-jnp.tanh(log_a) * (1.0 + a * a))
    return a, mult * (i * u)


def _time_strided_conv(x3, tail3, w, b):
    slabs = x3.shape[0]
    row = lax.broadcasted_iota(jnp.int32, tail3.shape, 1)
    wrapped = jnp.where(row >= 1, pltpu.roll(x3[slabs - (CONV_W - 1):], 1, axis=1), pltpu.roll(tail3, 1, axis=1))
    y = x3 * w[CONV_W - 1:CONV_W][None]
    for s in range(1, CONV_W):
        shifted = jnp.concatenate([wrapped[CONV_W - 1 - s:], x3[:slabs - s]], axis=0)
        y = y + shifted * w[CONV_W - 1 - s:CONV_W - s][None]
    return y + b[None]


def _rg_prompt_kernel(x_ref, perm_ref, permt_ref, wx_ref, wg_ref, cw_ref, cb_ref, wgate_ref, ba_ref,
                      bi_ref, lam_ref, out_ref, conv_ref, hlast_ref, tail_scr, h_scr, proj_scr, outp_scr, *,
                      tiles_per_seq):
    s = pl.program_id(0)

    @pl.when(s == 0)
    def _():
        proj_scr[...] = jnp.zeros_like(proj_scr)

    @pl.when(jnp.logical_or(s == 0, s % tiles_per_seq == 1 % tiles_per_seq))
    def _():
        tail_scr[...] = jnp.zeros_like(tail_scr)
        h_scr[...] = jnp.zeros_like(h_scr)

    rows = x_ref.shape[0]
    slabs = rows // SUBLANES
    width = wx_ref.shape[1]
    per_group = MXU_WIDTH // LANES
    row = lax.broadcasted_iota(jnp.int32, (SUBLANES, LANES), 0)
    xb = jnp.dot(perm_ref[...], x_ref[...].astype(BF16), preferred_element_type=F32).astype(BF16)

    for cg in range(width // MXU_WIDTH):
        gs = slice(cg * MXU_WIDTH, (cg + 1) * MXU_WIDTH)
        for kk in range(per_group):
            k = cg * per_group + kk
            ks = slice(k * LANES, (k + 1) * LANES)
            x3 = proj_scr[:, ks].reshape(slabs, SUBLANES, LANES)
            tail3 = tail_scr[:, :, ks]
            last3 = x3[slabs - (CONV_W - 1):]
            tail_scr[:, :, ks] = last3
            for i in range(CONV_W - 1):
                conv_ref[0, i:i + 1, ks] = last3[i, SUBLANES - 1:, :]
            u3 = _time_strided_conv(x3, tail3, cw_ref[:, ks], cb_ref[:, ks])
            a, b = _rg_gates(u3.reshape(rows, LANES), wgate_ref[k], ba_ref[:, ks], bi_ref[:, ks], lam_ref[:, ks])
            a3 = a.reshape(slabs, SUBLANES, LANES)
            b3 = b.reshape(slabs, SUBLANES, LANES)
            h_loc, a_cum = [b3[0]], [a3[0]]
            for j in range(1, slabs):
                h_loc.append(a3[j] * h_loc[j - 1] + b3[j])
                a_cum.append(a3[j] * a_cum[j - 1])
            a_run, h_run = _scan_in_tiles(a_cum[-1][None], h_loc[-1][None])
            h_prev = h_scr[:, ks]
            h_end = h_run[0] + a_run[0] * h_prev
            h_in = jnp.where(row >= 1, pltpu.roll(h_end, 1, axis=0), h_prev)
            h_scr[:, ks] = h_end[SUBLANES - 1:]
            hlast_ref[0, :, ks] = h_end[SUBLANES - 1:]
            h = jnp.concatenate([h_loc[j] + a_cum[j] * h_in for j in range(slabs)], axis=0)
            gate = proj_scr[:, width + k * LANES:width + (k + 1) * LANES]
            outp_scr[:, ks] = (h * _silu(gate)).astype(BF16)
        proj_scr[:, gs] = jnp.dot(xb, wx_ref[:, gs], preferred_element_type=F32)
        proj_scr[:, width + cg * MXU_WIDTH:width + (cg + 1) * MXU_WIDTH] = jnp.dot(
            xb, wg_ref[:, gs], preferred_element_type=F32)

    out_ref[...] = jnp.dot(permt_ref[...], outp_scr[...], preferred_element_type=F32).astype(out_ref.dtype)


def _rg_sample_kernel(x_ref, wx_ref, wg_ref, cw_ref, cb_ref, wgate_ref, ba_ref, bi_ref, lam_ref,
                      cstate_ref, h0_ref, out_ref, conv_ref, hlast_ref, *, valid):
    seqs = x_ref.shape[0]
    rows = seqs * SUBLANES
    xb = x_ref[...].reshape(rows, x_ref.shape[2]).astype(BF16)
    rgx = jnp.dot(xb, wx_ref[...], preferred_element_type=F32)
    gate = jnp.dot(xb, wg_ref[...], preferred_element_type=F32)
    x3 = rgx.reshape(seqs, SUBLANES, rgx.shape[1])
    conv_ref[...] = x3[:, valid - (CONV_W - 1):valid, :]
    row = lax.broadcasted_iota(jnp.int32, (seqs, SUBLANES, LANES), 1)
    for k in range(RG_BLOCKS):
        ks = slice(k * LANES, (k + 1) * LANES)
        u3 = _causal_conv_tiles(x3[:, :, ks], cstate_ref[:, :, ks], cw_ref[:, ks], cb_ref[:, ks])
        u = u3.reshape(rows, LANES)
        a, b = _rg_gates(u, wgate_ref[k], ba_ref[:, ks], bi_ref[:, ks], lam_ref[:, ks])
        a3 = a.reshape(seqs, SUBLANES, LANES)
        b3 = b.reshape(seqs, SUBLANES, LANES)
        b3 = b3 + jnp.where(row == 0, a3 * h0_ref[:, :, ks], 0.0)
        _, h3 = _scan_in_tiles(a3, b3)
        hlast_ref[:, :, ks] = h3[:, valid - 1:valid, :]
        g3 = gate[:, ks].reshape(seqs, SUBLANES, LANES)
        out_ref[:, :, ks] = (h3 * _silu(g3)).astype(out_ref.dtype)


def _cumsum_rows(x):
    rows = x.shape[0]
    row = lax.broadcasted_iota(jnp.int32, x.shape, 0)
    s = 1
    while s < rows:
        x = x + jnp.where(row >= s, pltpu.roll(x, s, axis=0), 0.0)
        s *= 2
    return x


def _group_rmsnorm(y, gain):
    width = y.shape[-1] // SSD_GROUPS
    parts = []
    for g in range(SSD_GROUPS):
        yg = y[..., g * width:(g + 1) * width]
        ms = jnp.sum(yg * yg, axis=-1, keepdims=True) * (1.0 / width)
        parts.append(yg * lax.rsqrt(ms + RMS_EPS))
    return jnp.concatenate(parts, axis=-1) * gain


def _time_strided_cumsum(x):
    slabs = x.shape[0] // SUBLANES
    x3 = x.reshape(slabs, SUBLANES, x.shape[1])
    acc = [x3[0]]
    for j in range(1, slabs):
        acc.append(acc[j - 1] + x3[j])
    _, run = _scan_in_tiles(jnp.ones_like(acc[-1])[None], acc[-1][None])
    row = lax.broadcasted_iota(jnp.int32, run[0].shape, 0)
    before = jnp.where(row >= 1, pltpu.roll(run[0], 1, axis=0), 0.0)
    return jnp.concatenate([a + before for a in acc], axis=0)


def _ssd_prompt_kernel(x_ref, perm_ref, permt_ref, wxbc_ref, wz_ref, wdt_ref, cw_ref, cb_ref, dtb_ref,
                       alog_ref, dexp_ref, ng_ref, out_ref, conv_ref, h_ref, tail_scr, proj_scr, xb_scr, *,
                       chunks_per_seq):
    s = pl.program_id(0)

    @pl.when(s == 0)
    def _():
        proj_scr[...] = jnp.zeros_like(proj_scr)
        xb_scr[...] = jnp.zeros_like(xb_scr)

    @pl.when(jnp.logical_or(s == 0, s % chunks_per_seq == 1 % chunks_per_seq))
    def _():
        tail_scr[...] = jnp.zeros_like(tail_scr)
        h_ref[...] = jnp.zeros_like(h_ref)

    q = x_ref.shape[0]
    slabs = q // SUBLANES
    d_ssd = wz_ref.shape[1]
    d_conv = wxbc_ref.shape[1]
    gn = SSD_GROUPS * SSD_STATE
    xb_new = jnp.dot(perm_ref[...], x_ref[...].astype(BF16), preferred_element_type=F32).astype(BF16)
    z = jnp.dot(xb_scr[...], wz_ref[...], preferred_element_type=F32)
    xb_scr[...] = xb_new

    x3 = proj_scr[:, :d_conv].reshape(slabs, SUBLANES, d_conv)
    dtr = proj_scr[:, d_conv:]
    tail3 = tail_scr[...]
    last3 = x3[slabs - (CONV_W - 1):]
    tail_scr[...] = last3
    for i in range(CONV_W - 1):
        conv_ref[0, i:i + 1, :] = last3[i, SUBLANES - 1:, :]
    xbc = _silu(_time_strided_conv(x3, tail3, cw_ref[...], cb_ref[...])).reshape(q, d_conv)
    sx = xbc[:, :d_ssd]
    bm = xbc[:, d_ssd:d_ssd + gn]
    cm = xbc[:, d_ssd + gn:]

    dt = _softplus(dtr + dtb_ref[...])
    da = dt * (-jnp.exp(alog_ref[...]))
    acum = _time_strided_cumsum(da)
    alast = acum[q - 1:q, :]
    wgt = dt * jnp.exp(alast - acum)
    tot = jnp.exp(alast)
    acum_t = acum.T
    dt_t = dt.T

    ii = lax.broadcasted_iota(jnp.int32, (q, q), 0)
    jj = lax.broadcasted_iota(jnp.int32, (q, q), 1)
    time_of = lambda r: (r % SUBLANES) * slabs + r // SUBLANES
    causal = time_of(ii) >= time_of(jj)
    lane = lax.broadcasted_iota(jnp.int32, (q, LANES), 1)
    lo = lane < SSD_HEAD_DIM
    srow = lax.broadcasted_iota(jnp.int32, (LANES, SSD_STATE), 0) < SSD_HEAD_DIM

    cb = [_mm_nt(cm[:, g * SSD_STATE:(g + 1) * SSD_STATE], bm[:, g * SSD_STATE:(g + 1) * SSD_STATE])
          for g in range(SSD_GROUPS)]
    heads = d_ssd // SSD_HEAD_DIM
    pairs = heads // 2
    conv_groups = d_conv // MXU_WIDTH
    assert conv_groups < pairs
    y_parts = []
    for pq in range(pairs):
        g = (2 * pq) // (heads // SSD_GROUPS)
        ps = slice(pq * LANES, (pq + 1) * LANES)
        xq = sx[:, ps]
        ms, es, ws, ts = [], [], [], []
        for h in (2 * pq, 2 * pq + 1):
            acol = jnp.broadcast_to(acum[:, h:h + 1], (q, q))
            arow = jnp.broadcast_to(acum_t[h:h + 1, :], (q, q))
            decay = jnp.exp(jnp.where(causal, acol - arow, -jnp.inf))
            ms.append((cb[g] * decay * jnp.broadcast_to(dt_t[h:h + 1, :], (q, q))).astype(BF16))
            es.append(jnp.exp(jnp.broadcast_to(acum[:, h:h + 1], (q, LANES))))
            ws.append(jnp.broadcast_to(wgt[:, h:h + 1], (q, LANES)))
            ts.append(jnp.broadcast_to(tot[:, h:h + 1], (LANES, SSD_STATE)))
        lhs = jnp.concatenate(ms, axis=1)
        rhs = jnp.concatenate([jnp.where(lo, xq, 0.0), jnp.where(lo, 0.0, xq)], axis=0)
        y_diag = _mm(lhs, rhs)
        hq = h_ref[0, ps, :]
        y_off = _mm_nt(cm[:, g * SSD_STATE:(g + 1) * SSD_STATE], hq) * jnp.where(lo, es[0], es[1])
        xw = xq * jnp.where(lo, ws[0], ws[1])
        h_ref[0, ps, :] = hq * jnp.where(srow, ts[0], ts[1]) + _mm_tn(xw, bm[:, g * SSD_STATE:(g + 1) * SSD_STATE])
        y_parts.append(y_diag + y_off + dexp_ref[:, ps] * xq)
        if pq < conv_groups:
            gs = slice(pq * MXU_WIDTH, (pq + 1) * MXU_WIDTH)
            proj_scr[:, gs] = jnp.dot(xb_new, wxbc_ref[:, gs], preferred_element_type=F32)
        elif pq == conv_groups:
            proj_scr[:, d_conv:] = jnp.dot(xb_new, wdt_ref[...], preferred_element_type=F32)
    y = jnp.concatenate(y_parts, axis=1) * _silu(z)
    y = _group_rmsnorm(y, ng_ref[...]).astype(BF16)
    out_ref[...] = jnp.dot(permt_ref[...], y, preferred_element_type=F32).astype(out_ref.dtype)


def _ssd_sample_kernel(x_ref, wxbc_ref, wz_ref, wdt_ref, cw_ref, cb_ref, dtb_ref, alog_ref, dexp_ref,
                       ng_ref, expand_ref, cstate_ref, h0_ref, out_ref, conv_ref, h_ref,
                       c_scr, b_scr, xw_scr, tot_scr, yoff_scr, *, valid):
    seqs = x_ref.shape[0]
    rows = seqs * SUBLANES
    d_ssd = wz_ref.shape[1]
    gn = SSD_GROUPS * SSD_STATE
    heads = d_ssd // SSD_HEAD_DIM
    hg = heads // SSD_GROUPS
    xb = x_ref[...].reshape(rows, x_ref.shape[2]).astype(BF16)
    xbc_raw = jnp.dot(xb, wxbc_ref[...], preferred_element_type=F32)
    z = jnp.dot(xb, wz_ref[...], preferred_element_type=F32)
    dtr = jnp.dot(xb, wdt_ref[...], preferred_element_type=F32)

    x3 = xbc_raw.reshape(seqs, SUBLANES, xbc_raw.shape[1])
    conv_ref[...] = x3[:, valid - (CONV_W - 1):valid, :]
    xbc3 = _silu(_causal_conv_tiles(x3, cstate_ref[...], cw_ref[...], cb_ref[...]))
    sx3 = xbc3[:, :, :d_ssd]
    b3 = xbc3[:, :, d_ssd:d_ssd + gn]
    c3 = xbc3[:, :, d_ssd + gn:]

    row = lax.broadcasted_iota(jnp.int32, (seqs, SUBLANES, LANES), 1)
    lane = lax.broadcasted_iota(jnp.int32, (seqs, SUBLANES, LANES), 2)
    dt3 = jnp.where(row < valid, _softplus(dtr + dtb_ref[...]).reshape(seqs, SUBLANES, LANES), 0.0)
    da3 = dt3 * (-jnp.exp(alog_ref[...]))[None]
    ones = jnp.ones_like(da3)
    _, acum3 = _scan_in_tiles(ones, da3)
    alast = acum3[:, SUBLANES - 1:, :]
    wgt3 = dt3 * jnp.exp(alast - acum3)
    tot_scr[...] = jnp.exp(alast)
    e3 = jnp.exp(acum3)

    def expand(v3):
        flat = jnp.dot(v3.reshape(rows, LANES), expand_ref[...], precision=lax.Precision.HIGHEST,
                       preferred_element_type=F32)
        return flat.reshape(seqs, SUBLANES, d_ssd)

    y_diag = jnp.zeros((seqs, SUBLANES, d_ssd), F32)
    for u in range(valid):
        prod = c3 * b3[:, u:u + 1, :]
        cbu = [jnp.sum(prod[:, :, g * SSD_STATE:(g + 1) * SSD_STATE], axis=-1, keepdims=True)
               for g in range(SSD_GROUPS)]
        cb_heads = jnp.where(lane < hg, cbu[0], cbu[1])
        coef = jnp.where(row >= u, cb_heads * jnp.exp(acum3 - acum3[:, u:u + 1, :]) * dt3[:, u:u + 1, :], 0.0)
        y_diag = y_diag + expand(coef) * sx3[:, u:u + 1, :]

    c_scr[...] = c3
    b_scr[...] = b3
    xw_scr[...] = sx3 * expand(wgt3)
    srow = lax.broadcasted_iota(jnp.int32, (2 * SSD_HEAD_DIM, SSD_STATE), 0) < SSD_HEAD_DIM

    def per_seq(s, carry):
        cs = c_scr[s]
        bs = b_scr[s]
        xws = xw_scr[s]
        tots = tot_scr[s]
        for g in range(SSD_GROUPS):
            gs = slice(g * SSD_STATE, (g + 1) * SSD_STATE)
            width = hg * SSD_HEAD_DIM
            cols = slice(g * width, (g + 1) * width)
            hin = h0_ref[s, cols, :]
            yoff_scr[s, :, cols] = _mm_nt(cs[:, gs], hin)
            upd = _mm_tn(xws[:, cols], bs[:, gs])
            for pq in range(hg // 2):
                h = g * hg + 2 * pq
                rs = slice(pq * LANES, (pq + 1) * LANES)
                t0 = jnp.broadcast_to(tots[0:1, h:h + 1], (LANES, SSD_STATE))
                t1 = jnp.broadcast_to(tots[0:1, h + 1:h + 2], (LANES, SSD_STATE))
                h_ref[s, g * width + pq * LANES:g * width + (pq + 1) * LANES, :] = (
                    hin[rs] * jnp.where(srow, t0, t1) + upd[rs])
        return carry

    lax.fori_loop(0, seqs, per_seq, 0)

    y = y_diag + yoff_scr[...] * expand(e3) + dexp_ref[...][None] * sx3
    y = y * _silu(z.reshape(seqs, SUBLANES, d_ssd))
    out_ref[...] = _group_rmsnorm(y, ng_ref[...][None]).astype(out_ref.dtype)


def _memkv_kernel(mem_ref, wk_ref, wv_ref, k_ref, v_ref, kb_ref, vb_ref):
    mb = mem_ref[0].astype(BF16)
    for w_ref, o_ref, ob_ref in ((wk_ref, k_ref, kb_ref), (wv_ref, v_ref, vb_ref)):
        proj = jnp.dot(mb, w_ref[...], preferred_element_type=F32)
        o_ref[0] = proj.reshape(o_ref.shape[1:])
        ob_ref[0] = proj.astype(BF16)


def _prompt_attention(q, gate, k_ref, v_ref, out_ref):
    d_head = q.shape[1] // MEM_HEADS
    scale = d_head ** -0.5
    for h in range(MEM_HEADS):
        hs = slice(h * d_head, (h + 1) * d_head)
        s = _mm_nt(q[:, hs], k_ref[0, :, hs]) * scale
        m = jnp.max(s, axis=-1, keepdims=True)
        p = jnp.exp(s - m)
        l = jnp.sum(p, axis=-1, keepdims=True)
        o = _mm(p, v_ref[0, :, hs]) * (1.0 / l)
        out_ref[0, :, hs] = (o * _silu(gate[:, hs])).astype(out_ref.dtype)


def _sample_attention(q, gate, k_ref, v_ref, out_ref):
    seqs, n_mem, heads, d_head = k_ref.shape
    scale = d_head ** -0.5
    shape = (heads * SUBLANES, n_mem * heads)
    same_head = (lax.broadcasted_iota(jnp.int32, shape, 0) // SUBLANES
                 == lax.broadcasted_iota(jnp.int32, shape, 1) % heads)
    for s in range(seqs):
        rs = slice(s * SUBLANES, (s + 1) * SUBLANES)
        qh = jnp.concatenate([q[rs, h * d_head:(h + 1) * d_head] for h in range(heads)], axis=0)
        sc = jnp.where(same_head, _mm_nt(qh, k_ref[s].reshape(n_mem * heads, d_head)) * scale, -jnp.inf)
        p = jnp.exp(sc - jnp.max(sc, axis=-1, keepdims=True))
        l = jnp.sum(p, axis=-1, keepdims=True)
        o = _mm(p, v_ref[s].reshape(n_mem * heads, d_head)) * (1.0 / l)
        o = jnp.concatenate([o[h * SUBLANES:(h + 1) * SUBLANES] for h in range(heads)], axis=1)
        out_ref[s] = (o * _silu(gate[rs])).astype(out_ref.dtype)


def _xattn_kernel(x_ref, xs_ref, wq_ref, wg_ref, k_ref, v_ref, ks_ref, vs_ref, out_ref, outs_ref):
    rows = x_ref.shape[1]
    seqs, _, d = xs_ref.shape
    xb = jnp.concatenate([x_ref[0], xs_ref[...].reshape(seqs * SUBLANES, d)], axis=0).astype(BF16)
    q = jnp.dot(xb, wq_ref[...], preferred_element_type=F32)
    gate = jnp.dot(xb, wg_ref[...], preferred_element_type=F32)
    _prompt_attention(q[:rows], gate[:rows], k_ref, v_ref, out_ref)
    _sample_attention(q[rows:], gate[rows:], ks_ref, vs_ref, outs_ref)


def _merge_kernel(rg_ref, ssd_ref, xa_ref, x_ref, w_ref, g_ref, b_ref, y_ref):
    d = rg_ref.shape[1]
    mix = (jnp.dot(rg_ref[...].astype(BF16), w_ref[0:d, :], preferred_element_type=F32)
           + jnp.dot(ssd_ref[...].astype(BF16), w_ref[d:2 * d, :], preferred_element_type=F32)
           + jnp.dot(xa_ref[...].astype(BF16), w_ref[2 * d:3 * d, :], preferred_element_type=F32))
    res = DEEPNORM_ALPHA * x_ref[...] + mix
    mu = jnp.mean(res, axis=-1, keepdims=True)
    cen = res - mu
    var = jnp.mean(cen * cen, axis=-1, keepdims=True)
    y_ref[...] = cen * lax.rsqrt(var + LN_EPS) * g_ref[...] + b_ref[...]


def _full(shape):
    return pl.BlockSpec(shape, lambda *_: (0,) * len(shape))


def _time_stride_perm(rows):
    p = np.arange(rows)
    t = (p % SUBLANES) * (rows // SUBLANES) + p // SUBLANES
    perm = t[:, None] == np.arange(rows)[None, :]
    return jnp.asarray(perm, dtype=BF16), jnp.asarray(perm.T, dtype=BF16)


def _rg_prompt(x, p, tile):
    b, l, d = x.shape
    c = p["wx"].shape[1]
    tiles_per_seq = l // tile
    n_tiles = b * tiles_per_seq
    perm, perm_t = _time_stride_perm(tile)
    done = lambda s: jnp.maximum(s - 1, 0)
    out, conv, hlast = pl.pallas_call(
        functools.partial(_rg_prompt_kernel, tiles_per_seq=tiles_per_seq),
        grid=(n_tiles + 1,),
        in_specs=[pl.BlockSpec((tile, d), lambda s: (jnp.minimum(s, n_tiles - 1), 0)),
                  _full(perm.shape), _full(perm.shape), _full(p["wx"].shape), _full(p["wg"].shape),
                  _full(p["cw"].shape), _full(p["cb"].shape), _full(p["wgate"].shape), _full(p["ba"].shape),
                  _full(p["bi"].shape), _full(p["lam"].shape)],
        out_specs=[pl.BlockSpec((tile, c), lambda s: (done(s), 0)),
                   pl.BlockSpec((1, CONV_W - 1, c), lambda s: (done(s) // tiles_per_seq, 0, 0)),
                   pl.BlockSpec((1, 1, c), lambda s: (done(s) // tiles_per_seq, 0, 0))],
        out_shape=[jax.ShapeDtypeStruct((b * l, c), BF16),
                   jax.ShapeDtypeStruct((b, CONV_W - 1, c), F32),
                   jax.ShapeDtypeStruct((b, 1, c), F32)],
        scratch_shapes=[pltpu.VMEM((CONV_W - 1, SUBLANES, c), F32), pltpu.VMEM((1, c), F32),
                        pltpu.VMEM((tile, 2 * c), F32), pltpu.VMEM((tile, c), BF16)],
        compiler_params=_cparams("arbitrary"),
        name="rg_prompt",
    )(x.reshape(b * l, d), perm, perm_t, p["wx"], p["wg"], p["cw"], p["cb"], p["wgate"], p["ba"], p["bi"],
      p["lam"])
    return out.reshape(b, l, c), conv, hlast


def _rg_sample(xpad, p, cstate, h0, seqs, valid):
    n, _, d = xpad.shape
    c = p["wx"].shape[1]
    blk = lambda w: pl.BlockSpec((seqs, w[0], w[1]), lambda i: (i, 0, 0))
    return pl.pallas_call(
        functools.partial(_rg_sample_kernel, valid=valid),
        grid=(n // seqs,),
        in_specs=[blk((SUBLANES, d)), _full(p["wx"].shape), _full(p["wg"].shape), _full(p["cw"].shape),
                  _full(p["cb"].shape), _full(p["wgate"].shape), _full(p["ba"].shape),
                  _full(p["bi"].shape), _full(p["lam"].shape), blk((SUBLANES, c)), blk((1, c))],
        out_specs=[blk((SUBLANES, c)), blk((CONV_W - 1, c)), blk((1, c))],
        out_shape=[jax.ShapeDtypeStruct((n, SUBLANES, c), F32),
                   jax.ShapeDtypeStruct((n, CONV_W - 1, c), F32),
                   jax.ShapeDtypeStruct((n, 1, c), F32)],
        compiler_params=_cparams("parallel"),
        name="rg_sample",
    )(xpad, p["wx"], p["wg"], p["cw"], p["cb"], p["wgate"], p["ba"], p["bi"], p["lam"], cstate, h0)


def _ssd_param_specs(p):
    return [_full(p[k].shape) for k in ("wxbc", "wz", "wdt", "cw", "cb", "dtb", "alog", "dexp", "ng")]


def _ssd_param_args(p):
    return [p[k] for k in ("wxbc", "wz", "wdt", "cw", "cb", "dtb", "alog", "dexp", "ng")]


def _ssd_prompt(x, p):
    b, l, d = x.shape
    cc = p["wxbc"].shape[1]
    c = p["wz"].shape[1]
    chunks_per_seq = l // SSD_CHUNK
    n_chunks = b * chunks_per_seq
    perm, perm_t = _time_stride_perm(SSD_CHUNK)
    done = lambda s: jnp.maximum(s - 1, 0)
    out, conv, hstate = pl.pallas_call(
        functools.partial(_ssd_prompt_kernel, chunks_per_seq=chunks_per_seq),
        grid=(n_chunks + 1,),
        in_specs=[pl.BlockSpec((SSD_CHUNK, d), lambda s: (jnp.minimum(s, n_chunks - 1), 0)),
                  _full(perm.shape), _full(perm.shape)] + _ssd_param_specs(p),
        out_specs=[pl.BlockSpec((SSD_CHUNK, c), lambda s: (done(s), 0)),
                   pl.BlockSpec((1, CONV_W - 1, cc), lambda s: (done(s) // chunks_per_seq, 0, 0)),
                   pl.BlockSpec((1, c, SSD_STATE), lambda s: (done(s) // chunks_per_seq, 0, 0))],
        out_shape=[jax.ShapeDtypeStruct((b * l, c), BF16),
                   jax.ShapeDtypeStruct((b, CONV_W - 1, cc), F32),
                   jax.ShapeDtypeStruct((b, c, SSD_STATE), F32)],
        scratch_shapes=[pltpu.VMEM((CONV_W - 1, SUBLANES, cc), F32),
                        pltpu.VMEM((SSD_CHUNK, cc + LANES), F32), pltpu.VMEM((SSD_CHUNK, d), BF16)],
        compiler_params=_cparams("arbitrary"),
        name="ssd_prompt",
    )(x.reshape(b * l, d), perm, perm_t, *_ssd_param_args(p))
    return out.reshape(b, l, c), conv, hstate


def _ssd_sample(xpad, p, cstate, h0, seqs, valid):
    n, _, d = xpad.shape
    cc = p["wxbc"].shape[1]
    c = p["wz"].shape[1]
    gn = SSD_GROUPS * SSD_STATE
    blk = lambda w: pl.BlockSpec((seqs, w[0], w[1]), lambda i: (i, 0, 0))
    return pl.pallas_call(
        functools.partial(_ssd_sample_kernel, valid=valid),
        grid=(n // seqs,),
        in_specs=[blk((SUBLANES, d))] + _ssd_param_specs(p)
        + [_full(p["expand"].shape), blk((SUBLANES, cc)), blk((c, SSD_STATE))],
        out_specs=[blk((SUBLANES, c)), blk((CONV_W - 1, cc)), blk((c, SSD_STATE))],
        out_shape=[jax.ShapeDtypeStruct((n, SUBLANES, c), F32),
                   jax.ShapeDtypeStruct((n, CONV_W - 1, cc), F32),
                   jax.ShapeDtypeStruct((n, c, SSD_STATE), F32)],
        scratch_shapes=[pltpu.VMEM((seqs, SUBLANES, gn), F32), pltpu.VMEM((seqs, SUBLANES, gn), F32),
                        pltpu.VMEM((seqs, SUBLANES, c), F32), pltpu.VMEM((seqs, 1, LANES), F32),
                        pltpu.VMEM((seqs, SUBLANES, c), F32)],
        compiler_params=_cparams("parallel"),
        name="ssd_sample",
    )(xpad, *_ssd_param_args(p), p["expand"], cstate, h0)


def _memkv(mem, wk, wv):
    b, m, d = mem.shape
    c = wk.shape[1]
    spec = pl.BlockSpec((1, m, c), lambda i: (i, 0, 0))
    spec4 = pl.BlockSpec((1, m, MEM_HEADS, c // MEM_HEADS), lambda i: (i, 0, 0, 0))
    return pl.pallas_call(
        _memkv_kernel,
        grid=(b,),
        in_specs=[pl.BlockSpec((1, m, d), lambda i: (i, 0, 0)), _full(wk.shape), _full(wv.shape)],
        out_specs=[spec4, spec4, spec, spec],
        out_shape=[jax.ShapeDtypeStruct((b, m, MEM_HEADS, c // MEM_HEADS), F32)] * 2
        + [jax.ShapeDtypeStruct((b, m, c), BF16)] * 2,
        compiler_params=_cparams("parallel"),
        name="mem_kv",
    )(mem, wk, wv)


def _xattn(x, xs_pad, wq, wg, k, v, ks, vs, tile):
    b, l, d = x.shape
    n = xs_pad.shape[0]
    c = wq.shape[1]
    m = k.shape[1]
    tiles = l // tile
    seqs = n // (b * tiles)
    assert seqs * b * tiles == n
    kv_spec = pl.BlockSpec((1, m, c), lambda i, j: (i, 0, 0))
    skv_spec = pl.BlockSpec((seqs,) + ks.shape[1:], lambda i, j: (i * tiles + j, 0, 0, 0))
    srow_spec = lambda w: pl.BlockSpec((seqs, SUBLANES, w), lambda i, j: (i * tiles + j, 0, 0))
    return pl.pallas_call(
        _xattn_kernel,
        grid=(b, tiles),
        in_specs=[pl.BlockSpec((1, tile, d), lambda i, j: (i, j, 0)), srow_spec(d), _full(wq.shape),
                  _full(wg.shape), kv_spec, kv_spec, skv_spec, skv_spec],
        out_specs=[pl.BlockSpec((1, tile, c), lambda i, j: (i, j, 0)), srow_spec(c)],
        out_shape=[jax.ShapeDtypeStruct((b, l, c), BF16), jax.ShapeDtypeStruct((n, SUBLANES, c), F32)],
        compiler_params=_cparams("parallel", "parallel"),
        name="xattn",
    )(x, xs_pad, wq, wg, k, v, ks, vs)


def _merge(rg, ssd, xa, x, w_out, ln_g, ln_b, tile, name):
    n, d = x.shape
    c = rg.shape[1]
    tile = min(tile, n)
    row = lambda w: pl.BlockSpec((tile, w), lambda i: (i, 0))
    return pl.pallas_call(
        _merge_kernel,
        grid=(n // tile,),
        in_specs=[row(c), row(c), row(c), row(d), _full(w_out.shape), _full(ln_g.shape), _full(ln_b.shape)],
        out_specs=row(d),
        out_shape=jax.ShapeDtypeStruct((n, d), F32),
        compiler_params=_cparams("parallel"),
        name=name,
    )(rg, ssd, xa, x, w_out, ln_g, ln_b)


def _layer_params(w_in, rg_conv_w, rg_conv_b, w_rg_a, b_rg_a, w_rg_i, b_rg_i, rg_lambda, ssd_conv_w,
                  ssd_conv_b, ssd_dt_bias, ssd_a_log, ssd_d, ssd_norm_g, w_out, ln_g, ln_b):
    d_rg = rg_conv_w.shape[1]
    d_conv = ssd_conv_w.shape[1]
    d_ssd = ssd_norm_g.shape[0]
    heads = ssd_d.shape[0]
    sizes = (d_rg, d_rg, d_conv, d_ssd, heads)
    offs = [0]
    for s in sizes:
        offs.append(offs[-1] + s)
    d_xa = (w_in.shape[1] - offs[-1]) // 2
    wb = lambda lo, hi: w_in[:, lo:hi].astype(BF16)
    row = lambda v: v.reshape(1, -1).astype(F32)
    pad_lanes = lambda v: jnp.pad(v, ((0, 0), (0, LANES - v.shape[1])))
    rg = dict(wx=wb(offs[0], offs[1]), wg=wb(offs[1], offs[2]), cw=rg_conv_w, cb=row(rg_conv_b),
              wgate=jnp.concatenate([w_rg_a, w_rg_i], axis=2).astype(BF16),
              ba=row(b_rg_a), bi=row(b_rg_i), lam=row(rg_lambda))
    head_of_channel = np.arange(d_ssd) // SSD_HEAD_DIM
    ssd = dict(wxbc=wb(offs[2], offs[3]), wz=wb(offs[3], offs[4]), wdt=pad_lanes(wb(offs[4], offs[5])),
               cw=ssd_conv_w, cb=row(ssd_conv_b), dtb=pad_lanes(row(ssd_dt_bias)),
               alog=pad_lanes(row(ssd_a_log)), dexp=row(jnp.repeat(ssd_d, SSD_HEAD_DIM)), ng=row(ssd_norm_g),
               expand=jnp.asarray(np.arange(LANES)[:, None] == head_of_channel[None, :], dtype=F32))
    xa = dict(wq=wb(offs[5], offs[5] + d_xa), wg=wb(offs[5] + d_xa, offs[5] + 2 * d_xa))
    merge = dict(w=w_out.astype(BF16), g=row(ln_g), b=row(ln_b))
    return rg, ssd, xa, merge


PROMPT_RG_TILE = 256
PROMPT_XA_TILE = 256
MERGE_TILE = 512
SAMPLE_RG_SEQS = 16
SAMPLE_SSD_SEQS = 8


def kernel(x_prompt, x_sample, mem_prompt, state_rg_conv, state_rg_h, state_ssd_conv, state_ssd_h,
           cache_mem_k, cache_mem_v, w_in, rg_conv_w, rg_conv_b, w_rg_a, b_rg_a, w_rg_i, b_rg_i,
           rg_lambda, ssd_conv_w, ssd_conv_b, ssd_dt_bias, ssd_a_log, ssd_d, ssd_norm_g, w_mem_k,
           w_mem_v, w_out, ln_g, ln_b):
    assert w_in.shape[0] == DEPTH
    bp, lp, d = x_prompt.shape
    bs, ls, _ = x_sample.shape
    heads = ssd_d.shape[1]
    outs = {k: [] for k in ("rgc_p", "rgh_p", "sc_p", "sh_p", "mk_p", "mv_p", "rgc_s", "rgh_s", "sc_s", "sh_s")}
    yp, ys = x_prompt, x_sample
    pad_rows = lambda v, before, after: jnp.pad(v, ((0, 0), (before, after), (0, 0)))
    for l in range(DEPTH):
        rg, ssd, xa, merge = _layer_params(
            w_in[l], rg_conv_w[l], rg_conv_b[l], w_rg_a[l], b_rg_a[l], w_rg_i[l], b_rg_i[l], rg_lambda[l],
            ssd_conv_w[l], ssd_conv_b[l], ssd_dt_bias[l], ssd_a_log[l], ssd_d[l], ssd_norm_g[l],
            w_out[l], ln_g[l], ln_b[l])
        xs_pad = pad_rows(ys, 0, SAMPLE_PAD - ls)
        mk, mv, mkb, mvb = _memkv(mem_prompt, w_mem_k[l].astype(BF16), w_mem_v[l].astype(BF16))
        rg_o, rgc, rgh = _rg_prompt(yp, rg, PROMPT_RG_TILE)
        ssd_o, sc, sh = _ssd_prompt(yp, ssd)
        xa_o, xa_s = _xattn(yp, xs_pad, xa["wq"], xa["wg"], mkb, mvb, cache_mem_k[l], cache_mem_v[l],
                            PROMPT_XA_TILE)
        flat = lambda v: v.reshape(bp * lp, v.shape[-1])
        yp = _merge(flat(rg_o), flat(ssd_o), flat(xa_o), flat(yp), merge["w"], merge["g"], merge["b"],
                    MERGE_TILE, "merge_prompt").reshape(bp, lp, d)
        outs["rgc_p"].append(rgc)
        outs["rgh_p"].append(rgh.reshape(bp, -1))
        outs["sc_p"].append(sc)
        outs["sh_p"].append(sh.reshape(bp, heads, SSD_HEAD_DIM, SSD_STATE))
        outs["mk_p"].append(mk)
        outs["mv_p"].append(mv)
        tail = SAMPLE_PAD - (CONV_W - 1)
        rg_o, rgc, rgh = _rg_sample(xs_pad, rg, pad_rows(state_rg_conv[l], tail, 0),
                                    state_rg_h[l][:, None, :], SAMPLE_RG_SEQS, ls)
        ssd_o, sc, sh = _ssd_sample(xs_pad, ssd, pad_rows(state_ssd_conv[l], tail, 0),
                                    state_ssd_h[l].reshape(bs, heads * SSD_HEAD_DIM, SSD_STATE),
                                    SAMPLE_SSD_SEQS, ls)
        flat = lambda v: v.reshape(bs * SAMPLE_PAD, v.shape[-1])
        ys_pad = _merge(flat(rg_o), flat(ssd_o), flat(xa_s), flat(xs_pad), merge["w"], merge["g"], merge["b"],
                        MERGE_TILE, "merge_sample").reshape(bs, SAMPLE_PAD, d)
        ys = ys_pad[:, :ls, :]
        outs["rgc_s"].append(rgc)
        outs["rgh_s"].append(rgh.reshape(bs, -1))
        outs["sc_s"].append(sc)
        outs["sh_s"].append(sh.reshape(bs, heads, SSD_HEAD_DIM, SSD_STATE))
    st = lambda k: jnp.stack(outs[k])
    return (yp, ys, st("rgc_p"), st("rgh_p"), st("sc_p"), st("sh_p"), st("mk_p"), st("mv_p"),
            st("rgc_s"), st("rgh_s"), st("sc_s"), st("sh_s"))
```

```python
import functools

import jax
import jax.numpy as jnp
import numpy as np
from jax import lax
from jax.experimental import pallas as pl
from jax.experimental.pallas import tpu as pltpu

F32 = jnp.float32
BF16 = jnp.bfloat16

SUBLANES = 8
LANES = 128
MXU_WIDTH = 256
VMEM_LIMIT_BYTES = 56 * 1024 * 1024

RG_C = 8.0
CONV_W = 4
RG_BLOCKS = 8
SSD_HEAD_DIM = 64
SSD_GROUPS = 2
SSD_STATE = 128
SSD_CHUNK = 128
MEM_HEADS = 4
LN_EPS = 1e-5
RMS_EPS = 1e-5
DEPTH = 1
DEEPNORM_ALPHA = (2 * DEPTH) ** 0.25
SAMPLE_PAD = SUBLANES


def _cparams(*sem):
    return pltpu.CompilerParams(dimension_semantics=sem, vmem_limit_bytes=VMEM_LIMIT_BYTES)


def _mm(a, b):
    return jnp.dot(a.astype(BF16), b.astype(BF16), preferred_element_type=F32)


def _mm_nt(a, b):
    return lax.dot_general(a.astype(BF16), b.astype(BF16), (((1,), (1,)), ((), ())),
                           preferred_element_type=F32)


def _mm_tn(a, b):
    return lax.dot_general(a.astype(BF16), b.astype(BF16), (((0,), (0,)), ((), ())),
                           preferred_element_type=F32)


def _sigmoid(x):
    return 1.0 / (1.0 + jnp.exp(-x))


def _silu(x):
    return x * _sigmoid(x)


def _softplus(x):
    return jnp.maximum(x, 0.0) + jnp.log(1.0 + jnp.exp(-jnp.abs(x)))


def _causal_conv_tiles(x3, p3, w, b):
    row = lax.broadcasted_iota(jnp.int32, x3.shape, 1)
    y = x3 * w[CONV_W - 1:CONV_W][None]
    for s in range(1, CONV_W):
        shifted = jnp.where(row >= s, pltpu.roll(x3, s, axis=1), pltpu.roll(p3, s, axis=1))
        y = y + shifted * w[CONV_W - 1 - s:CONV_W - s][None]
    return y + b[None]


def _scan_in_tiles(a3, b3):
    row = lax.broadcasted_iota(jnp.int32, a3.shape, 1)
    s = 1
    while s < SUBLANES:
        keep = row >= s
        a_sh = jnp.where(keep, pltpu.roll(a3, s, axis=1), 1.0)
        b_sh = jnp.where(keep, pltpu.roll(b3, s, axis=1), 0.0)
        b3 = a3 * b_sh + b3
        a3 = a3 * a_sh
        s *= 2
    return a3, b3


def _rg_gates(u, wgate, ba, bi, lam):
    pre = _mm(u, wgate)
    r = _sigmoid(pre[:, :LANES] + ba)
    i = _sigmoid(pre[:, LANES:] + bi)
    neg_log_a = r * (RG_C * _softplus(-lam))
    a = jnp.exp(-neg_log_a)
    v = jnp.tanh(neg_log_a) * (1.0 + a * a)
    mult = jnp.where(v > 0.0, v * lax.rsqrt(v), 0.0)
    return a, mult * (i * u)


def _time_strided_conv(x3, tail3, w, b):
    slabs = x3.shape[0]
    row = lax.broadcasted_iota(jnp.int32, tail3.shape, 1)
    wrapped = jnp.where(row >= 1, pltpu.roll(x3[slabs - (CONV_W - 1):], 1, axis=1), pltpu.roll(tail3, 1, axis=1))
    y = x3 * w[CONV_W - 1:CONV_W][None]
    for s in range(1, CONV_W):
        shifted = jnp.concatenate([wrapped[CONV_W - 1 - s:], x3[:slabs - s]], axis=0)
        y = y + shifted * w[CONV_W - 1 - s:CONV_W - s][None]
    return y + b[None]


def _rg_prompt_kernel(x_ref, perm_ref, permt_ref, wx_ref, wg_ref, cw_ref, cb_ref, wgate_ref, ba_ref,
                      bi_ref, lam_ref, out_ref, conv_ref, hlast_ref, tail_scr, h_scr, proj_scr, outp_scr, *,
                      tiles_per_seq):
    s = pl.program_id(0)

    @pl.when(s == 0)
    def _():
        proj_scr[...] = jnp.zeros_like(proj_scr)

    @pl.when(jnp.logical_or(s == 0, s % tiles_per_seq == 1 % tiles_per_seq))
    def _():
        tail_scr[...] = jnp.zeros_like(tail_scr)
        h_scr[...] = jnp.zeros_like(h_scr)

    rows = x_ref.shape[0]
    slabs = rows // SUBLANES
    width = wx_ref.shape[1]
    per_group = MXU_WIDTH // LANES
    row = lax.broadcasted_iota(jnp.int32, (SUBLANES, LANES), 0)
    xb = jnp.dot(perm_ref[...], x_ref[...].astype(BF16), preferred_element_type=F32).astype(BF16)

    for cg in range(width // MXU_WIDTH):
        gs = slice(cg * MXU_WIDTH, (cg + 1) * MXU_WIDTH)
        for kk in range(per_group):
            k = cg * per_group + kk
            ks = slice(k * LANES, (k + 1) * LANES)
            x3 = proj_scr[:, ks].reshape(slabs, SUBLANES, LANES)
            tail3 = tail_scr[:, :, ks]
            last3 = x3[slabs - (CONV_W - 1):]
            tail_scr[:, :, ks] = last3
            for i in range(CONV_W - 1):
                conv_ref[0, i:i + 1, ks] = last3[i, SUBLANES - 1:, :]
            u3 = _time_strided_conv(x3, tail3, cw_ref[:, ks], cb_ref[:, ks])
            a, b = _rg_gates(u3.reshape(rows, LANES), wgate_ref[k], ba_ref[:, ks], bi_ref[:, ks], lam_ref[:, ks])
            a3 = a.reshape(slabs, SUBLANES, LANES)
            b3 = b.reshape(slabs, SUBLANES, LANES)
            h_loc, a_cum = [b3[0]], [a3[0]]
            for j in range(1, slabs):
                h_loc.append(a3[j] * h_loc[j - 1] + b3[j])
                a_cum.append(a3[j] * a_cum[j - 1])
            a_run, h_run = _scan_in_tiles(a_cum[-1][None], h_loc[-1][None])
            h_prev = h_scr[:, ks]
            h_end = h_run[0] + a_run[0] * h_prev
            h_in = jnp.where(row >= 1, pltpu.roll(h_end, 1, axis=0), h_prev)
            h_scr[:, ks] = h_end[SUBLANES - 1:]
            hlast_ref[0, :, ks] = h_end[SUBLANES - 1:]
            h = jnp.concatenate([h_loc[j] + a_cum[j] * h_in for j in range(slabs)], axis=0)
            gate = proj_scr[:, width + k * LANES:width + (k + 1) * LANES]
            outp_scr[:, ks] = (h * _silu(gate)).astype(BF16)
        proj_scr[:, gs] = jnp.dot(xb, wx_ref[:, gs], preferred_element_type=F32)
        proj_scr[:, width + cg * MXU_WIDTH:width + (cg + 1) * MXU_WIDTH] = jnp.dot(
            xb, wg_ref[:, gs], preferred_element_type=F32)

    out_ref[...] = jnp.dot(permt_ref[...], outp_scr[...], preferred_element_type=F32).astype(out_ref.dtype)


def _rg_sample_kernel(x_ref, wx_ref, wg_ref, cw_ref, cb_ref, wgate_ref, ba_ref, bi_ref, lam_ref,
                      cstate_ref, h0_ref, out_ref, conv_ref, hlast_ref, *, valid):
    seqs = x_ref.shape[0]
    rows = seqs * SUBLANES
    xb = x_ref[...].reshape(rows, x_ref.shape[2]).astype(BF16)
    rgx = jnp.dot(xb, wx_ref[...], preferred_element_type=F32)
    gate = jnp.dot(xb, wg_ref[...], preferred_element_type=F32)
    x3 = rgx.reshape(seqs, SUBLANES, rgx.shape[1])
    conv_ref[...] = x3[:, valid - (CONV_W - 1):valid, :]
    row = lax.broadcasted_iota(jnp.int32, (seqs, SUBLANES, LANES), 1)
    for k in range(RG_BLOCKS):
        ks = slice(k * LANES, (k + 1) * LANES)
        u3 = _causal_conv_tiles(x3[:, :, ks], cstate_ref[:, :, ks], cw_ref[:, ks], cb_ref[:, ks])
        u = u3.reshape(rows, LANES)
        a, b = _rg_gates(u, wgate_ref[k], ba_ref[:, ks], bi_ref[:, ks], lam_ref[:, ks])
        a3 = a.reshape(seqs, SUBLANES, LANES)
        b3 = b.reshape(seqs, SUBLANES, LANES)
        b3 = b3 + jnp.where(row == 0, a3 * h0_ref[:, :, ks], 0.0)
        _, h3 = _scan_in_tiles(a3, b3)
        hlast_ref[:, :, ks] = h3[:, valid - 1:valid, :]
        g3 = gate[:, ks].reshape(seqs, SUBLANES, LANES)
        out_ref[:, :, ks] = (h3 * _silu(g3)).astype(out_ref.dtype)


def _cumsum_rows(x):
    rows = x.shape[0]
    row = lax.broadcasted_iota(jnp.int32, x.shape, 0)
    s = 1
    while s < rows:
        x = x + jnp.where(row >= s, pltpu.roll(x, s, axis=0), 0.0)
        s *= 2
    return x


def _group_rmsnorm(y, gain):
    width = y.shape[-1] // SSD_GROUPS
    parts = []
    for g in range(SSD_GROUPS):
        yg = y[..., g * width:(g + 1) * width]
        ms = jnp.sum(yg * yg, axis=-1, keepdims=True) * (1.0 / width)
        parts.append(yg * lax.rsqrt(ms + RMS_EPS))
    return jnp.concatenate(parts, axis=-1) * gain


def _time_strided_cumsum(x):
    slabs = x.shape[0] // SUBLANES
    x3 = x.reshape(slabs, SUBLANES, x.shape[1])
    acc = [x3[0]]
    for j in range(1, slabs):
        acc.append(acc[j - 1] + x3[j])
    _, run = _scan_in_tiles(jnp.ones_like(acc[-1])[None], acc[-1][None])
    row = lax.broadcasted_iota(jnp.int32, run[0].shape, 0)
    before = jnp.where(row >= 1, pltpu.roll(run[0], 1, axis=0), 0.0)
    return jnp.concatenate([a + before for a in acc], axis=0)


def _ssd_prompt_kernel(x_ref, perm_ref, permt_ref, wxbc_ref, wz_ref, wdt_ref, cw_ref, cb_ref, dtb_ref,
                       alog_ref, dexp_ref, ng_ref, out_ref, conv_ref, h_ref, tail_scr, proj_scr, xb_scr, *,
                       chunks_per_seq):
    s = pl.program_id(0)

    @pl.when(s == 0)
    def _():
        proj_scr[...] = jnp.zeros_like(proj_scr)
        xb_scr[...] = jnp.zeros_like(xb_scr)

    @pl.when(jnp.logical_or(s == 0, s % chunks_per_seq == 1 % chunks_per_seq))
    def _():
        tail_scr[...] = jnp.zeros_like(tail_scr)
        h_ref[...] = jnp.zeros_like(h_ref)

    q = x_ref.shape[0]
    slabs = q // SUBLANES
    d_ssd = wz_ref.shape[1]
    d_conv = wxbc_ref.shape[1]
    gn = SSD_GROUPS * SSD_STATE
    xb_new = jnp.dot(perm_ref[...], x_ref[...].astype(BF16), preferred_element_type=F32).astype(BF16)
    z = jnp.dot(xb_scr[...], wz_ref[...], preferred_element_type=F32)
    xb_scr[...] = xb_new

    x3 = proj_scr[:, :d_conv].reshape(slabs, SUBLANES, d_conv)
    dtr = proj_scr[:, d_conv:]
    tail3 = tail_scr[...]
    last3 = x3[slabs - (CONV_W - 1):]
    tail_scr[...] = last3
    for i in range(CONV_W - 1):
        conv_ref[0, i:i + 1, :] = last3[i, SUBLANES - 1:, :]
    xbc = _silu(_time_strided_conv(x3, tail3, cw_ref[...], cb_ref[...])).reshape(q, d_conv)
    sx = xbc[:, :d_ssd]
    bm = xbc[:, d_ssd:d_ssd + gn]
    cm = xbc[:, d_ssd + gn:]

    dt = _softplus(dtr + dtb_ref[...])
    da = dt * (-jnp.exp(alog_ref[...]))
    acum = _time_strided_cumsum(da)
    alast = acum[q - 1:q, :]
    wgt = dt * jnp.exp(alast - acum)
    tot = jnp.exp(alast)
    acum_t = acum.T
    dt_t = dt.T

    ii = lax.broadcasted_iota(jnp.int32, (q, q), 0)
    jj = lax.broadcasted_iota(jnp.int32, (q, q), 1)
    time_of = lambda r: (r % SUBLANES) * slabs + r // SUBLANES
    causal = time_of(ii) >= time_of(jj)
    lane = lax.broadcasted_iota(jnp.int32, (q, LANES), 1)
    lo = lane < SSD_HEAD_DIM
    srow = lax.broadcasted_iota(jnp.int32, (LANES, SSD_STATE), 0) < SSD_HEAD_DIM

    cb = [_mm_nt(cm[:, g * SSD_STATE:(g + 1) * SSD_STATE], bm[:, g * SSD_STATE:(g + 1) * SSD_STATE])
          for g in range(SSD_GROUPS)]
    heads = d_ssd // SSD_HEAD_DIM
    pairs = heads // 2
    conv_groups = d_conv // MXU_WIDTH
    assert conv_groups < pairs
    y_parts = []
    for pq in range(pairs):
        g = (2 * pq) // (heads // SSD_GROUPS)
        ps = slice(pq * LANES, (pq + 1) * LANES)
        xq = sx[:, ps]
        ms, es, ws, ts = [], [], [], []
        for h in (2 * pq, 2 * pq + 1):
            acol = jnp.broadcast_to(acum[:, h:h + 1], (q, q))
            arow = jnp.broadcast_to(acum_t[h:h + 1, :], (q, q))
            decay = jnp.exp(jnp.where(causal, acol - arow, -jnp.inf))
            ms.append((cb[g] * decay * jnp.broadcast_to(dt_t[h:h + 1, :], (q, q))).astype(BF16))
            es.append(jnp.exp(jnp.broadcast_to(acum[:, h:h + 1], (q, LANES))))
            ws.append(jnp.broadcast_to(wgt[:, h:h + 1], (q, LANES)))
            ts.append(jnp.broadcast_to(tot[:, h:h + 1], (LANES, SSD_STATE)))
        lhs = jnp.concatenate(ms, axis=1)
        rhs = jnp.concatenate([jnp.where(lo, xq, 0.0), jnp.where(lo, 0.0, xq)], axis=0)
        y_diag = _mm(lhs, rhs)
        hq = h_ref[0, ps, :]
        y_off = _mm_nt(cm[:, g * SSD_STATE:(g + 1) * SSD_STATE], hq) * jnp.where(lo, es[0], es[1])
        xw = xq * jnp.where(lo, ws[0], ws[1])
        h_ref[0, ps, :] = hq * jnp.where(srow, ts[0], ts[1]) + _mm_tn(xw, bm[:, g * SSD_STATE:(g + 1) * SSD_STATE])
        y_parts.append(y_diag + y_off + dexp_ref[:, ps] * xq)
        if pq < conv_groups:
            gs = slice(pq * MXU_WIDTH, (pq + 1) * MXU_WIDTH)
            proj_scr[:, gs] = jnp.dot(xb_new, wxbc_ref[:, gs], preferred_element_type=F32)
        elif pq == conv_groups:
            proj_scr[:, d_conv:] = jnp.dot(xb_new, wdt_ref[...], preferred_element_type=F32)
    y = jnp.concatenate(y_parts, axis=1) * _silu(z)
    y = _group_rmsnorm(y, ng_ref[...]).astype(BF16)
    out_ref[...] = jnp.dot(permt_ref[...], y, preferred_element_type=F32).astype(out_ref.dtype)


def _ssd_sample_kernel(x_ref, wxbc_ref, wz_ref, wdt_ref, cw_ref, cb_ref, dtb_ref, alog_ref, dexp_ref,
                       ng_ref, expand_ref, cstate_ref, h0_ref, out_ref, conv_ref, h_ref,
                       c_scr, b_scr, xw_scr, tot_scr, yoff_scr, *, valid):
    seqs = x_ref.shape[0]
    rows = seqs * SUBLANES
    d_ssd = wz_ref.shape[1]
    gn = SSD_GROUPS * SSD_STATE
    heads = d_ssd // SSD_HEAD_DIM
    hg = heads // SSD_GROUPS
    xb = x_ref[...].reshape(rows, x_ref.shape[2]).astype(BF16)
    xbc_raw = jnp.dot(xb, wxbc_ref[...], preferred_element_type=F32)
    z = jnp.dot(xb, wz_ref[...], preferred_element_type=F32)
    dtr = jnp.dot(xb, wdt_ref[...], preferred_element_type=F32)

    x3 = xbc_raw.reshape(seqs, SUBLANES, xbc_raw.shape[1])
    conv_ref[...] = x3[:, valid - (CONV_W - 1):valid, :]
    xbc3 = _silu(_causal_conv_tiles(x3, cstate_ref[...], cw_ref[...], cb_ref[...]))
    sx3 = xbc3[:, :, :d_ssd]
    b3 = xbc3[:, :, d_ssd:d_ssd + gn]
    c3 = xbc3[:, :, d_ssd + gn:]

    row = lax.broadcasted_iota(jnp.int32, (seqs, SUBLANES, LANES), 1)
    lane = lax.broadcasted_iota(jnp.int32, (seqs, SUBLANES, LANES), 2)
    dt3 = jnp.where(row < valid, _softplus(dtr + dtb_ref[...]).reshape(seqs, SUBLANES, LANES), 0.0)
    da3 = dt3 * (-jnp.exp(alog_ref[...]))[None]
    ones = jnp.ones_like(da3)
    _, acum3 = _scan_in_tiles(ones, da3)
    alast = acum3[:, SUBLANES - 1:, :]
    wgt3 = dt3 * jnp.exp(alast - acum3)
    tot_scr[...] = jnp.exp(alast)
    e3 = jnp.exp(acum3)

    coefs = []
    for u in range(valid):
        prod = c3 * b3[:, u:u + 1, :]
        cbu = [jnp.sum(prod[:, :, g * SSD_STATE:(g + 1) * SSD_STATE], axis=-1, keepdims=True)
               for g in range(SSD_GROUPS)]
        cb_heads = jnp.where(lane < hg, cbu[0], cbu[1])
        coefs.append(jnp.where(row >= u, cb_heads * jnp.exp(acum3 - acum3[:, u:u + 1, :]) * dt3[:, u:u + 1, :],
                               0.0))

    per_head = coefs + [wgt3, e3]
    stacked = jnp.concatenate([v.reshape(rows, LANES) for v in per_head], axis=0)
    expanded = None
    rest = stacked
    for _ in range(3):
        piece = rest.astype(BF16)
        rest = rest - piece.astype(F32)
        part = jnp.dot(piece, expand_ref[...], preferred_element_type=F32)
        expanded = part if expanded is None else expanded + part
    expanded = [expanded[i * rows:(i + 1) * rows].reshape(seqs, SUBLANES, d_ssd) for i in range(len(per_head))]
    y_diag = expanded[0] * sx3[:, 0:1, :]
    for u in range(1, valid):
        y_diag = y_diag + expanded[u] * sx3[:, u:u + 1, :]
    wgt_x, e_x = expanded[valid], expanded[valid + 1]

    c_scr[...] = c3
    b_scr[...] = b3
    xw_scr[...] = sx3 * wgt_x
    srow = lax.broadcasted_iota(jnp.int32, (2 * SSD_HEAD_DIM, SSD_STATE), 0) < SSD_HEAD_DIM

    def per_seq(s, carry):
        cs = c_scr[s]
        bs = b_scr[s]
        xws = xw_scr[s]
        tots = tot_scr[s]
        for g in range(SSD_GROUPS):
            gs = slice(g * SSD_STATE, (g + 1) * SSD_STATE)
            width = hg * SSD_HEAD_DIM
            cols = slice(g * width, (g + 1) * width)
            hin = h0_ref[s, cols, :]
            yoff_scr[s, :, cols] = _mm_nt(cs[:, gs], hin)
            upd = _mm_tn(xws[:, cols], bs[:, gs])
            for pq in range(hg // 2):
                h = g * hg + 2 * pq
                rs = slice(pq * LANES, (pq + 1) * LANES)
                t0 = jnp.broadcast_to(tots[0:1, h:h + 1], (LANES, SSD_STATE))
                t1 = jnp.broadcast_to(tots[0:1, h + 1:h + 2], (LANES, SSD_STATE))
                h_ref[s, g * width + pq * LANES:g * width + (pq + 1) * LANES, :] = (
                    hin[rs] * jnp.where(srow, t0, t1) + upd[rs])
        return carry

    lax.fori_loop(0, seqs, per_seq, 0)

    y = y_diag + yoff_scr[...] * e_x + dexp_ref[...][None] * sx3
    y = y * _silu(z.reshape(seqs, SUBLANES, d_ssd))
    out_ref[...] = _group_rmsnorm(y, ng_ref[...][None]).astype(out_ref.dtype)


def _memkv_kernel(mem_ref, wk_ref, wv_ref, k_ref, v_ref, kb_ref, vb_ref):
    mb = mem_ref[0].astype(BF16)
    for w_ref, o_ref, ob_ref in ((wk_ref, k_ref, kb_ref), (wv_ref, v_ref, vb_ref)):
        proj = jnp.dot(mb, w_ref[...], preferred_element_type=F32)
        o_ref[0] = proj.reshape(o_ref.shape[1:])
        ob_ref[0] = proj.astype(BF16)


def _softmax_terms(scores):
    p = jnp.exp(scores - jnp.max(scores, axis=-1, keepdims=True))
    return p, jnp.sum(p, axis=-1, keepdims=True)


def _prompt_attention_head(h, q_ref, rows, k_ref, v_ref, out_ref):
    c = k_ref.shape[2]
    d_head = c // MEM_HEADS
    hs = slice(h * d_head, (h + 1) * d_head)

    def scores():
        return _softmax_terms(_mm_nt(q_ref[:rows, hs], k_ref[0, :, hs]) * (d_head ** -0.5))

    def output(p, l):
        o = _mm(p, v_ref[0, :, hs]) * (1.0 / l)
        gate = q_ref[:rows, c + h * d_head:c + (h + 1) * d_head]
        out_ref[:, hs] = (o * _silu(gate)).astype(out_ref.dtype)

    return scores, output


def _sample_attention_seq(s, q_ref, rows, k_ref, v_ref, out_ref):
    _, n_mem, heads, d_head = k_ref.shape
    c = heads * d_head
    rs = slice(rows + s * SUBLANES, rows + (s + 1) * SUBLANES)

    def scores():
        shape = (heads * SUBLANES, n_mem * heads)
        same_head = (lax.broadcasted_iota(jnp.int32, shape, 0) // SUBLANES
                     == lax.broadcasted_iota(jnp.int32, shape, 1) % heads)
        qh = jnp.concatenate([q_ref[rs, h * d_head:(h + 1) * d_head] for h in range(heads)], axis=0)
        sc = _mm_nt(qh, k_ref[s].reshape(n_mem * heads, d_head)) * (d_head ** -0.5)
        return _softmax_terms(jnp.where(same_head, sc, -jnp.inf))

    def output(p, l):
        o = _mm(p, v_ref[s].reshape(n_mem * heads, d_head)) * (1.0 / l)
        o = jnp.concatenate([o[h * SUBLANES:(h + 1) * SUBLANES] for h in range(heads)], axis=1)
        out_ref[s] = (o * _silu(q_ref[rs, c:])).astype(out_ref.dtype)

    return scores, output


def _xattn_kernel(x_ref, xs_ref, wq_ref, wg_ref, k_ref, v_ref, ks_ref, vs_ref, out_ref, outs_ref):
    rows = x_ref.shape[0]
    seqs, _, d = xs_ref.shape
    xb = jnp.concatenate([x_ref[...], xs_ref[...].reshape(seqs * SUBLANES, d)], axis=0).astype(BF16)
    qg = jnp.concatenate([jnp.dot(xb, wq_ref[...], preferred_element_type=F32),
                          jnp.dot(xb, wg_ref[...], preferred_element_type=F32)], axis=1)
    items = [_prompt_attention_head(h, qg, rows, k_ref, v_ref, out_ref) for h in range(MEM_HEADS)]
    items += [_sample_attention_seq(i, qg, rows, ks_ref, vs_ref, outs_ref) for i in range(seqs)]
    for scores, output in items:
        output(*scores())


def _merge_kernel(rg_ref, ssd_ref, xa_ref, x_ref, w_ref, g_ref, b_ref, y_ref, *, sub_rows):
    d = rg_ref.shape[1]
    for r0 in range(0, x_ref.shape[0], sub_rows):
        rs = slice(r0, r0 + sub_rows)
        mix = (jnp.dot(rg_ref[rs, :].astype(BF16), w_ref[0:d, :], preferred_element_type=F32)
               + jnp.dot(ssd_ref[rs, :].astype(BF16), w_ref[d:2 * d, :], preferred_element_type=F32)
               + jnp.dot(xa_ref[rs, :].astype(BF16), w_ref[2 * d:3 * d, :], preferred_element_type=F32))
        res = DEEPNORM_ALPHA * x_ref[rs, :] + mix
        mu = jnp.mean(res, axis=-1, keepdims=True)
        cen = res - mu
        var = jnp.mean(cen * cen, axis=-1, keepdims=True)
        y_ref[rs, :] = cen * lax.rsqrt(var + LN_EPS) * g_ref[...] + b_ref[...]


def _full(shape):
    return pl.BlockSpec(shape, lambda *_: (0,) * len(shape))


def _time_stride_perm(rows):
    p = np.arange(rows)
    t = (p % SUBLANES) * (rows // SUBLANES) + p // SUBLANES
    perm = t[:, None] == np.arange(rows)[None, :]
    return jnp.asarray(perm, dtype=BF16), jnp.asarray(perm.T, dtype=BF16)


def _rg_prompt(x, p, tile):
    b, l, d = x.shape
    c = p["wx"].shape[1]
    tiles_per_seq = l // tile
    n_tiles = b * tiles_per_seq
    perm, perm_t = _time_stride_perm(tile)
    done = lambda s: jnp.maximum(s - 1, 0)
    out, conv, hlast = pl.pallas_call(
        functools.partial(_rg_prompt_kernel, tiles_per_seq=tiles_per_seq),
        grid=(n_tiles + 1,),
        in_specs=[pl.BlockSpec((tile, d), lambda s: (jnp.minimum(s, n_tiles - 1), 0)),
                  _full(perm.shape), _full(perm.shape), _full(p["wx"].shape), _full(p["wg"].shape),
                  _full(p["cw"].shape), _full(p["cb"].shape), _full(p["wgate"].shape), _full(p["ba"].shape),
                  _full(p["bi"].shape), _full(p["lam"].shape)],
        out_specs=[pl.BlockSpec((tile, c), lambda s: (done(s), 0)),
                   pl.BlockSpec((1, CONV_W - 1, c), lambda s: (done(s) // tiles_per_seq, 0, 0)),
                   pl.BlockSpec((1, 1, c), lambda s: (done(s) // tiles_per_seq, 0, 0))],
        out_shape=[jax.ShapeDtypeStruct((b * l, c), BF16),
                   jax.ShapeDtypeStruct((b, CONV_W - 1, c), F32),
                   jax.ShapeDtypeStruct((b, 1, c), F32)],
        scratch_shapes=[pltpu.VMEM((CONV_W - 1, SUBLANES, c), F32), pltpu.VMEM((1, c), F32),
                        pltpu.VMEM((tile, 2 * c), F32), pltpu.VMEM((tile, c), BF16)],
        compiler_params=_cparams("arbitrary"),
        name="rg_prompt",
    )(x.reshape(b * l, d), perm, perm_t, p["wx"], p["wg"], p["cw"], p["cb"], p["wgate"], p["ba"], p["bi"],
      p["lam"])
    return out.reshape(b, l, c), conv, hlast


def _rg_sample(xpad, p, cstate, h0, seqs, valid):
    n, _, d = xpad.shape
    c = p["wx"].shape[1]
    blk = lambda w: pl.BlockSpec((seqs, w[0], w[1]), lambda i: (i, 0, 0))
    return pl.pallas_call(
        functools.partial(_rg_sample_kernel, valid=valid),
        grid=(n // seqs,),
        in_specs=[blk((SUBLANES, d)), _full(p["wx"].shape), _full(p["wg"].shape), _full(p["cw"].shape),
                  _full(p["cb"].shape), _full(p["wgate"].shape), _full(p["ba"].shape),
                  _full(p["bi"].shape), _full(p["lam"].shape), blk((SUBLANES, c)), blk((1, c))],
        out_specs=[blk((SUBLANES, c)), blk((CONV_W - 1, c)), blk((1, c))],
        out_shape=[jax.ShapeDtypeStruct((n, SUBLANES, c), F32),
                   jax.ShapeDtypeStruct((n, CONV_W - 1, c), F32),
                   jax.ShapeDtypeStruct((n, 1, c), F32)],
        compiler_params=_cparams("parallel"),
        name="rg_sample",
    )(xpad, p["wx"], p["wg"], p["cw"], p["cb"], p["wgate"], p["ba"], p["bi"], p["lam"], cstate, h0)


def _ssd_param_specs(p):
    return [_full(p[k].shape) for k in ("wxbc", "wz", "wdt", "cw", "cb", "dtb", "alog", "dexp", "ng")]


def _ssd_param_args(p):
    return [p[k] for k in ("wxbc", "wz", "wdt", "cw", "cb", "dtb", "alog", "dexp", "ng")]


def _ssd_prompt(x, p):
    b, l, d = x.shape
    cc = p["wxbc"].shape[1]
    c = p["wz"].shape[1]
    chunks_per_seq = l // SSD_CHUNK
    n_chunks = b * chunks_per_seq
    perm, perm_t = _time_stride_perm(SSD_CHUNK)
    done = lambda s: jnp.maximum(s - 1, 0)
    out, conv, hstate = pl.pallas_call(
        functools.partial(_ssd_prompt_kernel, chunks_per_seq=chunks_per_seq),
        grid=(n_chunks + 1,),
        in_specs=[pl.BlockSpec((SSD_CHUNK, d), lambda s: (jnp.minimum(s, n_chunks - 1), 0)),
                  _full(perm.shape), _full(perm.shape)] + _ssd_param_specs(p),
        out_specs=[pl.BlockSpec((SSD_CHUNK, c), lambda s: (done(s), 0)),
                   pl.BlockSpec((1, CONV_W - 1, cc), lambda s: (done(s) // chunks_per_seq, 0, 0)),
                   pl.BlockSpec((1, c, SSD_STATE), lambda s: (done(s) // chunks_per_seq, 0, 0))],
        out_shape=[jax.ShapeDtypeStruct((b * l, c), BF16),
                   jax.ShapeDtypeStruct((b, CONV_W - 1, cc), F32),
                   jax.ShapeDtypeStruct((b, c, SSD_STATE), F32)],
        scratch_shapes=[pltpu.VMEM((CONV_W - 1, SUBLANES, cc), F32),
                        pltpu.VMEM((SSD_CHUNK, cc + LANES), F32), pltpu.VMEM((SSD_CHUNK, d), BF16)],
        compiler_params=_cparams("arbitrary"),
        name="ssd_prompt",
    )(x.reshape(b * l, d), perm, perm_t, *_ssd_param_args(p))
    return out.reshape(b, l, c), conv, hstate


def _ssd_sample(xpad, p, cstate, h0, seqs, valid):
    n, _, d = xpad.shape
    cc = p["wxbc"].shape[1]
    c = p["wz"].shape[1]
    gn = SSD_GROUPS * SSD_STATE
    blk = lambda w: pl.BlockSpec((seqs, w[0], w[1]), lambda i: (i, 0, 0))
    return pl.pallas_call(
        functools.partial(_ssd_sample_kernel, valid=valid),
        grid=(n // seqs,),
        in_specs=[blk((SUBLANES, d))] + _ssd_param_specs(p)
        + [_full(p["expand"].shape), blk((SUBLANES, cc)), blk((c, SSD_STATE))],
        out_specs=[blk((SUBLANES, c)), blk((CONV_W - 1, cc)), blk((c, SSD_STATE))],
        out_shape=[jax.ShapeDtypeStruct((n, SUBLANES, c), F32),
                   jax.ShapeDtypeStruct((n, CONV_W - 1, cc), F32),
                   jax.ShapeDtypeStruct((n, c, SSD_STATE), F32)],
        scratch_shapes=[pltpu.VMEM((seqs, SUBLANES, gn), F32), pltpu.VMEM((seqs, SUBLANES, gn), F32),
                        pltpu.VMEM((seqs, SUBLANES, c), F32), pltpu.VMEM((seqs, 1, LANES), F32),
                        pltpu.VMEM((seqs, SUBLANES, c), F32)],
        compiler_params=_cparams("parallel"),
        name="ssd_sample",
    )(xpad, *_ssd_param_args(p), p["expand"], cstate, h0)


def _memkv(mem, wk, wv):
    b, m, d = mem.shape
    c = wk.shape[1]
    spec = pl.BlockSpec((1, m, c), lambda i: (i, 0, 0))
    spec4 = pl.BlockSpec((1, m, MEM_HEADS, c // MEM_HEADS), lambda i: (i, 0, 0, 0))
    return pl.pallas_call(
        _memkv_kernel,
        grid=(b,),
        in_specs=[pl.BlockSpec((1, m, d), lambda i: (i, 0, 0)), _full(wk.shape), _full(wv.shape)],
        out_specs=[spec4, spec4, spec, spec],
        out_shape=[jax.ShapeDtypeStruct((b, m, MEM_HEADS, c // MEM_HEADS), F32)] * 2
        + [jax.ShapeDtypeStruct((b, m, c), BF16)] * 2,
        compiler_params=_cparams("parallel"),
        name="mem_kv",
    )(mem, wk, wv)


def _xattn(x, xs_pad, wq, wg, k, v, ks, vs, tile):
    b, l, d = x.shape
    n = xs_pad.shape[0]
    c = wq.shape[1]
    m = k.shape[1]
    tiles_per_seq = l // tile
    n_tiles = b * tiles_per_seq
    seqs = n // n_tiles
    assert seqs * n_tiles == n
    kv_spec = pl.BlockSpec((1, m, c), lambda s: (s // tiles_per_seq, 0, 0))
    skv_spec = pl.BlockSpec((seqs,) + ks.shape[1:], lambda s: (s, 0, 0, 0))
    out, outs = pl.pallas_call(
        _xattn_kernel,
        grid=(n_tiles,),
        in_specs=[pl.BlockSpec((tile, d), lambda s: (s, 0)),
                  pl.BlockSpec((seqs, SUBLANES, d), lambda s: (s, 0, 0)),
                  _full(wq.shape), _full(wg.shape), kv_spec, kv_spec, skv_spec, skv_spec],
        out_specs=[pl.BlockSpec((tile, c), lambda s: (s, 0)),
                   pl.BlockSpec((seqs, SUBLANES, c), lambda s: (s, 0, 0))],
        out_shape=[jax.ShapeDtypeStruct((b * l, c), BF16), jax.ShapeDtypeStruct((n, SUBLANES, c), F32)],
        compiler_params=_cparams("parallel"),
        name="xattn",
    )(x.reshape(b * l, d), xs_pad, wq, wg, k, v, ks, vs)
    return out.reshape(b, l, c), outs


def _merge(rg, ssd, xa, x, w_out, ln_g, ln_b, tile, name):
    n, d = x.shape
    c = rg.shape[1]
    tile = min(tile, n)
    row = lambda w: pl.BlockSpec((tile, w), lambda i: (i, 0))
    return pl.pallas_call(
        functools.partial(_merge_kernel, sub_rows=min(MERGE_SUB_ROWS, tile)),
        grid=(n // tile,),
        in_specs=[row(c), row(c), row(c), row(d), _full(w_out.shape), _full(ln_g.shape), _full(ln_b.shape)],
        out_specs=row(d),
        out_shape=jax.ShapeDtypeStruct((n, d), F32),
        compiler_params=_cparams("parallel"),
        name=name,
    )(rg, ssd, xa, x, w_out, ln_g, ln_b)


def _layer_params(w_in, rg_conv_w, rg_conv_b, w_rg_a, b_rg_a, w_rg_i, b_rg_i, rg_lambda, ssd_conv_w,
                  ssd_conv_b, ssd_dt_bias, ssd_a_log, ssd_d, ssd_norm_g, w_out, ln_g, ln_b):
    d_rg = rg_conv_w.shape[1]
    d_conv = ssd_conv_w.shape[1]
    d_ssd = ssd_norm_g.shape[0]
    heads = ssd_d.shape[0]
    sizes = (d_rg, d_rg, d_conv, d_ssd, heads)
    offs = [0]
    for s in sizes:
        offs.append(offs[-1] + s)
    d_xa = (w_in.shape[1] - offs[-1]) // 2
    wb = lambda lo, hi: w_in[:, lo:hi].astype(BF16)
    row = lambda v: v.reshape(1, -1).astype(F32)
    pad_lanes = lambda v: jnp.pad(v, ((0, 0), (0, LANES - v.shape[1])))
    rg = dict(wx=wb(offs[0], offs[1]), wg=wb(offs[1], offs[2]), cw=rg_conv_w, cb=row(rg_conv_b),
              wgate=jnp.concatenate([w_rg_a, w_rg_i], axis=2).astype(BF16),
              ba=row(b_rg_a), bi=row(b_rg_i), lam=row(rg_lambda))
    head_of_channel = np.arange(d_ssd) // SSD_HEAD_DIM
    ssd = dict(wxbc=wb(offs[2], offs[3]), wz=wb(offs[3], offs[4]), wdt=pad_lanes(wb(offs[4], offs[5])),
               cw=ssd_conv_w, cb=row(ssd_conv_b), dtb=pad_lanes(row(ssd_dt_bias)),
               alog=pad_lanes(row(ssd_a_log)), dexp=row(jnp.repeat(ssd_d, SSD_HEAD_DIM)), ng=row(ssd_norm_g),
               expand=jnp.asarray(np.arange(LANES)[:, None] == head_of_channel[None, :], dtype=BF16))
    xa = dict(wq=wb(offs[5], offs[5] + d_xa), wg=wb(offs[5] + d_xa, offs[5] + 2 * d_xa))
    merge = dict(w=w_out.astype(BF16), g=row(ln_g), b=row(ln_b))
    return rg, ssd, xa, merge


PROMPT_RG_TILE = 256
PROMPT_XA_TILE = 256
MERGE_TILE = 1024
MERGE_SUB_ROWS = 256
SAMPLE_RG_SEQS = 16
SAMPLE_SSD_SEQS = 16


def kernel(x_prompt, x_sample, mem_prompt, state_rg_conv, state_rg_h, state_ssd_conv, state_ssd_h,
           cache_mem_k, cache_mem_v, w_in, rg_conv_w, rg_conv_b, w_rg_a, b_rg_a, w_rg_i, b_rg_i,
           rg_lambda, ssd_conv_w, ssd_conv_b, ssd_dt_bias, ssd_a_log, ssd_d, ssd_norm_g, w_mem_k,
           w_mem_v, w_out, ln_g, ln_b):
    assert w_in.shape[0] == DEPTH
    bp, lp, d = x_prompt.shape
    bs, ls, _ = x_sample.shape
    heads = ssd_d.shape[1]
    outs = {k: [] for k in ("rgc_p", "rgh_p", "sc_p", "sh_p", "mk_p", "mv_p", "rgc_s", "rgh_s", "sc_s", "sh_s")}
    yp, ys = x_prompt, x_sample
    pad_rows = lambda v, before, after: jnp.pad(v, ((0, 0), (before, after), (0, 0)))
    for l in range(DEPTH):
        rg, ssd, xa, merge = _layer_params(
            w_in[l], rg_conv_w[l], rg_conv_b[l], w_rg_a[l], b_rg_a[l], w_rg_i[l], b_rg_i[l], rg_lambda[l],
            ssd_conv_w[l], ssd_conv_b[l], ssd_dt_bias[l], ssd_a_log[l], ssd_d[l], ssd_norm_g[l],
            w_out[l], ln_g[l], ln_b[l])
        xs_pad = pad_rows(ys, 0, SAMPLE_PAD - ls)
        mk, mv, mkb, mvb = _memkv(mem_prompt, w_mem_k[l].astype(BF16), w_mem_v[l].astype(BF16))
        rg_o, rgc, rgh = _rg_prompt(yp, rg, PROMPT_RG_TILE)
        ssd_o, sc, sh = _ssd_prompt(yp, ssd)
        xa_o, xa_s = _xattn(yp, xs_pad, xa["wq"], xa["wg"], mkb, mvb, cache_mem_k[l], cache_mem_v[l],
                            PROMPT_XA_TILE)
        flat = lambda v: v.reshape(bp * lp, v.shape[-1])
        yp = _merge(flat(rg_o), flat(ssd_o), flat(xa_o), flat(yp), merge["w"], merge["g"], merge["b"],
                    MERGE_TILE, "merge_prompt").reshape(bp, lp, d)
        outs["rgc_p"].append(rgc)
        outs["rgh_p"].append(rgh.reshape(bp, -1))
        outs["sc_p"].append(sc)
        outs["sh_p"].append(sh.reshape(bp, heads, SSD_HEAD_DIM, SSD_STATE))
        outs["mk_p"].append(mk)
        outs["mv_p"].append(mv)
        tail = SAMPLE_PAD - (CONV_W - 1)
        rg_o, rgc, rgh = _rg_sample(xs_pad, rg, pad_rows(state_rg_conv[l], tail, 0),
                                    state_rg_h[l][:, None, :], SAMPLE_RG_SEQS, ls)
        ssd_o, sc, sh = _ssd_sample(xs_pad, ssd, pad_rows(state_ssd_conv[l], tail, 0),
                                    state_ssd_h[l].reshape(bs, heads * SSD_HEAD_DIM, SSD_STATE),
                                    SAMPLE_SSD_SEQS, ls)
        flat = lambda v: v.reshape(bs * SAMPLE_PAD, v.shape[-1])
        ys_pad = _merge(flat(rg_o), flat(ssd_o), flat(xa_s), flat(xs_pad), merge["w"], merge["g"], merge["b"],
                        MERGE_TILE, "merge_sample").reshape(bs, SAMPLE_PAD, d)
        ys = ys_pad[:, :ls, :]
        outs["rgc_s"].append(rgc)
        outs["rgh_s"].append(rgh.reshape(bs, -1))
        outs["sc_s"].append(sc)
        outs["sh_s"].append(sh.reshape(bs, heads, SSD_HEAD_DIM, SSD_STATE))
    st = lambda k: jnp.stack(outs[k])
    return (yp, ys, st("rgc_p"), st("rgh_p"), st("sc_p"), st("sh_p"), st("mk_p"), st("mv_p"),
            st("rgc_s"), st("rgh_s"), st("sc_s"), st("sh_s"))
```

```python
import functools

import jax
import jax.numpy as jnp
import numpy as np
from jax import lax
from jax.experimental import pallas as pl
from jax.experimental.pallas import tpu as pltpu

F32 = jnp.float32
BF16 = jnp.bfloat16

SUBLANES = 8
LANES = 128
MXU_WIDTH = 256
VMEM_LIMIT_BYTES = 56 * 1024 * 1024

RG_C = 8.0
CONV_W = 4
RG_BLOCKS = 8
SSD_HEAD_DIM = 64
SSD_GROUPS = 2
SSD_STATE = 128
SSD_CHUNK = 128
MEM_HEADS = 4
LN_EPS = 1e-5
RMS_EPS = 1e-5
DEPTH = 1
DEEPNORM_ALPHA = (2 * DEPTH) ** 0.25
SAMPLE_PAD = SUBLANES


def _cparams(*sem):
    return pltpu.CompilerParams(dimension_semantics=sem, vmem_limit_bytes=VMEM_LIMIT_BYTES)


def _mm(a, b):
    return jnp.dot(a.astype(BF16), b.astype(BF16), preferred_element_type=F32)


def _mm_nt(a, b):
    return lax.dot_general(a.astype(BF16), b.astype(BF16), (((1,), (1,)), ((), ())),
                           preferred_element_type=F32)


def _mm_tn(a, b):
    return lax.dot_general(a.astype(BF16), b.astype(BF16), (((0,), (0,)), ((), ())),
                           preferred_element_type=F32)


def _sigmoid(x):
    return 1.0 / (1.0 + jnp.exp(-x))


def _silu(x):
    return x * _sigmoid(x)


def _softplus(x):
    return jnp.maximum(x, 0.0) + jnp.log(1.0 + jnp.exp(-jnp.abs(x)))


def _causal_conv_tiles(x3, p3, w, b):
    row = lax.broadcasted_iota(jnp.int32, x3.shape, 1)
    y = x3 * w[CONV_W - 1:CONV_W][None]
    for s in range(1, CONV_W):
        shifted = jnp.where(row >= s, pltpu.roll(x3, s, axis=1), pltpu.roll(p3, s, axis=1))
        y = y + shifted * w[CONV_W - 1 - s:CONV_W - s][None]
    return y + b[None]


def _scan_in_tiles(a3, b3):
    row = lax.broadcasted_iota(jnp.int32, a3.shape, 1)
    s = 1
    while s < SUBLANES:
        keep = row >= s
        a_sh = jnp.where(keep, pltpu.roll(a3, s, axis=1), 1.0)
        b_sh = jnp.where(keep, pltpu.roll(b3, s, axis=1), 0.0)
        b3 = a3 * b_sh + b3
        a3 = a3 * a_sh
        s *= 2
    return a3, b3


def _rg_gates(u, wgate, ba, bi, lam):
    pre = _mm(u, wgate)
    r = _sigmoid(pre[:, :LANES] + ba)
    i = _sigmoid(pre[:, LANES:] + bi)
    neg_log_a = r * (RG_C * _softplus(-lam))
    a = jnp.exp(-neg_log_a)
    v = jnp.tanh(neg_log_a) * (1.0 + a * a)
    mult = jnp.where(v > 0.0, v * lax.rsqrt(v), 0.0)
    return a, mult * (i * u)


def _time_strided_conv(x3, tail3, w, b):
    slabs = x3.shape[0]
    row = lax.broadcasted_iota(jnp.int32, tail3.shape, 1)
    wrapped = jnp.where(row >= 1, pltpu.roll(x3[slabs - (CONV_W - 1):], 1, axis=1), pltpu.roll(tail3, 1, axis=1))
    y = x3 * w[CONV_W - 1:CONV_W][None]
    for s in range(1, CONV_W):
        shifted = jnp.concatenate([wrapped[CONV_W - 1 - s:], x3[:slabs - s]], axis=0)
        y = y + shifted * w[CONV_W - 1 - s:CONV_W - s][None]
    return y + b[None]


def _rg_prompt_kernel(x_ref, perm_ref, permt_ref, wx_ref, wg_ref, cw_ref, cb_ref, wgate_ref, ba_ref,
                      bi_ref, lam_ref, out_ref, conv_ref, hlast_ref, tail_scr, h_scr, proj_scr, outp_scr, *,
                      tiles_per_seq):
    s = pl.program_id(0)

    @pl.when(s == 0)
    def _():
        proj_scr[...] = jnp.zeros_like(proj_scr)

    @pl.when(jnp.logical_or(s == 0, s % tiles_per_seq == 1 % tiles_per_seq))
    def _():
        tail_scr[...] = jnp.zeros_like(tail_scr)
        h_scr[...] = jnp.zeros_like(h_scr)

    rows = x_ref.shape[0]
    slabs = rows // SUBLANES
    width = wx_ref.shape[1]
    per_group = MXU_WIDTH // LANES
    row = lax.broadcasted_iota(jnp.int32, (SUBLANES, LANES), 0)
    xb = jnp.dot(perm_ref[...], x_ref[...].astype(BF16), preferred_element_type=F32).astype(BF16)

    for cg in range(width // MXU_WIDTH):
        gs = slice(cg * MXU_WIDTH, (cg + 1) * MXU_WIDTH)
        for kk in range(per_group):
            k = cg * per_group + kk
            ks = slice(k * LANES, (k + 1) * LANES)
            x3 = proj_scr[:, ks].reshape(slabs, SUBLANES, LANES)
            tail3 = tail_scr[:, :, ks]
            last3 = x3[slabs - (CONV_W - 1):]
            tail_scr[:, :, ks] = last3
            for i in range(CONV_W - 1):
                conv_ref[0, i:i + 1, ks] = last3[i, SUBLANES - 1:, :]
            u3 = _time_strided_conv(x3, tail3, cw_ref[:, ks], cb_ref[:, ks])
            a, b = _rg_gates(u3.reshape(rows, LANES), wgate_ref[k], ba_ref[:, ks], bi_ref[:, ks], lam_ref[:, ks])
            a3 = a.reshape(slabs, SUBLANES, LANES)
            b3 = b.reshape(slabs, SUBLANES, LANES)
            h_loc, a_cum = [b3[0]], [a3[0]]
            for j in range(1, slabs):
                h_loc.append(a3[j] * h_loc[j - 1] + b3[j])
                a_cum.append(a3[j] * a_cum[j - 1])
            a_run, h_run = _scan_in_tiles(a_cum[-1][None], h_loc[-1][None])
            h_prev = h_scr[:, ks]
            h_end = h_run[0] + a_run[0] * h_prev
            h_in = jnp.where(row >= 1, pltpu.roll(h_end, 1, axis=0), h_prev)
            h_scr[:, ks] = h_end[SUBLANES - 1:]
            hlast_ref[0, :, ks] = h_end[SUBLANES - 1:]
            h = jnp.concatenate([h_loc[j] + a_cum[j] * h_in for j in range(slabs)], axis=0)
            gate = proj_scr[:, width + k * LANES:width + (k + 1) * LANES]
            outp_scr[:, ks] = (h * _silu(gate)).astype(BF16)
        proj_scr[:, gs] = jnp.dot(xb, wx_ref[:, gs], preferred_element_type=F32)
        proj_scr[:, width + cg * MXU_WIDTH:width + (cg + 1) * MXU_WIDTH] = jnp.dot(
            xb, wg_ref[:, gs], preferred_element_type=F32)

    out_ref[...] = jnp.dot(permt_ref[...], outp_scr[...], preferred_element_type=F32).astype(out_ref.dtype)


def _rg_sample_kernel(x_ref, wx_ref, wg_ref, cw_ref, cb_ref, wgate_ref, ba_ref, bi_ref, lam_ref,
                      cstate_ref, h0_ref, out_ref, conv_ref, hlast_ref, *, valid):
    seqs = x_ref.shape[0]
    rows = seqs * SUBLANES
    xb = x_ref[...].reshape(rows, x_ref.shape[2]).astype(BF16)
    rgx = jnp.dot(xb, wx_ref[...], preferred_element_type=F32)
    gate = jnp.dot(xb, wg_ref[...], preferred_element_type=F32)
    x3 = rgx.reshape(seqs, SUBLANES, rgx.shape[1])
    conv_ref[...] = x3[:, valid - (CONV_W - 1):valid, :]
    row = lax.broadcasted_iota(jnp.int32, (seqs, SUBLANES, LANES), 1)
    for k in range(RG_BLOCKS):
        ks = slice(k * LANES, (k + 1) * LANES)
        u3 = _causal_conv_tiles(x3[:, :, ks], cstate_ref[:, :, ks], cw_ref[:, ks], cb_ref[:, ks])
        u = u3.reshape(rows, LANES)
        a, b = _rg_gates(u, wgate_ref[k], ba_ref[:, ks], bi_ref[:, ks], lam_ref[:, ks])
        a3 = a.reshape(seqs, SUBLANES, LANES)
        b3 = b.reshape(seqs, SUBLANES, LANES)
        b3 = b3 + jnp.where(row == 0, a3 * h0_ref[:, :, ks], 0.0)
        _, h3 = _scan_in_tiles(a3, b3)
        hlast_ref[:, :, ks] = h3[:, valid - 1:valid, :]
        g3 = gate[:, ks].reshape(seqs, SUBLANES, LANES)
        out_ref[:, :, ks] = (h3 * _silu(g3)).astype(out_ref.dtype)


def _cumsum_rows(x):
    rows = x.shape[0]
    row = lax.broadcasted_iota(jnp.int32, x.shape, 0)
    s = 1
    while s < rows:
        x = x + jnp.where(row >= s, pltpu.roll(x, s, axis=0), 0.0)
        s *= 2
    return x


def _group_rmsnorm(y, gain):
    width = y.shape[-1] // SSD_GROUPS
    parts = []
    for g in range(SSD_GROUPS):
        yg = y[..., g * width:(g + 1) * width]
        ms = jnp.sum(yg * yg, axis=-1, keepdims=True) * (1.0 / width)
        parts.append(yg * lax.rsqrt(ms + RMS_EPS))
    return jnp.concatenate(parts, axis=-1) * gain


def _time_strided_cumsum(x):
    slabs = x.shape[0] // SUBLANES
    x3 = x.reshape(slabs, SUBLANES, x.shape[1])
    acc = [x3[0]]
    for j in range(1, slabs):
        acc.append(acc[j - 1] + x3[j])
    _, run = _scan_in_tiles(jnp.ones_like(acc[-1])[None], acc[-1][None])
    row = lax.broadcasted_iota(jnp.int32, run[0].shape, 0)
    before = jnp.where(row >= 1, pltpu.roll(run[0], 1, axis=0), 0.0)
    return jnp.concatenate([a + before for a in acc], axis=0)


def _ssd_merge_prompt_kernel(x_ref, perm_ref, permt_ref, wxbc_ref, wz_ref, wdt_ref, cw_ref, cb_ref, dtb_ref,
                             alog_ref, dexp_ref, ng_ref, rg_ref, xa_ref, xres_ref, wout_ref, lng_ref, lnb_ref,
                             y_ref, conv_ref, h_ref, tail_scr, proj_scr, xb_scr, h_scr, ssd_scr, *,
                             chunks_per_seq, n_chunks):
    s = pl.program_id(0)

    @pl.when(s == 0)
    def _():
        proj_scr[...] = jnp.zeros_like(proj_scr)
        xb_scr[...] = jnp.zeros_like(xb_scr)
        ssd_scr[...] = jnp.zeros_like(ssd_scr)

    @pl.when(jnp.logical_or(s == 0, s % chunks_per_seq == 1 % chunks_per_seq))
    def _():
        tail_scr[...] = jnp.zeros_like(tail_scr)
        h_scr[...] = jnp.zeros_like(h_scr)

    q = x_ref.shape[0]
    slabs = q // SUBLANES
    d_ssd = wz_ref.shape[1]
    d_conv = wxbc_ref.shape[1]
    gn = SSD_GROUPS * SSD_STATE
    mix, row0 = None, 0
    for src in (rg_ref, ssd_scr, xa_ref):
        part = jnp.dot(src[...], wout_ref[row0:row0 + src.shape[1], :], preferred_element_type=F32)
        mix = part if mix is None else mix + part
        row0 += src.shape[1]
    res = DEEPNORM_ALPHA * xres_ref[...] + mix
    mu = jnp.mean(res, axis=-1, keepdims=True)
    cen = res - mu
    var = jnp.mean(cen * cen, axis=-1, keepdims=True)
    y_ref[...] = cen * lax.rsqrt(var + LN_EPS) * lng_ref[...] + lnb_ref[...]

    xb_new = jnp.dot(perm_ref[...], x_ref[...].astype(BF16), preferred_element_type=F32).astype(BF16)
    z = jnp.dot(xb_scr[...], wz_ref[...], preferred_element_type=F32)
    xb_scr[...] = xb_new

    x3 = proj_scr[:, :d_conv].reshape(slabs, SUBLANES, d_conv)
    dtr = proj_scr[:, d_conv:]
    tail3 = tail_scr[...]
    tail_scr[...] = x3[slabs - (CONV_W - 1):]
    xbc = _silu(_time_strided_conv(x3, tail3, cw_ref[...], cb_ref[...])).reshape(q, d_conv)
    sx = xbc[:, :d_ssd]
    bm = xbc[:, d_ssd:d_ssd + gn]
    cm = xbc[:, d_ssd + gn:]

    dt = _softplus(dtr + dtb_ref[...])
    da = dt * (-jnp.exp(alog_ref[...]))
    acum = _time_strided_cumsum(da)
    alast = acum[q - 1:q, :]
    wgt = dt * jnp.exp(alast - acum)
    tot = jnp.exp(alast)
    acum_t = acum.T
    dt_t = dt.T

    ii = lax.broadcasted_iota(jnp.int32, (q, q), 0)
    jj = lax.broadcasted_iota(jnp.int32, (q, q), 1)
    time_of = lambda r: (r % SUBLANES) * slabs + r // SUBLANES
    causal = time_of(ii) >= time_of(jj)
    lane = lax.broadcasted_iota(jnp.int32, (q, LANES), 1)
    lo = lane < SSD_HEAD_DIM
    srow = lax.broadcasted_iota(jnp.int32, (LANES, SSD_STATE), 0) < SSD_HEAD_DIM

    cb = [_mm_nt(cm[:, g * SSD_STATE:(g + 1) * SSD_STATE], bm[:, g * SSD_STATE:(g + 1) * SSD_STATE])
          for g in range(SSD_GROUPS)]
    heads = d_ssd // SSD_HEAD_DIM
    pairs = heads // 2
    conv_groups = d_conv // MXU_WIDTH
    assert conv_groups < pairs
    y_parts = []
    for pq in range(pairs):
        g = (2 * pq) // (heads // SSD_GROUPS)
        ps = slice(pq * LANES, (pq + 1) * LANES)
        xq = sx[:, ps]
        ms, es, ws, ts = [], [], [], []
        for h in (2 * pq, 2 * pq + 1):
            acol = jnp.broadcast_to(acum[:, h:h + 1], (q, q))
            arow = jnp.broadcast_to(acum_t[h:h + 1, :], (q, q))
            decay = jnp.exp(jnp.where(causal, acol - arow, -jnp.inf))
            ms.append((cb[g] * decay * jnp.broadcast_to(dt_t[h:h + 1, :], (q, q))).astype(BF16))
            es.append(jnp.exp(jnp.broadcast_to(acum[:, h:h + 1], (q, LANES))))
            ws.append(jnp.broadcast_to(wgt[:, h:h + 1], (q, LANES)))
            ts.append(jnp.broadcast_to(tot[:, h:h + 1], (LANES, SSD_STATE)))
        lhs = jnp.concatenate(ms, axis=1)
        rhs = jnp.concatenate([jnp.where(lo, xq, 0.0), jnp.where(lo, 0.0, xq)], axis=0)
        y_diag = _mm(lhs, rhs)
        hq = h_scr[ps, :]
        y_off = _mm_nt(cm[:, g * SSD_STATE:(g + 1) * SSD_STATE], hq) * jnp.where(lo, es[0], es[1])
        xw = xq * jnp.where(lo, ws[0], ws[1])
        h_scr[ps, :] = hq * jnp.where(srow, ts[0], ts[1]) + _mm_tn(xw, bm[:, g * SSD_STATE:(g + 1) * SSD_STATE])
        y_parts.append(y_diag + y_off + dexp_ref[:, ps] * xq)
        if pq < conv_groups:
            gs = slice(pq * MXU_WIDTH, (pq + 1) * MXU_WIDTH)
            proj_scr[:, gs] = jnp.dot(xb_new, wxbc_ref[:, gs], preferred_element_type=F32)
        elif pq == conv_groups:
            proj_scr[:, d_conv:] = jnp.dot(xb_new, wdt_ref[...], preferred_element_type=F32)

    y = jnp.concatenate(y_parts, axis=1) * _silu(z)
    y = _group_rmsnorm(y, ng_ref[...]).astype(BF16)
    ssd_scr[...] = jnp.dot(permt_ref[...], y, preferred_element_type=F32).astype(BF16)

    @pl.when(jnp.logical_and(s % chunks_per_seq == 0, jnp.logical_and(s > 0, s <= n_chunks)))
    def _():
        h_ref[0] = h_scr[...]
        for i in range(CONV_W - 1):
            conv_ref[0, i:i + 1, :] = tail_scr[i, SUBLANES - 1:, :]


def _ssd_sample_kernel(x_ref, wxbc_ref, wz_ref, wdt_ref, cw_ref, cb_ref, dtb_ref, alog_ref, dexp_ref,
                       ng_ref, expand_ref, cstate_ref, h0_ref, out_ref, conv_ref, h_ref,
                       c_scr, b_scr, xw_scr, tot_scr, yoff_scr, *, valid):
    seqs = x_ref.shape[0]
    rows = seqs * SUBLANES
    d_ssd = wz_ref.shape[1]
    gn = SSD_GROUPS * SSD_STATE
    heads = d_ssd // SSD_HEAD_DIM
    hg = heads // SSD_GROUPS
    xb = x_ref[...].reshape(rows, x_ref.shape[2]).astype(BF16)
    xbc_raw = jnp.dot(xb, wxbc_ref[...], preferred_element_type=F32)
    z = jnp.dot(xb, wz_ref[...], preferred_element_type=F32)
    dtr = jnp.dot(xb, wdt_ref[...], preferred_element_type=F32)

    x3 = xbc_raw.reshape(seqs, SUBLANES, xbc_raw.shape[1])
    conv_ref[...] = x3[:, valid - (CONV_W - 1):valid, :]
    xbc3 = _silu(_causal_conv_tiles(x3, cstate_ref[...], cw_ref[...], cb_ref[...]))
    sx3 = xbc3[:, :, :d_ssd]
    b3 = xbc3[:, :, d_ssd:d_ssd + gn]
    c3 = xbc3[:, :, d_ssd + gn:]

    row = lax.broadcasted_iota(jnp.int32, (seqs, SUBLANES, LANES), 1)
    lane = lax.broadcasted_iota(jnp.int32, (seqs, SUBLANES, LANES), 2)
    dt3 = jnp.where(row < valid, _softplus(dtr + dtb_ref[...]).reshape(seqs, SUBLANES, LANES), 0.0)
    da3 = dt3 * (-jnp.exp(alog_ref[...]))[None]
    ones = jnp.ones_like(da3)
    _, acum3 = _scan_in_tiles(ones, da3)
    alast = acum3[:, SUBLANES - 1:, :]
    wgt3 = dt3 * jnp.exp(alast - acum3)
    tot_scr[...] = jnp.exp(alast)
    e3 = jnp.exp(acum3)

    coefs = []
    for u in range(valid):
        prod = c3 * b3[:, u:u + 1, :]
        cbu = [jnp.sum(prod[:, :, g * SSD_STATE:(g + 1) * SSD_STATE], axis=-1, keepdims=True)
               for g in range(SSD_GROUPS)]
        cb_heads = jnp.where(lane < hg, cbu[0], cbu[1])
        coefs.append(jnp.where(row >= u, cb_heads * jnp.exp(acum3 - acum3[:, u:u + 1, :]) * dt3[:, u:u + 1, :],
                               0.0))

    per_head = coefs + [wgt3, e3]
    stacked = jnp.concatenate([v.reshape(rows, LANES) for v in per_head], axis=0)
    expanded = None
    rest = stacked
    for _ in range(3):
        piece = rest.astype(BF16)
        rest = rest - piece.astype(F32)
        part = jnp.dot(piece, expand_ref[...], preferred_element_type=F32)
        expanded = part if expanded is None else expanded + part
    expanded = [expanded[i * rows:(i + 1) * rows].reshape(seqs, SUBLANES, d_ssd) for i in range(len(per_head))]
    y_diag = expanded[0] * sx3[:, 0:1, :]
    for u in range(1, valid):
        y_diag = y_diag + expanded[u] * sx3[:, u:u + 1, :]
    wgt_x, e_x = expanded[valid], expanded[valid + 1]

    c_scr[...] = c3
    b_scr[...] = b3
    xw_scr[...] = sx3 * wgt_x
    srow = lax.broadcasted_iota(jnp.int32, (2 * SSD_HEAD_DIM, SSD_STATE), 0) < SSD_HEAD_DIM

    def per_seq(s, carry):
        cs = c_scr[s]
        bs = b_scr[s]
        xws = xw_scr[s]
        tots = tot_scr[s]
        for g in range(SSD_GROUPS):
            gs = slice(g * SSD_STATE, (g + 1) * SSD_STATE)
            width = hg * SSD_HEAD_DIM
            cols = slice(g * width, (g + 1) * width)
            hin = h0_ref[s, cols, :]
            yoff_scr[s, :, cols] = _mm_nt(cs[:, gs], hin)
            upd = _mm_tn(xws[:, cols], bs[:, gs])
            for pq in range(hg // 2):
                h = g * hg + 2 * pq
                rs = slice(pq * LANES, (pq + 1) * LANES)
                t0 = jnp.broadcast_to(tots[0:1, h:h + 1], (LANES, SSD_STATE))
                t1 = jnp.broadcast_to(tots[0:1, h + 1:h + 2], (LANES, SSD_STATE))
                h_ref[s, g * width + pq * LANES:g * width + (pq + 1) * LANES, :] = (
                    hin[rs] * jnp.where(srow, t0, t1) + upd[rs])
        return carry

    lax.fori_loop(0, seqs, per_seq, 0)

    y = y_diag + yoff_scr[...] * e_x + dexp_ref[...][None] * sx3
    y = y * _silu(z.reshape(seqs, SUBLANES, d_ssd))
    out_ref[...] = _group_rmsnorm(y, ng_ref[...][None]).astype(out_ref.dtype)


def _memkv_kernel(mem_ref, wk_ref, wv_ref, k_ref, v_ref, kb_ref, vb_ref):
    mb = mem_ref[0].astype(BF16)
    for w_ref, o_ref, ob_ref in ((wk_ref, k_ref, kb_ref), (wv_ref, v_ref, vb_ref)):
        proj = jnp.dot(mb, w_ref[...], preferred_element_type=F32)
        o_ref[0] = proj.reshape(o_ref.shape[1:])
        ob_ref[0] = proj.astype(BF16)


def _softmax_terms(scores):
    p = jnp.exp(scores - jnp.max(scores, axis=-1, keepdims=True))
    return p, jnp.sum(p, axis=-1, keepdims=True)


def _prompt_attention_head(h, q_ref, rows, k_ref, v_ref, out_ref):
    c = k_ref.shape[2]
    d_head = c // MEM_HEADS
    hs = slice(h * d_head, (h + 1) * d_head)

    def scores():
        return _softmax_terms(_mm_nt(q_ref[:rows, hs], k_ref[0, :, hs]) * (d_head ** -0.5))

    def output(p, l):
        o = _mm(p, v_ref[0, :, hs]) * (1.0 / l)
        gate = q_ref[:rows, c + h * d_head:c + (h + 1) * d_head]
        out_ref[:, hs] = (o * _silu(gate)).astype(out_ref.dtype)

    return scores, output


def _sample_attention_seq(s, q_ref, rows, k_ref, v_ref, out_ref):
    _, n_mem, heads, d_head = k_ref.shape
    c = heads * d_head
    rs = slice(rows + s * SUBLANES, rows + (s + 1) * SUBLANES)

    def scores():
        shape = (heads * SUBLANES, n_mem * heads)
        same_head = (lax.broadcasted_iota(jnp.int32, shape, 0) // SUBLANES
                     == lax.broadcasted_iota(jnp.int32, shape, 1) % heads)
        qh = jnp.concatenate([q_ref[rs, h * d_head:(h + 1) * d_head] for h in range(heads)], axis=0)
        sc = _mm_nt(qh, k_ref[s].reshape(n_mem * heads, d_head)) * (d_head ** -0.5)
        return _softmax_terms(jnp.where(same_head, sc, -jnp.inf))

    def output(p, l):
        o = _mm(p, v_ref[s].reshape(n_mem * heads, d_head)) * (1.0 / l)
        o = jnp.concatenate([o[h * SUBLANES:(h + 1) * SUBLANES] for h in range(heads)], axis=1)
        out_ref[s] = (o * _silu(q_ref[rs, c:])).astype(out_ref.dtype)

    return scores, output


def _xattn_kernel(x_ref, xs_ref, wq_ref, wg_ref, k_ref, v_ref, ks_ref, vs_ref, out_ref, outs_ref):
    rows = x_ref.shape[0]
    seqs, _, d = xs_ref.shape
    xb = jnp.concatenate([x_ref[...], xs_ref[...].reshape(seqs * SUBLANES, d)], axis=0).astype(BF16)
    qg = jnp.concatenate([jnp.dot(xb, wq_ref[...], preferred_element_type=F32),
                          jnp.dot(xb, wg_ref[...], preferred_element_type=F32)], axis=1)
    items = [_prompt_attention_head(h, qg, rows, k_ref, v_ref, out_ref) for h in range(MEM_HEADS)]
    items += [_sample_attention_seq(i, qg, rows, ks_ref, vs_ref, outs_ref) for i in range(seqs)]
    for scores, output in items:
        output(*scores())


def _merge_kernel(rg_ref, ssd_ref, xa_ref, x_ref, w_ref, g_ref, b_ref, y_ref, *, sub_rows):
    d = rg_ref.shape[1]
    for r0 in range(0, x_ref.shape[0], sub_rows):
        rs = slice(r0, r0 + sub_rows)
        mix = (jnp.dot(rg_ref[rs, :].astype(BF16), w_ref[0:d, :], preferred_element_type=F32)
               + jnp.dot(ssd_ref[rs, :].astype(BF16), w_ref[d:2 * d, :], preferred_element_type=F32)
               + jnp.dot(xa_ref[rs, :].astype(BF16), w_ref[2 * d:3 * d, :], preferred_element_type=F32))
        res = DEEPNORM_ALPHA * x_ref[rs, :] + mix
        mu = jnp.mean(res, axis=-1, keepdims=True)
        cen = res - mu
        var = jnp.mean(cen * cen, axis=-1, keepdims=True)
        y_ref[rs, :] = cen * lax.rsqrt(var + LN_EPS) * g_ref[...] + b_ref[...]


def _full(shape):
    return pl.BlockSpec(shape, lambda *_: (0,) * len(shape))


def _time_stride_perm(rows):
    p = np.arange(rows)
    t = (p % SUBLANES) * (rows // SUBLANES) + p // SUBLANES
    perm = t[:, None] == np.arange(rows)[None, :]
    return jnp.asarray(perm, dtype=BF16), jnp.asarray(perm.T, dtype=BF16)


def _rg_prompt(x, p, tile):
    b, l, d = x.shape
    c = p["wx"].shape[1]
    tiles_per_seq = l // tile
    n_tiles = b * tiles_per_seq
    perm, perm_t = _time_stride_perm(tile)
    done = lambda s: jnp.maximum(s - 1, 0)
    out, conv, hlast = pl.pallas_call(
        functools.partial(_rg_prompt_kernel, tiles_per_seq=tiles_per_seq),
        grid=(n_tiles + 1,),
        in_specs=[pl.BlockSpec((tile, d), lambda s: (jnp.minimum(s, n_tiles - 1), 0)),
                  _full(perm.shape), _full(perm.shape), _full(p["wx"].shape), _full(p["wg"].shape),
                  _full(p["cw"].shape), _full(p["cb"].shape), _full(p["wgate"].shape), _full(p["ba"].shape),
                  _full(p["bi"].shape), _full(p["lam"].shape)],
        out_specs=[pl.BlockSpec((tile, c), lambda s: (done(s), 0)),
                   pl.BlockSpec((1, CONV_W - 1, c), lambda s: (done(s) // tiles_per_seq, 0, 0)),
                   pl.BlockSpec((1, 1, c), lambda s: (done(s) // tiles_per_seq, 0, 0))],
        out_shape=[jax.ShapeDtypeStruct((b * l, c), BF16),
                   jax.ShapeDtypeStruct((b, CONV_W - 1, c), F32),
                   jax.ShapeDtypeStruct((b, 1, c), F32)],
        scratch_shapes=[pltpu.VMEM((CONV_W - 1, SUBLANES, c), F32), pltpu.VMEM((1, c), F32),
                        pltpu.VMEM((tile, 2 * c), F32), pltpu.VMEM((tile, c), BF16)],
        compiler_params=_cparams("arbitrary"),
        name="rg_prompt",
    )(x.reshape(b * l, d), perm, perm_t, p["wx"], p["wg"], p["cw"], p["cb"], p["wgate"], p["ba"], p["bi"],
      p["lam"])
    return out.reshape(b, l, c), conv, hlast


def _rg_sample(xpad, p, cstate, h0, seqs, valid):
    n, _, d = xpad.shape
    c = p["wx"].shape[1]
    blk = lambda w: pl.BlockSpec((seqs, w[0], w[1]), lambda i: (i, 0, 0))
    return pl.pallas_call(
        functools.partial(_rg_sample_kernel, valid=valid),
        grid=(n // seqs,),
        in_specs=[blk((SUBLANES, d)), _full(p["wx"].shape), _full(p["wg"].shape), _full(p["cw"].shape),
                  _full(p["cb"].shape), _full(p["wgate"].shape), _full(p["ba"].shape),
                  _full(p["bi"].shape), _full(p["lam"].shape), blk((SUBLANES, c)), blk((1, c))],
        out_specs=[blk((SUBLANES, c)), blk((CONV_W - 1, c)), blk((1, c))],
        out_shape=[jax.ShapeDtypeStruct((n, SUBLANES, c), F32),
                   jax.ShapeDtypeStruct((n, CONV_W - 1, c), F32),
                   jax.ShapeDtypeStruct((n, 1, c), F32)],
        compiler_params=_cparams("parallel"),
        name="rg_sample",
    )(xpad, p["wx"], p["wg"], p["cw"], p["cb"], p["wgate"], p["ba"], p["bi"], p["lam"], cstate, h0)


def _ssd_param_specs(p):
    return [_full(p[k].shape) for k in ("wxbc", "wz", "wdt", "cw", "cb", "dtb", "alog", "dexp", "ng")]


def _ssd_param_args(p):
    return [p[k] for k in ("wxbc", "wz", "wdt", "cw", "cb", "dtb", "alog", "dexp", "ng")]


def _ssd_merge_prompt(x, p, rg_out, xa_out, mp):
    b, l, d = x.shape
    cc = p["wxbc"].shape[1]
    c = p["wz"].shape[1]
    chunks_per_seq = l // SSD_CHUNK
    n_chunks = b * chunks_per_seq
    perm, perm_t = _time_stride_perm(SSD_CHUNK)
    flat = lambda v: v.reshape(b * l, v.shape[-1])
    clamp = lambda i: jnp.clip(i, 0, n_chunks - 1)
    rows = lambda w, back: pl.BlockSpec((SSD_CHUNK, w), lambda s: (clamp(s - back), 0))
    state = lambda shape: pl.BlockSpec((1,) + shape, lambda s: (clamp(s - 1) // chunks_per_seq, 0, 0))
    y, conv, hstate = pl.pallas_call(
        functools.partial(_ssd_merge_prompt_kernel, chunks_per_seq=chunks_per_seq, n_chunks=n_chunks),
        grid=(n_chunks + 2,),
        in_specs=[rows(d, 0), _full(perm.shape), _full(perm.shape)] + _ssd_param_specs(p)
        + [rows(rg_out.shape[-1], 2), rows(xa_out.shape[-1], 2), rows(d, 2), _full(mp["w"].shape),
           _full(mp["g"].shape), _full(mp["b"].shape)],
        out_specs=[rows(d, 2), state((CONV_W - 1, cc)), state((c, SSD_STATE))],
        out_shape=[jax.ShapeDtypeStruct((b * l, d), F32),
                   jax.ShapeDtypeStruct((b, CONV_W - 1, cc), F32),
                   jax.ShapeDtypeStruct((b, c, SSD_STATE), F32)],
        scratch_shapes=[pltpu.VMEM((CONV_W - 1, SUBLANES, cc), F32),
                        pltpu.VMEM((SSD_CHUNK, cc + LANES), F32), pltpu.VMEM((SSD_CHUNK, d), BF16),
                        pltpu.VMEM((c, SSD_STATE), F32), pltpu.VMEM((SSD_CHUNK, c), BF16)],
        compiler_params=_cparams("arbitrary"),
        name="ssd_merge_prompt",
    )(flat(x), perm, perm_t, *_ssd_param_args(p), flat(rg_out), flat(xa_out), flat(x), mp["w"], mp["g"], mp["b"])
    return y.reshape(b, l, d), conv, hstate


def _ssd_sample(xpad, p, cstate, h0, seqs, valid):
    n, _, d = xpad.shape
    cc = p["wxbc"].shape[1]
    c = p["wz"].shape[1]
    gn = SSD_GROUPS * SSD_STATE
    blk = lambda w: pl.BlockSpec((seqs, w[0], w[1]), lambda i: (i, 0, 0))
    return pl.pallas_call(
        functools.partial(_ssd_sample_kernel, valid=valid),
        grid=(n // seqs,),
        in_specs=[blk((SUBLANES, d))] + _ssd_param_specs(p)
        + [_full(p["expand"].shape), blk((SUBLANES, cc)), blk((c, SSD_STATE))],
        out_specs=[blk((SUBLANES, c)), blk((CONV_W - 1, cc)), blk((c, SSD_STATE))],
        out_shape=[jax.ShapeDtypeStruct((n, SUBLANES, c), F32),
                   jax.ShapeDtypeStruct((n, CONV_W - 1, cc), F32),
                   jax.ShapeDtypeStruct((n, c, SSD_STATE), F32)],
        scratch_shapes=[pltpu.VMEM((seqs, SUBLANES, gn), F32), pltpu.VMEM((seqs, SUBLANES, gn), F32),
                        pltpu.VMEM((seqs, SUBLANES, c), F32), pltpu.VMEM((seqs, 1, LANES), F32),
                        pltpu.VMEM((seqs, SUBLANES, c), F32)],
        compiler_params=_cparams("parallel"),
        name="ssd_sample",
    )(xpad, *_ssd_param_args(p), p["expand"], cstate, h0)


def _memkv(mem, wk, wv):
    b, m, d = mem.shape
    c = wk.shape[1]
    spec = pl.BlockSpec((1, m, c), lambda i: (i, 0, 0))
    spec4 = pl.BlockSpec((1, m, MEM_HEADS, c // MEM_HEADS), lambda i: (i, 0, 0, 0))
    return pl.pallas_call(
        _memkv_kernel,
        grid=(b,),
        in_specs=[pl.BlockSpec((1, m, d), lambda i: (i, 0, 0)), _full(wk.shape), _full(wv.shape)],
        out_specs=[spec4, spec4, spec, spec],
        out_shape=[jax.ShapeDtypeStruct((b, m, MEM_HEADS, c // MEM_HEADS), F32)] * 2
        + [jax.ShapeDtypeStruct((b, m, c), BF16)] * 2,
        compiler_params=_cparams("parallel"),
        name="mem_kv",
    )(mem, wk, wv)


def _xattn(x, xs_pad, wq, wg, k, v, ks, vs, tile):
    b, l, d = x.shape
    n = xs_pad.shape[0]
    c = wq.shape[1]
    m = k.shape[1]
    tiles_per_seq = l // tile
    n_tiles = b * tiles_per_seq
    seqs = n // n_tiles
    assert seqs * n_tiles == n
    kv_spec = pl.BlockSpec((1, m, c), lambda s: (s // tiles_per_seq, 0, 0))
    skv_spec = pl.BlockSpec((seqs,) + ks.shape[1:], lambda s: (s, 0, 0, 0))
    out, outs = pl.pallas_call(
        _xattn_kernel,
        grid=(n_tiles,),
        in_specs=[pl.BlockSpec((tile, d), lambda s: (s, 0)),
                  pl.BlockSpec((seqs, SUBLANES, d), lambda s: (s, 0, 0)),
                  _full(wq.shape), _full(wg.shape), kv_spec, kv_spec, skv_spec, skv_spec],
        out_specs=[pl.BlockSpec((tile, c), lambda s: (s, 0)),
                   pl.BlockSpec((seqs, SUBLANES, c), lambda s: (s, 0, 0))],
        out_shape=[jax.ShapeDtypeStruct((b * l, c), BF16), jax.ShapeDtypeStruct((n, SUBLANES, c), F32)],
        compiler_params=_cparams("parallel"),
        name="xattn",
    )(x.reshape(b * l, d), xs_pad, wq, wg, k, v, ks, vs)
    return out.reshape(b, l, c), outs


def _merge(rg, ssd, xa, x, w_out, ln_g, ln_b, tile, name):
    n, d = x.shape
    c = rg.shape[1]
    tile = min(tile, n)
    row = lambda w: pl.BlockSpec((tile, w), lambda i: (i, 0))
    return pl.pallas_call(
        functools.partial(_merge_kernel, sub_rows=min(MERGE_SUB_ROWS, tile)),
        grid=(n // tile,),
        in_specs=[row(c), row(c), row(c), row(d), _full(w_out.shape), _full(ln_g.shape), _full(ln_b.shape)],
        out_specs=row(d),
        out_shape=jax.ShapeDtypeStruct((n, d), F32),
        compiler_params=_cparams("parallel"),
        name=name,
    )(rg, ssd, xa, x, w_out, ln_g, ln_b)


def _layer_params(w_in, rg_conv_w, rg_conv_b, w_rg_a, b_rg_a, w_rg_i, b_rg_i, rg_lambda, ssd_conv_w,
                  ssd_conv_b, ssd_dt_bias, ssd_a_log, ssd_d, ssd_norm_g, w_out, ln_g, ln_b):
    d_rg = rg_conv_w.shape[1]
    d_conv = ssd_conv_w.shape[1]
    d_ssd = ssd_norm_g.shape[0]
    heads = ssd_d.shape[0]
    sizes = (d_rg, d_rg, d_conv, d_ssd, heads)
    offs = [0]
    for s in sizes:
        offs.append(offs[-1] + s)
    d_xa = (w_in.shape[1] - offs[-1]) // 2
    wb = lambda lo, hi: w_in[:, lo:hi].astype(BF16)
    row = lambda v: v.reshape(1, -1).astype(F32)
    pad_lanes = lambda v: jnp.pad(v, ((0, 0), (0, LANES - v.shape[1])))
    rg = dict(wx=wb(offs[0], offs[1]), wg=wb(offs[1], offs[2]), cw=rg_conv_w, cb=row(rg_conv_b),
              wgate=jnp.concatenate([w_rg_a, w_rg_i], axis=2).astype(BF16),
              ba=row(b_rg_a), bi=row(b_rg_i), lam=row(rg_lambda))
    head_of_channel = np.arange(d_ssd) // SSD_HEAD_DIM
    ssd = dict(wxbc=wb(offs[2], offs[3]), wz=wb(offs[3], offs[4]), wdt=pad_lanes(wb(offs[4], offs[5])),
               cw=ssd_conv_w, cb=row(ssd_conv_b), dtb=pad_lanes(row(ssd_dt_bias)),
               alog=pad_lanes(row(ssd_a_log)), dexp=row(jnp.repeat(ssd_d, SSD_HEAD_DIM)), ng=row(ssd_norm_g),
               expand=jnp.asarray(np.arange(LANES)[:, None] == head_of_channel[None, :], dtype=BF16))
    xa = dict(wq=wb(offs[5], offs[5] + d_xa), wg=wb(offs[5] + d_xa, offs[5] + 2 * d_xa))
    merge = dict(w=w_out.astype(BF16), g=row(ln_g), b=row(ln_b))
    return rg, ssd, xa, merge


PROMPT_RG_TILE = 256
PROMPT_XA_TILE = 256
MERGE_TILE = 1024
MERGE_SUB_ROWS = 256
SAMPLE_RG_SEQS = 16
SAMPLE_SSD_SEQS = 16


def kernel(x_prompt, x_sample, mem_prompt, state_rg_conv, state_rg_h, state_ssd_conv, state_ssd_h,
           cache_mem_k, cache_mem_v, w_in, rg_conv_w, rg_conv_b, w_rg_a, b_rg_a, w_rg_i, b_rg_i,
           rg_lambda, ssd_conv_w, ssd_conv_b, ssd_dt_bias, ssd_a_log, ssd_d, ssd_norm_g, w_mem_k,
           w_mem_v, w_out, ln_g, ln_b):
    assert w_in.shape[0] == DEPTH
    bp, lp, d = x_prompt.shape
    bs, ls, _ = x_sample.shape
    heads = ssd_d.shape[1]
    outs = {k: [] for k in ("rgc_p", "rgh_p", "sc_p", "sh_p", "mk_p", "mv_p", "rgc_s", "rgh_s", "sc_s", "sh_s")}
    yp, ys = x_prompt, x_sample
    pad_rows = lambda v, before, after: jnp.pad(v, ((0, 0), (before, after), (0, 0)))
    for l in range(DEPTH):
        rg, ssd, xa, merge = _layer_params(
            w_in[l], rg_conv_w[l], rg_conv_b[l], w_rg_a[l], b_rg_a[l], w_rg_i[l], b_rg_i[l], rg_lambda[l],
            ssd_conv_w[l], ssd_conv_b[l], ssd_dt_bias[l], ssd_a_log[l], ssd_d[l], ssd_norm_g[l],
            w_out[l], ln_g[l], ln_b[l])
        xs_pad = pad_rows(ys, 0, SAMPLE_PAD - ls)
        mk, mv, mkb, mvb = _memkv(mem_prompt, w_mem_k[l].astype(BF16), w_mem_v[l].astype(BF16))
        rg_o, rgc, rgh = _rg_prompt(yp, rg, PROMPT_RG_TILE)
        xa_o, xa_s = _xattn(yp, xs_pad, xa["wq"], xa["wg"], mkb, mvb, cache_mem_k[l], cache_mem_v[l],
                            PROMPT_XA_TILE)
        yp, sc, sh = _ssd_merge_prompt(yp, ssd, rg_o, xa_o, merge)
        outs["rgc_p"].append(rgc)
        outs["rgh_p"].append(rgh.reshape(bp, -1))
        outs["sc_p"].append(sc)
        outs["sh_p"].append(sh.reshape(bp, heads, SSD_HEAD_DIM, SSD_STATE))
        outs["mk_p"].append(mk)
        outs["mv_p"].append(mv)
        tail = SAMPLE_PAD - (CONV_W - 1)
        rg_o, rgc, rgh = _rg_sample(xs_pad, rg, pad_rows(state_rg_conv[l], tail, 0),
                                    state_rg_h[l][:, None, :], SAMPLE_RG_SEQS, ls)
        ssd_o, sc, sh = _ssd_sample(xs_pad, ssd, pad_rows(state_ssd_conv[l], tail, 0),
                                    state_ssd_h[l].reshape(bs, heads * SSD_HEAD_DIM, SSD_STATE),
                                    SAMPLE_SSD_SEQS, ls)
        flat = lambda v: v.reshape(bs * SAMPLE_PAD, v.shape[-1])
        ys_pad = _merge(flat(rg_o), flat(ssd_o), flat(xa_s), flat(xs_pad), merge["w"], merge["g"], merge["b"],
                        MERGE_TILE, "merge_sample").reshape(bs, SAMPLE_PAD, d)
        ys = ys_pad[:, :ls, :]
        outs["rgc_s"].append(rgc)
        outs["rgh_s"].append(rgh.reshape(bs, -1))
        outs["sc_s"].append(sc)
        outs["sh_s"].append(sh.reshape(bs, heads, SSD_HEAD_DIM, SSD_STATE))
    st = lambda k: jnp.stack(outs[k])
    return (yp, ys, st("rgc_p"), st("rgh_p"), st("sc_p"), st("sh_p"), st("mk_p"), st("mv_p"),
            st("rgc_s"), st("rgh_s"), st("sc_s"), st("sh_s"))
```

```python
import functools

import jax
import jax.numpy as jnp
import numpy as np
from jax import lax
from jax.experimental import pallas as pl
from jax.experimental.pallas import tpu as pltpu

F32 = jnp.float32
BF16 = jnp.bfloat16

SUBLANES = 8
LANES = 128
MXU_WIDTH = 256
VMEM_LIMIT_BYTES = 56 * 1024 * 1024

RG_C = 8.0
CONV_W = 4
RG_BLOCKS = 8
SSD_HEAD_DIM = 64
SSD_GROUPS = 2
SSD_STATE = 128
SSD_CHUNK = 128
MEM_HEADS = 4
LN_EPS = 1e-5
RMS_EPS = 1e-5
DEPTH = 1
DEEPNORM_ALPHA = (2 * DEPTH) ** 0.25
SAMPLE_PAD = SUBLANES


def _cparams(*sem):
    return pltpu.CompilerParams(dimension_semantics=sem, vmem_limit_bytes=VMEM_LIMIT_BYTES)


def _mm(a, b):
    return jnp.dot(a.astype(BF16), b.astype(BF16), preferred_element_type=F32)


def _mm_nt(a, b):
    return lax.dot_general(a.astype(BF16), b.astype(BF16), (((1,), (1,)), ((), ())),
                           preferred_element_type=F32)


def _mm_tn(a, b):
    return lax.dot_general(a.astype(BF16), b.astype(BF16), (((0,), (0,)), ((), ())),
                           preferred_element_type=F32)


def _sigmoid(x):
    return 1.0 / (1.0 + jnp.exp(-x))


def _silu(x):
    return x * _sigmoid(x)


def _softplus(x):
    return jnp.maximum(x, 0.0) + jnp.log(1.0 + jnp.exp(-jnp.abs(x)))


def _causal_conv_tiles(x3, p3, w, b):
    row = lax.broadcasted_iota(jnp.int32, x3.shape, 1)
    y = x3 * w[CONV_W - 1:CONV_W][None]
    for s in range(1, CONV_W):
        shifted = jnp.where(row >= s, pltpu.roll(x3, s, axis=1), pltpu.roll(p3, s, axis=1))
        y = y + shifted * w[CONV_W - 1 - s:CONV_W - s][None]
    return y + b[None]


def _scan_in_tiles(a3, b3):
    row = lax.broadcasted_iota(jnp.int32, a3.shape, 1)
    s = 1
    while s < SUBLANES:
        keep = row >= s
        a_sh = jnp.where(keep, pltpu.roll(a3, s, axis=1), 1.0)
        b_sh = jnp.where(keep, pltpu.roll(b3, s, axis=1), 0.0)
        b3 = a3 * b_sh + b3
        a3 = a3 * a_sh
        s *= 2
    return a3, b3


def _rg_gates(u, wgate, ba, bi, lam):
    pre = _mm(u, wgate)
    r = _sigmoid(pre[:, :LANES] + ba)
    i = _sigmoid(pre[:, LANES:] + bi)
    neg_log_a = r * (RG_C * _softplus(-lam))
    a = jnp.exp(-neg_log_a)
    v = jnp.tanh(neg_log_a) * (1.0 + a * a)
    mult = jnp.where(v > 0.0, v * lax.rsqrt(v), 0.0)
    return a, mult * (i * u)


def _time_strided_conv(x3, tail3, w, b):
    slabs = x3.shape[0]
    row = lax.broadcasted_iota(jnp.int32, tail3.shape, 1)
    wrapped = jnp.where(row >= 1, pltpu.roll(x3[slabs - (CONV_W - 1):], 1, axis=1), pltpu.roll(tail3, 1, axis=1))
    y = x3 * w[CONV_W - 1:CONV_W][None]
    for s in range(1, CONV_W):
        shifted = jnp.concatenate([wrapped[CONV_W - 1 - s:], x3[:slabs - s]], axis=0)
        y = y + shifted * w[CONV_W - 1 - s:CONV_W - s][None]
    return y + b[None]


def _rg_prompt_kernel(x_ref, perm_ref, permt_ref, wx_ref, wg_ref, cw_ref, cb_ref, wgate_ref, ba_ref,
                      bi_ref, lam_ref, out_ref, conv_ref, hlast_ref, tail_scr, h_scr, proj_scr, outp_scr, *,
                      tiles_per_seq):
    s = pl.program_id(0)

    @pl.when(s == 0)
    def _():
        proj_scr[...] = jnp.zeros_like(proj_scr)

    @pl.when(jnp.logical_or(s == 0, s % tiles_per_seq == 1 % tiles_per_seq))
    def _():
        tail_scr[...] = jnp.zeros_like(tail_scr)
        h_scr[...] = jnp.zeros_like(h_scr)

    rows = x_ref.shape[0]
    slabs = rows // SUBLANES
    width = wx_ref.shape[1]
    per_group = MXU_WIDTH // LANES
    row = lax.broadcasted_iota(jnp.int32, (SUBLANES, LANES), 0)
    xb = jnp.dot(perm_ref[...], x_ref[...].astype(BF16), preferred_element_type=F32).astype(BF16)

    for cg in range(width // MXU_WIDTH):
        gs = slice(cg * MXU_WIDTH, (cg + 1) * MXU_WIDTH)
        for kk in range(per_group):
            k = cg * per_group + kk
            ks = slice(k * LANES, (k + 1) * LANES)
            x3 = proj_scr[:, ks].reshape(slabs, SUBLANES, LANES)
            tail3 = tail_scr[:, :, ks]
            last3 = x3[slabs - (CONV_W - 1):]
            tail_scr[:, :, ks] = last3
            for i in range(CONV_W - 1):
                conv_ref[0, i:i + 1, ks] = last3[i, SUBLANES - 1:, :]
            u3 = _time_strided_conv(x3, tail3, cw_ref[:, ks], cb_ref[:, ks])
            a, b = _rg_gates(u3.reshape(rows, LANES), wgate_ref[k], ba_ref[:, ks], bi_ref[:, ks], lam_ref[:, ks])
            a3 = a.reshape(slabs, SUBLANES, LANES)
            b3 = b.reshape(slabs, SUBLANES, LANES)
            h_loc, a_cum = [b3[0]], [a3[0]]
            for j in range(1, slabs):
                h_loc.append(a3[j] * h_loc[j - 1] + b3[j])
                a_cum.append(a3[j] * a_cum[j - 1])
            a_run, h_run = _scan_in_tiles(a_cum[-1][None], h_loc[-1][None])
            h_prev = h_scr[:, ks]
            h_end = h_run[0] + a_run[0] * h_prev
            h_in = jnp.where(row >= 1, pltpu.roll(h_end, 1, axis=0), h_prev)
            h_scr[:, ks] = h_end[SUBLANES - 1:]
            hlast_ref[0, :, ks] = h_end[SUBLANES - 1:]
            h = jnp.concatenate([h_loc[j] + a_cum[j] * h_in for j in range(slabs)], axis=0)
            gate = proj_scr[:, width + k * LANES:width + (k + 1) * LANES]
            outp_scr[:, ks] = (h * _silu(gate)).astype(BF16)
        proj_scr[:, gs] = jnp.dot(xb, wx_ref[:, gs], preferred_element_type=F32)
        proj_scr[:, width + cg * MXU_WIDTH:width + (cg + 1) * MXU_WIDTH] = jnp.dot(
            xb, wg_ref[:, gs], preferred_element_type=F32)

    out_ref[...] = jnp.dot(permt_ref[...], outp_scr[...], preferred_element_type=F32).astype(out_ref.dtype)


def _rg_sample_kernel(x_ref, wx_ref, wg_ref, cw_ref, cb_ref, wgate_ref, ba_ref, bi_ref, lam_ref,
                      cstate_ref, h0_ref, out_ref, conv_ref, hlast_ref, *, valid):
    seqs = x_ref.shape[0]
    rows = seqs * SUBLANES
    xb = x_ref[...].reshape(rows, x_ref.shape[2]).astype(BF16)
    rgx = jnp.dot(xb, wx_ref[...], preferred_element_type=F32)
    gate = jnp.dot(xb, wg_ref[...], preferred_element_type=F32)
    x3 = rgx.reshape(seqs, SUBLANES, rgx.shape[1])
    conv_ref[...] = x3[:, valid - (CONV_W - 1):valid, :]
    row = lax.broadcasted_iota(jnp.int32, (seqs, SUBLANES, LANES), 1)
    for k in range(RG_BLOCKS):
        ks = slice(k * LANES, (k + 1) * LANES)
        u3 = _causal_conv_tiles(x3[:, :, ks], cstate_ref[:, :, ks], cw_ref[:, ks], cb_ref[:, ks])
        u = u3.reshape(rows, LANES)
        a, b = _rg_gates(u, wgate_ref[k], ba_ref[:, ks], bi_ref[:, ks], lam_ref[:, ks])
        a3 = a.reshape(seqs, SUBLANES, LANES)
        b3 = b.reshape(seqs, SUBLANES, LANES)
        b3 = b3 + jnp.where(row == 0, a3 * h0_ref[:, :, ks], 0.0)
        _, h3 = _scan_in_tiles(a3, b3)
        hlast_ref[:, :, ks] = h3[:, valid - 1:valid, :]
        g3 = gate[:, ks].reshape(seqs, SUBLANES, LANES)
        out_ref[:, :, ks] = (h3 * _silu(g3)).astype(out_ref.dtype)


def _cumsum_rows(x):
    rows = x.shape[0]
    row = lax.broadcasted_iota(jnp.int32, x.shape, 0)
    s = 1
    while s < rows:
        x = x + jnp.where(row >= s, pltpu.roll(x, s, axis=0), 0.0)
        s *= 2
    return x


def _group_rmsnorm(y, gain):
    width = y.shape[-1] // SSD_GROUPS
    parts = []
    for g in range(SSD_GROUPS):
        yg = y[..., g * width:(g + 1) * width]
        ms = jnp.sum(yg * yg, axis=-1, keepdims=True) * (1.0 / width)
        parts.append(yg * lax.rsqrt(ms + RMS_EPS))
    return jnp.concatenate(parts, axis=-1) * gain


def _time_strided_cumsum(x):
    slabs = x.shape[0] // SUBLANES
    x3 = x.reshape(slabs, SUBLANES, x.shape[1])
    acc = [x3[0]]
    for j in range(1, slabs):
        acc.append(acc[j - 1] + x3[j])
    _, run = _scan_in_tiles(jnp.ones_like(acc[-1])[None], acc[-1][None])
    row = lax.broadcasted_iota(jnp.int32, run[0].shape, 0)
    before = jnp.where(row >= 1, pltpu.roll(run[0], 1, axis=0), 0.0)
    return jnp.concatenate([a + before for a in acc], axis=0)


def _ssd_merge_prompt_kernel(x_ref, perm_ref, permt_ref, wxbc_ref, wz_ref, wdt_ref, cw_ref, cb_ref, dtb_ref,
                             alog_ref, dexp_ref, ng_ref, rg_ref, xa_ref, xres_ref, wout_ref, lng_ref, lnb_ref,
                             y_ref, conv_ref, h_ref, tail_scr, proj_scr, xb_scr, h_scr, ssd_scr, *,
                             chunks_per_seq, n_chunks):
    s = pl.program_id(0)

    @pl.when(s == 0)
    def _():
        proj_scr[...] = jnp.zeros_like(proj_scr)
        xb_scr[...] = jnp.zeros_like(xb_scr)
        ssd_scr[...] = jnp.zeros_like(ssd_scr)

    @pl.when(jnp.logical_or(s == 0, s % chunks_per_seq == 1 % chunks_per_seq))
    def _():
        tail_scr[...] = jnp.zeros_like(tail_scr)
        h_scr[...] = jnp.zeros_like(h_scr)

    q = x_ref.shape[0]
    slabs = q // SUBLANES
    d_ssd = wz_ref.shape[1]
    d_conv = wxbc_ref.shape[1]
    gn = SSD_GROUPS * SSD_STATE
    mix, row0 = None, 0
    for src in (rg_ref, ssd_scr, xa_ref):
        part = jnp.dot(src[...], wout_ref[row0:row0 + src.shape[1], :], preferred_element_type=F32)
        mix = part if mix is None else mix + part
        row0 += src.shape[1]
    res = DEEPNORM_ALPHA * xres_ref[...] + mix
    mu = jnp.mean(res, axis=-1, keepdims=True)
    cen = res - mu
    var = jnp.mean(cen * cen, axis=-1, keepdims=True)
    y_ref[...] = cen * lax.rsqrt(var + LN_EPS) * lng_ref[...] + lnb_ref[...]

    xb_new = jnp.dot(perm_ref[...], x_ref[...].astype(BF16), preferred_element_type=F32).astype(BF16)
    z = jnp.dot(xb_scr[...], wz_ref[...], preferred_element_type=F32)
    xb_scr[...] = xb_new

    x3 = proj_scr[:, :d_conv].reshape(slabs, SUBLANES, d_conv)
    dtr = proj_scr[:, d_conv:]
    tail3 = tail_scr[...]
    tail_scr[...] = x3[slabs - (CONV_W - 1):]
    xbc = _silu(_time_strided_conv(x3, tail3, cw_ref[...], cb_ref[...])).reshape(q, d_conv)
    sx = xbc[:, :d_ssd]
    bm = xbc[:, d_ssd:d_ssd + gn]
    cm = xbc[:, d_ssd + gn:]

    dt = _softplus(dtr + dtb_ref[...])
    da = dt * (-jnp.exp(alog_ref[...]))
    acum = _time_strided_cumsum(da)
    alast = acum[q - 1:q, :]
    wgt = dt * jnp.exp(alast - acum)
    tot = jnp.exp(alast)
    acum_t = acum.T
    dt_t = dt.T

    ii = lax.broadcasted_iota(jnp.int32, (q, q), 0)
    jj = lax.broadcasted_iota(jnp.int32, (q, q), 1)
    time_of = lambda r: (r % SUBLANES) * slabs + r // SUBLANES
    causal = time_of(ii) >= time_of(jj)
    lane = lax.broadcasted_iota(jnp.int32, (q, LANES), 1)
    lo = lane < SSD_HEAD_DIM
    srow = lax.broadcasted_iota(jnp.int32, (LANES, SSD_STATE), 0) < SSD_HEAD_DIM

    cb = [_mm_nt(cm[:, g * SSD_STATE:(g + 1) * SSD_STATE], bm[:, g * SSD_STATE:(g + 1) * SSD_STATE])
          for g in range(SSD_GROUPS)]
    heads = d_ssd // SSD_HEAD_DIM
    pairs = heads // 2
    conv_groups = d_conv // MXU_WIDTH
    assert conv_groups < pairs
    pairs_per_group = pairs // SSD_GROUPS
    y_parts = []
    for g in range(SSD_GROUPS):
        ns = slice(g * SSD_STATE, (g + 1) * SSD_STATE)
        rows_g = slice(g * pairs_per_group * LANES, (g + 1) * pairs_per_group * LANES)
        h_in = h_scr[rows_g, :]
        y_off_g = _mm_nt(cm[:, ns], h_in)
        xws, decays = [], []
        for pl_ in range(pairs_per_group):
            pq = g * pairs_per_group + pl_
            ps = slice(pq * LANES, (pq + 1) * LANES)
            xq = sx[:, ps]
            ms, es, ws, ts = [], [], [], []
            for h in (2 * pq, 2 * pq + 1):
                acol = jnp.broadcast_to(acum[:, h:h + 1], (q, q))
                arow = jnp.broadcast_to(acum_t[h:h + 1, :], (q, q))
                decay = jnp.exp(jnp.where(causal, acol - arow, -jnp.inf))
                ms.append((cb[g] * decay * jnp.broadcast_to(dt_t[h:h + 1, :], (q, q))).astype(BF16))
                es.append(jnp.exp(jnp.broadcast_to(acum[:, h:h + 1], (q, LANES))))
                ws.append(jnp.broadcast_to(wgt[:, h:h + 1], (q, LANES)))
                ts.append(jnp.broadcast_to(tot[:, h:h + 1], (LANES, SSD_STATE)))
            lhs = jnp.concatenate(ms, axis=1)
            rhs = jnp.concatenate([jnp.where(lo, xq, 0.0), jnp.where(lo, 0.0, xq)], axis=0)
            y_diag = _mm(lhs, rhs)
            y_off = y_off_g[:, pl_ * LANES:(pl_ + 1) * LANES] * jnp.where(lo, es[0], es[1])
            xws.append(xq * jnp.where(lo, ws[0], ws[1]))
            decays.append(jnp.where(srow, ts[0], ts[1]))
            y_parts.append(y_diag + y_off + dexp_ref[:, ps] * xq)
            if pq < conv_groups:
                gs = slice(pq * MXU_WIDTH, (pq + 1) * MXU_WIDTH)
                proj_scr[:, gs] = jnp.dot(xb_new, wxbc_ref[:, gs], preferred_element_type=F32)
            elif pq == conv_groups:
                proj_scr[:, d_conv:] = jnp.dot(xb_new, wdt_ref[...], preferred_element_type=F32)
        h_scr[rows_g, :] = (h_in * jnp.concatenate(decays, axis=0)
                            + _mm_tn(jnp.concatenate(xws, axis=1), bm[:, ns]))

    y = jnp.concatenate(y_parts, axis=1) * _silu(z)
    y = _group_rmsnorm(y, ng_ref[...]).astype(BF16)
    ssd_scr[...] = jnp.dot(permt_ref[...], y, preferred_element_type=F32).astype(BF16)

    @pl.when(jnp.logical_and(s % chunks_per_seq == 0, jnp.logical_and(s > 0, s <= n_chunks)))
    def _():
        h_ref[0] = h_scr[...]
        for i in range(CONV_W - 1):
            conv_ref[0, i:i + 1, :] = tail_scr[i, SUBLANES - 1:, :]


def _ssd_sample_kernel(x_ref, wxbc_ref, wz_ref, wdt_ref, cw_ref, cb_ref, dtb_ref, alog_ref, dexp_ref,
                       ng_ref, expand_ref, cstate_ref, h0_ref, out_ref, conv_ref, h_ref,
                       c_scr, b_scr, xw_scr, tot_scr, yoff_scr, *, valid):
    seqs = x_ref.shape[0]
    rows = seqs * SUBLANES
    d_ssd = wz_ref.shape[1]
    gn = SSD_GROUPS * SSD_STATE
    heads = d_ssd // SSD_HEAD_DIM
    hg = heads // SSD_GROUPS
    xb = x_ref[...].reshape(rows, x_ref.shape[2]).astype(BF16)
    xbc_raw = jnp.dot(xb, wxbc_ref[...], preferred_element_type=F32)
    z = jnp.dot(xb, wz_ref[...], preferred_element_type=F32)
    dtr = jnp.dot(xb, wdt_ref[...], preferred_element_type=F32)

    x3 = xbc_raw.reshape(seqs, SUBLANES, xbc_raw.shape[1])
    conv_ref[...] = x3[:, valid - (CONV_W - 1):valid, :]
    xbc3 = _silu(_causal_conv_tiles(x3, cstate_ref[...], cw_ref[...], cb_ref[...]))
    sx3 = xbc3[:, :, :d_ssd]
    b3 = xbc3[:, :, d_ssd:d_ssd + gn]
    c3 = xbc3[:, :, d_ssd + gn:]

    row = lax.broadcasted_iota(jnp.int32, (seqs, SUBLANES, LANES), 1)
    lane = lax.broadcasted_iota(jnp.int32, (seqs, SUBLANES, LANES), 2)
    dt3 = jnp.where(row < valid, _softplus(dtr + dtb_ref[...]).reshape(seqs, SUBLANES, LANES), 0.0)
    da3 = dt3 * (-jnp.exp(alog_ref[...]))[None]
    ones = jnp.ones_like(da3)
    _, acum3 = _scan_in_tiles(ones, da3)
    alast = acum3[:, SUBLANES - 1:, :]
    wgt3 = dt3 * jnp.exp(alast - acum3)
    tot_scr[...] = jnp.exp(alast)
    e3 = jnp.exp(acum3)

    coefs = []
    for u in range(valid):
        prod = c3 * b3[:, u:u + 1, :]
        cbu = [jnp.sum(prod[:, :, g * SSD_STATE:(g + 1) * SSD_STATE], axis=-1, keepdims=True)
               for g in range(SSD_GROUPS)]
        cb_heads = jnp.where(lane < hg, cbu[0], cbu[1])
        coefs.append(jnp.where(row >= u, cb_heads * jnp.exp(acum3 - acum3[:, u:u + 1, :]) * dt3[:, u:u + 1, :],
                               0.0))

    per_head = coefs + [wgt3, e3]
    stacked = jnp.concatenate([v.reshape(rows, LANES) for v in per_head], axis=0)
    expanded = None
    rest = stacked
    for _ in range(3):
        piece = rest.astype(BF16)
        rest = rest - piece.astype(F32)
        part = jnp.dot(piece, expand_ref[...], preferred_element_type=F32)
        expanded = part if expanded is None else expanded + part
    expanded = [expanded[i * rows:(i + 1) * rows].reshape(seqs, SUBLANES, d_ssd) for i in range(len(per_head))]
    y_diag = expanded[0] * sx3[:, 0:1, :]
    for u in range(1, valid):
        y_diag = y_diag + expanded[u] * sx3[:, u:u + 1, :]
    wgt_x, e_x = expanded[valid], expanded[valid + 1]

    c_scr[...] = c3
    b_scr[...] = b3
    xw_scr[...] = sx3 * wgt_x
    srow = lax.broadcasted_iota(jnp.int32, (2 * SSD_HEAD_DIM, SSD_STATE), 0) < SSD_HEAD_DIM

    def per_seq(s, carry):
        cs = c_scr[s]
        bs = b_scr[s]
        xws = xw_scr[s]
        tots = tot_scr[s]
        for g in range(SSD_GROUPS):
            gs = slice(g * SSD_STATE, (g + 1) * SSD_STATE)
            width = hg * SSD_HEAD_DIM
            cols = slice(g * width, (g + 1) * width)
            hin = h0_ref[s, cols, :]
            yoff_scr[s, :, cols] = _mm_nt(cs[:, gs], hin)
            upd = _mm_tn(xws[:, cols], bs[:, gs])
            for pq in range(hg // 2):
                h = g * hg + 2 * pq
                rs = slice(pq * LANES, (pq + 1) * LANES)
                t0 = jnp.broadcast_to(tots[0:1, h:h + 1], (LANES, SSD_STATE))
                t1 = jnp.broadcast_to(tots[0:1, h + 1:h + 2], (LANES, SSD_STATE))
                h_ref[s, g * width + pq * LANES:g * width + (pq + 1) * LANES, :] = (
                    hin[rs] * jnp.where(srow, t0, t1) + upd[rs])
        return carry

    lax.fori_loop(0, seqs, per_seq, 0)

    y = y_diag + yoff_scr[...] * e_x + dexp_ref[...][None] * sx3
    y = y * _silu(z.reshape(seqs, SUBLANES, d_ssd))
    out_ref[...] = _group_rmsnorm(y, ng_ref[...][None]).astype(out_ref.dtype)


def _memkv_kernel(mem_ref, wk_ref, wv_ref, k_ref, v_ref, kb_ref, vb_ref):
    nb, m, d = mem_ref.shape
    mb = mem_ref[...].reshape(nb * m, d).astype(BF16)
    for w_ref, o_ref, ob_ref in ((wk_ref, k_ref, kb_ref), (wv_ref, v_ref, vb_ref)):
        proj = jnp.dot(mb, w_ref[...], preferred_element_type=F32)
        for i in range(nb):
            o_ref[i] = proj[i * m:(i + 1) * m].reshape(o_ref.shape[1:])
        ob_ref[...] = proj.astype(BF16).reshape(ob_ref.shape)


def _softmax_terms(scores):
    p = jnp.exp(scores - jnp.max(scores, axis=-1, keepdims=True))
    return p, jnp.sum(p, axis=-1, keepdims=True)


def _prompt_attention_head(h, q_ref, rows, k_ref, v_ref, out_ref):
    c = k_ref.shape[2]
    d_head = c // MEM_HEADS
    hs = slice(h * d_head, (h + 1) * d_head)

    def scores():
        return _softmax_terms(_mm_nt(q_ref[:rows, hs], k_ref[0, :, hs]) * (d_head ** -0.5))

    def output(p, l):
        o = _mm(p, v_ref[0, :, hs]) * (1.0 / l)
        gate = q_ref[:rows, c + h * d_head:c + (h + 1) * d_head]
        out_ref[:, hs] = (o * _silu(gate)).astype(out_ref.dtype)

    return scores, output


def _sample_attention_seq(s, q_ref, rows, k_ref, v_ref, out_ref):
    _, n_mem, heads, d_head = k_ref.shape
    c = heads * d_head
    rs = slice(rows + s * SUBLANES, rows + (s + 1) * SUBLANES)

    def scores():
        shape = (heads * SUBLANES, n_mem * heads)
        same_head = (lax.broadcasted_iota(jnp.int32, shape, 0) // SUBLANES
                     == lax.broadcasted_iota(jnp.int32, shape, 1) % heads)
        qh = jnp.concatenate([q_ref[rs, h * d_head:(h + 1) * d_head] for h in range(heads)], axis=0)
        sc = _mm_nt(qh, k_ref[s].reshape(n_mem * heads, d_head)) * (d_head ** -0.5)
        return _softmax_terms(jnp.where(same_head, sc, -jnp.inf))

    def output(p, l):
        o = _mm(p, v_ref[s].reshape(n_mem * heads, d_head)) * (1.0 / l)
        o = jnp.concatenate([o[h * SUBLANES:(h + 1) * SUBLANES] for h in range(heads)], axis=1)
        out_ref[s] = (o * _silu(q_ref[rs, c:])).astype(out_ref.dtype)

    return scores, output


def _xattn_kernel(x_ref, xs_ref, wq_ref, wg_ref, k_ref, v_ref, ks_ref, vs_ref, out_ref, outs_ref):
    rows = x_ref.shape[0]
    seqs, _, d = xs_ref.shape
    xb = jnp.concatenate([x_ref[...], xs_ref[...].reshape(seqs * SUBLANES, d)], axis=0).astype(BF16)
    qg = jnp.concatenate([jnp.dot(xb, wq_ref[...], preferred_element_type=F32),
                          jnp.dot(xb, wg_ref[...], preferred_element_type=F32)], axis=1)
    items = [_prompt_attention_head(h, qg, rows, k_ref, v_ref, out_ref) for h in range(MEM_HEADS)]
    items += [_sample_attention_seq(i, qg, rows, ks_ref, vs_ref, outs_ref) for i in range(seqs)]
    for scores, output in items:
        output(*scores())


def _merge_kernel(rg_ref, ssd_ref, xa_ref, x_ref, w_ref, g_ref, b_ref, y_ref, *, sub_rows):
    d = rg_ref.shape[1]
    for r0 in range(0, x_ref.shape[0], sub_rows):
        rs = slice(r0, r0 + sub_rows)
        mix = (jnp.dot(rg_ref[rs, :].astype(BF16), w_ref[0:d, :], preferred_element_type=F32)
               + jnp.dot(ssd_ref[rs, :].astype(BF16), w_ref[d:2 * d, :], preferred_element_type=F32)
               + jnp.dot(xa_ref[rs, :].astype(BF16), w_ref[2 * d:3 * d, :], preferred_element_type=F32))
        res = DEEPNORM_ALPHA * x_ref[rs, :] + mix
        mu = jnp.mean(res, axis=-1, keepdims=True)
        cen = res - mu
        var = jnp.mean(cen * cen, axis=-1, keepdims=True)
        y_ref[rs, :] = cen * lax.rsqrt(var + LN_EPS) * g_ref[...] + b_ref[...]


def _full(shape):
    return pl.BlockSpec(shape, lambda *_: (0,) * len(shape))


def _time_stride_perm(rows):
    p = np.arange(rows)
    t = (p % SUBLANES) * (rows // SUBLANES) + p // SUBLANES
    perm = t[:, None] == np.arange(rows)[None, :]
    return jnp.asarray(perm, dtype=BF16), jnp.asarray(perm.T, dtype=BF16)


def _rg_prompt(x, p, tile):
    b, l, d = x.shape
    c = p["wx"].shape[1]
    tiles_per_seq = l // tile
    n_tiles = b * tiles_per_seq
    perm, perm_t = _time_stride_perm(tile)
    done = lambda s: jnp.maximum(s - 1, 0)
    out, conv, hlast = pl.pallas_call(
        functools.partial(_rg_prompt_kernel, tiles_per_seq=tiles_per_seq),
        grid=(n_tiles + 1,),
        in_specs=[pl.BlockSpec((tile, d), lambda s: (jnp.minimum(s, n_tiles - 1), 0)),
                  _full(perm.shape), _full(perm.shape), _full(p["wx"].shape), _full(p["wg"].shape),
                  _full(p["cw"].shape), _full(p["cb"].shape), _full(p["wgate"].shape), _full(p["ba"].shape),
                  _full(p["bi"].shape), _full(p["lam"].shape)],
        out_specs=[pl.BlockSpec((tile, c), lambda s: (done(s), 0)),
                   pl.BlockSpec((1, CONV_W - 1, c), lambda s: (done(s) // tiles_per_seq, 0, 0)),
                   pl.BlockSpec((1, 1, c), lambda s: (done(s) // tiles_per_seq, 0, 0))],
        out_shape=[jax.ShapeDtypeStruct((b * l, c), BF16),
                   jax.ShapeDtypeStruct((b, CONV_W - 1, c), F32),
                   jax.ShapeDtypeStruct((b, 1, c), F32)],
        scratch_shapes=[pltpu.VMEM((CONV_W - 1, SUBLANES, c), F32), pltpu.VMEM((1, c), F32),
                        pltpu.VMEM((tile, 2 * c), F32), pltpu.VMEM((tile, c), BF16)],
        compiler_params=_cparams("arbitrary"),
        name="rg_prompt",
    )(x.reshape(b * l, d), perm, perm_t, p["wx"], p["wg"], p["cw"], p["cb"], p["wgate"], p["ba"], p["bi"],
      p["lam"])
    return out.reshape(b, l, c), conv, hlast


def _rg_sample(xpad, p, cstate, h0, seqs, valid):
    n, _, d = xpad.shape
    c = p["wx"].shape[1]
    blk = lambda w: pl.BlockSpec((seqs, w[0], w[1]), lambda i: (i, 0, 0))
    return pl.pallas_call(
        functools.partial(_rg_sample_kernel, valid=valid),
        grid=(n // seqs,),
        in_specs=[blk((SUBLANES, d)), _full(p["wx"].shape), _full(p["wg"].shape), _full(p["cw"].shape),
                  _full(p["cb"].shape), _full(p["wgate"].shape), _full(p["ba"].shape),
                  _full(p["bi"].shape), _full(p["lam"].shape), blk((SUBLANES, c)), blk((1, c))],
        out_specs=[blk((SUBLANES, c)), blk((CONV_W - 1, c)), blk((1, c))],
        out_shape=[jax.ShapeDtypeStruct((n, SUBLANES, c), F32),
                   jax.ShapeDtypeStruct((n, CONV_W - 1, c), F32),
                   jax.ShapeDtypeStruct((n, 1, c), F32)],
        compiler_params=_cparams("parallel"),
        name="rg_sample",
    )(xpad, p["wx"], p["wg"], p["cw"], p["cb"], p["wgate"], p["ba"], p["bi"], p["lam"], cstate, h0)


def _ssd_param_specs(p):
    return [_full(p[k].shape) for k in ("wxbc", "wz", "wdt", "cw", "cb", "dtb", "alog", "dexp", "ng")]


def _ssd_param_args(p):
    return [p[k] for k in ("wxbc", "wz", "wdt", "cw", "cb", "dtb", "alog", "dexp", "ng")]


def _ssd_merge_prompt(x, p, rg_out, xa_out, mp):
    b, l, d = x.shape
    cc = p["wxbc"].shape[1]
    c = p["wz"].shape[1]
    chunks_per_seq = l // SSD_CHUNK
    n_chunks = b * chunks_per_seq
    perm, perm_t = _time_stride_perm(SSD_CHUNK)
    flat = lambda v: v.reshape(b * l, v.shape[-1])
    clamp = lambda i: jnp.clip(i, 0, n_chunks - 1)
    rows = lambda w, back: pl.BlockSpec((SSD_CHUNK, w), lambda s: (clamp(s - back), 0))
    state = lambda shape: pl.BlockSpec((1,) + shape, lambda s: (clamp(s - 1) // chunks_per_seq, 0, 0))
    y, conv, hstate = pl.pallas_call(
        functools.partial(_ssd_merge_prompt_kernel, chunks_per_seq=chunks_per_seq, n_chunks=n_chunks),
        grid=(n_chunks + 2,),
        in_specs=[rows(d, 0), _full(perm.shape), _full(perm.shape)] + _ssd_param_specs(p)
        + [rows(rg_out.shape[-1], 2), rows(xa_out.shape[-1], 2), rows(d, 2), _full(mp["w"].shape),
           _full(mp["g"].shape), _full(mp["b"].shape)],
        out_specs=[rows(d, 2), state((CONV_W - 1, cc)), state((c, SSD_STATE))],
        out_shape=[jax.ShapeDtypeStruct((b * l, d), F32),
                   jax.ShapeDtypeStruct((b, CONV_W - 1, cc), F32),
                   jax.ShapeDtypeStruct((b, c, SSD_STATE), F32)],
        scratch_shapes=[pltpu.VMEM((CONV_W - 1, SUBLANES, cc), F32),
                        pltpu.VMEM((SSD_CHUNK, cc + LANES), F32), pltpu.VMEM((SSD_CHUNK, d), BF16),
                        pltpu.VMEM((c, SSD_STATE), F32), pltpu.VMEM((SSD_CHUNK, c), BF16)],
        compiler_params=_cparams("arbitrary"),
        name="ssd_merge_prompt",
    )(flat(x), perm, perm_t, *_ssd_param_args(p), flat(rg_out), flat(xa_out), flat(x), mp["w"], mp["g"], mp["b"])
    return y.reshape(b, l, d), conv, hstate


def _ssd_sample(xpad, p, cstate, h0, seqs, valid):
    n, _, d = xpad.shape
    cc = p["wxbc"].shape[1]
    c = p["wz"].shape[1]
    gn = SSD_GROUPS * SSD_STATE
    blk = lambda w: pl.BlockSpec((seqs, w[0], w[1]), lambda i: (i, 0, 0))
    return pl.pallas_call(
        functools.partial(_ssd_sample_kernel, valid=valid),
        grid=(n // seqs,),
        in_specs=[blk((SUBLANES, d))] + _ssd_param_specs(p)
        + [_full(p["expand"].shape), blk((SUBLANES, cc)), blk((c, SSD_STATE))],
        out_specs=[blk((SUBLANES, c)), blk((CONV_W - 1, cc)), blk((c, SSD_STATE))],
        out_shape=[jax.ShapeDtypeStruct((n, SUBLANES, c), F32),
                   jax.ShapeDtypeStruct((n, CONV_W - 1, cc), F32),
                   jax.ShapeDtypeStruct((n, c, SSD_STATE), F32)],
        scratch_shapes=[pltpu.VMEM((seqs, SUBLANES, gn), F32), pltpu.VMEM((seqs, SUBLANES, gn), F32),
                        pltpu.VMEM((seqs, SUBLANES, c), F32), pltpu.VMEM((seqs, 1, LANES), F32),
                        pltpu.VMEM((seqs, SUBLANES, c), F32)],
        compiler_params=_cparams("parallel"),
        name="ssd_sample",
    )(xpad, *_ssd_param_args(p), p["expand"], cstate, h0)


def _memkv(mem, wk, wv):
    b, m, d = mem.shape
    c = wk.shape[1]
    nb = MEMKV_BATCHES if b % MEMKV_BATCHES == 0 else 1
    spec = pl.BlockSpec((nb, m, c), lambda i: (i, 0, 0))
    spec4 = pl.BlockSpec((nb, m, MEM_HEADS, c // MEM_HEADS), lambda i: (i, 0, 0, 0))
    return pl.pallas_call(
        _memkv_kernel,
        grid=(b // nb,),
        in_specs=[pl.BlockSpec((nb, m, d), lambda i: (i, 0, 0)), _full(wk.shape), _full(wv.shape)],
        out_specs=[spec4, spec4, spec, spec],
        out_shape=[jax.ShapeDtypeStruct((b, m, MEM_HEADS, c // MEM_HEADS), F32)] * 2
        + [jax.ShapeDtypeStruct((b, m, c), BF16)] * 2,
        compiler_params=_cparams("parallel"),
        name="mem_kv",
    )(mem, wk, wv)


def _xattn(x, xs_pad, wq, wg, k, v, ks, vs, tile):
    b, l, d = x.shape
    n = xs_pad.shape[0]
    c = wq.shape[1]
    m = k.shape[1]
    tiles_per_seq = l // tile
    n_tiles = b * tiles_per_seq
    seqs = n // n_tiles
    assert seqs * n_tiles == n
    kv_spec = pl.BlockSpec((1, m, c), lambda s: (s // tiles_per_seq, 0, 0))
    skv_spec = pl.BlockSpec((seqs,) + ks.shape[1:], lambda s: (s, 0, 0, 0))
    out, outs = pl.pallas_call(
        _xattn_kernel,
        grid=(n_tiles,),
        in_specs=[pl.BlockSpec((tile, d), lambda s: (s, 0)),
                  pl.BlockSpec((seqs, SUBLANES, d), lambda s: (s, 0, 0)),
                  _full(wq.shape), _full(wg.shape), kv_spec, kv_spec, skv_spec, skv_spec],
        out_specs=[pl.BlockSpec((tile, c), lambda s: (s, 0)),
                   pl.BlockSpec((seqs, SUBLANES, c), lambda s: (s, 0, 0))],
        out_shape=[jax.ShapeDtypeStruct((b * l, c), BF16), jax.ShapeDtypeStruct((n, SUBLANES, c), F32)],
        compiler_params=_cparams("parallel"),
        name="xattn",
    )(x.reshape(b * l, d), xs_pad, wq, wg, k, v, ks, vs)
    return out.reshape(b, l, c), outs


def _merge(rg, ssd, xa, x, w_out, ln_g, ln_b, tile, name):
    n, d = x.shape
    c = rg.shape[1]
    tile = min(tile, n)
    row = lambda w: pl.BlockSpec((tile, w), lambda i: (i, 0))
    return pl.pallas_call(
        functools.partial(_merge_kernel, sub_rows=min(MERGE_SUB_ROWS, tile)),
        grid=(n // tile,),
        in_specs=[row(c), row(c), row(c), row(d), _full(w_out.shape), _full(ln_g.shape), _full(ln_b.shape)],
        out_specs=row(d),
        out_shape=jax.ShapeDtypeStruct((n, d), F32),
        compiler_params=_cparams("parallel"),
        name=name,
    )(rg, ssd, xa, x, w_out, ln_g, ln_b)


def _weight_prep_kernel(bounds, w_ref, wo_ref, *out_refs):
    for (lo, hi), o_ref in zip(bounds, out_refs[:-1]):
        if hi - lo == o_ref.shape[1]:
            o_ref[...] = w_ref[:, lo:hi].astype(BF16)
        else:
            o_ref[...] = jnp.zeros_like(o_ref)
            o_ref[:, :hi - lo] = w_ref[:, lo:hi].astype(BF16)
    out_refs[-1][...] = wo_ref[...].astype(BF16)


def _weight_prep(w_in, w_out, bounds, row_blocks=4):
    k, n_out = w_in.shape[0], w_out.shape[0]
    widths = [-(-(hi - lo) // LANES) * LANES for lo, hi in bounds]
    rows = lambda total, w: pl.BlockSpec((total // row_blocks, w), lambda i: (i, 0))
    return pl.pallas_call(
        functools.partial(_weight_prep_kernel, tuple(bounds)),
        grid=(row_blocks,),
        in_specs=[rows(k, w_in.shape[1]), rows(n_out, w_out.shape[1])],
        out_specs=[rows(k, w) for w in widths] + [rows(n_out, w_out.shape[1])],
        out_shape=[jax.ShapeDtypeStruct((k, w), BF16) for w in widths]
        + [jax.ShapeDtypeStruct(w_out.shape, BF16)],
        compiler_params=_cparams("parallel"),
        name="weight_prep",
    )(w_in, w_out)


def _layer_params(w_in, rg_conv_w, rg_conv_b, w_rg_a, b_rg_a, w_rg_i, b_rg_i, rg_lambda, ssd_conv_w,
                  ssd_conv_b, ssd_dt_bias, ssd_a_log, ssd_d, ssd_norm_g, w_out, ln_g, ln_b):
    d_rg = rg_conv_w.shape[1]
    d_conv = ssd_conv_w.shape[1]
    d_ssd = ssd_norm_g.shape[0]
    heads = ssd_d.shape[0]
    sizes = (d_rg, d_rg, d_conv, d_ssd, heads)
    offs = [0]
    for s in sizes:
        offs.append(offs[-1] + s)
    d_xa = (w_in.shape[1] - offs[-1]) // 2
    offs += [offs[-1] + d_xa, offs[-1] + 2 * d_xa]
    wx, wg, wxbc, wz, wdt, wq, wxg, wo = _weight_prep(w_in, w_out, list(zip(offs[:-1], offs[1:])))
    row = lambda v: v.reshape(1, -1).astype(F32)
    pad_lanes = lambda v: jnp.pad(v, ((0, 0), (0, LANES - v.shape[1])))
    rg = dict(wx=wx, wg=wg, cw=rg_conv_w, cb=row(rg_conv_b),
              wgate=jnp.concatenate([w_rg_a, w_rg_i], axis=2).astype(BF16),
              ba=row(b_rg_a), bi=row(b_rg_i), lam=row(rg_lambda))
    head_of_channel = np.arange(d_ssd) // SSD_HEAD_DIM
    ssd = dict(wxbc=wxbc, wz=wz, wdt=wdt,
               cw=ssd_conv_w, cb=row(ssd_conv_b), dtb=pad_lanes(row(ssd_dt_bias)),
               alog=pad_lanes(row(ssd_a_log)), dexp=row(jnp.repeat(ssd_d, SSD_HEAD_DIM)), ng=row(ssd_norm_g),
               expand=jnp.asarray(np.arange(LANES)[:, None] == head_of_channel[None, :], dtype=BF16))
    xa = dict(wq=wq, wg=wxg)
    merge = dict(w=wo, g=row(ln_g), b=row(ln_b))
    return rg, ssd, xa, merge


PROMPT_RG_TILE = 256
PROMPT_XA_TILE = 256
MEMKV_BATCHES = 4
MERGE_TILE = 1024
MERGE_SUB_ROWS = 256
SAMPLE_RG_SEQS = 16
SAMPLE_SSD_SEQS = 16


def kernel(x_prompt, x_sample, mem_prompt, state_rg_conv, state_rg_h, state_ssd_conv, state_ssd_h,
           cache_mem_k, cache_mem_v, w_in, rg_conv_w, rg_conv_b, w_rg_a, b_rg_a, w_rg_i, b_rg_i,
           rg_lambda, ssd_conv_w, ssd_conv_b, ssd_dt_bias, ssd_a_log, ssd_d, ssd_norm_g, w_mem_k,
           w_mem_v, w_out, ln_g, ln_b):
    assert w_in.shape[0] == DEPTH
    bp, lp, d = x_prompt.shape
    bs, ls, _ = x_sample.shape
    heads = ssd_d.shape[1]
    outs = {k: [] for k in ("rgc_p", "rgh_p", "sc_p", "sh_p", "mk_p", "mv_p", "rgc_s", "rgh_s", "sc_s", "sh_s")}
    yp, ys = x_prompt, x_sample
    pad_rows = lambda v, before, after: jnp.pad(v, ((0, 0), (before, after), (0, 0)))
    for l in range(DEPTH):
        rg, ssd, xa, merge = _layer_params(
            w_in[l], rg_conv_w[l], rg_conv_b[l], w_rg_a[l], b_rg_a[l], w_rg_i[l], b_rg_i[l], rg_lambda[l],
            ssd_conv_w[l], ssd_conv_b[l], ssd_dt_bias[l], ssd_a_log[l], ssd_d[l], ssd_norm_g[l],
            w_out[l], ln_g[l], ln_b[l])
        xs_pad = pad_rows(ys, 0, SAMPLE_PAD - ls)
        mk, mv, mkb, mvb = _memkv(mem_prompt, w_mem_k[l].astype(BF16), w_mem_v[l].astype(BF16))
        rg_o, rgc, rgh = _rg_prompt(yp, rg, PROMPT_RG_TILE)
        xa_o, xa_s = _xattn(yp, xs_pad, xa["wq"], xa["wg"], mkb, mvb, cache_mem_k[l], cache_mem_v[l],
                            PROMPT_XA_TILE)
        yp, sc, sh = _ssd_merge_prompt(yp, ssd, rg_o, xa_o, merge)
        outs["rgc_p"].append(rgc)
        outs["rgh_p"].append(rgh.reshape(bp, -1))
        outs["sc_p"].append(sc)
        outs["sh_p"].append(sh.reshape(bp, heads, SSD_HEAD_DIM, SSD_STATE))
        outs["mk_p"].append(mk)
        outs["mv_p"].append(mv)
        tail = SAMPLE_PAD - (CONV_W - 1)
        rg_o, rgc, rgh = _rg_sample(xs_pad, rg, pad_rows(state_rg_conv[l], tail, 0),
                                    state_rg_h[l][:, None, :], SAMPLE_RG_SEQS, ls)
        ssd_o, sc, sh = _ssd_sample(xs_pad, ssd, pad_rows(state_ssd_conv[l], tail, 0),
                                    state_ssd_h[l].reshape(bs, heads * SSD_HEAD_DIM, SSD_STATE),
                                    SAMPLE_SSD_SEQS, ls)
        flat = lambda v: v.reshape(bs * SAMPLE_PAD, v.shape[-1])
        ys_pad = _merge(flat(rg_o), flat(ssd_o), flat(xa_s), flat(xs_pad), merge["w"], merge["g"], merge["b"],
                        MERGE_TILE, "merge_sample").reshape(bs, SAMPLE_PAD, d)
        ys = ys_pad[:, :ls, :]
        outs["rgc_s"].append(rgc)
        outs["rgh_s"].append(rgh.reshape(bs, -1))
        outs["sc_s"].append(sc)
        outs["sh_s"].append(sh.reshape(bs, heads, SSD_HEAD_DIM, SSD_STATE))
    st = lambda k: jnp.stack(outs[k])
    return (yp, ys, st("rgc_p"), st("rgh_p"), st("sc_p"), st("sh_p"), st("mk_p"), st("mv_p"),
            st("rgc_s"), st("rgh_s"), st("sc_s"), st("sh_s"))
```

```python
import functools
from typing import NamedTuple

import jax
import jax.numpy as jnp
import numpy as np
from jax import lax
from jax.experimental import pallas as pl
from jax.experimental.pallas import tpu as pltpu

F32 = jnp.float32
BF16 = jnp.bfloat16

SUBLANES = 8
LANES = 128
MXU_WIDTH = 256
VMEM_LIMIT_BYTES = 56 * 1024 * 1024

RG_C = 8.0
CONV_W = 4
RG_BLOCKS = 8
SSD_HEAD_DIM = 64
SSD_GROUPS = 2
SSD_STATE = 128
SSD_CHUNK = 128
MEM_HEADS = 4
LN_EPS = 1e-5
RMS_EPS = 1e-5
DEPTH = 1
DEEPNORM_ALPHA = (2 * DEPTH) ** 0.25
SAMPLE_PAD = SUBLANES


def _cparams(*sem):
    return pltpu.CompilerParams(dimension_semantics=sem, vmem_limit_bytes=VMEM_LIMIT_BYTES)


def _mm(a, b):
    return jnp.dot(a.astype(BF16), b.astype(BF16), preferred_element_type=F32)


def _mm_nt(a, b):
    return lax.dot_general(a.astype(BF16), b.astype(BF16), (((1,), (1,)), ((), ())),
                           preferred_element_type=F32)


def _mm_tn(a, b):
    return lax.dot_general(a.astype(BF16), b.astype(BF16), (((0,), (0,)), ((), ())),
                           preferred_element_type=F32)


def _sigmoid(x):
    return 1.0 / (1.0 + jnp.exp(-x))


def _silu(x):
    return x * _sigmoid(x)


def _softplus(x):
    return jnp.maximum(x, 0.0) + jnp.log(1.0 + jnp.exp(-jnp.abs(x)))


def _causal_conv_tiles(x3, p3, w, b):
    row = lax.broadcasted_iota(jnp.int32, x3.shape, 1)
    y = x3 * w[CONV_W - 1:CONV_W][None]
    for s in range(1, CONV_W):
        shifted = jnp.where(row >= s, pltpu.roll(x3, s, axis=1), pltpu.roll(p3, s, axis=1))
        y = y + shifted * w[CONV_W - 1 - s:CONV_W - s][None]
    return y + b[None]


def _scan_in_tiles(a3, b3):
    row = lax.broadcasted_iota(jnp.int32, a3.shape, 1)
    s = 1
    while s < SUBLANES:
        keep = row >= s
        a_sh = jnp.where(keep, pltpu.roll(a3, s, axis=1), 1.0)
        b_sh = jnp.where(keep, pltpu.roll(b3, s, axis=1), 0.0)
        b3 = a3 * b_sh + b3
        a3 = a3 * a_sh
        s *= 2
    return a3, b3


def _rg_gates(u, wgate, ba, bi, lam):
    pre = _mm(u, wgate)
    r = _sigmoid(pre[:, :LANES] + ba)
    i = _sigmoid(pre[:, LANES:] + bi)
    neg_log_a = r * (RG_C * _softplus(-lam))
    a = jnp.exp(-neg_log_a)
    v = jnp.tanh(neg_log_a) * (1.0 + a * a)
    mult = jnp.where(v > 0.0, v * lax.rsqrt(v), 0.0)
    return a, mult * (i * u)


def _time_strided_conv(x3, tail3, w, b):
    slabs = x3.shape[0]
    row = lax.broadcasted_iota(jnp.int32, tail3.shape, 1)
    wrapped = jnp.where(row >= 1, pltpu.roll(x3[slabs - (CONV_W - 1):], 1, axis=1), pltpu.roll(tail3, 1, axis=1))
    y = x3 * w[CONV_W - 1:CONV_W][None]
    for s in range(1, CONV_W):
        shifted = jnp.concatenate([wrapped[CONV_W - 1 - s:], x3[:slabs - s]], axis=0)
        y = y + shifted * w[CONV_W - 1 - s:CONV_W - s][None]
    return y + b[None]


def _rg_prompt_kernel(x_ref, perm_ref, permt_ref, wx_ref, wg_ref, cw_ref, cb_ref, wgate_ref, ba_ref,
                      bi_ref, lam_ref, out_ref, conv_ref, hlast_ref, tail_scr, h_scr, proj_scr, outp_scr, *,
                      tiles_per_seq):
    s = pl.program_id(0)

    @pl.when(s == 0)
    def _():
        proj_scr[...] = jnp.zeros_like(proj_scr)

    @pl.when(jnp.logical_or(s == 0, s % tiles_per_seq == 1 % tiles_per_seq))
    def _():
        tail_scr[...] = jnp.zeros_like(tail_scr)
        h_scr[...] = jnp.zeros_like(h_scr)

    rows = x_ref.shape[0]
    slabs = rows // SUBLANES
    width = wx_ref.shape[1]
    per_group = MXU_WIDTH // LANES
    row = lax.broadcasted_iota(jnp.int32, (SUBLANES, LANES), 0)
    xb = jnp.dot(perm_ref[...], x_ref[...].astype(BF16), preferred_element_type=F32).astype(BF16)

    for cg in range(width // MXU_WIDTH):
        gs = slice(cg * MXU_WIDTH, (cg + 1) * MXU_WIDTH)
        for kk in range(per_group):
            k = cg * per_group + kk
            ks = slice(k * LANES, (k + 1) * LANES)
            x3 = proj_scr[:, ks].reshape(slabs, SUBLANES, LANES)
            tail3 = tail_scr[:, :, ks]
            last3 = x3[slabs - (CONV_W - 1):]
            tail_scr[:, :, ks] = last3
            for i in range(CONV_W - 1):
                conv_ref[0, i:i + 1, ks] = last3[i, SUBLANES - 1:, :]
            u3 = _time_strided_conv(x3, tail3, cw_ref[:, ks], cb_ref[:, ks])
            a, b = _rg_gates(u3.reshape(rows, LANES), wgate_ref[k], ba_ref[:, ks], bi_ref[:, ks], lam_ref[:, ks])
            a3 = a.reshape(slabs, SUBLANES, LANES)
            b3 = b.reshape(slabs, SUBLANES, LANES)
            h_loc, a_cum = [b3[0]], [a3[0]]
            for j in range(1, slabs):
                h_loc.append(a3[j] * h_loc[j - 1] + b3[j])
                a_cum.append(a3[j] * a_cum[j - 1])
            a_run, h_run = _scan_in_tiles(a_cum[-1][None], h_loc[-1][None])
            h_prev = h_scr[:, ks]
            h_end = h_run[0] + a_run[0] * h_prev
            h_in = jnp.where(row >= 1, pltpu.roll(h_end, 1, axis=0), h_prev)
            h_scr[:, ks] = h_end[SUBLANES - 1:]
            hlast_ref[0, :, ks] = h_end[SUBLANES - 1:]
            h = jnp.concatenate([h_loc[j] + a_cum[j] * h_in for j in range(slabs)], axis=0)
            gate = proj_scr[:, width + k * LANES:width + (k + 1) * LANES]
            outp_scr[:, ks] = (h * _silu(gate)).astype(BF16)
        proj_scr[:, gs] = jnp.dot(xb, wx_ref[:, gs], preferred_element_type=F32)
        proj_scr[:, width + cg * MXU_WIDTH:width + (cg + 1) * MXU_WIDTH] = jnp.dot(
            xb, wg_ref[:, gs], preferred_element_type=F32)

    out_ref[...] = jnp.dot(permt_ref[...], outp_scr[...], preferred_element_type=F32).astype(out_ref.dtype)


def _rg_sample_kernel(x_ref, wx_ref, wg_ref, cw_ref, cb_ref, wgate_ref, ba_ref, bi_ref, lam_ref,
                      cstate_ref, h0_ref, out_ref, conv_ref, hlast_ref, *, valid):
    seqs = x_ref.shape[0]
    rows = seqs * SUBLANES
    xb = x_ref[...].reshape(rows, x_ref.shape[2]).astype(BF16)
    rgx = jnp.dot(xb, wx_ref[...], preferred_element_type=F32)
    gate = jnp.dot(xb, wg_ref[...], preferred_element_type=F32)
    x3 = rgx.reshape(seqs, SUBLANES, rgx.shape[1])
    conv_ref[...] = x3[:, valid - (CONV_W - 1):valid, :]
    row = lax.broadcasted_iota(jnp.int32, (seqs, SUBLANES, LANES), 1)
    for k in range(RG_BLOCKS):
        ks = slice(k * LANES, (k + 1) * LANES)
        u3 = _causal_conv_tiles(x3[:, :, ks], cstate_ref[:, :, ks], cw_ref[:, ks], cb_ref[:, ks])
        u = u3.reshape(rows, LANES)
        a, b = _rg_gates(u, wgate_ref[k], ba_ref[:, ks], bi_ref[:, ks], lam_ref[:, ks])
        a3 = a.reshape(seqs, SUBLANES, LANES)
        b3 = b.reshape(seqs, SUBLANES, LANES)
        b3 = b3 + jnp.where(row == 0, a3 * h0_ref[:, :, ks], 0.0)
        _, h3 = _scan_in_tiles(a3, b3)
        hlast_ref[:, :, ks] = h3[:, valid - 1:valid, :]
        g3 = gate[:, ks].reshape(seqs, SUBLANES, LANES)
        out_ref[:, :, ks] = (h3 * _silu(g3)).astype(out_ref.dtype)


def _cumsum_rows(x):
    rows = x.shape[0]
    row = lax.broadcasted_iota(jnp.int32, x.shape, 0)
    s = 1
    while s < rows:
        x = x + jnp.where(row >= s, pltpu.roll(x, s, axis=0), 0.0)
        s *= 2
    return x


def _group_rmsnorm(y, gain):
    width = y.shape[-1] // SSD_GROUPS
    parts = []
    for g in range(SSD_GROUPS):
        yg = y[..., g * width:(g + 1) * width]
        ms = jnp.sum(yg * yg, axis=-1, keepdims=True) * (1.0 / width)
        parts.append(yg * lax.rsqrt(ms + RMS_EPS))
    return jnp.concatenate(parts, axis=-1) * gain


def _time_strided_cumsum(x):
    slabs = x.shape[0] // SUBLANES
    x3 = x.reshape(slabs, SUBLANES, x.shape[1])
    acc = [x3[0]]
    for j in range(1, slabs):
        acc.append(acc[j - 1] + x3[j])
    _, run = _scan_in_tiles(jnp.ones_like(acc[-1])[None], acc[-1][None])
    row = lax.broadcasted_iota(jnp.int32, run[0].shape, 0)
    before = jnp.where(row >= 1, pltpu.roll(run[0], 1, axis=0), 0.0)
    return jnp.concatenate([a + before for a in acc], axis=0)


def _ssd_merge_prompt_kernel(x_ref, perm_ref, permt_ref, wxbc_ref, wz_ref, wdt_ref, cw_ref, cb_ref, dtb_ref,
                             alog_ref, dexp_ref, ng_ref, rg_ref, xa_ref, xres_ref, wout_ref, lng_ref, lnb_ref,
                             y_ref, conv_ref, h_ref, tail_scr, proj_scr, xb_scr, h_scr, ssd_scr, *,
                             chunks_per_seq, n_chunks):
    s = pl.program_id(0)

    @pl.when(s == 0)
    def _():
        proj_scr[...] = jnp.zeros_like(proj_scr)
        xb_scr[...] = jnp.zeros_like(xb_scr)
        ssd_scr[...] = jnp.zeros_like(ssd_scr)

    @pl.when(jnp.logical_or(s == 0, s % chunks_per_seq == 1 % chunks_per_seq))
    def _():
        tail_scr[...] = jnp.zeros_like(tail_scr)
        h_scr[...] = jnp.zeros_like(h_scr)

    q = x_ref.shape[0]
    slabs = q // SUBLANES
    d_ssd = wz_ref.shape[1]
    d_conv = wxbc_ref.shape[1]
    gn = SSD_GROUPS * SSD_STATE
    mix, row0 = None, 0
    for src in (rg_ref, ssd_scr, xa_ref):
        part = jnp.dot(src[...], wout_ref[row0:row0 + src.shape[1], :], preferred_element_type=F32)
        mix = part if mix is None else mix + part
        row0 += src.shape[1]
    res = DEEPNORM_ALPHA * xres_ref[...] + mix
    mu = jnp.mean(res, axis=-1, keepdims=True)
    cen = res - mu
    var = jnp.mean(cen * cen, axis=-1, keepdims=True)
    y_ref[...] = cen * lax.rsqrt(var + LN_EPS) * lng_ref[...] + lnb_ref[...]

    xb_new = jnp.dot(perm_ref[...], x_ref[...].astype(BF16), preferred_element_type=F32).astype(BF16)
    z = jnp.dot(xb_scr[...], wz_ref[...], preferred_element_type=F32)
    xb_scr[...] = xb_new

    x3 = proj_scr[:, :d_conv].reshape(slabs, SUBLANES, d_conv)
    dtr = proj_scr[:, d_conv:]
    tail3 = tail_scr[...]
    tail_scr[...] = x3[slabs - (CONV_W - 1):]
    xbc = _silu(_time_strided_conv(x3, tail3, cw_ref[...], cb_ref[...])).reshape(q, d_conv)
    sx = xbc[:, :d_ssd]
    bm = xbc[:, d_ssd:d_ssd + gn]
    cm = xbc[:, d_ssd + gn:]

    dt = _softplus(dtr + dtb_ref[...])
    da = dt * (-jnp.exp(alog_ref[...]))
    acum = _time_strided_cumsum(da)
    alast = acum[q - 1:q, :]
    wgt = dt * jnp.exp(alast - acum)
    tot = jnp.exp(alast)
    acum_t = acum.T
    dt_t = dt.T

    ii = lax.broadcasted_iota(jnp.int32, (q, q), 0)
    jj = lax.broadcasted_iota(jnp.int32, (q, q), 1)
    time_of = lambda r: (r % SUBLANES) * slabs + r // SUBLANES
    causal = time_of(ii) >= time_of(jj)
    lane = lax.broadcasted_iota(jnp.int32, (q, LANES), 1)
    lo = lane < SSD_HEAD_DIM
    srow = lax.broadcasted_iota(jnp.int32, (LANES, SSD_STATE), 0) < SSD_HEAD_DIM

    cb = [_mm_nt(cm[:, g * SSD_STATE:(g + 1) * SSD_STATE], bm[:, g * SSD_STATE:(g + 1) * SSD_STATE])
          for g in range(SSD_GROUPS)]
    heads = d_ssd // SSD_HEAD_DIM
    pairs = heads // 2
    conv_groups = d_conv // MXU_WIDTH
    assert conv_groups < pairs
    pairs_per_group = pairs // SSD_GROUPS
    y_parts = []
    for g in range(SSD_GROUPS):
        ns = slice(g * SSD_STATE, (g + 1) * SSD_STATE)
        rows_g = slice(g * pairs_per_group * LANES, (g + 1) * pairs_per_group * LANES)
        h_in = h_scr[rows_g, :]
        y_off_g = _mm_nt(cm[:, ns], h_in)
        xws, decays = [], []
        for pl_ in range(pairs_per_group):
            pq = g * pairs_per_group + pl_
            ps = slice(pq * LANES, (pq + 1) * LANES)
            xq = sx[:, ps]
            ms, es, ws, ts = [], [], [], []
            for h in (2 * pq, 2 * pq + 1):
                acol = jnp.broadcast_to(acum[:, h:h + 1], (q, q))
                arow = jnp.broadcast_to(acum_t[h:h + 1, :], (q, q))
                decay = jnp.exp(jnp.where(causal, acol - arow, -jnp.inf))
                ms.append((cb[g] * decay * jnp.broadcast_to(dt_t[h:h + 1, :], (q, q))).astype(BF16))
                es.append(jnp.exp(jnp.broadcast_to(acum[:, h:h + 1], (q, LANES))))
                ws.append(jnp.broadcast_to(wgt[:, h:h + 1], (q, LANES)))
                ts.append(jnp.broadcast_to(tot[:, h:h + 1], (LANES, SSD_STATE)))
            lhs = jnp.concatenate(ms, axis=1)
            rhs = jnp.concatenate([jnp.where(lo, xq, 0.0), jnp.where(lo, 0.0, xq)], axis=0)
            y_diag = _mm(lhs, rhs)
            y_off = y_off_g[:, pl_ * LANES:(pl_ + 1) * LANES] * jnp.where(lo, es[0], es[1])
            xws.append(xq * jnp.where(lo, ws[0], ws[1]))
            decays.append(jnp.where(srow, ts[0], ts[1]))
            y_parts.append(y_diag + y_off + dexp_ref[:, ps] * xq)
            if pq < conv_groups:
                gs = slice(pq * MXU_WIDTH, (pq + 1) * MXU_WIDTH)
                proj_scr[:, gs] = jnp.dot(xb_new, wxbc_ref[:, gs], preferred_element_type=F32)
            elif pq == conv_groups:
                proj_scr[:, d_conv:] = jnp.dot(xb_new, wdt_ref[...], preferred_element_type=F32)
        h_scr[rows_g, :] = (h_in * jnp.concatenate(decays, axis=0)
                            + _mm_tn(jnp.concatenate(xws, axis=1), bm[:, ns]))

    y = jnp.concatenate(y_parts, axis=1) * _silu(z)
    y = _group_rmsnorm(y, ng_ref[...]).astype(BF16)
    ssd_scr[...] = jnp.dot(permt_ref[...], y, preferred_element_type=F32).astype(BF16)

    @pl.when(jnp.logical_and(s % chunks_per_seq == 0, jnp.logical_and(s > 0, s <= n_chunks)))
    def _():
        h_ref[0] = h_scr[...]
        for i in range(CONV_W - 1):
            conv_ref[0, i:i + 1, :] = tail_scr[i, SUBLANES - 1:, :]


def _ssd_sample_kernel(x_ref, wxbc_ref, wz_ref, wdt_ref, cw_ref, cb_ref, dtb_ref, alog_ref, dexp_ref,
                       ng_ref, expand_ref, cstate_ref, h0_ref, out_ref, conv_ref, h_ref,
                       c_scr, b_scr, xw_scr, tot_scr, yoff_scr, *, valid):
    seqs = x_ref.shape[0]
    rows = seqs * SUBLANES
    d_ssd = wz_ref.shape[1]
    gn = SSD_GROUPS * SSD_STATE
    heads = d_ssd // SSD_HEAD_DIM
    hg = heads // SSD_GROUPS
    xb = x_ref[...].reshape(rows, x_ref.shape[2]).astype(BF16)
    xbc_raw = jnp.dot(xb, wxbc_ref[...], preferred_element_type=F32)
    z = jnp.dot(xb, wz_ref[...], preferred_element_type=F32)
    dtr = jnp.dot(xb, wdt_ref[...], preferred_element_type=F32)

    x3 = xbc_raw.reshape(seqs, SUBLANES, xbc_raw.shape[1])
    conv_ref[...] = x3[:, valid - (CONV_W - 1):valid, :]
    xbc3 = _silu(_causal_conv_tiles(x3, cstate_ref[...], cw_ref[...], cb_ref[...]))
    sx3 = xbc3[:, :, :d_ssd]
    b3 = xbc3[:, :, d_ssd:d_ssd + gn]
    c3 = xbc3[:, :, d_ssd + gn:]

    row = lax.broadcasted_iota(jnp.int32, (seqs, SUBLANES, LANES), 1)
    lane = lax.broadcasted_iota(jnp.int32, (seqs, SUBLANES, LANES), 2)
    dt3 = jnp.where(row < valid, _softplus(dtr + dtb_ref[...]).reshape(seqs, SUBLANES, LANES), 0.0)
    da3 = dt3 * (-jnp.exp(alog_ref[...]))[None]
    ones = jnp.ones_like(da3)
    _, acum3 = _scan_in_tiles(ones, da3)
    alast = acum3[:, SUBLANES - 1:, :]
    wgt3 = dt3 * jnp.exp(alast - acum3)
    tot_scr[...] = jnp.exp(alast)
    e3 = jnp.exp(acum3)

    coefs = []
    for u in range(valid):
        prod = c3 * b3[:, u:u + 1, :]
        cbu = [jnp.sum(prod[:, :, g * SSD_STATE:(g + 1) * SSD_STATE], axis=-1, keepdims=True)
               for g in range(SSD_GROUPS)]
        cb_heads = jnp.where(lane < hg, cbu[0], cbu[1])
        coefs.append(jnp.where(row >= u, cb_heads * jnp.exp(acum3 - acum3[:, u:u + 1, :]) * dt3[:, u:u + 1, :],
                               0.0))

    per_head = coefs + [wgt3, e3]
    stacked = jnp.concatenate([v.reshape(rows, LANES) for v in per_head], axis=0)
    expanded = None
    rest = stacked
    for _ in range(3):
        piece = rest.astype(BF16)
        rest = rest - piece.astype(F32)
        part = jnp.dot(piece, expand_ref[...], preferred_element_type=F32)
        expanded = part if expanded is None else expanded + part
    expanded = [expanded[i * rows:(i + 1) * rows].reshape(seqs, SUBLANES, d_ssd) for i in range(len(per_head))]
    y_diag = expanded[0] * sx3[:, 0:1, :]
    for u in range(1, valid):
        y_diag = y_diag + expanded[u] * sx3[:, u:u + 1, :]
    wgt_x, e_x = expanded[valid], expanded[valid + 1]

    c_scr[...] = c3
    b_scr[...] = b3
    xw_scr[...] = sx3 * wgt_x
    srow = lax.broadcasted_iota(jnp.int32, (2 * SSD_HEAD_DIM, SSD_STATE), 0) < SSD_HEAD_DIM

    def per_seq(s, carry):
        cs = c_scr[s]
        bs = b_scr[s]
        xws = xw_scr[s]
        tots = tot_scr[s]
        for g in range(SSD_GROUPS):
            gs = slice(g * SSD_STATE, (g + 1) * SSD_STATE)
            width = hg * SSD_HEAD_DIM
            cols = slice(g * width, (g + 1) * width)
            hin = h0_ref[s, cols, :]
            yoff_scr[s, :, cols] = _mm_nt(cs[:, gs], hin)
            upd = _mm_tn(xws[:, cols], bs[:, gs])
            for pq in range(hg // 2):
                h = g * hg + 2 * pq
                rs = slice(pq * LANES, (pq + 1) * LANES)
                t0 = jnp.broadcast_to(tots[0:1, h:h + 1], (LANES, SSD_STATE))
                t1 = jnp.broadcast_to(tots[0:1, h + 1:h + 2], (LANES, SSD_STATE))
                h_ref[s, g * width + pq * LANES:g * width + (pq + 1) * LANES, :] = (
                    hin[rs] * jnp.where(srow, t0, t1) + upd[rs])
        return carry

    lax.fori_loop(0, seqs, per_seq, 0)

    y = y_diag + yoff_scr[...] * e_x + dexp_ref[...][None] * sx3
    y = y * _silu(z.reshape(seqs, SUBLANES, d_ssd))
    out_ref[...] = _group_rmsnorm(y, ng_ref[...][None]).astype(out_ref.dtype)


def _memkv_kernel(mem_ref, wk_ref, wv_ref, k_ref, v_ref, kb_ref, vb_ref):
    nb, m, d = mem_ref.shape
    mb = mem_ref[...].reshape(nb * m, d).astype(BF16)
    for w_ref, o_ref, ob_ref in ((wk_ref, k_ref, kb_ref), (wv_ref, v_ref, vb_ref)):
        proj = jnp.dot(mb, w_ref[...], preferred_element_type=F32)
        for i in range(nb):
            o_ref[i] = proj[i * m:(i + 1) * m].reshape(o_ref.shape[1:])
        ob_ref[...] = proj.astype(BF16).reshape(ob_ref.shape)


def _softmax_terms(scores):
    p = jnp.exp(scores - jnp.max(scores, axis=-1, keepdims=True))
    return p, jnp.sum(p, axis=-1, keepdims=True)


def _prompt_attention_head(h, q_ref, rows, k_ref, v_ref, out_ref):
    c = k_ref.shape[2]
    d_head = c // MEM_HEADS
    hs = slice(h * d_head, (h + 1) * d_head)

    def scores():
        return _softmax_terms(_mm_nt(q_ref[:rows, hs], k_ref[0, :, hs]) * (d_head ** -0.5))

    def output(p, l):
        o = _mm(p, v_ref[0, :, hs]) * (1.0 / l)
        gate = q_ref[:rows, c + h * d_head:c + (h + 1) * d_head]
        out_ref[:, hs] = (o * _silu(gate)).astype(out_ref.dtype)

    return scores, output


def _sample_attention_seq(s, q_ref, rows, k_ref, v_ref, out_ref):
    _, n_mem, heads, d_head = k_ref.shape
    c = heads * d_head
    rs = slice(rows + s * SUBLANES, rows + (s + 1) * SUBLANES)

    def scores():
        shape = (heads * SUBLANES, n_mem * heads)
        same_head = (lax.broadcasted_iota(jnp.int32, shape, 0) // SUBLANES
                     == lax.broadcasted_iota(jnp.int32, shape, 1) % heads)
        qh = jnp.concatenate([q_ref[rs, h * d_head:(h + 1) * d_head] for h in range(heads)], axis=0)
        sc = _mm_nt(qh, k_ref[s].reshape(n_mem * heads, d_head)) * (d_head ** -0.5)
        return _softmax_terms(jnp.where(same_head, sc, -jnp.inf))

    def output(p, l):
        o = _mm(p, v_ref[s].reshape(n_mem * heads, d_head)) * (1.0 / l)
        o = jnp.concatenate([o[h * SUBLANES:(h + 1) * SUBLANES] for h in range(heads)], axis=1)
        out_ref[s] = (o * _silu(q_ref[rs, c:])).astype(out_ref.dtype)

    return scores, output


def _xattn_kernel(x_ref, xs_ref, wq_ref, wg_ref, k_ref, v_ref, ks_ref, vs_ref, out_ref, outs_ref):
    rows = x_ref.shape[0]
    seqs, _, d = xs_ref.shape
    xb = jnp.concatenate([x_ref[...], xs_ref[...].reshape(seqs * SUBLANES, d)], axis=0).astype(BF16)
    qg = jnp.concatenate([jnp.dot(xb, wq_ref[...], preferred_element_type=F32),
                          jnp.dot(xb, wg_ref[...], preferred_element_type=F32)], axis=1)
    items = [_prompt_attention_head(h, qg, rows, k_ref, v_ref, out_ref) for h in range(MEM_HEADS)]
    items += [_sample_attention_seq(i, qg, rows, ks_ref, vs_ref, outs_ref) for i in range(seqs)]
    for scores, output in items:
        output(*scores())


def _merge_kernel(rg_ref, ssd_ref, xa_ref, x_ref, w_ref, g_ref, b_ref, y_ref, *, sub_rows):
    d = rg_ref.shape[1]
    for r0 in range(0, x_ref.shape[0], sub_rows):
        rs = slice(r0, r0 + sub_rows)
        mix = (jnp.dot(rg_ref[rs, :].astype(BF16), w_ref[0:d, :], preferred_element_type=F32)
               + jnp.dot(ssd_ref[rs, :].astype(BF16), w_ref[d:2 * d, :], preferred_element_type=F32)
               + jnp.dot(xa_ref[rs, :].astype(BF16), w_ref[2 * d:3 * d, :], preferred_element_type=F32))
        res = DEEPNORM_ALPHA * x_ref[rs, :] + mix
        mu = jnp.mean(res, axis=-1, keepdims=True)
        cen = res - mu
        var = jnp.mean(cen * cen, axis=-1, keepdims=True)
        y_ref[rs, :] = cen * lax.rsqrt(var + LN_EPS) * g_ref[...] + b_ref[...]


def _full(shape):
    return pl.BlockSpec(shape, lambda *_: (0,) * len(shape))


def _time_stride_perm(rows):
    p = np.arange(rows)
    t = (p % SUBLANES) * (rows // SUBLANES) + p // SUBLANES
    perm = t[:, None] == np.arange(rows)[None, :]
    return jnp.asarray(perm, dtype=BF16), jnp.asarray(perm.T, dtype=BF16)


def _rg_prompt(x, p, tile):
    b, l, d = x.shape
    c = p["wx"].shape[1]
    tiles_per_seq = l // tile
    n_tiles = b * tiles_per_seq
    perm, perm_t = _time_stride_perm(tile)
    done = lambda s: jnp.maximum(s - 1, 0)
    out, conv, hlast = pl.pallas_call(
        functools.partial(_rg_prompt_kernel, tiles_per_seq=tiles_per_seq),
        grid=(n_tiles + 1,),
        in_specs=[pl.BlockSpec((tile, d), lambda s: (jnp.minimum(s, n_tiles - 1), 0)),
                  _full(perm.shape), _full(perm.shape), p["wx"].spec, p["wg"].spec,
                  _full(p["cw"].shape), _full(p["cb"].shape), _full(p["wgate"].shape), _full(p["ba"].shape),
                  _full(p["bi"].shape), _full(p["lam"].shape)],
        out_specs=[pl.BlockSpec((tile, c), lambda s: (done(s), 0)),
                   pl.BlockSpec((1, CONV_W - 1, c), lambda s: (done(s) // tiles_per_seq, 0, 0)),
                   pl.BlockSpec((1, 1, c), lambda s: (done(s) // tiles_per_seq, 0, 0))],
        out_shape=[jax.ShapeDtypeStruct((b * l, c), BF16),
                   jax.ShapeDtypeStruct((b, CONV_W - 1, c), F32),
                   jax.ShapeDtypeStruct((b, 1, c), F32)],
        scratch_shapes=[pltpu.VMEM((CONV_W - 1, SUBLANES, c), F32), pltpu.VMEM((1, c), F32),
                        pltpu.VMEM((tile, 2 * c), F32), pltpu.VMEM((tile, c), BF16)],
        compiler_params=_cparams("arbitrary"),
        name="rg_prompt",
    )(x.reshape(b * l, d), perm, perm_t, p["wx"].array, p["wg"].array, p["cw"], p["cb"], p["wgate"], p["ba"],
      p["bi"], p["lam"])
    return out.reshape(b, l, c), conv, hlast


def _rg_sample(xpad, p, cstate, h0, seqs, valid):
    n, _, d = xpad.shape
    c = p["wx"].shape[1]
    blk = lambda w: pl.BlockSpec((seqs, w[0], w[1]), lambda i: (i, 0, 0))
    return pl.pallas_call(
        functools.partial(_rg_sample_kernel, valid=valid),
        grid=(n // seqs,),
        in_specs=[blk((SUBLANES, d)), p["wx"].spec, p["wg"].spec, _full(p["cw"].shape),
                  _full(p["cb"].shape), _full(p["wgate"].shape), _full(p["ba"].shape),
                  _full(p["bi"].shape), _full(p["lam"].shape), blk((SUBLANES, c)), blk((1, c))],
        out_specs=[blk((SUBLANES, c)), blk((CONV_W - 1, c)), blk((1, c))],
        out_shape=[jax.ShapeDtypeStruct((n, SUBLANES, c), F32),
                   jax.ShapeDtypeStruct((n, CONV_W - 1, c), F32),
                   jax.ShapeDtypeStruct((n, 1, c), F32)],
        compiler_params=_cparams("parallel"),
        name="rg_sample",
    )(xpad, p["wx"].array, p["wg"].array, p["cw"], p["cb"], p["wgate"], p["ba"], p["bi"], p["lam"], cstate, h0)


_SSD_WEIGHTS = ("wxbc", "wz", "wdt")
_SSD_SMALL = ("cw", "cb", "dtb", "alog", "dexp", "ng")


def _ssd_param_specs(p):
    return [p[k].spec for k in _SSD_WEIGHTS] + [_full(p[k].shape) for k in _SSD_SMALL]


def _ssd_param_args(p):
    return [p[k].array for k in _SSD_WEIGHTS] + [p[k] for k in _SSD_SMALL]


def _ssd_merge_prompt(x, p, rg_out, xa_out, mp):
    b, l, d = x.shape
    cc = p["wxbc"].shape[1]
    c = p["wz"].shape[1]
    chunks_per_seq = l // SSD_CHUNK
    n_chunks = b * chunks_per_seq
    perm, perm_t = _time_stride_perm(SSD_CHUNK)
    flat = lambda v: v.reshape(b * l, v.shape[-1])
    clamp = lambda i: jnp.clip(i, 0, n_chunks - 1)
    rows = lambda w, back: pl.BlockSpec((SSD_CHUNK, w), lambda s: (clamp(s - back), 0))
    state = lambda shape: pl.BlockSpec((1,) + shape, lambda s: (clamp(s - 1) // chunks_per_seq, 0, 0))
    y, conv, hstate = pl.pallas_call(
        functools.partial(_ssd_merge_prompt_kernel, chunks_per_seq=chunks_per_seq, n_chunks=n_chunks),
        grid=(n_chunks + 2,),
        in_specs=[rows(d, 0), _full(perm.shape), _full(perm.shape)] + _ssd_param_specs(p)
        + [rows(rg_out.shape[-1], 2), rows(xa_out.shape[-1], 2), rows(d, 2), _full(mp["w"].shape),
           _full(mp["g"].shape), _full(mp["b"].shape)],
        out_specs=[rows(d, 2), state((CONV_W - 1, cc)), state((c, SSD_STATE))],
        out_shape=[jax.ShapeDtypeStruct((b * l, d), F32),
                   jax.ShapeDtypeStruct((b, CONV_W - 1, cc), F32),
                   jax.ShapeDtypeStruct((b, c, SSD_STATE), F32)],
        scratch_shapes=[pltpu.VMEM((CONV_W - 1, SUBLANES, cc), F32),
                        pltpu.VMEM((SSD_CHUNK, cc + LANES), F32), pltpu.VMEM((SSD_CHUNK, d), BF16),
                        pltpu.VMEM((c, SSD_STATE), F32), pltpu.VMEM((SSD_CHUNK, c), BF16)],
        compiler_params=_cparams("arbitrary"),
        name="ssd_merge_prompt",
    )(flat(x), perm, perm_t, *_ssd_param_args(p), flat(rg_out), flat(xa_out), flat(x), mp["w"], mp["g"], mp["b"])
    return y.reshape(b, l, d), conv, hstate


def _ssd_sample(xpad, p, cstate, h0, seqs, valid):
    n, _, d = xpad.shape
    cc = p["wxbc"].shape[1]
    c = p["wz"].shape[1]
    gn = SSD_GROUPS * SSD_STATE
    blk = lambda w: pl.BlockSpec((seqs, w[0], w[1]), lambda i: (i, 0, 0))
    return pl.pallas_call(
        functools.partial(_ssd_sample_kernel, valid=valid),
        grid=(n // seqs,),
        in_specs=[blk((SUBLANES, d))] + _ssd_param_specs(p)
        + [_full(p["expand"].shape), blk((SUBLANES, cc)), blk((c, SSD_STATE))],
        out_specs=[blk((SUBLANES, c)), blk((CONV_W - 1, cc)), blk((c, SSD_STATE))],
        out_shape=[jax.ShapeDtypeStruct((n, SUBLANES, c), F32),
                   jax.ShapeDtypeStruct((n, CONV_W - 1, cc), F32),
                   jax.ShapeDtypeStruct((n, c, SSD_STATE), F32)],
        scratch_shapes=[pltpu.VMEM((seqs, SUBLANES, gn), F32), pltpu.VMEM((seqs, SUBLANES, gn), F32),
                        pltpu.VMEM((seqs, SUBLANES, c), F32), pltpu.VMEM((seqs, 1, LANES), F32),
                        pltpu.VMEM((seqs, SUBLANES, c), F32)],
        compiler_params=_cparams("parallel"),
        name="ssd_sample",
    )(xpad, *_ssd_param_args(p), p["expand"], cstate, h0)


def _memkv(mem, wk, wv):
    b, m, d = mem.shape
    c = wk.shape[1]
    nb = MEMKV_BATCHES if b % MEMKV_BATCHES == 0 else 1
    spec = pl.BlockSpec((nb, m, c), lambda i: (i, 0, 0))
    spec4 = pl.BlockSpec((nb, m, MEM_HEADS, c // MEM_HEADS), lambda i: (i, 0, 0, 0))
    return pl.pallas_call(
        _memkv_kernel,
        grid=(b // nb,),
        in_specs=[pl.BlockSpec((nb, m, d), lambda i: (i, 0, 0)), _full(wk.shape), _full(wv.shape)],
        out_specs=[spec4, spec4, spec, spec],
        out_shape=[jax.ShapeDtypeStruct((b, m, MEM_HEADS, c // MEM_HEADS), F32)] * 2
        + [jax.ShapeDtypeStruct((b, m, c), BF16)] * 2,
        compiler_params=_cparams("parallel"),
        name="mem_kv",
    )(mem, wk, wv)


def _xattn(x, xs_pad, wq, wg, k, v, ks, vs, tile):
    b, l, d = x.shape
    n = xs_pad.shape[0]
    c = wq.shape[1]
    m = k.shape[1]
    tiles_per_seq = l // tile
    n_tiles = b * tiles_per_seq
    seqs = n // n_tiles
    assert seqs * n_tiles == n
    kv_spec = pl.BlockSpec((1, m, c), lambda s: (s // tiles_per_seq, 0, 0))
    skv_spec = pl.BlockSpec((seqs,) + ks.shape[1:], lambda s: (s, 0, 0, 0))
    out, outs = pl.pallas_call(
        _xattn_kernel,
        grid=(n_tiles,),
        in_specs=[pl.BlockSpec((tile, d), lambda s: (s, 0)),
                  pl.BlockSpec((seqs, SUBLANES, d), lambda s: (s, 0, 0)),
                  wq.spec, wg.spec, kv_spec, kv_spec, skv_spec, skv_spec],
        out_specs=[pl.BlockSpec((tile, c), lambda s: (s, 0)),
                   pl.BlockSpec((seqs, SUBLANES, c), lambda s: (s, 0, 0))],
        out_shape=[jax.ShapeDtypeStruct((b * l, c), BF16), jax.ShapeDtypeStruct((n, SUBLANES, c), F32)],
        compiler_params=_cparams("parallel"),
        name="xattn",
    )(x.reshape(b * l, d), xs_pad, wq.array, wg.array, k, v, ks, vs)
    return out.reshape(b, l, c), outs


def _merge(rg, ssd, xa, x, w_out, ln_g, ln_b, tile, name):
    n, d = x.shape
    c = rg.shape[1]
    tile = min(tile, n)
    row = lambda w: pl.BlockSpec((tile, w), lambda i: (i, 0))
    return pl.pallas_call(
        functools.partial(_merge_kernel, sub_rows=min(MERGE_SUB_ROWS, tile)),
        grid=(n // tile,),
        in_specs=[row(c), row(c), row(c), row(d), _full(w_out.shape), _full(ln_g.shape), _full(ln_b.shape)],
        out_specs=row(d),
        out_shape=jax.ShapeDtypeStruct((n, d), F32),
        compiler_params=_cparams("parallel"),
        name=name,
    )(rg, ssd, xa, x, w_out, ln_g, ln_b)


class _Cols(NamedTuple):
    array: jax.Array
    width: int
    index: int

    @property
    def shape(self):
        return (self.array.shape[0], self.width)

    @property
    def spec(self):
        return pl.BlockSpec(self.shape, lambda *_, i=self.index: (0, i))


def _weight_prep_kernel(in_blk_ref, out_blk_ref, shift_ref, valid_ref, a_ref, b_ref, wo_ref, o_ref, oo_ref, *,
                        shift_rows):
    del in_blk_ref, out_blk_ref
    i = pl.program_id(0)
    rows = a_ref.shape[0]
    keep = lax.broadcasted_iota(jnp.int32, a_ref.shape, 0) < valid_ref[i]

    @pl.when(shift_ref[i] == 0)
    def _():
        o_ref[...] = jnp.where(keep, a_ref[...], 0.0).T.astype(BF16)

    @pl.when(shift_ref[i] != 0)
    def _():
        blk = jnp.concatenate([a_ref[shift_rows:, :], b_ref[:shift_rows, :]], axis=0)
        o_ref[...] = jnp.where(keep, blk, 0.0).T.astype(BF16)

    oo_ref[...] = wo_ref[...].astype(BF16)


def _weight_prep(w_in_t, w_out, bounds, placement):
    n_rows, k = w_in_t.shape
    in_blk, out_blk, shift, valid = [], [], [], []
    shifts = {lo % MXU_WIDTH for lo, _ in bounds} - {0}
    assert len(shifts) <= 1
    shift_rows = shifts.pop() if shifts else SUBLANES
    assert shift_rows % SUBLANES == 0
    for (lo, hi), place in zip(bounds, placement):
        for j in range(-(-(hi - lo) // MXU_WIDTH)):
            in_blk.append((lo + j * MXU_WIDTH) // MXU_WIDTH)
            out_blk.append(place + j)
            shift.append(lo % MXU_WIDTH)
            valid.append(min(MXU_WIDTH, hi - lo - j * MXU_WIDTH))
    steps = len(in_blk)
    n_cols = (max(out_blk) + 1) * MXU_WIDTH
    last_in = -(-n_rows // MXU_WIDTH) - 1
    oo_rows = -(-(-(-w_out.shape[0] // steps)) // SUBLANES) * SUBLANES
    oo_steps = -(-w_out.shape[0] // oo_rows)
    assert oo_steps <= steps
    tables = [jnp.asarray(np.asarray(t, np.int32)) for t in (in_blk, out_blk, shift, valid)]
    wo_spec = pl.BlockSpec((oo_rows, w_out.shape[1]), lambda i, *_: (jnp.minimum(i, oo_steps - 1), 0))
    return pl.pallas_call(
        functools.partial(_weight_prep_kernel, shift_rows=shift_rows),
        grid_spec=pltpu.PrefetchScalarGridSpec(
            num_scalar_prefetch=4,
            grid=(steps,),
            in_specs=[pl.BlockSpec((MXU_WIDTH, k), lambda i, ib, ob, sh, va: (ib[i], 0)),
                      pl.BlockSpec((MXU_WIDTH, k), lambda i, ib, ob, sh, va: (jnp.minimum(ib[i] + 1, last_in), 0)),
                      wo_spec],
            out_specs=[pl.BlockSpec((k, MXU_WIDTH), lambda i, ib, ob, sh, va: (0, ob[i])), wo_spec],
        ),
        out_shape=[jax.ShapeDtypeStruct((k, n_cols), BF16), jax.ShapeDtypeStruct(w_out.shape, BF16)],
        compiler_params=_cparams("arbitrary"),
        name="weight_prep",
    )(*tables, w_in_t, w_in_t, w_out)


def _layer_params(w_in, rg_conv_w, rg_conv_b, w_rg_a, b_rg_a, w_rg_i, b_rg_i, rg_lambda, ssd_conv_w,
                  ssd_conv_b, ssd_dt_bias, ssd_a_log, ssd_d, ssd_norm_g, w_out, ln_g, ln_b):
    d_rg = rg_conv_w.shape[1]
    d_conv = ssd_conv_w.shape[1]
    d_ssd = ssd_norm_g.shape[0]
    heads = ssd_d.shape[0]
    sizes = (d_rg, d_rg, d_conv, d_ssd, heads)
    offs = [0]
    for s in sizes:
        offs.append(offs[-1] + s)
    d_xa = (w_in.shape[1] - offs[-1]) // 2
    offs += [offs[-1] + d_xa, offs[-1] + 2 * d_xa]
    blocks = lambda w: -(-w // MXU_WIDTH)
    assert d_rg == d_ssd == d_xa and blocks(d_conv) * MXU_WIDTH <= 2 * d_rg and heads <= LANES
    unit = blocks(d_rg)
    place = dict(xbc=0, z=2 * unit, rg_x=3 * unit, rg_g=4 * unit, xa_q=5 * unit, xa_g=6 * unit, dt=7 * unit)
    order = ("rg_x", "rg_g", "xbc", "z", "dt", "xa_q", "xa_g")
    wall, wo = _weight_prep(jnp.swapaxes(w_in, 0, 1), w_out, list(zip(offs[:-1], offs[1:])),
                            [place[k] for k in order])
    col = lambda key, width: _Cols(wall, width, place[key] * MXU_WIDTH // width)
    wx, wg, wxbc, wz = col("rg_x", d_rg), col("rg_g", d_rg), col("xbc", d_conv), col("z", d_ssd)
    wdt, wq, wxg = col("dt", LANES), col("xa_q", d_xa), col("xa_g", d_xa)
    row = lambda v: v.reshape(1, -1).astype(F32)
    pad_lanes = lambda v: jnp.pad(v, ((0, 0), (0, LANES - v.shape[1])))
    rg = dict(wx=wx, wg=wg, cw=rg_conv_w, cb=row(rg_conv_b),
              wgate=jnp.concatenate([w_rg_a, w_rg_i], axis=2).astype(BF16),
              ba=row(b_rg_a), bi=row(b_rg_i), lam=row(rg_lambda))
    head_of_channel = np.arange(d_ssd) // SSD_HEAD_DIM
    ssd = dict(wxbc=wxbc, wz=wz, wdt=wdt,
               cw=ssd_conv_w, cb=row(ssd_conv_b), dtb=pad_lanes(row(ssd_dt_bias)),
               alog=pad_lanes(row(ssd_a_log)), dexp=row(jnp.repeat(ssd_d, SSD_HEAD_DIM)), ng=row(ssd_norm_g),
               expand=jnp.asarray(np.arange(LANES)[:, None] == head_of_channel[None, :], dtype=BF16))
    xa = dict(wq=wq, wg=wxg)
    merge = dict(w=wo, g=row(ln_g), b=row(ln_b))
    return rg, ssd, xa, merge


PROMPT_RG_TILE = 256
PROMPT_XA_TILE = 256
MEMKV_BATCHES = 4
MERGE_TILE = 1024
MERGE_SUB_ROWS = 256
SAMPLE_RG_SEQS = 16
SAMPLE_SSD_SEQS = 16


def kernel(x_prompt, x_sample, mem_prompt, state_rg_conv, state_rg_h, state_ssd_conv, state_ssd_h,
           cache_mem_k, cache_mem_v, w_in, rg_conv_w, rg_conv_b, w_rg_a, b_rg_a, w_rg_i, b_rg_i,
           rg_lambda, ssd_conv_w, ssd_conv_b, ssd_dt_bias, ssd_a_log, ssd_d, ssd_norm_g, w_mem_k,
           w_mem_v, w_out, ln_g, ln_b):
    assert w_in.shape[0] == DEPTH
    bp, lp, d = x_prompt.shape
    bs, ls, _ = x_sample.shape
    heads = ssd_d.shape[1]
    outs = {k: [] for k in ("rgc_p", "rgh_p", "sc_p", "sh_p", "mk_p", "mv_p", "rgc_s", "rgh_s", "sc_s", "sh_s")}
    yp, ys = x_prompt, x_sample
    pad_rows = lambda v, before, after: jnp.pad(v, ((0, 0), (before, after), (0, 0)))
    for l in range(DEPTH):
        rg, ssd, xa, merge = _layer_params(
            w_in[l], rg_conv_w[l], rg_conv_b[l], w_rg_a[l], b_rg_a[l], w_rg_i[l], b_rg_i[l], rg_lambda[l],
            ssd_conv_w[l], ssd_conv_b[l], ssd_dt_bias[l], ssd_a_log[l], ssd_d[l], ssd_norm_g[l],
            w_out[l], ln_g[l], ln_b[l])
        xs_pad = pad_rows(ys, 0, SAMPLE_PAD - ls)
        mk, mv, mkb, mvb = _memkv(mem_prompt, w_mem_k[l].astype(BF16), w_mem_v[l].astype(BF16))
        rg_o, rgc, rgh = _rg_prompt(yp, rg, PROMPT_RG_TILE)
        xa_o, xa_s = _xattn(yp, xs_pad, xa["wq"], xa["wg"], mkb, mvb, cache_mem_k[l], cache_mem_v[l],
                            PROMPT_XA_TILE)
        yp, sc, sh = _ssd_merge_prompt(yp, ssd, rg_o, xa_o, merge)
        outs["rgc_p"].append(rgc)
        outs["rgh_p"].append(rgh.reshape(bp, -1))
        outs["sc_p"].append(sc)
        outs["sh_p"].append(sh.reshape(bp, heads, SSD_HEAD_DIM, SSD_STATE))
        outs["mk_p"].append(mk)
        outs["mv_p"].append(mv)
        tail = SAMPLE_PAD - (CONV_W - 1)
        rg_o, rgc, rgh = _rg_sample(xs_pad, rg, pad_rows(state_rg_conv[l], tail, 0),
                                    state_rg_h[l][:, None, :], SAMPLE_RG_SEQS, ls)
        ssd_o, sc, sh = _ssd_sample(xs_pad, ssd, pad_rows(state_ssd_conv[l], tail, 0),
                                    state_ssd_h[l].reshape(bs, heads * SSD_HEAD_DIM, SSD_STATE),
                                    SAMPLE_SSD_SEQS, ls)
        flat = lambda v: v.reshape(bs * SAMPLE_PAD, v.shape[-1])
        ys_pad = _merge(flat(rg_o), flat(ssd_o), flat(xa_s), flat(xs_pad), merge["w"], merge["g"], merge["b"],
                        MERGE_TILE, "merge_sample").reshape(bs, SAMPLE_PAD, d)
        ys = ys_pad[:, :ls, :]
        outs["rgc_s"].append(rgc)
        outs["rgh_s"].append(rgh.reshape(bs, -1))
        outs["sc_s"].append(sc)
        outs["sh_s"].append(sh.reshape(bs, heads, SSD_HEAD_DIM, SSD_STATE))
    st = lambda k: jnp.stack(outs[k])
    return (yp, ys, st("rgc_p"), st("rgh_p"), st("sc_p"), st("sh_p"), st("mk_p"), st("mv_p"),
            st("rgc_s"), st("rgh_s"), st("sc_s"), st("sh_s"))
```

```python
import functools
from typing import NamedTuple

import jax
import jax.numpy as jnp
import numpy as np
from jax import lax
from jax.experimental import pallas as pl
from jax.experimental.pallas import tpu as pltpu

F32 = jnp.float32
BF16 = jnp.bfloat16

SUBLANES = 8
LANES = 128
MXU_WIDTH = 256
VMEM_LIMIT_BYTES = 56 * 1024 * 1024

RG_C = 8.0
CONV_W = 4
RG_BLOCKS = 8
SSD_HEAD_DIM = 64
SSD_GROUPS = 2
SSD_STATE = 128
SSD_CHUNK = 128
MEM_HEADS = 4
LN_EPS = 1e-5
RMS_EPS = 1e-5
DEPTH = 1
DEEPNORM_ALPHA = (2 * DEPTH) ** 0.25
SAMPLE_PAD = SUBLANES


def _cparams(*sem):
    return pltpu.CompilerParams(dimension_semantics=sem, vmem_limit_bytes=VMEM_LIMIT_BYTES)


def _mm(a, b):
    return jnp.dot(a.astype(BF16), b.astype(BF16), preferred_element_type=F32)


def _mm_nt(a, b):
    return lax.dot_general(a.astype(BF16), b.astype(BF16), (((1,), (1,)), ((), ())),
                           preferred_element_type=F32)


def _mm_tn(a, b):
    return lax.dot_general(a.astype(BF16), b.astype(BF16), (((0,), (0,)), ((), ())),
                           preferred_element_type=F32)


def _sigmoid(x):
    return 1.0 / (1.0 + jnp.exp(-x))


def _silu(x):
    return x * _sigmoid(x)


def _softplus(x):
    return jnp.maximum(x, 0.0) + jnp.log(1.0 + jnp.exp(-jnp.abs(x)))


def _causal_conv_tiles(x3, p3, w, b):
    row = lax.broadcasted_iota(jnp.int32, x3.shape, 1)
    y = x3 * w[CONV_W - 1:CONV_W][None]
    for s in range(1, CONV_W):
        shifted = jnp.where(row >= s, pltpu.roll(x3, s, axis=1), pltpu.roll(p3, s, axis=1))
        y = y + shifted * w[CONV_W - 1 - s:CONV_W - s][None]
    return y + b[None]


def _scan_in_tiles(a3, b3):
    row = lax.broadcasted_iota(jnp.int32, a3.shape, 1)
    s = 1
    while s < SUBLANES:
        keep = row >= s
        a_sh = jnp.where(keep, pltpu.roll(a3, s, axis=1), 1.0)
        b_sh = jnp.where(keep, pltpu.roll(b3, s, axis=1), 0.0)
        b3 = a3 * b_sh + b3
        a3 = a3 * a_sh
        s *= 2
    return a3, b3


def _rg_gates(u, wgate, ba, bi, lam):
    pre = _mm(u, wgate)
    r = _sigmoid(pre[:, :LANES] + ba)
    i = _sigmoid(pre[:, LANES:] + bi)
    neg_log_a = r * (RG_C * _softplus(-lam))
    a = jnp.exp(-neg_log_a)
    v = jnp.tanh(neg_log_a) * (1.0 + a * a)
    mult = jnp.where(v > 0.0, v * lax.rsqrt(v), 0.0)
    return a, mult * (i * u)


def _time_strided_conv(x3, tail3, w, b):
    slabs = x3.shape[0]
    row = lax.broadcasted_iota(jnp.int32, tail3.shape, 1)
    wrapped = jnp.where(row >= 1, pltpu.roll(x3[slabs - (CONV_W - 1):], 1, axis=1), pltpu.roll(tail3, 1, axis=1))
    y = x3 * w[CONV_W - 1:CONV_W][None]
    for s in range(1, CONV_W):
        shifted = jnp.concatenate([wrapped[CONV_W - 1 - s:], x3[:slabs - s]], axis=0)
        y = y + shifted * w[CONV_W - 1 - s:CONV_W - s][None]
    return y + b[None]


def _rg_prompt_kernel(x_ref, perm_ref, permt_ref, wx_ref, wg_ref, cw_ref, cb_ref, wgate_ref, ba_ref,
                      bi_ref, lam_ref, out_ref, conv_ref, hlast_ref, tail_scr, h_scr, proj_scr, outp_scr, *,
                      tiles_per_seq):
    s = pl.program_id(0)

    @pl.when(s == 0)
    def _():
        proj_scr[...] = jnp.zeros_like(proj_scr)

    @pl.when(jnp.logical_or(s == 0, s % tiles_per_seq == 1 % tiles_per_seq))
    def _():
        tail_scr[...] = jnp.zeros_like(tail_scr)
        h_scr[...] = jnp.zeros_like(h_scr)

    rows = x_ref.shape[0]
    slabs = rows // SUBLANES
    width = wx_ref.shape[1]
    per_group = MXU_WIDTH // LANES
    row = lax.broadcasted_iota(jnp.int32, (SUBLANES, LANES), 0)
    xb = jnp.dot(perm_ref[...], x_ref[...].astype(BF16), preferred_element_type=F32).astype(BF16)

    for cg in range(width // MXU_WIDTH):
        gs = slice(cg * MXU_WIDTH, (cg + 1) * MXU_WIDTH)
        for kk in range(per_group):
            k = cg * per_group + kk
            ks = slice(k * LANES, (k + 1) * LANES)
            x3 = proj_scr[:, ks].reshape(slabs, SUBLANES, LANES)
            tail3 = tail_scr[:, :, ks]
            last3 = x3[slabs - (CONV_W - 1):]
            tail_scr[:, :, ks] = last3
            for i in range(CONV_W - 1):
                conv_ref[0, i:i + 1, ks] = last3[i, SUBLANES - 1:, :]
            u3 = _time_strided_conv(x3, tail3, cw_ref[:, ks], cb_ref[:, ks])
            a, b = _rg_gates(u3.reshape(rows, LANES), wgate_ref[k], ba_ref[:, ks], bi_ref[:, ks], lam_ref[:, ks])
            a3 = a.reshape(slabs, SUBLANES, LANES)
            b3 = b.reshape(slabs, SUBLANES, LANES)
            h_loc, a_cum = [b3[0]], [a3[0]]
            for j in range(1, slabs):
                h_loc.append(a3[j] * h_loc[j - 1] + b3[j])
                a_cum.append(a3[j] * a_cum[j - 1])
            a_run, h_run = _scan_in_tiles(a_cum[-1][None], h_loc[-1][None])
            h_prev = h_scr[:, ks]
            h_end = h_run[0] + a_run[0] * h_prev
            h_in = jnp.where(row >= 1, pltpu.roll(h_end, 1, axis=0), h_prev)
            h_scr[:, ks] = h_end[SUBLANES - 1:]
            hlast_ref[0, :, ks] = h_end[SUBLANES - 1:]
            h = jnp.concatenate([h_loc[j] + a_cum[j] * h_in for j in range(slabs)], axis=0)
            gate = proj_scr[:, width + k * LANES:width + (k + 1) * LANES]
            outp_scr[:, ks] = (h * _silu(gate)).astype(BF16)
        proj_scr[:, gs] = jnp.dot(xb, wx_ref[:, gs], preferred_element_type=F32)
        proj_scr[:, width + cg * MXU_WIDTH:width + (cg + 1) * MXU_WIDTH] = jnp.dot(
            xb, wg_ref[:, gs], preferred_element_type=F32)

    out_ref[...] = jnp.dot(permt_ref[...], outp_scr[...], preferred_element_type=F32).astype(out_ref.dtype)


def _rg_sample_kernel(x_ref, wx_ref, wg_ref, cw_ref, cb_ref, wgate_ref, ba_ref, bi_ref, lam_ref,
                      cstate_ref, h0_ref, out_ref, conv_ref, hlast_ref, *, valid):
    seqs = x_ref.shape[0]
    rows = seqs * SUBLANES
    xb = x_ref[...].reshape(rows, x_ref.shape[2]).astype(BF16)
    rgx = jnp.dot(xb, wx_ref[...], preferred_element_type=F32)
    gate = jnp.dot(xb, wg_ref[...], preferred_element_type=F32)
    x3 = rgx.reshape(seqs, SUBLANES, rgx.shape[1])
    conv_ref[...] = x3[:, valid - (CONV_W - 1):valid, :]
    row = lax.broadcasted_iota(jnp.int32, (seqs, SUBLANES, LANES), 1)
    for k in range(RG_BLOCKS):
        ks = slice(k * LANES, (k + 1) * LANES)
        u3 = _causal_conv_tiles(x3[:, :, ks], cstate_ref[:, :, ks], cw_ref[:, ks], cb_ref[:, ks])
        u = u3.reshape(rows, LANES)
        a, b = _rg_gates(u, wgate_ref[k], ba_ref[:, ks], bi_ref[:, ks], lam_ref[:, ks])
        a3 = a.reshape(seqs, SUBLANES, LANES)
        b3 = b.reshape(seqs, SUBLANES, LANES)
        b3 = b3 + jnp.where(row == 0, a3 * h0_ref[:, :, ks], 0.0)
        _, h3 = _scan_in_tiles(a3, b3)
        hlast_ref[:, :, ks] = h3[:, valid - 1:valid, :]
        g3 = gate[:, ks].reshape(seqs, SUBLANES, LANES)
        out_ref[:, :, ks] = (h3 * _silu(g3)).astype(out_ref.dtype)


def _cumsum_rows(x):
    rows = x.shape[0]
    row = lax.broadcasted_iota(jnp.int32, x.shape, 0)
    s = 1
    while s < rows:
        x = x + jnp.where(row >= s, pltpu.roll(x, s, axis=0), 0.0)
        s *= 2
    return x


def _group_rmsnorm(y, gain):
    width = y.shape[-1] // SSD_GROUPS
    parts = []
    for g in range(SSD_GROUPS):
        yg = y[..., g * width:(g + 1) * width]
        ms = jnp.sum(yg * yg, axis=-1, keepdims=True) * (1.0 / width)
        parts.append(yg * lax.rsqrt(ms + RMS_EPS))
    return jnp.concatenate(parts, axis=-1) * gain


def _time_strided_cumsum(x):
    slabs = x.shape[0] // SUBLANES
    x3 = x.reshape(slabs, SUBLANES, x.shape[1])
    acc = [x3[0]]
    for j in range(1, slabs):
        acc.append(acc[j - 1] + x3[j])
    _, run = _scan_in_tiles(jnp.ones_like(acc[-1])[None], acc[-1][None])
    row = lax.broadcasted_iota(jnp.int32, run[0].shape, 0)
    before = jnp.where(row >= 1, pltpu.roll(run[0], 1, axis=0), 0.0)
    return jnp.concatenate([a + before for a in acc], axis=0)


def _ssd_merge_prompt_kernel(x_ref, perm_ref, permt_ref, wxbc_ref, wz_ref, wdt_ref, cw_ref, cb_ref, dtb_ref,
                             alog_ref, dexp_ref, ng_ref, rg_ref, xa_ref, xres_ref, wout_ref, lng_ref, lnb_ref,
                             y_ref, conv_ref, h_ref, tail_scr, proj_scr, xb_scr, h_scr, ssd_scr, xbc_scr, *,
                             tiles_per_seq, n_tiles):
    s = pl.program_id(0)

    @pl.when(s == 0)
    def _():
        proj_scr[...] = jnp.zeros_like(proj_scr)
        xb_scr[...] = jnp.zeros_like(xb_scr)
        ssd_scr[...] = jnp.zeros_like(ssd_scr)

    @pl.when(jnp.logical_or(s == 0, s % tiles_per_seq == 1 % tiles_per_seq))
    def _():
        tail_scr[...] = jnp.zeros_like(tail_scr)
        h_scr[...] = jnp.zeros_like(h_scr)

    q = SSD_CHUNK
    n_sub = x_ref.shape[0] // q
    slabs = q // SUBLANES
    d_ssd = wz_ref.shape[1]
    d_conv = wxbc_ref.shape[1]
    gn = SSD_GROUPS * SSD_STATE
    tail3 = tail_scr[...]
    for c in range(n_sub):
        x3 = proj_scr[c * q:(c + 1) * q, :d_conv].reshape(slabs, SUBLANES, d_conv)
        xbc_scr[c * q:(c + 1) * q, :] = _silu(_time_strided_conv(x3, tail3, cw_ref[...], cb_ref[...])).reshape(
            q, d_conv)
        tail3 = x3[slabs - (CONV_W - 1):]
    tail_scr[...] = tail3
    dtr_all = proj_scr[:, d_conv:]

    mix, row0 = None, 0
    for src in (rg_ref, ssd_scr, xa_ref):
        part = jnp.dot(src[...], wout_ref[row0:row0 + src.shape[1], :], preferred_element_type=F32)
        mix = part if mix is None else mix + part
        row0 += src.shape[1]
    res = DEEPNORM_ALPHA * xres_ref[...] + mix
    mu = jnp.mean(res, axis=-1, keepdims=True)
    cen = res - mu
    var = jnp.mean(cen * cen, axis=-1, keepdims=True)
    y_ref[...] = cen * lax.rsqrt(var + LN_EPS) * lng_ref[...] + lnb_ref[...]

    xin = x_ref[...].astype(BF16)
    xb_new = jnp.concatenate(
        [jnp.dot(perm_ref[...], xin[c * q:(c + 1) * q], preferred_element_type=F32).astype(BF16)
         for c in range(n_sub)], axis=0)
    z_all = jnp.dot(xb_scr[...], wz_ref[...], preferred_element_type=F32)
    xb_scr[...] = xb_new

    ii = lax.broadcasted_iota(jnp.int32, (q, q), 0)
    jj = lax.broadcasted_iota(jnp.int32, (q, q), 1)
    time_of = lambda r: (r % SUBLANES) * slabs + r // SUBLANES
    causal = time_of(ii) >= time_of(jj)
    lane = lax.broadcasted_iota(jnp.int32, (q, LANES), 1)
    lo = lane < SSD_HEAD_DIM
    srow = lax.broadcasted_iota(jnp.int32, (LANES, SSD_STATE), 0) < SSD_HEAD_DIM
    heads = d_ssd // SSD_HEAD_DIM
    pairs = heads // 2
    pairs_per_group = pairs // SSD_GROUPS

    projections = [(wxbc_ref, slice(g * MXU_WIDTH, (g + 1) * MXU_WIDTH), slice(g * MXU_WIDTH, (g + 1) * MXU_WIDTH))
                   for g in range(d_conv // MXU_WIDTH)]
    projections.append((wdt_ref, slice(None), slice(d_conv, None)))
    every = (n_sub * pairs) // len(projections)
    assert every >= 1
    issued = 0

    for c in range(n_sub):
        cs = slice(c * q, (c + 1) * q)
        sx = xbc_scr[cs, :d_ssd]
        bm = xbc_scr[cs, d_ssd:d_ssd + gn]
        cm = xbc_scr[cs, d_ssd + gn:]
        dt = _softplus(dtr_all[cs] + dtb_ref[...])
        da = dt * (-jnp.exp(alog_ref[...]))
        acum = _time_strided_cumsum(da)
        alast = acum[q - 1:q, :]
        wgt = dt * jnp.exp(alast - acum)
        tot = jnp.exp(alast)
        acum_t = acum.T
        dt_t = dt.T
        cb = [_mm_nt(cm[:, g * SSD_STATE:(g + 1) * SSD_STATE], bm[:, g * SSD_STATE:(g + 1) * SSD_STATE])
              for g in range(SSD_GROUPS)]
        y_parts = []
        for g in range(SSD_GROUPS):
            ns = slice(g * SSD_STATE, (g + 1) * SSD_STATE)
            rows_g = slice(g * pairs_per_group * LANES, (g + 1) * pairs_per_group * LANES)
            h_in = h_scr[rows_g, :]
            y_off_g = _mm_nt(cm[:, ns], h_in)
            xws, decays = [], []
            for pl_ in range(pairs_per_group):
                pq = g * pairs_per_group + pl_
                ps = slice(pq * LANES, (pq + 1) * LANES)
                xq = sx[:, ps]
                ms, es, ws, ts = [], [], [], []
                for h in (2 * pq, 2 * pq + 1):
                    acol = jnp.broadcast_to(acum[:, h:h + 1], (q, q))
                    arow = jnp.broadcast_to(acum_t[h:h + 1, :], (q, q))
                    decay = jnp.exp(jnp.where(causal, acol - arow, -jnp.inf))
                    ms.append((cb[g] * decay * jnp.broadcast_to(dt_t[h:h + 1, :], (q, q))).astype(BF16))
                    es.append(jnp.exp(jnp.broadcast_to(acum[:, h:h + 1], (q, LANES))))
                    ws.append(jnp.broadcast_to(wgt[:, h:h + 1], (q, LANES)))
                    ts.append(jnp.broadcast_to(tot[:, h:h + 1], (LANES, SSD_STATE)))
                lhs = jnp.concatenate(ms, axis=1)
                rhs = jnp.concatenate([jnp.where(lo, xq, 0.0), jnp.where(lo, 0.0, xq)], axis=0)
                y_diag = _mm(lhs, rhs)
                y_off = y_off_g[:, pl_ * LANES:(pl_ + 1) * LANES] * jnp.where(lo, es[0], es[1])
                xws.append(xq * jnp.where(lo, ws[0], ws[1]))
                decays.append(jnp.where(srow, ts[0], ts[1]))
                y_parts.append(y_diag + y_off + dexp_ref[:, ps] * xq)
                if (c * pairs + pq + 1) % every == 0 and issued < len(projections):
                    w_ref, src, dst = projections[issued]
                    proj_scr[:, dst] = jnp.dot(xb_new, w_ref[:, src], preferred_element_type=F32)
                    issued += 1
            h_scr[rows_g, :] = (h_in * jnp.concatenate(decays, axis=0)
                                + _mm_tn(jnp.concatenate(xws, axis=1), bm[:, ns]))
        y = jnp.concatenate(y_parts, axis=1) * _silu(z_all[cs])
        y = _group_rmsnorm(y, ng_ref[...]).astype(BF16)
        ssd_scr[cs, :] = jnp.dot(permt_ref[...], y, preferred_element_type=F32).astype(BF16)
    assert issued == len(projections)

    @pl.when(jnp.logical_and(s % tiles_per_seq == 0, jnp.logical_and(s > 0, s <= n_tiles)))
    def _():
        h_ref[0] = h_scr[...]
        for i in range(CONV_W - 1):
            conv_ref[0, i:i + 1, :] = tail_scr[i, SUBLANES - 1:, :]


def _ssd_sample_kernel(x_ref, wxbc_ref, wz_ref, wdt_ref, cw_ref, cb_ref, dtb_ref, alog_ref, dexp_ref,
                       ng_ref, expand_ref, cstate_ref, h0_ref, out_ref, conv_ref, h_ref,
                       c_scr, b_scr, xw_scr, tot_scr, yoff_scr, *, valid):
    seqs = x_ref.shape[0]
    rows = seqs * SUBLANES
    d_ssd = wz_ref.shape[1]
    gn = SSD_GROUPS * SSD_STATE
    heads = d_ssd // SSD_HEAD_DIM
    hg = heads // SSD_GROUPS
    xb = x_ref[...].reshape(rows, x_ref.shape[2]).astype(BF16)
    xbc_raw = jnp.dot(xb, wxbc_ref[...], preferred_element_type=F32)
    z = jnp.dot(xb, wz_ref[...], preferred_element_type=F32)
    dtr = jnp.dot(xb, wdt_ref[...], preferred_element_type=F32)

    x3 = xbc_raw.reshape(seqs, SUBLANES, xbc_raw.shape[1])
    conv_ref[...] = x3[:, valid - (CONV_W - 1):valid, :]
    xbc3 = _silu(_causal_conv_tiles(x3, cstate_ref[...], cw_ref[...], cb_ref[...]))
    sx3 = xbc3[:, :, :d_ssd]
    b3 = xbc3[:, :, d_ssd:d_ssd + gn]
    c3 = xbc3[:, :, d_ssd + gn:]

    row = lax.broadcasted_iota(jnp.int32, (seqs, SUBLANES, LANES), 1)
    lane = lax.broadcasted_iota(jnp.int32, (seqs, SUBLANES, LANES), 2)
    dt3 = jnp.where(row < valid, _softplus(dtr + dtb_ref[...]).reshape(seqs, SUBLANES, LANES), 0.0)
    da3 = dt3 * (-jnp.exp(alog_ref[...]))[None]
    ones = jnp.ones_like(da3)
    _, acum3 = _scan_in_tiles(ones, da3)
    alast = acum3[:, SUBLANES - 1:, :]
    wgt3 = dt3 * jnp.exp(alast - acum3)
    tot_scr[...] = jnp.exp(alast)
    e3 = jnp.exp(acum3)

    coefs = []
    for u in range(valid):
        prod = c3 * b3[:, u:u + 1, :]
        cbu = [jnp.sum(prod[:, :, g * SSD_STATE:(g + 1) * SSD_STATE], axis=-1, keepdims=True)
               for g in range(SSD_GROUPS)]
        cb_heads = jnp.where(lane < hg, cbu[0], cbu[1])
        coefs.append(jnp.where(row >= u, cb_heads * jnp.exp(acum3 - acum3[:, u:u + 1, :]) * dt3[:, u:u + 1, :],
                               0.0))

    per_head = coefs + [wgt3, e3]
    stacked = jnp.concatenate([v.reshape(rows, LANES) for v in per_head], axis=0)
    expanded = None
    rest = stacked
    for _ in range(3):
        piece = rest.astype(BF16)
        rest = rest - piece.astype(F32)
        part = jnp.dot(piece, expand_ref[...], preferred_element_type=F32)
        expanded = part if expanded is None else expanded + part
    expanded = [expanded[i * rows:(i + 1) * rows].reshape(seqs, SUBLANES, d_ssd) for i in range(len(per_head))]
    y_diag = expanded[0] * sx3[:, 0:1, :]
    for u in range(1, valid):
        y_diag = y_diag + expanded[u] * sx3[:, u:u + 1, :]
    wgt_x, e_x = expanded[valid], expanded[valid + 1]

    c_scr[...] = c3
    b_scr[...] = b3
    xw_scr[...] = sx3 * wgt_x
    srow = lax.broadcasted_iota(jnp.int32, (2 * SSD_HEAD_DIM, SSD_STATE), 0) < SSD_HEAD_DIM

    def per_seq(s, carry):
        cs = c_scr[s]
        bs = b_scr[s]
        xws = xw_scr[s]
        tots = tot_scr[s]
        for g in range(SSD_GROUPS):
            gs = slice(g * SSD_STATE, (g + 1) * SSD_STATE)
            width = hg * SSD_HEAD_DIM
            cols = slice(g * width, (g + 1) * width)
            hin = h0_ref[s, cols, :]
            yoff_scr[s, :, cols] = _mm_nt(cs[:, gs], hin)
            upd = _mm_tn(xws[:, cols], bs[:, gs])
            for pq in range(hg // 2):
                h = g * hg + 2 * pq
                rs = slice(pq * LANES, (pq + 1) * LANES)
                t0 = jnp.broadcast_to(tots[0:1, h:h + 1], (LANES, SSD_STATE))
                t1 = jnp.broadcast_to(tots[0:1, h + 1:h + 2], (LANES, SSD_STATE))
                h_ref[s, g * width + pq * LANES:g * width + (pq + 1) * LANES, :] = (
                    hin[rs] * jnp.where(srow, t0, t1) + upd[rs])
        return carry

    lax.fori_loop(0, seqs, per_seq, 0)

    y = y_diag + yoff_scr[...] * e_x + dexp_ref[...][None] * sx3
    y = y * _silu(z.reshape(seqs, SUBLANES, d_ssd))
    out_ref[...] = _group_rmsnorm(y, ng_ref[...][None]).astype(out_ref.dtype)


def _memkv_kernel(mem_ref, wk_ref, wv_ref, k_ref, v_ref, kb_ref, vb_ref):
    nb, m, d = mem_ref.shape
    mb = mem_ref[...].reshape(nb * m, d).astype(BF16)
    for w_ref, o_ref, ob_ref in ((wk_ref, k_ref, kb_ref), (wv_ref, v_ref, vb_ref)):
        proj = jnp.dot(mb, w_ref[...], preferred_element_type=F32)
        for i in range(nb):
            o_ref[i] = proj[i * m:(i + 1) * m].reshape(o_ref.shape[1:])
        ob_ref[...] = proj.astype(BF16).reshape(ob_ref.shape)


def _softmax_terms(scores):
    p = jnp.exp(scores - jnp.max(scores, axis=-1, keepdims=True))
    return p, jnp.sum(p, axis=-1, keepdims=True)


def _prompt_attention_head(h, q_ref, rows, k_ref, v_ref, out_ref):
    c = k_ref.shape[2]
    d_head = c // MEM_HEADS
    hs = slice(h * d_head, (h + 1) * d_head)

    def scores():
        return _softmax_terms(_mm_nt(q_ref[:rows, hs], k_ref[0, :, hs]) * (d_head ** -0.5))

    def output(p, l):
        o = _mm(p, v_ref[0, :, hs]) * (1.0 / l)
        gate = q_ref[:rows, c + h * d_head:c + (h + 1) * d_head]
        out_ref[:, hs] = (o * _silu(gate)).astype(out_ref.dtype)

    return scores, output


def _sample_attention_seq(s, q_ref, rows, k_ref, v_ref, out_ref):
    _, n_mem, heads, d_head = k_ref.shape
    c = heads * d_head
    rs = slice(rows + s * SUBLANES, rows + (s + 1) * SUBLANES)

    def scores():
        shape = (heads * SUBLANES, n_mem * heads)
        same_head = (lax.broadcasted_iota(jnp.int32, shape, 0) // SUBLANES
                     == lax.broadcasted_iota(jnp.int32, shape, 1) % heads)
        qh = jnp.concatenate([q_ref[rs, h * d_head:(h + 1) * d_head] for h in range(heads)], axis=0)
        sc = _mm_nt(qh, k_ref[s].reshape(n_mem * heads, d_head)) * (d_head ** -0.5)
        return _softmax_terms(jnp.where(same_head, sc, -jnp.inf))

    def output(p, l):
        o = _mm(p, v_ref[s].reshape(n_mem * heads, d_head)) * (1.0 / l)
        o = jnp.concatenate([o[h * SUBLANES:(h + 1) * SUBLANES] for h in range(heads)], axis=1)
        out_ref[s] = (o * _silu(q_ref[rs, c:])).astype(out_ref.dtype)

    return scores, output


def _xattn_kernel(x_ref, xs_ref, wq_ref, wg_ref, k_ref, v_ref, ks_ref, vs_ref, out_ref, outs_ref):
    rows = x_ref.shape[0]
    seqs, _, d = xs_ref.shape
    xb = jnp.concatenate([x_ref[...], xs_ref[...].reshape(seqs * SUBLANES, d)], axis=0).astype(BF16)
    qg = jnp.concatenate([jnp.dot(xb, wq_ref[...], preferred_element_type=F32),
                          jnp.dot(xb, wg_ref[...], preferred_element_type=F32)], axis=1)
    items = [_prompt_attention_head(h, qg, rows, k_ref, v_ref, out_ref) for h in range(MEM_HEADS)]
    items += [_sample_attention_seq(i, qg, rows, ks_ref, vs_ref, outs_ref) for i in range(seqs)]
    for scores, output in items:
        output(*scores())


def _merge_kernel(rg_ref, ssd_ref, xa_ref, x_ref, w_ref, g_ref, b_ref, y_ref, *, sub_rows):
    d = rg_ref.shape[1]
    for r0 in range(0, x_ref.shape[0], sub_rows):
        rs = slice(r0, r0 + sub_rows)
        mix = (jnp.dot(rg_ref[rs, :].astype(BF16), w_ref[0:d, :], preferred_element_type=F32)
               + jnp.dot(ssd_ref[rs, :].astype(BF16), w_ref[d:2 * d, :], preferred_element_type=F32)
               + jnp.dot(xa_ref[rs, :].astype(BF16), w_ref[2 * d:3 * d, :], preferred_element_type=F32))
        res = DEEPNORM_ALPHA * x_ref[rs, :] + mix
        mu = jnp.mean(res, axis=-1, keepdims=True)
        cen = res - mu
        var = jnp.mean(cen * cen, axis=-1, keepdims=True)
        y_ref[rs, :] = cen * lax.rsqrt(var + LN_EPS) * g_ref[...] + b_ref[...]


def _full(shape):
    return pl.BlockSpec(shape, lambda *_: (0,) * len(shape))


def _time_stride_perm(rows):
    p = np.arange(rows)
    t = (p % SUBLANES) * (rows // SUBLANES) + p // SUBLANES
    perm = t[:, None] == np.arange(rows)[None, :]
    return jnp.asarray(perm, dtype=BF16), jnp.asarray(perm.T, dtype=BF16)


def _rg_prompt(x, p, tile):
    b, l, d = x.shape
    c = p["wx"].shape[1]
    tiles_per_seq = l // tile
    n_tiles = b * tiles_per_seq
    perm, perm_t = _time_stride_perm(tile)
    done = lambda s: jnp.maximum(s - 1, 0)
    out, conv, hlast = pl.pallas_call(
        functools.partial(_rg_prompt_kernel, tiles_per_seq=tiles_per_seq),
        grid=(n_tiles + 1,),
        in_specs=[pl.BlockSpec((tile, d), lambda s: (jnp.minimum(s, n_tiles - 1), 0)),
                  _full(perm.shape), _full(perm.shape), p["wx"].spec, p["wg"].spec,
                  _full(p["cw"].shape), _full(p["cb"].shape), _full(p["wgate"].shape), _full(p["ba"].shape),
                  _full(p["bi"].shape), _full(p["lam"].shape)],
        out_specs=[pl.BlockSpec((tile, c), lambda s: (done(s), 0)),
                   pl.BlockSpec((1, CONV_W - 1, c), lambda s: (done(s) // tiles_per_seq, 0, 0)),
                   pl.BlockSpec((1, 1, c), lambda s: (done(s) // tiles_per_seq, 0, 0))],
        out_shape=[jax.ShapeDtypeStruct((b * l, c), BF16),
                   jax.ShapeDtypeStruct((b, CONV_W - 1, c), F32),
                   jax.ShapeDtypeStruct((b, 1, c), F32)],
        scratch_shapes=[pltpu.VMEM((CONV_W - 1, SUBLANES, c), F32), pltpu.VMEM((1, c), F32),
                        pltpu.VMEM((tile, 2 * c), F32), pltpu.VMEM((tile, c), BF16)],
        compiler_params=_cparams("arbitrary"),
        name="rg_prompt",
    )(x.reshape(b * l, d), perm, perm_t, p["wx"].array, p["wg"].array, p["cw"], p["cb"], p["wgate"], p["ba"],
      p["bi"], p["lam"])
    return out.reshape(b, l, c), conv, hlast


def _rg_sample(xpad, p, cstate, h0, seqs, valid):
    n, _, d = xpad.shape
    c = p["wx"].shape[1]
    blk = lambda w: pl.BlockSpec((seqs, w[0], w[1]), lambda i: (i, 0, 0))
    return pl.pallas_call(
        functools.partial(_rg_sample_kernel, valid=valid),
        grid=(n // seqs,),
        in_specs=[blk((SUBLANES, d)), p["wx"].spec, p["wg"].spec, _full(p["cw"].shape),
                  _full(p["cb"].shape), _full(p["wgate"].shape), _full(p["ba"].shape),
                  _full(p["bi"].shape), _full(p["lam"].shape), blk((SUBLANES, c)), blk((1, c))],
        out_specs=[blk((SUBLANES, c)), blk((CONV_W - 1, c)), blk((1, c))],
        out_shape=[jax.ShapeDtypeStruct((n, SUBLANES, c), F32),
                   jax.ShapeDtypeStruct((n, CONV_W - 1, c), F32),
                   jax.ShapeDtypeStruct((n, 1, c), F32)],
        compiler_params=_cparams("parallel"),
        name="rg_sample",
    )(xpad, p["wx"].array, p["wg"].array, p["cw"], p["cb"], p["wgate"], p["ba"], p["bi"], p["lam"], cstate, h0)


_SSD_WEIGHTS = ("wxbc", "wz", "wdt")
_SSD_SMALL = ("cw", "cb", "dtb", "alog", "dexp", "ng")


def _ssd_param_specs(p):
    return [p[k].spec for k in _SSD_WEIGHTS] + [_full(p[k].shape) for k in _SSD_SMALL]


def _ssd_param_args(p):
    return [p[k].array for k in _SSD_WEIGHTS] + [p[k] for k in _SSD_SMALL]


def _ssd_merge_prompt(x, p, rg_out, xa_out, mp):
    b, l, d = x.shape
    cc = p["wxbc"].shape[1]
    c = p["wz"].shape[1]
    tile = min(SSD_TILE, l)
    tiles_per_seq = l // tile
    n_tiles = b * tiles_per_seq
    perm, perm_t = _time_stride_perm(SSD_CHUNK)
    flat = lambda v: v.reshape(b * l, v.shape[-1])
    clamp = lambda i: jnp.clip(i, 0, n_tiles - 1)
    rows = lambda w, back: pl.BlockSpec((tile, w), lambda s: (clamp(s - back), 0))
    state = lambda shape: pl.BlockSpec((1,) + shape, lambda s: (clamp(s - 1) // tiles_per_seq, 0, 0))
    y, conv, hstate = pl.pallas_call(
        functools.partial(_ssd_merge_prompt_kernel, tiles_per_seq=tiles_per_seq, n_tiles=n_tiles),
        grid=(n_tiles + 2,),
        in_specs=[rows(d, 0), _full(perm.shape), _full(perm.shape)] + _ssd_param_specs(p)
        + [rows(rg_out.shape[-1], 2), rows(xa_out.shape[-1], 2), rows(d, 2), _full(mp["w"].shape),
           _full(mp["g"].shape), _full(mp["b"].shape)],
        out_specs=[rows(d, 2), state((CONV_W - 1, cc)), state((c, SSD_STATE))],
        out_shape=[jax.ShapeDtypeStruct((b * l, d), F32),
                   jax.ShapeDtypeStruct((b, CONV_W - 1, cc), F32),
                   jax.ShapeDtypeStruct((b, c, SSD_STATE), F32)],
        scratch_shapes=[pltpu.VMEM((CONV_W - 1, SUBLANES, cc), F32),
                        pltpu.VMEM((tile, cc + LANES), F32), pltpu.VMEM((tile, d), BF16),
                        pltpu.VMEM((c, SSD_STATE), F32), pltpu.VMEM((tile, c), BF16),
                        pltpu.VMEM((tile, cc), F32)],
        compiler_params=_cparams("arbitrary"),
        name="ssd_merge_prompt",
    )(flat(x), perm, perm_t, *_ssd_param_args(p), flat(rg_out), flat(xa_out), flat(x), mp["w"], mp["g"], mp["b"])
    return y.reshape(b, l, d), conv, hstate


def _ssd_sample(xpad, p, cstate, h0, seqs, valid):
    n, _, d = xpad.shape
    cc = p["wxbc"].shape[1]
    c = p["wz"].shape[1]
    gn = SSD_GROUPS * SSD_STATE
    blk = lambda w: pl.BlockSpec((seqs, w[0], w[1]), lambda i: (i, 0, 0))
    return pl.pallas_call(
        functools.partial(_ssd_sample_kernel, valid=valid),
        grid=(n // seqs,),
        in_specs=[blk((SUBLANES, d))] + _ssd_param_specs(p)
        + [_full(p["expand"].shape), blk((SUBLANES, cc)), blk((c, SSD_STATE))],
        out_specs=[blk((SUBLANES, c)), blk((CONV_W - 1, cc)), blk((c, SSD_STATE))],
        out_shape=[jax.ShapeDtypeStruct((n, SUBLANES, c), F32),
                   jax.ShapeDtypeStruct((n, CONV_W - 1, cc), F32),
                   jax.ShapeDtypeStruct((n, c, SSD_STATE), F32)],
        scratch_shapes=[pltpu.VMEM((seqs, SUBLANES, gn), F32), pltpu.VMEM((seqs, SUBLANES, gn), F32),
                        pltpu.VMEM((seqs, SUBLANES, c), F32), pltpu.VMEM((seqs, 1, LANES), F32),
                        pltpu.VMEM((seqs, SUBLANES, c), F32)],
        compiler_params=_cparams("parallel"),
        name="ssd_sample",
    )(xpad, *_ssd_param_args(p), p["expand"], cstate, h0)


def _memkv(mem, wk, wv):
    b, m, d = mem.shape
    c = wk.shape[1]
    nb = MEMKV_BATCHES if b % MEMKV_BATCHES == 0 else 1
    spec = pl.BlockSpec((nb, m, c), lambda i: (i, 0, 0))
    spec4 = pl.BlockSpec((nb, m, MEM_HEADS, c // MEM_HEADS), lambda i: (i, 0, 0, 0))
    return pl.pallas_call(
        _memkv_kernel,
        grid=(b // nb,),
        in_specs=[pl.BlockSpec((nb, m, d), lambda i: (i, 0, 0)), _full(wk.shape), _full(wv.shape)],
        out_specs=[spec4, spec4, spec, spec],
        out_shape=[jax.ShapeDtypeStruct((b, m, MEM_HEADS, c // MEM_HEADS), F32)] * 2
        + [jax.ShapeDtypeStruct((b, m, c), BF16)] * 2,
        compiler_params=_cparams("parallel"),
        name="mem_kv",
    )(mem, wk, wv)


def _xattn(x, xs_pad, wq, wg, k, v, ks, vs, tile):
    b, l, d = x.shape
    n = xs_pad.shape[0]
    c = wq.shape[1]
    m = k.shape[1]
    tile = min(tile, l)
    tiles_per_seq = l // tile
    n_tiles = b * tiles_per_seq
    seqs = n // n_tiles
    assert seqs * n_tiles == n
    kv_spec = pl.BlockSpec((1, m, c), lambda s: (s // tiles_per_seq, 0, 0))
    skv_spec = pl.BlockSpec((seqs,) + ks.shape[1:], lambda s: (s, 0, 0, 0))
    out, outs = pl.pallas_call(
        _xattn_kernel,
        grid=(n_tiles,),
        in_specs=[pl.BlockSpec((tile, d), lambda s: (s, 0)),
                  pl.BlockSpec((seqs, SUBLANES, d), lambda s: (s, 0, 0)),
                  wq.spec, wg.spec, kv_spec, kv_spec, skv_spec, skv_spec],
        out_specs=[pl.BlockSpec((tile, c), lambda s: (s, 0)),
                   pl.BlockSpec((seqs, SUBLANES, c), lambda s: (s, 0, 0))],
        out_shape=[jax.ShapeDtypeStruct((b * l, c), BF16), jax.ShapeDtypeStruct((n, SUBLANES, c), F32)],
        compiler_params=_cparams("parallel"),
        name="xattn",
    )(x.reshape(b * l, d), xs_pad, wq.array, wg.array, k, v, ks, vs)
    return out.reshape(b, l, c), outs


def _merge(rg, ssd, xa, x, w_out, ln_g, ln_b, tile, name):
    n, d = x.shape
    c = rg.shape[1]
    tile = min(tile, n)
    row = lambda w: pl.BlockSpec((tile, w), lambda i: (i, 0))
    return pl.pallas_call(
        functools.partial(_merge_kernel, sub_rows=min(MERGE_SUB_ROWS, tile)),
        grid=(n // tile,),
        in_specs=[row(c), row(c), row(c), row(d), _full(w_out.shape), _full(ln_g.shape), _full(ln_b.shape)],
        out_specs=row(d),
        out_shape=jax.ShapeDtypeStruct((n, d), F32),
        compiler_params=_cparams("parallel"),
        name=name,
    )(rg, ssd, xa, x, w_out, ln_g, ln_b)


class _Cols(NamedTuple):
    array: jax.Array
    width: int
    index: int

    @property
    def shape(self):
        return (self.array.shape[0], self.width)

    @property
    def spec(self):
        return pl.BlockSpec(self.shape, lambda *_, i=self.index: (0, i))


def _weight_prep_kernel(in_blk_ref, out_blk_ref, shift_ref, valid_ref, nxt_blk_ref, a_ref, b_ref, wo_ref,
                        o_ref, oo_ref, *, shift_rows):
    del in_blk_ref, out_blk_ref, nxt_blk_ref
    i = pl.program_id(0)
    rows = a_ref.shape[0]
    keep = lax.broadcasted_iota(jnp.int32, a_ref.shape, 0) < valid_ref[i]

    @pl.when(shift_ref[i] == 0)
    def _():
        o_ref[...] = jnp.where(keep, a_ref[...], 0.0).T.astype(BF16)

    @pl.when(shift_ref[i] != 0)
    def _():
        blk = jnp.concatenate([a_ref[shift_rows:, :], b_ref[:shift_rows, :]], axis=0)
        o_ref[...] = jnp.where(keep, blk, 0.0).T.astype(BF16)

    oo_ref[...] = wo_ref[...].astype(BF16)


def _weight_prep(w_in_t, w_out, bounds, placement):
    n_rows, k = w_in_t.shape
    in_blk, out_blk, shift, valid = [], [], [], []
    shifts = {lo % MXU_WIDTH for lo, _ in bounds} - {0}
    assert len(shifts) <= 1
    shift_rows = shifts.pop() if shifts else SUBLANES
    assert shift_rows % SUBLANES == 0
    for (lo, hi), place in zip(bounds, placement):
        for j in range(-(-(hi - lo) // MXU_WIDTH)):
            in_blk.append((lo + j * MXU_WIDTH) // MXU_WIDTH)
            out_blk.append(place + j)
            shift.append(lo % MXU_WIDTH)
            valid.append(min(MXU_WIDTH, hi - lo - j * MXU_WIDTH))
    steps = len(in_blk)
    n_cols = (max(out_blk) + 1) * MXU_WIDTH
    last_in = -(-n_rows // MXU_WIDTH) - 1
    oo_rows = -(-(-(-w_out.shape[0] // steps)) // SUBLANES) * SUBLANES
    oo_steps = -(-w_out.shape[0] // oo_rows)
    assert oo_steps <= steps
    nxt_blk = [min(a + 1, last_in) if sh else 0 for a, sh in zip(in_blk, shift)]
    tables = [jnp.asarray(np.asarray(t, np.int32)) for t in (in_blk, out_blk, shift, valid, nxt_blk)]
    wo_spec = pl.BlockSpec((oo_rows, w_out.shape[1]), lambda i, *_: (jnp.minimum(i, oo_steps - 1), 0))
    return pl.pallas_call(
        functools.partial(_weight_prep_kernel, shift_rows=shift_rows),
        grid_spec=pltpu.PrefetchScalarGridSpec(
            num_scalar_prefetch=5,
            grid=(steps,),
            in_specs=[pl.BlockSpec((MXU_WIDTH, k), lambda i, ib, ob, sh, va, nb: (ib[i], 0)),
                      pl.BlockSpec((MXU_WIDTH, k), lambda i, ib, ob, sh, va, nb: (nb[i], 0)),
                      wo_spec],
            out_specs=[pl.BlockSpec((k, MXU_WIDTH), lambda i, ib, ob, sh, va, nb: (0, ob[i])), wo_spec],
        ),
        out_shape=[jax.ShapeDtypeStruct((k, n_cols), BF16), jax.ShapeDtypeStruct(w_out.shape, BF16)],
        compiler_params=_cparams("arbitrary"),
        name="weight_prep",
    )(*tables, w_in_t, w_in_t, w_out)


def _layer_params(w_in, rg_conv_w, rg_conv_b, w_rg_a, b_rg_a, w_rg_i, b_rg_i, rg_lambda, ssd_conv_w,
                  ssd_conv_b, ssd_dt_bias, ssd_a_log, ssd_d, ssd_norm_g, w_out, ln_g, ln_b):
    d_rg = rg_conv_w.shape[1]
    d_conv = ssd_conv_w.shape[1]
    d_ssd = ssd_norm_g.shape[0]
    heads = ssd_d.shape[0]
    sizes = (d_rg, d_rg, d_conv, d_ssd, heads)
    offs = [0]
    for s in sizes:
        offs.append(offs[-1] + s)
    d_xa = (w_in.shape[1] - offs[-1]) // 2
    offs += [offs[-1] + d_xa, offs[-1] + 2 * d_xa]
    blocks = lambda w: -(-w // MXU_WIDTH)
    assert d_rg == d_ssd == d_xa and blocks(d_conv) * MXU_WIDTH <= 2 * d_rg and heads <= LANES
    unit = blocks(d_rg)
    place = dict(xbc=0, z=2 * unit, rg_x=3 * unit, rg_g=4 * unit, xa_q=5 * unit, xa_g=6 * unit, dt=7 * unit)
    order = ("rg_x", "rg_g", "xbc", "z", "dt", "xa_q", "xa_g")
    wall, wo = _weight_prep(jnp.swapaxes(w_in, 0, 1), w_out, list(zip(offs[:-1], offs[1:])),
                            [place[k] for k in order])
    col = lambda key, width: _Cols(wall, width, place[key] * MXU_WIDTH // width)
    wx, wg, wxbc, wz = col("rg_x", d_rg), col("rg_g", d_rg), col("xbc", d_conv), col("z", d_ssd)
    wdt, wq, wxg = col("dt", LANES), col("xa_q", d_xa), col("xa_g", d_xa)
    row = lambda v: v.reshape(1, -1).astype(F32)
    pad_lanes = lambda v: jnp.pad(v, ((0, 0), (0, LANES - v.shape[1])))
    rg = dict(wx=wx, wg=wg, cw=rg_conv_w, cb=row(rg_conv_b),
              wgate=jnp.concatenate([w_rg_a, w_rg_i], axis=2).astype(BF16),
              ba=row(b_rg_a), bi=row(b_rg_i), lam=row(rg_lambda))
    head_of_channel = np.arange(d_ssd) // SSD_HEAD_DIM
    ssd = dict(wxbc=wxbc, wz=wz, wdt=wdt,
               cw=ssd_conv_w, cb=row(ssd_conv_b), dtb=pad_lanes(row(ssd_dt_bias)),
               alog=pad_lanes(row(ssd_a_log)), dexp=row(jnp.repeat(ssd_d, SSD_HEAD_DIM)), ng=row(ssd_norm_g),
               expand=jnp.asarray(np.arange(LANES)[:, None] == head_of_channel[None, :], dtype=BF16))
    xa = dict(wq=wq, wg=wxg)
    merge = dict(w=wo, g=row(ln_g), b=row(ln_b))
    return rg, ssd, xa, merge


PROMPT_RG_TILE = 256
PROMPT_XA_TILE = 512
SSD_TILE = 256
MEMKV_BATCHES = 4
MERGE_TILE = 1024
MERGE_SUB_ROWS = 256
SAMPLE_RG_SEQS = 16
SAMPLE_SSD_SEQS = 16


def kernel(x_prompt, x_sample, mem_prompt, state_rg_conv, state_rg_h, state_ssd_conv, state_ssd_h,
           cache_mem_k, cache_mem_v, w_in, rg_conv_w, rg_conv_b, w_rg_a, b_rg_a, w_rg_i, b_rg_i,
           rg_lambda, ssd_conv_w, ssd_conv_b, ssd_dt_bias, ssd_a_log, ssd_d, ssd_norm_g, w_mem_k,
           w_mem_v, w_out, ln_g, ln_b):
    assert w_in.shape[0] == DEPTH
    bp, lp, d = x_prompt.shape
    bs, ls, _ = x_sample.shape
    heads = ssd_d.shape[1]
    outs = {k: [] for k in ("rgc_p", "rgh_p", "sc_p", "sh_p", "mk_p", "mv_p", "rgc_s", "rgh_s", "sc_s", "sh_s")}
    yp, ys = x_prompt, x_sample
    pad_rows = lambda v, before, after: jnp.pad(v, ((0, 0), (before, after), (0, 0)))
    for l in range(DEPTH):
        rg, ssd, xa, merge = _layer_params(
            w_in[l], rg_conv_w[l], rg_conv_b[l], w_rg_a[l], b_rg_a[l], w_rg_i[l], b_rg_i[l], rg_lambda[l],
            ssd_conv_w[l], ssd_conv_b[l], ssd_dt_bias[l], ssd_a_log[l], ssd_d[l], ssd_norm_g[l],
            w_out[l], ln_g[l], ln_b[l])
        xs_pad = pad_rows(ys, 0, SAMPLE_PAD - ls)
        mk, mv, mkb, mvb = _memkv(mem_prompt, w_mem_k[l].astype(BF16), w_mem_v[l].astype(BF16))
        rg_o, rgc, rgh = _rg_prompt(yp, rg, PROMPT_RG_TILE)
        xa_o, xa_s = _xattn(yp, xs_pad, xa["wq"], xa["wg"], mkb, mvb, cache_mem_k[l], cache_mem_v[l],
                            PROMPT_XA_TILE)
        yp, sc, sh = _ssd_merge_prompt(yp, ssd, rg_o, xa_o, merge)
        outs["rgc_p"].append(rgc)
        outs["rgh_p"].append(rgh.reshape(bp, -1))
        outs["sc_p"].append(sc)
        outs["sh_p"].append(sh.reshape(bp, heads, SSD_HEAD_DIM, SSD_STATE))
        outs["mk_p"].append(mk)
        outs["mv_p"].append(mv)
        tail = SAMPLE_PAD - (CONV_W - 1)
        rg_o, rgc, rgh = _rg_sample(xs_pad, rg, pad_rows(state_rg_conv[l], tail, 0),
                                    state_rg_h[l][:, None, :], SAMPLE_RG_SEQS, ls)
        ssd_o, sc, sh = _ssd_sample(xs_pad, ssd, pad_rows(state_ssd_conv[l], tail, 0),
                                    state_ssd_h[l].reshape(bs, heads * SSD_HEAD_DIM, SSD_STATE),
                                    SAMPLE_SSD_SEQS, ls)
        flat = lambda v: v.reshape(bs * SAMPLE_PAD, v.shape[-1])
        ys_pad = _merge(flat(rg_o), flat(ssd_o), flat(xa_s), flat(xs_pad), merge["w"], merge["g"], merge["b"],
                        MERGE_TILE, "merge_sample").reshape(bs, SAMPLE_PAD, d)
        ys = ys_pad[:, :ls, :]
        outs["rgc_s"].append(rgc)
        outs["rgh_s"].append(rgh.reshape(bs, -1))
        outs["sc_s"].append(sc)
        outs["sh_s"].append(sh.reshape(bs, heads, SSD_HEAD_DIM, SSD_STATE))
    st = lambda k: jnp.stack(outs[k])
    return (yp, ys, st("rgc_p"), st("rgh_p"), st("sc_p"), st("sh_p"), st("mk_p"), st("mv_p"),
            st("rgc_s"), st("rgh_s"), st("sc_s"), st("sh_s"))
```

```python
import functools
from typing import NamedTuple

import jax
import jax.numpy as jnp
import numpy as np
from jax import lax
from jax.experimental import pallas as pl
from jax.experimental.pallas import tpu as pltpu

F32 = jnp.float32
BF16 = jnp.bfloat16

SUBLANES = 8
LANES = 128
MXU_WIDTH = 256
VMEM_LIMIT_BYTES = 56 * 1024 * 1024

RG_C = 8.0
CONV_W = 4
RG_BLOCKS = 8
SSD_HEAD_DIM = 64
SSD_GROUPS = 2
SSD_STATE = 128
SSD_CHUNK = 128
MEM_HEADS = 4
LN_EPS = 1e-5
RMS_EPS = 1e-5
DEPTH = 1
DEEPNORM_ALPHA = (2 * DEPTH) ** 0.25
SAMPLE_PAD = SUBLANES


def _cparams(*sem):
    return pltpu.CompilerParams(dimension_semantics=sem, vmem_limit_bytes=VMEM_LIMIT_BYTES)


def _mm(a, b):
    return jnp.dot(a.astype(BF16), b.astype(BF16), preferred_element_type=F32)


def _mm_nt(a, b):
    return lax.dot_general(a.astype(BF16), b.astype(BF16), (((1,), (1,)), ((), ())),
                           preferred_element_type=F32)


def _mm_tn(a, b):
    return lax.dot_general(a.astype(BF16), b.astype(BF16), (((0,), (0,)), ((), ())),
                           preferred_element_type=F32)


def _sigmoid(x):
    return 1.0 / (1.0 + jnp.exp(-x))


def _silu(x):
    return x * _sigmoid(x)


def _softplus(x):
    return jnp.maximum(x, 0.0) + jnp.log(1.0 + jnp.exp(-jnp.abs(x)))


def _causal_conv_tiles(x3, p3, w, b):
    row = lax.broadcasted_iota(jnp.int32, x3.shape, 1)
    y = x3 * w[CONV_W - 1:CONV_W][None]
    for s in range(1, CONV_W):
        shifted = jnp.where(row >= s, pltpu.roll(x3, s, axis=1), pltpu.roll(p3, s, axis=1))
        y = y + shifted * w[CONV_W - 1 - s:CONV_W - s][None]
    return y + b[None]


def _scan_in_tiles(a3, b3):
    row = lax.broadcasted_iota(jnp.int32, a3.shape, 1)
    s = 1
    while s < SUBLANES:
        keep = row >= s
        a_sh = jnp.where(keep, pltpu.roll(a3, s, axis=1), 1.0)
        b_sh = jnp.where(keep, pltpu.roll(b3, s, axis=1), 0.0)
        b3 = a3 * b_sh + b3
        a3 = a3 * a_sh
        s *= 2
    return a3, b3


def _rg_gates(u, wgate, ba, bi, lam):
    pre = _mm(u, wgate)
    r = _sigmoid(pre[:, :LANES] + ba)
    i = _sigmoid(pre[:, LANES:] + bi)
    neg_log_a = r * (RG_C * _softplus(-lam))
    a = jnp.exp(-neg_log_a)
    v = jnp.tanh(neg_log_a) * (1.0 + a * a)
    mult = jnp.where(v > 0.0, v * lax.rsqrt(v), 0.0)
    return a, mult * (i * u)


def _time_strided_conv(x3, tail3, w, b):
    slabs = x3.shape[0]
    row = lax.broadcasted_iota(jnp.int32, tail3.shape, 1)
    wrapped = jnp.where(row >= 1, pltpu.roll(x3[slabs - (CONV_W - 1):], 1, axis=1), pltpu.roll(tail3, 1, axis=1))
    y = x3 * w[CONV_W - 1:CONV_W][None]
    for s in range(1, CONV_W):
        shifted = jnp.concatenate([wrapped[CONV_W - 1 - s:], x3[:slabs - s]], axis=0)
        y = y + shifted * w[CONV_W - 1 - s:CONV_W - s][None]
    return y + b[None]


def _rg_prompt_kernel(x_ref, perm_ref, permt_ref, wx_ref, wg_ref, cw_ref, cb_ref, wgate_ref, ba_ref,
                      bi_ref, lam_ref, out_ref, conv_ref, hlast_ref, tail_scr, h_scr, proj_scr, outp_scr, *,
                      tiles_per_seq):
    s = pl.program_id(0)

    @pl.when(s == 0)
    def _():
        proj_scr[...] = jnp.zeros_like(proj_scr)

    @pl.when(jnp.logical_or(s == 0, s % tiles_per_seq == 1 % tiles_per_seq))
    def _():
        tail_scr[...] = jnp.zeros_like(tail_scr)
        h_scr[...] = jnp.zeros_like(h_scr)

    rows = x_ref.shape[0]
    slabs = rows // SUBLANES
    width = wx_ref.shape[1]
    per_group = MXU_WIDTH // LANES
    row = lax.broadcasted_iota(jnp.int32, (SUBLANES, LANES), 0)
    xb = jnp.dot(perm_ref[...], x_ref[...].astype(BF16), preferred_element_type=F32).astype(BF16)

    for cg in range(width // MXU_WIDTH):
        gs = slice(cg * MXU_WIDTH, (cg + 1) * MXU_WIDTH)
        for kk in range(per_group):
            k = cg * per_group + kk
            ks = slice(k * LANES, (k + 1) * LANES)
            x3 = proj_scr[:, ks].reshape(slabs, SUBLANES, LANES)
            tail3 = tail_scr[:, :, ks]
            last3 = x3[slabs - (CONV_W - 1):]
            tail_scr[:, :, ks] = last3
            for i in range(CONV_W - 1):
                conv_ref[0, i:i + 1, ks] = last3[i, SUBLANES - 1:, :]
            u3 = _time_strided_conv(x3, tail3, cw_ref[:, ks], cb_ref[:, ks])
            a, b = _rg_gates(u3.reshape(rows, LANES), wgate_ref[k], ba_ref[:, ks], bi_ref[:, ks], lam_ref[:, ks])
            a3 = a.reshape(slabs, SUBLANES, LANES)
            b3 = b.reshape(slabs, SUBLANES, LANES)
            h_loc, a_cum = [b3[0]], [a3[0]]
            for j in range(1, slabs):
                h_loc.append(a3[j] * h_loc[j - 1] + b3[j])
                a_cum.append(a3[j] * a_cum[j - 1])
            a_run, h_run = _scan_in_tiles(a_cum[-1][None], h_loc[-1][None])
            h_prev = h_scr[:, ks]
            h_end = h_run[0] + a_run[0] * h_prev
            h_in = jnp.where(row >= 1, pltpu.roll(h_end, 1, axis=0), h_prev)
            h_scr[:, ks] = h_end[SUBLANES - 1:]
            hlast_ref[0, :, ks] = h_end[SUBLANES - 1:]
            h = jnp.concatenate([h_loc[j] + a_cum[j] * h_in for j in range(slabs)], axis=0)
            gate = proj_scr[:, width + k * LANES:width + (k + 1) * LANES]
            outp_scr[:, ks] = (h * _silu(gate)).astype(BF16)
        proj_scr[:, gs] = jnp.dot(xb, wx_ref[:, gs], preferred_element_type=F32)
        proj_scr[:, width + cg * MXU_WIDTH:width + (cg + 1) * MXU_WIDTH] = jnp.dot(
            xb, wg_ref[:, gs], preferred_element_type=F32)

    out_ref[...] = jnp.dot(permt_ref[...], outp_scr[...], preferred_element_type=F32).astype(out_ref.dtype)


def _rg_sample_kernel(x_ref, wx_ref, wg_ref, cw_ref, cb_ref, wgate_ref, ba_ref, bi_ref, lam_ref,
                      cstate_ref, h0_ref, out_ref, conv_ref, hlast_ref, *, valid):
    seqs = x_ref.shape[0]
    rows = seqs * SUBLANES
    xb = x_ref[...].reshape(rows, x_ref.shape[2]).astype(BF16)
    rgx = jnp.dot(xb, wx_ref[...], preferred_element_type=F32)
    gate = jnp.dot(xb, wg_ref[...], preferred_element_type=F32)
    x3 = rgx.reshape(seqs, SUBLANES, rgx.shape[1])
    conv_ref[...] = x3[:, valid - (CONV_W - 1):valid, :]
    row = lax.broadcasted_iota(jnp.int32, (seqs, SUBLANES, LANES), 1)
    for k in range(RG_BLOCKS):
        ks = slice(k * LANES, (k + 1) * LANES)
        u3 = _causal_conv_tiles(x3[:, :, ks], cstate_ref[:, :, ks], cw_ref[:, ks], cb_ref[:, ks])
        u = u3.reshape(rows, LANES)
        a, b = _rg_gates(u, wgate_ref[k], ba_ref[:, ks], bi_ref[:, ks], lam_ref[:, ks])
        a3 = a.reshape(seqs, SUBLANES, LANES)
        b3 = b.reshape(seqs, SUBLANES, LANES)
        b3 = b3 + jnp.where(row == 0, a3 * h0_ref[:, :, ks], 0.0)
        _, h3 = _scan_in_tiles(a3, b3)
        hlast_ref[:, :, ks] = h3[:, valid - 1:valid, :]
        g3 = gate[:, ks].reshape(seqs, SUBLANES, LANES)
        out_ref[:, :, ks] = (h3 * _silu(g3)).astype(out_ref.dtype)


def _cumsum_rows(x):
    rows = x.shape[0]
    row = lax.broadcasted_iota(jnp.int32, x.shape, 0)
    s = 1
    while s < rows:
        x = x + jnp.where(row >= s, pltpu.roll(x, s, axis=0), 0.0)
        s *= 2
    return x


def _group_rmsnorm(y, gain):
    width = y.shape[-1] // SSD_GROUPS
    parts = []
    for g in range(SSD_GROUPS):
        yg = y[..., g * width:(g + 1) * width]
        ms = jnp.sum(yg * yg, axis=-1, keepdims=True) * (1.0 / width)
        parts.append(yg * lax.rsqrt(ms + RMS_EPS))
    return jnp.concatenate(parts, axis=-1) * gain


def _time_strided_cumsum(x):
    slabs = x.shape[0] // SUBLANES
    x3 = x.reshape(slabs, SUBLANES, x.shape[1])
    acc = [x3[0]]
    for j in range(1, slabs):
        acc.append(acc[j - 1] + x3[j])
    _, run = _scan_in_tiles(jnp.ones_like(acc[-1])[None], acc[-1][None])
    row = lax.broadcasted_iota(jnp.int32, run[0].shape, 0)
    before = jnp.where(row >= 1, pltpu.roll(run[0], 1, axis=0), 0.0)
    return jnp.concatenate([a + before for a in acc], axis=0)


def _ssd_merge_prompt_kernel(x_ref, perm_ref, permt_ref, wxbc_ref, wz_ref, wdt_ref, cw_ref, cb_ref, dtb_ref,
                             alog_ref, dexp_ref, ng_ref, rg_ref, xa_ref, xres_ref, wout_ref, lng_ref, lnb_ref,
                             y_ref, conv_ref, h_ref, tail_scr, proj_scr, xb_scr, h_scr, ssd_scr, xbc_scr, *,
                             tiles_per_seq, n_tiles):
    s = pl.program_id(0)

    @pl.when(s == 0)
    def _():
        proj_scr[...] = jnp.zeros_like(proj_scr)
        xb_scr[...] = jnp.zeros_like(xb_scr)
        ssd_scr[...] = jnp.zeros_like(ssd_scr)

    @pl.when(jnp.logical_or(s == 0, s % tiles_per_seq == 1 % tiles_per_seq))
    def _():
        tail_scr[...] = jnp.zeros_like(tail_scr)
        h_scr[...] = jnp.zeros_like(h_scr)

    q = SSD_CHUNK
    n_sub = x_ref.shape[0] // q
    slabs = q // SUBLANES
    d_ssd = wz_ref.shape[1]
    d_conv = wxbc_ref.shape[1]
    gn = SSD_GROUPS * SSD_STATE
    tail3 = tail_scr[...]
    for c in range(n_sub):
        x3 = proj_scr[c * q:(c + 1) * q, :d_conv].reshape(slabs, SUBLANES, d_conv)
        xbc_scr[c * q:(c + 1) * q, :] = _silu(_time_strided_conv(x3, tail3, cw_ref[...], cb_ref[...])).reshape(
            q, d_conv)
        tail3 = x3[slabs - (CONV_W - 1):]
    tail_scr[...] = tail3
    dtr_all = proj_scr[:, d_conv:]

    mix, row0 = None, 0
    for src in (rg_ref, ssd_scr, xa_ref):
        part = jnp.dot(src[...], wout_ref[row0:row0 + src.shape[1], :], preferred_element_type=F32)
        mix = part if mix is None else mix + part
        row0 += src.shape[1]
    res = DEEPNORM_ALPHA * xres_ref[...] + mix
    mu = jnp.mean(res, axis=-1, keepdims=True)
    cen = res - mu
    var = jnp.mean(cen * cen, axis=-1, keepdims=True)
    y_ref[...] = cen * lax.rsqrt(var + LN_EPS) * lng_ref[...] + lnb_ref[...]

    xin = x_ref[...].astype(BF16)
    xb_new = jnp.concatenate(
        [jnp.dot(perm_ref[...], xin[c * q:(c + 1) * q], preferred_element_type=F32).astype(BF16)
         for c in range(n_sub)], axis=0)
    z_all = jnp.dot(xb_scr[...], wz_ref[...], preferred_element_type=F32)
    xb_scr[...] = xb_new

    ii = lax.broadcasted_iota(jnp.int32, (q, q), 0)
    jj = lax.broadcasted_iota(jnp.int32, (q, q), 1)
    time_of = lambda r: (r % SUBLANES) * slabs + r // SUBLANES
    causal = time_of(ii) >= time_of(jj)
    lane = lax.broadcasted_iota(jnp.int32, (q, LANES), 1)
    lo = lane < SSD_HEAD_DIM
    srow = lax.broadcasted_iota(jnp.int32, (LANES, SSD_STATE), 0) < SSD_HEAD_DIM
    heads = d_ssd // SSD_HEAD_DIM
    pairs = heads // 2
    pairs_per_group = pairs // SSD_GROUPS

    projections = [(wxbc_ref, slice(g * MXU_WIDTH, (g + 1) * MXU_WIDTH), slice(g * MXU_WIDTH, (g + 1) * MXU_WIDTH))
                   for g in range(d_conv // MXU_WIDTH)]
    projections.append((wdt_ref, slice(None), slice(d_conv, None)))
    every = (n_sub * pairs) // len(projections)
    assert every >= 1
    issued = 0

    for c in range(n_sub):
        cs = slice(c * q, (c + 1) * q)
        sx = xbc_scr[cs, :d_ssd]
        bm = xbc_scr[cs, d_ssd:d_ssd + gn]
        cm = xbc_scr[cs, d_ssd + gn:]
        dt = _softplus(dtr_all[cs] + dtb_ref[...])
        da = dt * (-jnp.exp(alog_ref[...]))
        acum = _time_strided_cumsum(da)
        alast = acum[q - 1:q, :]
        wgt = dt * jnp.exp(alast - acum)
        tot = jnp.exp(alast)
        acum_t = acum.T
        dt_t = dt.T
        cb = [_mm_nt(cm[:, g * SSD_STATE:(g + 1) * SSD_STATE], bm[:, g * SSD_STATE:(g + 1) * SSD_STATE])
              for g in range(SSD_GROUPS)]
        y_parts = []
        for g in range(SSD_GROUPS):
            ns = slice(g * SSD_STATE, (g + 1) * SSD_STATE)
            rows_g = slice(g * pairs_per_group * LANES, (g + 1) * pairs_per_group * LANES)
            h_in = h_scr[rows_g, :]
            y_off_g = _mm_nt(cm[:, ns], h_in)
            xws, decays = [], []
            for pl_ in range(pairs_per_group):
                pq = g * pairs_per_group + pl_
                ps = slice(pq * LANES, (pq + 1) * LANES)
                xq = sx[:, ps]
                ms, es, ws, ts = [], [], [], []
                for h in (2 * pq, 2 * pq + 1):
                    acol = jnp.broadcast_to(acum[:, h:h + 1], (q, q))
                    arow = jnp.broadcast_to(acum_t[h:h + 1, :], (q, q))
                    decay = jnp.exp(jnp.where(causal, acol - arow, -jnp.inf))
                    ms.append((cb[g] * decay * jnp.broadcast_to(dt_t[h:h + 1, :], (q, q))).astype(BF16))
                    es.append(jnp.exp(jnp.broadcast_to(acum[:, h:h + 1], (q, LANES))))
                    ws.append(jnp.broadcast_to(wgt[:, h:h + 1], (q, LANES)))
                    ts.append(jnp.broadcast_to(tot[:, h:h + 1], (LANES, SSD_STATE)))
                lhs = jnp.concatenate(ms, axis=1)
                rhs = jnp.concatenate([jnp.where(lo, xq, 0.0), jnp.where(lo, 0.0, xq)], axis=0)
                y_diag = _mm(lhs, rhs)
                y_off = y_off_g[:, pl_ * LANES:(pl_ + 1) * LANES] * jnp.where(lo, es[0], es[1])
                xws.append(xq * jnp.where(lo, ws[0], ws[1]))
                decays.append(jnp.where(srow, ts[0], ts[1]))
                y_parts.append(y_diag + y_off + dexp_ref[:, ps] * xq)
                if (c * pairs + pq + 1) % every == 0 and issued < len(projections):
                    w_ref, src, dst = projections[issued]
                    proj_scr[:, dst] = jnp.dot(xb_new, w_ref[:, src], preferred_element_type=F32)
                    issued += 1
            h_scr[rows_g, :] = (h_in * jnp.concatenate(decays, axis=0)
                                + _mm_tn(jnp.concatenate(xws, axis=1), bm[:, ns]))
        y = jnp.concatenate(y_parts, axis=1) * _silu(z_all[cs])
        y = _group_rmsnorm(y, ng_ref[...]).astype(BF16)
        ssd_scr[cs, :] = jnp.dot(permt_ref[...], y, preferred_element_type=F32).astype(BF16)
    assert issued == len(projections)

    @pl.when(jnp.logical_and(s % tiles_per_seq == 0, jnp.logical_and(s > 0, s <= n_tiles)))
    def _():
        h_ref[0] = h_scr[...]
        for i in range(CONV_W - 1):
            conv_ref[0, i:i + 1, :] = tail_scr[i, SUBLANES - 1:, :]


def _ssd_sample_kernel(x_ref, wxbc_ref, wz_ref, wdt_ref, cw_ref, cb_ref, dtb_ref, alog_ref, dexp_ref,
                       ng_ref, expand_ref, cstate_ref, h0_ref, out_ref, conv_ref, h_ref,
                       c_scr, b_scr, xw_scr, tot_scr, yoff_scr, *, valid):
    seqs = x_ref.shape[0]
    rows = seqs * SUBLANES
    d_ssd = wz_ref.shape[1]
    gn = SSD_GROUPS * SSD_STATE
    heads = d_ssd // SSD_HEAD_DIM
    hg = heads // SSD_GROUPS
    xb = x_ref[...].reshape(rows, x_ref.shape[2]).astype(BF16)
    xbc_raw = jnp.dot(xb, wxbc_ref[...], preferred_element_type=F32)
    z = jnp.dot(xb, wz_ref[...], preferred_element_type=F32)
    dtr = jnp.dot(xb, wdt_ref[...], preferred_element_type=F32)

    x3 = xbc_raw.reshape(seqs, SUBLANES, xbc_raw.shape[1])
    conv_ref[...] = x3[:, valid - (CONV_W - 1):valid, :]
    xbc3 = _silu(_causal_conv_tiles(x3, cstate_ref[...], cw_ref[...], cb_ref[...]))
    sx3 = xbc3[:, :, :d_ssd]
    b3 = xbc3[:, :, d_ssd:d_ssd + gn]
    c3 = xbc3[:, :, d_ssd + gn:]

    row = lax.broadcasted_iota(jnp.int32, (seqs, SUBLANES, LANES), 1)
    lane = lax.broadcasted_iota(jnp.int32, (seqs, SUBLANES, LANES), 2)
    dt3 = jnp.where(row < valid, _softplus(dtr + dtb_ref[...]).reshape(seqs, SUBLANES, LANES), 0.0)
    da3 = dt3 * (-jnp.exp(alog_ref[...]))[None]
    ones = jnp.ones_like(da3)
    _, acum3 = _scan_in_tiles(ones, da3)
    alast = acum3[:, SUBLANES - 1:, :]
    wgt3 = dt3 * jnp.exp(alast - acum3)
    tot_scr[...] = jnp.exp(alast)
    e3 = jnp.exp(acum3)

    coefs = []
    for u in range(valid):
        prod = c3 * b3[:, u:u + 1, :]
        cbu = [jnp.sum(prod[:, :, g * SSD_STATE:(g + 1) * SSD_STATE], axis=-1, keepdims=True)
               for g in range(SSD_GROUPS)]
        cb_heads = jnp.where(lane < hg, cbu[0], cbu[1])
        coefs.append(jnp.where(row >= u, cb_heads * jnp.exp(acum3 - acum3[:, u:u + 1, :]) * dt3[:, u:u + 1, :],
                               0.0))

    per_head = coefs + [wgt3, e3]
    stacked = jnp.concatenate([v.reshape(rows, LANES) for v in per_head], axis=0)
    expanded = None
    rest = stacked
    for _ in range(3):
        piece = rest.astype(BF16)
        rest = rest - piece.astype(F32)
        part = jnp.dot(piece, expand_ref[...], preferred_element_type=F32)
        expanded = part if expanded is None else expanded + part
    expanded = [expanded[i * rows:(i + 1) * rows].reshape(seqs, SUBLANES, d_ssd) for i in range(len(per_head))]
    y_diag = expanded[0] * sx3[:, 0:1, :]
    for u in range(1, valid):
        y_diag = y_diag + expanded[u] * sx3[:, u:u + 1, :]
    wgt_x, e_x = expanded[valid], expanded[valid + 1]

    c_scr[...] = c3
    b_scr[...] = b3
    xw_scr[...] = sx3 * wgt_x
    srow = lax.broadcasted_iota(jnp.int32, (2 * SSD_HEAD_DIM, SSD_STATE), 0) < SSD_HEAD_DIM

    def per_seq(s, carry):
        cs = c_scr[s]
        bs = b_scr[s]
        xws = xw_scr[s]
        tots = tot_scr[s]
        for g in range(SSD_GROUPS):
            gs = slice(g * SSD_STATE, (g + 1) * SSD_STATE)
            width = hg * SSD_HEAD_DIM
            cols = slice(g * width, (g + 1) * width)
            hin = h0_ref[s, cols, :]
            yoff_scr[s, :, cols] = _mm_nt(cs[:, gs], hin)
            upd = _mm_tn(xws[:, cols], bs[:, gs])
            for pq in range(hg // 2):
                h = g * hg + 2 * pq
                rs = slice(pq * LANES, (pq + 1) * LANES)
                t0 = jnp.broadcast_to(tots[0:1, h:h + 1], (LANES, SSD_STATE))
                t1 = jnp.broadcast_to(tots[0:1, h + 1:h + 2], (LANES, SSD_STATE))
                h_ref[s, g * width + pq * LANES:g * width + (pq + 1) * LANES, :] = (
                    hin[rs] * jnp.where(srow, t0, t1) + upd[rs])
        return carry

    lax.fori_loop(0, seqs, per_seq, 0)

    y = y_diag + yoff_scr[...] * e_x + dexp_ref[...][None] * sx3
    y = y * _silu(z.reshape(seqs, SUBLANES, d_ssd))
    out_ref[...] = _group_rmsnorm(y, ng_ref[...][None]).astype(out_ref.dtype)


def _memkv_kernel(mem_ref, wk_ref, wv_ref, k_ref, v_ref, kb_ref, vb_ref):
    nb, m, d = mem_ref.shape
    mb = mem_ref[...].reshape(nb * m, d).astype(BF16)
    for w_ref, o_ref, ob_ref in ((wk_ref, k_ref, kb_ref), (wv_ref, v_ref, vb_ref)):
        proj = jnp.dot(mb, w_ref[...], preferred_element_type=F32)
        for i in range(nb):
            o_ref[i] = proj[i * m:(i + 1) * m].reshape(o_ref.shape[1:])
        ob_ref[...] = proj.astype(BF16).reshape(ob_ref.shape)


def _softmax_terms(scores):
    p = jnp.exp(scores - jnp.max(scores, axis=-1, keepdims=True))
    return p, jnp.sum(p, axis=-1, keepdims=True)


def _prompt_attention_head(h, q_ref, rows, k_ref, v_ref, out_ref):
    c = k_ref.shape[2]
    d_head = c // MEM_HEADS
    hs = slice(h * d_head, (h + 1) * d_head)

    def scores():
        return _softmax_terms(_mm_nt(q_ref[:rows, hs], k_ref[0, :, hs]) * (d_head ** -0.5))

    def output(p, l):
        o = _mm(p, v_ref[0, :, hs]) * (1.0 / l)
        gate = q_ref[:rows, c + h * d_head:c + (h + 1) * d_head]
        out_ref[:, hs] = (o * _silu(gate)).astype(out_ref.dtype)

    return scores, output


def _sample_attention_seq(s, q_ref, rows, k_ref, v_ref, out_ref):
    _, n_mem, heads, d_head = k_ref.shape
    c = heads * d_head
    rs = slice(rows + s * SUBLANES, rows + (s + 1) * SUBLANES)

    def scores():
        shape = (heads * SUBLANES, n_mem * heads)
        same_head = (lax.broadcasted_iota(jnp.int32, shape, 0) // SUBLANES
                     == lax.broadcasted_iota(jnp.int32, shape, 1) % heads)
        qh = jnp.concatenate([q_ref[rs, h * d_head:(h + 1) * d_head] for h in range(heads)], axis=0)
        sc = _mm_nt(qh, k_ref[s].reshape(n_mem * heads, d_head)) * (d_head ** -0.5)
        return _softmax_terms(jnp.where(same_head, sc, -jnp.inf))

    def output(p, l):
        o = _mm(p, v_ref[s].reshape(n_mem * heads, d_head)) * (1.0 / l)
        o = jnp.concatenate([o[h * SUBLANES:(h + 1) * SUBLANES] for h in range(heads)], axis=1)
        out_ref[s] = (o * _silu(q_ref[rs, c:])).astype(out_ref.dtype)

    return scores, output


def _xattn_kernel(x_ref, xs_ref, wq_ref, wg_ref, k_ref, v_ref, ks_ref, vs_ref, out_ref, outs_ref):
    rows = x_ref.shape[0]
    seqs, _, d = xs_ref.shape
    xb = jnp.concatenate([x_ref[...], xs_ref[...].reshape(seqs * SUBLANES, d)], axis=0).astype(BF16)
    qg = jnp.concatenate([jnp.dot(xb, wq_ref[...], preferred_element_type=F32),
                          jnp.dot(xb, wg_ref[...], preferred_element_type=F32)], axis=1)
    items = [_prompt_attention_head(h, qg, rows, k_ref, v_ref, out_ref) for h in range(MEM_HEADS)]
    items += [_sample_attention_seq(i, qg, rows, ks_ref, vs_ref, outs_ref) for i in range(seqs)]
    for scores, output in items:
        output(*scores())


def _merge_kernel(rg_ref, ssd_ref, xa_ref, x_ref, w_ref, g_ref, b_ref, y_ref, *, sub_rows):
    d = rg_ref.shape[1]
    for r0 in range(0, x_ref.shape[0], sub_rows):
        rs = slice(r0, r0 + sub_rows)
        mix = (jnp.dot(rg_ref[rs, :].astype(BF16), w_ref[0:d, :], preferred_element_type=F32)
               + jnp.dot(ssd_ref[rs, :].astype(BF16), w_ref[d:2 * d, :], preferred_element_type=F32)
               + jnp.dot(xa_ref[rs, :].astype(BF16), w_ref[2 * d:3 * d, :], preferred_element_type=F32))
        res = DEEPNORM_ALPHA * x_ref[rs, :] + mix
        mu = jnp.mean(res, axis=-1, keepdims=True)
        cen = res - mu
        var = jnp.mean(cen * cen, axis=-1, keepdims=True)
        y_ref[rs, :] = cen * lax.rsqrt(var + LN_EPS) * g_ref[...] + b_ref[...]


def _full(shape):
    return pl.BlockSpec(shape, lambda *_: (0,) * len(shape))


def _time_stride_perm(rows):
    p = np.arange(rows)
    t = (p % SUBLANES) * (rows // SUBLANES) + p // SUBLANES
    perm = t[:, None] == np.arange(rows)[None, :]
    return jnp.asarray(perm, dtype=BF16), jnp.asarray(perm.T, dtype=BF16)


def _rg_prompt(x, p, tile):
    b, l, d = x.shape
    c = p["wx"].shape[1]
    tiles_per_seq = l // tile
    n_tiles = b * tiles_per_seq
    perm, perm_t = _time_stride_perm(tile)
    done = lambda s: jnp.maximum(s - 1, 0)
    out, conv, hlast = pl.pallas_call(
        functools.partial(_rg_prompt_kernel, tiles_per_seq=tiles_per_seq),
        grid=(n_tiles + 1,),
        in_specs=[pl.BlockSpec((tile, d), lambda s: (jnp.minimum(s, n_tiles - 1), 0)),
                  _full(perm.shape), _full(perm.shape), p["wx"].spec, p["wg"].spec,
                  _full(p["cw"].shape), _full(p["cb"].shape), _full(p["wgate"].shape), _full(p["ba"].shape),
                  _full(p["bi"].shape), _full(p["lam"].shape)],
        out_specs=[pl.BlockSpec((tile, c), lambda s: (done(s), 0)),
                   pl.BlockSpec((1, CONV_W - 1, c), lambda s: (done(s) // tiles_per_seq, 0, 0)),
                   pl.BlockSpec((1, 1, c), lambda s: (done(s) // tiles_per_seq, 0, 0))],
        out_shape=[jax.ShapeDtypeStruct((b * l, c), BF16),
                   jax.ShapeDtypeStruct((b, CONV_W - 1, c), F32),
                   jax.ShapeDtypeStruct((b, 1, c), F32)],
        scratch_shapes=[pltpu.VMEM((CONV_W - 1, SUBLANES, c), F32), pltpu.VMEM((1, c), F32),
                        pltpu.VMEM((tile, 2 * c), F32), pltpu.VMEM((tile, c), BF16)],
        compiler_params=_cparams("arbitrary"),
        name="rg_prompt",
    )(x.reshape(b * l, d), perm, perm_t, p["wx"].array, p["wg"].array, p["cw"], p["cb"], p["wgate"], p["ba"],
      p["bi"], p["lam"])
    return out.reshape(b, l, c), conv, hlast


def _rg_sample(xpad, p, cstate, h0, seqs, valid):
    n, _, d = xpad.shape
    c = p["wx"].shape[1]
    blk = lambda w: pl.BlockSpec((seqs, w[0], w[1]), lambda i: (i, 0, 0))
    return pl.pallas_call(
        functools.partial(_rg_sample_kernel, valid=valid),
        grid=(n // seqs,),
        in_specs=[blk((SUBLANES, d)), p["wx"].spec, p["wg"].spec, _full(p["cw"].shape),
                  _full(p["cb"].shape), _full(p["wgate"].shape), _full(p["ba"].shape),
                  _full(p["bi"].shape), _full(p["lam"].shape), blk((SUBLANES, c)), blk((1, c))],
        out_specs=[blk((SUBLANES, c)), blk((CONV_W - 1, c)), blk((1, c))],
        out_shape=[jax.ShapeDtypeStruct((n, SUBLANES, c), F32),
                   jax.ShapeDtypeStruct((n, CONV_W - 1, c), F32),
                   jax.ShapeDtypeStruct((n, 1, c), F32)],
        compiler_params=_cparams("parallel"),
        name="rg_sample",
    )(xpad, p["wx"].array, p["wg"].array, p["cw"], p["cb"], p["wgate"], p["ba"], p["bi"], p["lam"], cstate, h0)


_SSD_WEIGHTS = ("wxbc", "wz", "wdt")
_SSD_SMALL = ("cw", "cb", "dtb", "alog", "dexp", "ng")


def _ssd_param_specs(p):
    return [p[k].spec for k in _SSD_WEIGHTS] + [_full(p[k].shape) for k in _SSD_SMALL]


def _ssd_param_args(p):
    return [p[k].array for k in _SSD_WEIGHTS] + [p[k] for k in _SSD_SMALL]


def _ssd_merge_prompt(x, p, rg_out, xa_out, mp):
    b, l, d = x.shape
    cc = p["wxbc"].shape[1]
    c = p["wz"].shape[1]
    tile = min(SSD_TILE, l)
    tiles_per_seq = l // tile
    n_tiles = b * tiles_per_seq
    perm, perm_t = _time_stride_perm(SSD_CHUNK)
    flat = lambda v: v.reshape(b * l, v.shape[-1])
    clamp = lambda i: jnp.clip(i, 0, n_tiles - 1)
    rows = lambda w, back: pl.BlockSpec((tile, w), lambda s: (clamp(s - back), 0))
    state = lambda shape: pl.BlockSpec((1,) + shape, lambda s: (clamp(s - 1) // tiles_per_seq, 0, 0))
    y, conv, hstate = pl.pallas_call(
        functools.partial(_ssd_merge_prompt_kernel, tiles_per_seq=tiles_per_seq, n_tiles=n_tiles),
        grid=(n_tiles + 2,),
        in_specs=[rows(d, 0), _full(perm.shape), _full(perm.shape)] + _ssd_param_specs(p)
        + [rows(rg_out.shape[-1], 2), rows(xa_out.shape[-1], 2), rows(d, 2), _full(mp["w"].shape),
           _full(mp["g"].shape), _full(mp["b"].shape)],
        out_specs=[rows(d, 2), state((CONV_W - 1, cc)), state((c, SSD_STATE))],
        out_shape=[jax.ShapeDtypeStruct((b * l, d), F32),
                   jax.ShapeDtypeStruct((b, CONV_W - 1, cc), F32),
                   jax.ShapeDtypeStruct((b, c, SSD_STATE), F32)],
        scratch_shapes=[pltpu.VMEM((CONV_W - 1, SUBLANES, cc), F32),
                        pltpu.VMEM((tile, cc + LANES), F32), pltpu.VMEM((tile, d), BF16),
                        pltpu.VMEM((c, SSD_STATE), F32), pltpu.VMEM((tile, c), BF16),
                        pltpu.VMEM((tile, cc), F32)],
        compiler_params=_cparams("arbitrary"),
        name="ssd_merge_prompt",
    )(flat(x), perm, perm_t, *_ssd_param_args(p), flat(rg_out), flat(xa_out), flat(x), mp["w"], mp["g"], mp["b"])
    return y.reshape(b, l, d), conv, hstate


def _ssd_sample(xpad, p, cstate, h0, seqs, valid):
    n, _, d = xpad.shape
    cc = p["wxbc"].shape[1]
    c = p["wz"].shape[1]
    gn = SSD_GROUPS * SSD_STATE
    blk = lambda w: pl.BlockSpec((seqs, w[0], w[1]), lambda i: (i, 0, 0))
    return pl.pallas_call(
        functools.partial(_ssd_sample_kernel, valid=valid),
        grid=(n // seqs,),
        in_specs=[blk((SUBLANES, d))] + _ssd_param_specs(p)
        + [_full(p["expand"].shape), blk((SUBLANES, cc)), blk((c, SSD_STATE))],
        out_specs=[blk((SUBLANES, c)), blk((CONV_W - 1, cc)), blk((c, SSD_STATE))],
        out_shape=[jax.ShapeDtypeStruct((n, SUBLANES, c), F32),
                   jax.ShapeDtypeStruct((n, CONV_W - 1, cc), F32),
                   jax.ShapeDtypeStruct((n, c, SSD_STATE), F32)],
        scratch_shapes=[pltpu.VMEM((seqs, SUBLANES, gn), F32), pltpu.VMEM((seqs, SUBLANES, gn), F32),
                        pltpu.VMEM((seqs, SUBLANES, c), F32), pltpu.VMEM((seqs, 1, LANES), F32),
                        pltpu.VMEM((seqs, SUBLANES, c), F32)],
        compiler_params=_cparams("parallel"),
        name="ssd_sample",
    )(xpad, *_ssd_param_args(p), p["expand"], cstate, h0)


def _memkv(mem, wk, wv):
    b, m, d = mem.shape
    c = wk.shape[1]
    nb = MEMKV_BATCHES if b % MEMKV_BATCHES == 0 else 1
    spec = pl.BlockSpec((nb, m, c), lambda i: (i, 0, 0))
    spec4 = pl.BlockSpec((nb, m, MEM_HEADS, c // MEM_HEADS), lambda i: (i, 0, 0, 0))
    return pl.pallas_call(
        _memkv_kernel,
        grid=(b // nb,),
        in_specs=[pl.BlockSpec((nb, m, d), lambda i: (i, 0, 0)), _full(wk.shape), _full(wv.shape)],
        out_specs=[spec4, spec4, spec, spec],
        out_shape=[jax.ShapeDtypeStruct((b, m, MEM_HEADS, c // MEM_HEADS), F32)] * 2
        + [jax.ShapeDtypeStruct((b, m, c), BF16)] * 2,
        compiler_params=_cparams("parallel"),
        name="mem_kv",
    )(mem, wk, wv)


def _xattn(x, xs_pad, wq, wg, k, v, ks, vs, tile):
    b, l, d = x.shape
    n = xs_pad.shape[0]
    c = wq.shape[1]
    m = k.shape[1]
    tile = min(tile, l)
    tiles_per_seq = l // tile
    n_tiles = b * tiles_per_seq
    seqs = n // n_tiles
    assert seqs * n_tiles == n
    kv_spec = pl.BlockSpec((1, m, c), lambda s: (s // tiles_per_seq, 0, 0))
    skv_spec = pl.BlockSpec((seqs,) + ks.shape[1:], lambda s: (s, 0, 0, 0))
    out, outs = pl.pallas_call(
        _xattn_kernel,
        grid=(n_tiles,),
        in_specs=[pl.BlockSpec((tile, d), lambda s: (s, 0)),
                  pl.BlockSpec((seqs, SUBLANES, d), lambda s: (s, 0, 0)),
                  wq.spec, wg.spec, kv_spec, kv_spec, skv_spec, skv_spec],
        out_specs=[pl.BlockSpec((tile, c), lambda s: (s, 0)),
                   pl.BlockSpec((seqs, SUBLANES, c), lambda s: (s, 0, 0))],
        out_shape=[jax.ShapeDtypeStruct((b * l, c), BF16), jax.ShapeDtypeStruct((n, SUBLANES, c), F32)],
        compiler_params=_cparams("parallel"),
        name="xattn",
    )(x.reshape(b * l, d), xs_pad, wq.array, wg.array, k, v, ks, vs)
    return out.reshape(b, l, c), outs


def _merge(rg, ssd, xa, x, w_out, ln_g, ln_b, tile, name):
    n, d = x.shape
    c = rg.shape[1]
    tile = min(tile, n)
    row = lambda w: pl.BlockSpec((tile, w), lambda i: (i, 0))
    return pl.pallas_call(
        functools.partial(_merge_kernel, sub_rows=min(MERGE_SUB_ROWS, tile)),
        grid=(n // tile,),
        in_specs=[row(c), row(c), row(c), row(d), _full(w_out.shape), _full(ln_g.shape), _full(ln_b.shape)],
        out_specs=row(d),
        out_shape=jax.ShapeDtypeStruct((n, d), F32),
        compiler_params=_cparams("parallel"),
        name=name,
    )(rg, ssd, xa, x, w_out, ln_g, ln_b)


class _Cols(NamedTuple):
    array: jax.Array
    width: int
    index: int

    @property
    def shape(self):
        return (self.array.shape[0], self.width)

    @property
    def spec(self):
        return pl.BlockSpec(self.shape, lambda *_, i=self.index: (0, i))


def _weight_prep_kernel(in_blk_ref, out_blk_ref, shift_ref, valid_ref, nxt_blk_ref, a_ref, b_ref, wo_ref,
                        o_ref, oo_ref, *, shift_rows):
    del in_blk_ref, out_blk_ref, nxt_blk_ref
    i = pl.program_id(0)
    rows = a_ref.shape[0]
    keep = lax.broadcasted_iota(jnp.int32, a_ref.shape, 0) < valid_ref[i]

    @pl.when(shift_ref[i] == 0)
    def _():
        o_ref[...] = jnp.where(keep, a_ref[...], 0.0).T.astype(BF16)

    @pl.when(shift_ref[i] != 0)
    def _():
        blk = jnp.concatenate([a_ref[shift_rows:, :], b_ref[:shift_rows, :]], axis=0)
        o_ref[...] = jnp.where(keep, blk, 0.0).T.astype(BF16)

    oo_ref[...] = wo_ref[...].astype(BF16)


def _weight_prep(w_in_t, w_out, bounds, placement):
    n_rows, k = w_in_t.shape
    in_blk, out_blk, shift, valid = [], [], [], []
    shifts = {lo % MXU_WIDTH for lo, _ in bounds} - {0}
    assert len(shifts) <= 1
    shift_rows = shifts.pop() if shifts else SUBLANES
    assert shift_rows % SUBLANES == 0
    for (lo, hi), place in zip(bounds, placement):
        for j in range(-(-(hi - lo) // MXU_WIDTH)):
            in_blk.append((lo + j * MXU_WIDTH) // MXU_WIDTH)
            out_blk.append(place + j)
            shift.append(lo % MXU_WIDTH)
            valid.append(min(MXU_WIDTH, hi - lo - j * MXU_WIDTH))
    n_cols = (max(out_blk) + 1) * MXU_WIDTH
    for gap in sorted(set(range(n_cols // MXU_WIDTH)) - set(out_blk)):
        in_blk.append(0)
        out_blk.append(gap)
        shift.append(0)
        valid.append(0)
    steps = len(in_blk)
    last_in = -(-n_rows // MXU_WIDTH) - 1
    oo_rows = -(-(-(-w_out.shape[0] // steps)) // SUBLANES) * SUBLANES
    oo_steps = -(-w_out.shape[0] // oo_rows)
    assert oo_steps <= steps
    nxt_blk = [min(a + 1, last_in) if sh else 0 for a, sh in zip(in_blk, shift)]
    tables = [jnp.asarray(np.asarray(t, np.int32)) for t in (in_blk, out_blk, shift, valid, nxt_blk)]
    wo_spec = pl.BlockSpec((oo_rows, w_out.shape[1]), lambda i, *_: (jnp.minimum(i, oo_steps - 1), 0))
    return pl.pallas_call(
        functools.partial(_weight_prep_kernel, shift_rows=shift_rows),
        grid_spec=pltpu.PrefetchScalarGridSpec(
            num_scalar_prefetch=5,
            grid=(steps,),
            in_specs=[pl.BlockSpec((MXU_WIDTH, k), lambda i, ib, ob, sh, va, nb: (ib[i], 0)),
                      pl.BlockSpec((MXU_WIDTH, k), lambda i, ib, ob, sh, va, nb: (nb[i], 0)),
                      wo_spec],
            out_specs=[pl.BlockSpec((k, MXU_WIDTH), lambda i, ib, ob, sh, va, nb: (0, ob[i])), wo_spec],
        ),
        out_shape=[jax.ShapeDtypeStruct((k, n_cols), BF16), jax.ShapeDtypeStruct(w_out.shape, BF16)],
        compiler_params=_cparams("arbitrary"),
        name="weight_prep",
    )(*tables, w_in_t, w_in_t, w_out)


def _layer_params(w_in, rg_conv_w, rg_conv_b, w_rg_a, b_rg_a, w_rg_i, b_rg_i, rg_lambda, ssd_conv_w,
                  ssd_conv_b, ssd_dt_bias, ssd_a_log, ssd_d, ssd_norm_g, w_out, ln_g, ln_b):
    d_rg = rg_conv_w.shape[1]
    d_conv = ssd_conv_w.shape[1]
    d_ssd = ssd_norm_g.shape[0]
    heads = ssd_d.shape[0]
    sizes = (d_rg, d_rg, d_conv, d_ssd, heads)
    offs = [0]
    for s in sizes:
        offs.append(offs[-1] + s)
    d_xa = (w_in.shape[1] - offs[-1]) // 2
    offs += [offs[-1] + d_xa, offs[-1] + 2 * d_xa]
    blocks = lambda w: -(-w // MXU_WIDTH)
    assert d_rg == d_ssd == d_xa and blocks(d_conv) * MXU_WIDTH <= 2 * d_rg and heads <= LANES
    unit = blocks(d_rg)
    place = dict(xbc=0, z=2 * unit, rg_x=3 * unit, rg_g=4 * unit, xa_q=5 * unit, xa_g=6 * unit, dt=7 * unit)
    order = ("rg_x", "rg_g", "xbc", "z", "dt", "xa_q", "xa_g")
    wall, wo = _weight_prep(jnp.swapaxes(w_in, 0, 1), w_out, list(zip(offs[:-1], offs[1:])),
                            [place[k] for k in order])
    col = lambda key, width: _Cols(wall, width, place[key] * MXU_WIDTH // width)
    wx, wg, wxbc, wz = col("rg_x", d_rg), col("rg_g", d_rg), col("xbc", d_conv), col("z", d_ssd)
    wdt, wq, wxg = col("dt", LANES), col("xa_q", d_xa), col("xa_g", d_xa)
    row = lambda v: v.reshape(1, -1).astype(F32)
    pad_lanes = lambda v: jnp.pad(v, ((0, 0), (0, LANES - v.shape[1])))
    rg = dict(wx=wx, wg=wg, cw=rg_conv_w, cb=row(rg_conv_b),
              wgate=jnp.concatenate([w_rg_a, w_rg_i], axis=2).astype(BF16),
              ba=row(b_rg_a), bi=row(b_rg_i), lam=row(rg_lambda))
    head_of_channel = np.arange(d_ssd) // SSD_HEAD_DIM
    ssd = dict(wxbc=wxbc, wz=wz, wdt=wdt,
               cw=ssd_conv_w, cb=row(ssd_conv_b), dtb=pad_lanes(row(ssd_dt_bias)),
               alog=pad_lanes(row(ssd_a_log)), dexp=row(jnp.repeat(ssd_d, SSD_HEAD_DIM)), ng=row(ssd_norm_g),
               expand=jnp.asarray(np.arange(LANES)[:, None] == head_of_channel[None, :], dtype=BF16))
    xa = dict(wq=wq, wg=wxg)
    merge = dict(w=wo, g=row(ln_g), b=row(ln_b))
    return rg, ssd, xa, merge


PROMPT_RG_TILE = 256
PROMPT_XA_TILE = 512
SSD_TILE = 512
MEMKV_BATCHES = 2
MERGE_TILE = 256
MERGE_SUB_ROWS = 256
SAMPLE_RG_SEQS = 16
SAMPLE_SSD_SEQS = 16


def kernel(x_prompt, x_sample, mem_prompt, state_rg_conv, state_rg_h, state_ssd_conv, state_ssd_h,
           cache_mem_k, cache_mem_v, w_in, rg_conv_w, rg_conv_b, w_rg_a, b_rg_a, w_rg_i, b_rg_i,
           rg_lambda, ssd_conv_w, ssd_conv_b, ssd_dt_bias, ssd_a_log, ssd_d, ssd_norm_g, w_mem_k,
           w_mem_v, w_out, ln_g, ln_b):
    assert w_in.shape[0] == DEPTH
    bp, lp, d = x_prompt.shape
    bs, ls, _ = x_sample.shape
    heads = ssd_d.shape[1]
    outs = {k: [] for k in ("rgc_p", "rgh_p", "sc_p", "sh_p", "mk_p", "mv_p", "rgc_s", "rgh_s", "sc_s", "sh_s")}
    yp, ys = x_prompt, x_sample
    pad_rows = lambda v, before, after: jnp.pad(v, ((0, 0), (before, after), (0, 0)))
    for l in range(DEPTH):
        rg, ssd, xa, merge = _layer_params(
            w_in[l], rg_conv_w[l], rg_conv_b[l], w_rg_a[l], b_rg_a[l], w_rg_i[l], b_rg_i[l], rg_lambda[l],
            ssd_conv_w[l], ssd_conv_b[l], ssd_dt_bias[l], ssd_a_log[l], ssd_d[l], ssd_norm_g[l],
            w_out[l], ln_g[l], ln_b[l])
        xs_pad = pad_rows(ys, 0, SAMPLE_PAD - ls)
        mk, mv, mkb, mvb = _memkv(mem_prompt, w_mem_k[l].astype(BF16), w_mem_v[l].astype(BF16))
        rg_o, rgc, rgh = _rg_prompt(yp, rg, PROMPT_RG_TILE)
        xa_o, xa_s = _xattn(yp, xs_pad, xa["wq"], xa["wg"], mkb, mvb, cache_mem_k[l], cache_mem_v[l],
                            PROMPT_XA_TILE)
        yp, sc, sh = _ssd_merge_prompt(yp, ssd, rg_o, xa_o, merge)
        outs["rgc_p"].append(rgc)
        outs["rgh_p"].append(rgh.reshape(bp, -1))
        outs["sc_p"].append(sc)
        outs["sh_p"].append(sh.reshape(bp, heads, SSD_HEAD_DIM, SSD_STATE))
        outs["mk_p"].append(mk)
        outs["mv_p"].append(mv)
        tail = SAMPLE_PAD - (CONV_W - 1)
        rg_o, rgc, rgh = _rg_sample(xs_pad, rg, pad_rows(state_rg_conv[l], tail, 0),
                                    state_rg_h[l][:, None, :], SAMPLE_RG_SEQS, ls)
        ssd_o, sc, sh = _ssd_sample(xs_pad, ssd, pad_rows(state_ssd_conv[l], tail, 0),
                                    state_ssd_h[l].reshape(bs, heads * SSD_HEAD_DIM, SSD_STATE),
                                    SAMPLE_SSD_SEQS, ls)
        flat = lambda v: v.reshape(bs * SAMPLE_PAD, v.shape[-1])
        ys_pad = _merge(flat(rg_o), flat(ssd_o), flat(xa_s), flat(xs_pad), merge["w"], merge["g"], merge["b"],
                        MERGE_TILE, "merge_sample").reshape(bs, SAMPLE_PAD, d)
        ys = ys_pad[:, :ls, :]
        outs["rgc_s"].append(rgc)
        outs["rgh_s"].append(rgh.reshape(bs, -1))
        outs["sc_s"].append(sc)
        outs["sh_s"].append(sh.reshape(bs, heads, SSD_HEAD_DIM, SSD_STATE))
    st = lambda k: jnp.stack(outs[k])
    return (yp, ys, st("rgc_p"), st("rgh_p"), st("sc_p"), st("sh_p"), st("mk_p"), st("mv_p"),
            st("rgc_s"), st("rgh_s"), st("sc_s"), st("sh_s"))
```

```python
import functools
from typing import NamedTuple

import jax
import jax.numpy as jnp
import numpy as np
from jax import lax
from jax.experimental import pallas as pl
from jax.experimental.pallas import tpu as pltpu

F32 = jnp.float32
BF16 = jnp.bfloat16

SUBLANES = 8
LANES = 128
MXU_WIDTH = 256
VMEM_LIMIT_BYTES = 56 * 1024 * 1024

RG_C = 8.0
CONV_W = 4
RG_BLOCKS = 8
SSD_HEAD_DIM = 64
SSD_GROUPS = 2
SSD_STATE = 128
SSD_CHUNK = 128
MEM_HEADS = 4
LN_EPS = 1e-5
RMS_EPS = 1e-5
DEPTH = 1
DEEPNORM_ALPHA = (2 * DEPTH) ** 0.25
SAMPLE_PAD = SUBLANES


def _cparams(*sem):
    return pltpu.CompilerParams(dimension_semantics=sem, vmem_limit_bytes=VMEM_LIMIT_BYTES)


def _mm(a, b):
    return jnp.dot(a.astype(BF16), b.astype(BF16), preferred_element_type=F32)


def _mm_nt(a, b):
    return lax.dot_general(a.astype(BF16), b.astype(BF16), (((1,), (1,)), ((), ())),
                           preferred_element_type=F32)


def _mm_tn(a, b):
    return lax.dot_general(a.astype(BF16), b.astype(BF16), (((0,), (0,)), ((), ())),
                           preferred_element_type=F32)


def _sigmoid(x):
    return 1.0 / (1.0 + jnp.exp(-x))


def _silu(x):
    return x * _sigmoid(x)


def _softplus(x):
    return jnp.maximum(x, 0.0) + jnp.log(1.0 + jnp.exp(-jnp.abs(x)))


def _causal_conv_tiles(x3, p3, w, b):
    row = lax.broadcasted_iota(jnp.int32, x3.shape, 1)
    y = x3 * w[CONV_W - 1:CONV_W][None]
    for s in range(1, CONV_W):
        shifted = jnp.where(row >= s, pltpu.roll(x3, s, axis=1), pltpu.roll(p3, s, axis=1))
        y = y + shifted * w[CONV_W - 1 - s:CONV_W - s][None]
    return y + b[None]


def _scan_in_tiles(a3, b3):
    row = lax.broadcasted_iota(jnp.int32, a3.shape, 1)
    s = 1
    while s < SUBLANES:
        keep = row >= s
        a_sh = jnp.where(keep, pltpu.roll(a3, s, axis=1), 1.0)
        b_sh = jnp.where(keep, pltpu.roll(b3, s, axis=1), 0.0)
        b3 = a3 * b_sh + b3
        a3 = a3 * a_sh
        s *= 2
    return a3, b3


def _rg_gates(u, wgate, ba, bi, lam):
    pre = _mm(u, wgate)
    r = _sigmoid(pre[:, :LANES] + ba)
    i = _sigmoid(pre[:, LANES:] + bi)
    neg_log_a = r * (RG_C * _softplus(-lam))
    a = jnp.exp(-neg_log_a)
    v = jnp.tanh(neg_log_a) * (1.0 + a * a)
    mult = jnp.where(v > 0.0, v * lax.rsqrt(v), 0.0)
    return a, mult * (i * u)


def _time_strided_conv(x3, tail3, w, b):
    slabs = x3.shape[0]
    row = lax.broadcasted_iota(jnp.int32, tail3.shape, 1)
    wrapped = jnp.where(row >= 1, pltpu.roll(x3[slabs - (CONV_W - 1):], 1, axis=1), pltpu.roll(tail3, 1, axis=1))
    y = x3 * w[CONV_W - 1:CONV_W][None]
    for s in range(1, CONV_W):
        shifted = jnp.concatenate([wrapped[CONV_W - 1 - s:], x3[:slabs - s]], axis=0)
        y = y + shifted * w[CONV_W - 1 - s:CONV_W - s][None]
    return y + b[None]


def _rg_prompt_kernel(x_ref, perm_ref, permt_ref, wx_ref, wg_ref, cw_ref, cb_ref, wgate_ref, ba_ref,
                      bi_ref, lam_ref, out_ref, conv_ref, hlast_ref, tail_scr, h_scr, proj_scr, outp_scr, *,
                      tiles_per_seq):
    s = pl.program_id(0)

    @pl.when(s == 0)
    def _():
        proj_scr[...] = jnp.zeros_like(proj_scr)

    @pl.when(jnp.logical_or(s == 0, s % tiles_per_seq == 1 % tiles_per_seq))
    def _():
        tail_scr[...] = jnp.zeros_like(tail_scr)
        h_scr[...] = jnp.zeros_like(h_scr)

    rows = x_ref.shape[0]
    slabs = rows // SUBLANES
    width = wx_ref.shape[1]
    per_group = MXU_WIDTH // LANES
    row = lax.broadcasted_iota(jnp.int32, (SUBLANES, LANES), 0)
    xb = jnp.dot(perm_ref[...], x_ref[...].astype(BF16), preferred_element_type=F32).astype(BF16)

    for cg in range(width // MXU_WIDTH):
        gs = slice(cg * MXU_WIDTH, (cg + 1) * MXU_WIDTH)
        for kk in range(per_group):
            k = cg * per_group + kk
            ks = slice(k * LANES, (k + 1) * LANES)
            x3 = proj_scr[:, ks].reshape(slabs, SUBLANES, LANES)
            tail3 = tail_scr[:, :, ks]
            last3 = x3[slabs - (CONV_W - 1):]
            tail_scr[:, :, ks] = last3
            for i in range(CONV_W - 1):
                conv_ref[0, i:i + 1, ks] = last3[i, SUBLANES - 1:, :]
            u3 = _time_strided_conv(x3, tail3, cw_ref[:, ks], cb_ref[:, ks])
            a, b = _rg_gates(u3.reshape(rows, LANES), wgate_ref[k], ba_ref[:, ks], bi_ref[:, ks], lam_ref[:, ks])
            a3 = a.reshape(slabs, SUBLANES, LANES)
            b3 = b.reshape(slabs, SUBLANES, LANES)
            h_loc, a_cum = [b3[0]], [a3[0]]
            for j in range(1, slabs):
                h_loc.append(a3[j] * h_loc[j - 1] + b3[j])
                a_cum.append(a3[j] * a_cum[j - 1])
            a_run, h_run = _scan_in_tiles(a_cum[-1][None], h_loc[-1][None])
            h_prev = h_scr[:, ks]
            h_end = h_run[0] + a_run[0] * h_prev
            h_in = jnp.where(row >= 1, pltpu.roll(h_end, 1, axis=0), h_prev)
            h_scr[:, ks] = h_end[SUBLANES - 1:]
            hlast_ref[0, :, ks] = h_end[SUBLANES - 1:]
            h = jnp.concatenate([h_loc[j] + a_cum[j] * h_in for j in range(slabs)], axis=0)
            gate = proj_scr[:, width + k * LANES:width + (k + 1) * LANES]
            outp_scr[:, ks] = (h * _silu(gate)).astype(BF16)
        proj_scr[:, gs] = jnp.dot(xb, wx_ref[:, gs], preferred_element_type=F32)
        proj_scr[:, width + cg * MXU_WIDTH:width + (cg + 1) * MXU_WIDTH] = jnp.dot(
            xb, wg_ref[:, gs], preferred_element_type=F32)

    out_ref[...] = jnp.dot(permt_ref[...], outp_scr[...], preferred_element_type=F32).astype(out_ref.dtype)


def _rg_sample_kernel(x_ref, wx_ref, wg_ref, cw_ref, cb_ref, wgate_ref, ba_ref, bi_ref, lam_ref,
                      cstate_ref, h0_ref, out_ref, conv_ref, hlast_ref, *, valid):
    seqs = x_ref.shape[0]
    rows = seqs * SUBLANES
    xb = x_ref[...].reshape(rows, x_ref.shape[2]).astype(BF16)
    rgx = jnp.dot(xb, wx_ref[...], preferred_element_type=F32)
    gate = jnp.dot(xb, wg_ref[...], preferred_element_type=F32)
    x3 = rgx.reshape(seqs, SUBLANES, rgx.shape[1])
    conv_ref[...] = x3[:, valid - (CONV_W - 1):valid, :]
    row = lax.broadcasted_iota(jnp.int32, (seqs, SUBLANES, LANES), 1)
    for k in range(RG_BLOCKS):
        ks = slice(k * LANES, (k + 1) * LANES)
        u3 = _causal_conv_tiles(x3[:, :, ks], cstate_ref[:, :, ks], cw_ref[:, ks], cb_ref[:, ks])
        u = u3.reshape(rows, LANES)
        a, b = _rg_gates(u, wgate_ref[k], ba_ref[:, ks], bi_ref[:, ks], lam_ref[:, ks])
        a3 = a.reshape(seqs, SUBLANES, LANES)
        b3 = b.reshape(seqs, SUBLANES, LANES)
        b3 = b3 + jnp.where(row == 0, a3 * h0_ref[:, :, ks], 0.0)
        _, h3 = _scan_in_tiles(a3, b3)
        hlast_ref[:, :, ks] = h3[:, valid - 1:valid, :]
        g3 = gate[:, ks].reshape(seqs, SUBLANES, LANES)
        out_ref[:, :, ks] = (h3 * _silu(g3)).astype(out_ref.dtype)


def _cumsum_rows(x):
    rows = x.shape[0]
    row = lax.broadcasted_iota(jnp.int32, x.shape, 0)
    s = 1
    while s < rows:
        x = x + jnp.where(row >= s, pltpu.roll(x, s, axis=0), 0.0)
        s *= 2
    return x


def _group_rmsnorm(y, gain):
    width = y.shape[-1] // SSD_GROUPS
    parts = []
    for g in range(SSD_GROUPS):
        yg = y[..., g * width:(g + 1) * width]
        ms = jnp.sum(yg * yg, axis=-1, keepdims=True) * (1.0 / width)
        parts.append(yg * lax.rsqrt(ms + RMS_EPS))
    return jnp.concatenate(parts, axis=-1) * gain


def _time_strided_cumsum(x):
    slabs = x.shape[0] // SUBLANES
    x3 = x.reshape(slabs, SUBLANES, x.shape[1])
    acc = [x3[0]]
    for j in range(1, slabs):
        acc.append(acc[j - 1] + x3[j])
    _, run = _scan_in_tiles(jnp.ones_like(acc[-1])[None], acc[-1][None])
    row = lax.broadcasted_iota(jnp.int32, run[0].shape, 0)
    before = jnp.where(row >= 1, pltpu.roll(run[0], 1, axis=0), 0.0)
    return jnp.concatenate([a + before for a in acc], axis=0)


def _ssd_merge_prompt_kernel(x_ref, perm_ref, permt_ref, wxbc_ref, wz_ref, wdt_ref, cw_ref, cb_ref, dtb_ref,
                             alog_ref, dexp_ref, ng_ref, rg_ref, xa_ref, xres_ref, wout_ref, lng_ref, lnb_ref,
                             y_ref, conv_ref, h_ref, tail_scr, proj_scr, xb_scr, h_scr, ssd_scr, xbc_scr, *,
                             tiles_per_seq, n_tiles):
    s = pl.program_id(0)

    @pl.when(s == 0)
    def _():
        proj_scr[...] = jnp.zeros_like(proj_scr)
        xb_scr[...] = jnp.zeros_like(xb_scr)
        ssd_scr[...] = jnp.zeros_like(ssd_scr)

    @pl.when(jnp.logical_or(s == 0, s % tiles_per_seq == 1 % tiles_per_seq))
    def _():
        tail_scr[...] = jnp.zeros_like(tail_scr)
        h_scr[...] = jnp.zeros_like(h_scr)

    q = SSD_CHUNK
    n_sub = x_ref.shape[0] // q
    slabs = q // SUBLANES
    d_ssd = wz_ref.shape[1]
    d_conv = wxbc_ref.shape[1]
    gn = SSD_GROUPS * SSD_STATE
    tail3 = tail_scr[...]
    for c in range(n_sub):
        x3 = proj_scr[c * q:(c + 1) * q, :d_conv].reshape(slabs, SUBLANES, d_conv)
        xbc_scr[c * q:(c + 1) * q, :] = _silu(_time_strided_conv(x3, tail3, cw_ref[...], cb_ref[...])).reshape(
            q, d_conv)
        tail3 = x3[slabs - (CONV_W - 1):]
    tail_scr[...] = tail3
    dtr_all = proj_scr[:, d_conv:]

    mix, row0 = None, 0
    for src in (rg_ref, ssd_scr, xa_ref):
        part = jnp.dot(src[...], wout_ref[row0:row0 + src.shape[1], :], preferred_element_type=F32)
        mix = part if mix is None else mix + part
        row0 += src.shape[1]
    res = DEEPNORM_ALPHA * xres_ref[...] + mix
    mu = jnp.mean(res, axis=-1, keepdims=True)
    cen = res - mu
    var = jnp.mean(cen * cen, axis=-1, keepdims=True)
    y_ref[...] = cen * lax.rsqrt(var + LN_EPS) * lng_ref[...] + lnb_ref[...]

    xin = x_ref[...].astype(BF16)
    xb_new = jnp.concatenate(
        [jnp.dot(perm_ref[...], xin[c * q:(c + 1) * q], preferred_element_type=F32).astype(BF16)
         for c in range(n_sub)], axis=0)
    z_all = jnp.dot(xb_scr[...], wz_ref[...], preferred_element_type=F32)
    xb_scr[...] = xb_new

    ii = lax.broadcasted_iota(jnp.int32, (q, q), 0)
    jj = lax.broadcasted_iota(jnp.int32, (q, q), 1)
    time_of = lambda r: (r % SUBLANES) * slabs + r // SUBLANES
    causal = time_of(ii) >= time_of(jj)
    lane = lax.broadcasted_iota(jnp.int32, (q, LANES), 1)
    lo = lane < SSD_HEAD_DIM
    srow = lax.broadcasted_iota(jnp.int32, (LANES, SSD_STATE), 0) < SSD_HEAD_DIM
    heads = d_ssd // SSD_HEAD_DIM
    pairs = heads // 2
    pairs_per_group = pairs // SSD_GROUPS

    projections = [(wxbc_ref, slice(g * MXU_WIDTH, (g + 1) * MXU_WIDTH), slice(g * MXU_WIDTH, (g + 1) * MXU_WIDTH))
                   for g in range(d_conv // MXU_WIDTH)]
    projections.append((wdt_ref, slice(None), slice(d_conv, None)))
    every = (n_sub * pairs) // len(projections)
    assert every >= 1
    issued = 0

    for c in range(n_sub):
        cs = slice(c * q, (c + 1) * q)
        sx = xbc_scr[cs, :d_ssd]
        bm = xbc_scr[cs, d_ssd:d_ssd + gn]
        cm = xbc_scr[cs, d_ssd + gn:]
        dt = _softplus(dtr_all[cs] + dtb_ref[...])
        da = dt * (-jnp.exp(alog_ref[...]))
        acum = _time_strided_cumsum(da)
        alast = acum[q - 1:q, :]
        wgt = dt * jnp.exp(alast - acum)
        tot = jnp.exp(alast)
        acum_t = acum.T
        dt_t = dt.T
        cb = [_mm_nt(cm[:, g * SSD_STATE:(g + 1) * SSD_STATE], bm[:, g * SSD_STATE:(g + 1) * SSD_STATE])
              for g in range(SSD_GROUPS)]
        y_parts = []
        for g in range(SSD_GROUPS):
            ns = slice(g * SSD_STATE, (g + 1) * SSD_STATE)
            rows_g = slice(g * pairs_per_group * LANES, (g + 1) * pairs_per_group * LANES)
            h_in = h_scr[rows_g, :]
            y_off_g = _mm_nt(cm[:, ns], h_in)
            xws, decays = [], []
            for pl_ in range(pairs_per_group):
                pq = g * pairs_per_group + pl_
                ps = slice(pq * LANES, (pq + 1) * LANES)
                xq = sx[:, ps]
                ms, es, ws, ts = [], [], [], []
                for h in (2 * pq, 2 * pq + 1):
                    acol = jnp.broadcast_to(acum[:, h:h + 1], (q, q))
                    arow = jnp.broadcast_to(acum_t[h:h + 1, :], (q, q))
                    decay = jnp.exp(jnp.where(causal, acol - arow, -jnp.inf))
                    ms.append((cb[g] * decay * jnp.broadcast_to(dt_t[h:h + 1, :], (q, q))).astype(BF16))
                    es.append(jnp.exp(jnp.broadcast_to(acum[:, h:h + 1], (q, LANES))))
                    ws.append(jnp.broadcast_to(wgt[:, h:h + 1], (q, LANES)))
                    ts.append(jnp.broadcast_to(tot[:, h:h + 1], (LANES, SSD_STATE)))
                lhs = jnp.concatenate(ms, axis=1)
                rhs = jnp.concatenate([jnp.where(lo, xq, 0.0), jnp.where(lo, 0.0, xq)], axis=0)
                y_diag = _mm(lhs, rhs)
                y_off = y_off_g[:, pl_ * LANES:(pl_ + 1) * LANES] * jnp.where(lo, es[0], es[1])
                xws.append(xq * jnp.where(lo, ws[0], ws[1]))
                decays.append(jnp.where(srow, ts[0], ts[1]))
                y_parts.append(y_diag + y_off + dexp_ref[:, ps] * xq)
                if (c * pairs + pq + 1) % every == 0 and issued < len(projections):
                    w_ref, src, dst = projections[issued]
                    proj_scr[:, dst] = jnp.dot(xb_new, w_ref[:, src], preferred_element_type=F32)
                    issued += 1
            h_scr[rows_g, :] = (h_in * jnp.concatenate(decays, axis=0)
                                + _mm_tn(jnp.concatenate(xws, axis=1), bm[:, ns]))
        y = jnp.concatenate(y_parts, axis=1) * _silu(z_all[cs])
        y = _group_rmsnorm(y, ng_ref[...]).astype(BF16)
        ssd_scr[cs, :] = jnp.dot(permt_ref[...], y, preferred_element_type=F32).astype(BF16)
    assert issued == len(projections)

    @pl.when(jnp.logical_and(s % tiles_per_seq == 0, jnp.logical_and(s > 0, s <= n_tiles)))
    def _():
        h_ref[0] = h_scr[...]
        for i in range(CONV_W - 1):
            conv_ref[0, i:i + 1, :] = tail_scr[i, SUBLANES - 1:, :]


def _ssd_sample_kernel(x_ref, wxbc_ref, wz_ref, wdt_ref, cw_ref, cb_ref, dtb_ref, alog_ref, dexp_ref,
                       ng_ref, expand_ref, cstate_ref, h0_ref, out_ref, conv_ref, h_ref,
                       c_scr, b_scr, xw_scr, tot_scr, yoff_scr, *, valid):
    seqs = x_ref.shape[0]
    rows = seqs * SUBLANES
    d_ssd = wz_ref.shape[1]
    gn = SSD_GROUPS * SSD_STATE
    heads = d_ssd // SSD_HEAD_DIM
    hg = heads // SSD_GROUPS
    xb = x_ref[...].reshape(rows, x_ref.shape[2]).astype(BF16)
    xbc_raw = jnp.dot(xb, wxbc_ref[...], preferred_element_type=F32)
    z = jnp.dot(xb, wz_ref[...], preferred_element_type=F32)
    dtr = jnp.dot(xb, wdt_ref[...], preferred_element_type=F32)

    x3 = xbc_raw.reshape(seqs, SUBLANES, xbc_raw.shape[1])
    conv_ref[...] = x3[:, valid - (CONV_W - 1):valid, :]
    xbc3 = _silu(_causal_conv_tiles(x3, cstate_ref[...], cw_ref[...], cb_ref[...]))
    sx3 = xbc3[:, :, :d_ssd]
    b3 = xbc3[:, :, d_ssd:d_ssd + gn]
    c3 = xbc3[:, :, d_ssd + gn:]

    row = lax.broadcasted_iota(jnp.int32, (seqs, SUBLANES, LANES), 1)
    lane = lax.broadcasted_iota(jnp.int32, (seqs, SUBLANES, LANES), 2)
    dt3 = jnp.where(row < valid, _softplus(dtr + dtb_ref[...]).reshape(seqs, SUBLANES, LANES), 0.0)
    da3 = dt3 * (-jnp.exp(alog_ref[...]))[None]
    ones = jnp.ones_like(da3)
    _, acum3 = _scan_in_tiles(ones, da3)
    alast = acum3[:, SUBLANES - 1:, :]
    wgt3 = dt3 * jnp.exp(alast - acum3)
    tot_scr[...] = jnp.exp(alast)
    e3 = jnp.exp(acum3)

    coefs = []
    for u in range(valid):
        prod = c3 * b3[:, u:u + 1, :]
        cbu = [jnp.sum(prod[:, :, g * SSD_STATE:(g + 1) * SSD_STATE], axis=-1, keepdims=True)
               for g in range(SSD_GROUPS)]
        cb_heads = jnp.where(lane < hg, cbu[0], cbu[1])
        coefs.append(jnp.where(row >= u, cb_heads * jnp.exp(acum3 - acum3[:, u:u + 1, :]) * dt3[:, u:u + 1, :],
                               0.0))

    per_head = coefs + [wgt3, e3]
    stacked = jnp.concatenate([v.reshape(rows, LANES) for v in per_head], axis=0)
    expanded = None
    rest = stacked
    for _ in range(3):
        piece = rest.astype(BF16)
        rest = rest - piece.astype(F32)
        part = jnp.dot(piece, expand_ref[...], preferred_element_type=F32)
        expanded = part if expanded is None else expanded + part
    expanded = [expanded[i * rows:(i + 1) * rows].reshape(seqs, SUBLANES, d_ssd) for i in range(len(per_head))]
    y_diag = expanded[0] * sx3[:, 0:1, :]
    for u in range(1, valid):
        y_diag = y_diag + expanded[u] * sx3[:, u:u + 1, :]
    wgt_x, e_x = expanded[valid], expanded[valid + 1]

    c_scr[...] = c3
    b_scr[...] = b3
    xw_scr[...] = sx3 * wgt_x
    srow = lax.broadcasted_iota(jnp.int32, (2 * SSD_HEAD_DIM, SSD_STATE), 0) < SSD_HEAD_DIM

    def per_seq(s, carry):
        cs = c_scr[s]
        bs = b_scr[s]
        xws = xw_scr[s]
        tots = tot_scr[s]
        for g in range(SSD_GROUPS):
            gs = slice(g * SSD_STATE, (g + 1) * SSD_STATE)
            width = hg * SSD_HEAD_DIM
            cols = slice(g * width, (g + 1) * width)
            hin = h0_ref[s, cols, :]
            yoff_scr[s, :, cols] = _mm_nt(cs[:, gs], hin)
            upd = _mm_tn(xws[:, cols], bs[:, gs])
            for pq in range(hg // 2):
                h = g * hg + 2 * pq
                rs = slice(pq * LANES, (pq + 1) * LANES)
                t0 = jnp.broadcast_to(tots[0:1, h:h + 1], (LANES, SSD_STATE))
                t1 = jnp.broadcast_to(tots[0:1, h + 1:h + 2], (LANES, SSD_STATE))
                h_ref[s, g * width + pq * LANES:g * width + (pq + 1) * LANES, :] = (
                    hin[rs] * jnp.where(srow, t0, t1) + upd[rs])
        return carry

    lax.fori_loop(0, seqs, per_seq, 0)

    y = y_diag + yoff_scr[...] * e_x + dexp_ref[...][None] * sx3
    y = y * _silu(z.reshape(seqs, SUBLANES, d_ssd))
    out_ref[...] = _group_rmsnorm(y, ng_ref[...][None]).astype(out_ref.dtype)


def _memkv_kernel(mem_ref, wk_ref, wv_ref, k_ref, v_ref, kb_ref, vb_ref):
    nb, m, d = mem_ref.shape
    mb = mem_ref[...].reshape(nb * m, d).astype(BF16)
    for w_ref, o_ref, ob_ref in ((wk_ref, k_ref, kb_ref), (wv_ref, v_ref, vb_ref)):
        proj = jnp.dot(mb, w_ref[...], preferred_element_type=F32)
        for i in range(nb):
            o_ref[i] = proj[i * m:(i + 1) * m].reshape(o_ref.shape[1:])
        ob_ref[...] = proj.astype(BF16).reshape(ob_ref.shape)


def _softmax_terms(scores):
    p = jnp.exp(scores - jnp.max(scores, axis=-1, keepdims=True))
    return p, jnp.sum(p, axis=-1, keepdims=True)


def _prompt_attention_head(h, q_ref, rows, k_ref, v_ref, out_ref):
    c = k_ref.shape[2]
    d_head = c // MEM_HEADS
    hs = slice(h * d_head, (h + 1) * d_head)

    def scores():
        return _softmax_terms(_mm_nt(q_ref[:rows, hs], k_ref[0, :, hs]) * (d_head ** -0.5))

    def output(p, l):
        o = _mm(p, v_ref[0, :, hs]) * (1.0 / l)
        gate = q_ref[:rows, c + h * d_head:c + (h + 1) * d_head]
        out_ref[:, hs] = (o * _silu(gate)).astype(out_ref.dtype)

    return scores, output


def _sample_attention_seq(s, q_ref, rows, k_ref, v_ref, out_ref):
    _, n_mem, heads, d_head = k_ref.shape
    c = heads * d_head
    rs = slice(rows + s * SUBLANES, rows + (s + 1) * SUBLANES)

    def scores():
        shape = (heads * SUBLANES, n_mem * heads)
        same_head = (lax.broadcasted_iota(jnp.int32, shape, 0) // SUBLANES
                     == lax.broadcasted_iota(jnp.int32, shape, 1) % heads)
        qh = jnp.concatenate([q_ref[rs, h * d_head:(h + 1) * d_head] for h in range(heads)], axis=0)
        sc = _mm_nt(qh, k_ref[s].reshape(n_mem * heads, d_head)) * (d_head ** -0.5)
        return _softmax_terms(jnp.where(same_head, sc, -jnp.inf))

    def output(p, l):
        o = _mm(p, v_ref[s].reshape(n_mem * heads, d_head)) * (1.0 / l)
        o = jnp.concatenate([o[h * SUBLANES:(h + 1) * SUBLANES] for h in range(heads)], axis=1)
        out_ref[s] = (o * _silu(q_ref[rs, c:])).astype(out_ref.dtype)

    return scores, output


def _xattn_kernel(x_ref, xs_ref, wq_ref, wg_ref, k_ref, v_ref, ks_ref, vs_ref, out_ref, outs_ref):
    rows = x_ref.shape[0]
    seqs, _, d = xs_ref.shape
    xb = jnp.concatenate([x_ref[...], xs_ref[...].reshape(seqs * SUBLANES, d)], axis=0).astype(BF16)
    qg = jnp.concatenate([jnp.dot(xb, wq_ref[...], preferred_element_type=F32),
                          jnp.dot(xb, wg_ref[...], preferred_element_type=F32)], axis=1)
    items = [_prompt_attention_head(h, qg, rows, k_ref, v_ref, out_ref) for h in range(MEM_HEADS)]
    items += [_sample_attention_seq(i, qg, rows, ks_ref, vs_ref, outs_ref) for i in range(seqs)]
    for scores, output in items:
        output(*scores())


def _merge_kernel(rg_ref, ssd_ref, xa_ref, x_ref, w_ref, g_ref, b_ref, y_ref, *, sub_rows):
    d = rg_ref.shape[1]
    for r0 in range(0, x_ref.shape[0], sub_rows):
        rs = slice(r0, r0 + sub_rows)
        mix = (jnp.dot(rg_ref[rs, :].astype(BF16), w_ref[0:d, :], preferred_element_type=F32)
               + jnp.dot(ssd_ref[rs, :].astype(BF16), w_ref[d:2 * d, :], preferred_element_type=F32)
               + jnp.dot(xa_ref[rs, :].astype(BF16), w_ref[2 * d:3 * d, :], preferred_element_type=F32))
        res = DEEPNORM_ALPHA * x_ref[rs, :] + mix
        mu = jnp.mean(res, axis=-1, keepdims=True)
        cen = res - mu
        var = jnp.mean(cen * cen, axis=-1, keepdims=True)
        y_ref[rs, :] = cen * lax.rsqrt(var + LN_EPS) * g_ref[...] + b_ref[...]


def _full(shape):
    return pl.BlockSpec(shape, lambda *_: (0,) * len(shape))


def _time_stride_perm(rows):
    p = np.arange(rows)
    t = (p % SUBLANES) * (rows // SUBLANES) + p // SUBLANES
    perm = t[:, None] == np.arange(rows)[None, :]
    return jnp.asarray(perm, dtype=BF16), jnp.asarray(perm.T, dtype=BF16)


def _rg_prompt(x, p, tile):
    b, l, d = x.shape
    c = p["wx"].shape[1]
    tiles_per_seq = l // tile
    n_tiles = b * tiles_per_seq
    perm, perm_t = _time_stride_perm(tile)
    done = lambda s: jnp.maximum(s - 1, 0)
    out, conv, hlast = pl.pallas_call(
        functools.partial(_rg_prompt_kernel, tiles_per_seq=tiles_per_seq),
        grid=(n_tiles + 1,),
        in_specs=[pl.BlockSpec((tile, d), lambda s: (jnp.minimum(s, n_tiles - 1), 0)),
                  _full(perm.shape), _full(perm.shape), p["wx"].spec, p["wg"].spec,
                  _full(p["cw"].shape), _full(p["cb"].shape), _full(p["wgate"].shape), _full(p["ba"].shape),
                  _full(p["bi"].shape), _full(p["lam"].shape)],
        out_specs=[pl.BlockSpec((tile, c), lambda s: (done(s), 0)),
                   pl.BlockSpec((1, CONV_W - 1, c), lambda s: (done(s) // tiles_per_seq, 0, 0)),
                   pl.BlockSpec((1, 1, c), lambda s: (done(s) // tiles_per_seq, 0, 0))],
        out_shape=[jax.ShapeDtypeStruct((b * l, c), BF16),
                   jax.ShapeDtypeStruct((b, CONV_W - 1, c), F32),
                   jax.ShapeDtypeStruct((b, 1, c), F32)],
        scratch_shapes=[pltpu.VMEM((CONV_W - 1, SUBLANES, c), F32), pltpu.VMEM((1, c), F32),
                        pltpu.VMEM((tile, 2 * c), F32), pltpu.VMEM((tile, c), BF16)],
        compiler_params=_cparams("arbitrary"),
        name="rg_prompt",
    )(x.reshape(b * l, d), perm, perm_t, p["wx"].array, p["wg"].array, p["cw"], p["cb"], p["wgate"], p["ba"],
      p["bi"], p["lam"])
    return out.reshape(b, l, c), conv, hlast


def _rg_sample(xpad, p, cstate, h0, seqs, valid):
    n, _, d = xpad.shape
    c = p["wx"].shape[1]
    blk = lambda w: pl.BlockSpec((seqs, w[0], w[1]), lambda i: (i, 0, 0))
    return pl.pallas_call(
        functools.partial(_rg_sample_kernel, valid=valid),
        grid=(n // seqs,),
        in_specs=[blk((SUBLANES, d)), p["wx"].spec, p["wg"].spec, _full(p["cw"].shape),
                  _full(p["cb"].shape), _full(p["wgate"].shape), _full(p["ba"].shape),
                  _full(p["bi"].shape), _full(p["lam"].shape), blk((SUBLANES, c)), blk((1, c))],
        out_specs=[blk((SUBLANES, c)), blk((CONV_W - 1, c)), blk((1, c))],
        out_shape=[jax.ShapeDtypeStruct((n, SUBLANES, c), F32),
                   jax.ShapeDtypeStruct((n, CONV_W - 1, c), F32),
                   jax.ShapeDtypeStruct((n, 1, c), F32)],
        compiler_params=_cparams("parallel"),
        name="rg_sample",
    )(xpad, p["wx"].array, p["wg"].array, p["cw"], p["cb"], p["wgate"], p["ba"], p["bi"], p["lam"], cstate, h0)


_SSD_WEIGHTS = ("wxbc", "wz", "wdt")
_SSD_SMALL = ("cw", "cb", "dtb", "alog", "dexp", "ng")


def _ssd_param_specs(p):
    return [p[k].spec for k in _SSD_WEIGHTS] + [_full(p[k].shape) for k in _SSD_SMALL]


def _ssd_param_args(p):
    return [p[k].array for k in _SSD_WEIGHTS] + [p[k] for k in _SSD_SMALL]


def _ssd_merge_prompt(x, p, rg_out, xa_out, mp):
    b, l, d = x.shape
    cc = p["wxbc"].shape[1]
    c = p["wz"].shape[1]
    tile = min(SSD_TILE, l)
    tiles_per_seq = l // tile
    n_tiles = b * tiles_per_seq
    perm, perm_t = _time_stride_perm(SSD_CHUNK)
    flat = lambda v: v.reshape(b * l, v.shape[-1])
    clamp = lambda i: jnp.clip(i, 0, n_tiles - 1)
    rows = lambda w, back: pl.BlockSpec((tile, w), lambda s: (clamp(s - back), 0))
    state = lambda shape: pl.BlockSpec((1,) + shape, lambda s: (clamp(s - 1) // tiles_per_seq, 0, 0))
    y, conv, hstate = pl.pallas_call(
        functools.partial(_ssd_merge_prompt_kernel, tiles_per_seq=tiles_per_seq, n_tiles=n_tiles),
        grid=(n_tiles + 2,),
        in_specs=[rows(d, 0), _full(perm.shape), _full(perm.shape)] + _ssd_param_specs(p)
        + [rows(rg_out.shape[-1], 2), rows(xa_out.shape[-1], 2), rows(d, 2), _full(mp["w"].shape),
           _full(mp["g"].shape), _full(mp["b"].shape)],
        out_specs=[rows(d, 2), state((CONV_W - 1, cc)), state((c, SSD_STATE))],
        out_shape=[jax.ShapeDtypeStruct((b * l, d), F32),
                   jax.ShapeDtypeStruct((b, CONV_W - 1, cc), F32),
                   jax.ShapeDtypeStruct((b, c, SSD_STATE), F32)],
        scratch_shapes=[pltpu.VMEM((CONV_W - 1, SUBLANES, cc), F32),
                        pltpu.VMEM((tile, cc + LANES), F32), pltpu.VMEM((tile, d), BF16),
                        pltpu.VMEM((c, SSD_STATE), F32), pltpu.VMEM((tile, c), BF16),
                        pltpu.VMEM((tile, cc), F32)],
        compiler_params=_cparams("arbitrary"),
        name="ssd_merge_prompt",
    )(flat(x), perm, perm_t, *_ssd_param_args(p), flat(rg_out), flat(xa_out), flat(x), mp["w"], mp["g"], mp["b"])
    return y.reshape(b, l, d), conv, hstate


def _ssd_sample(xpad, p, cstate, h0, seqs, valid):
    n, _, d = xpad.shape
    cc = p["wxbc"].shape[1]
    c = p["wz"].shape[1]
    gn = SSD_GROUPS * SSD_STATE
    blk = lambda w: pl.BlockSpec((seqs, w[0], w[1]), lambda i: (i, 0, 0))
    return pl.pallas_call(
        functools.partial(_ssd_sample_kernel, valid=valid),
        grid=(n // seqs,),
        in_specs=[blk((SUBLANES, d))] + _ssd_param_specs(p)
        + [_full(p["expand"].shape), blk((SUBLANES, cc)), blk((c, SSD_STATE))],
        out_specs=[blk((SUBLANES, c)), blk((CONV_W - 1, cc)), blk((c, SSD_STATE))],
        out_shape=[jax.ShapeDtypeStruct((n, SUBLANES, c), F32),
                   jax.ShapeDtypeStruct((n, CONV_W - 1, cc), F32),
                   jax.ShapeDtypeStruct((n, c, SSD_STATE), F32)],
        scratch_shapes=[pltpu.VMEM((seqs, SUBLANES, gn), F32), pltpu.VMEM((seqs, SUBLANES, gn), F32),
                        pltpu.VMEM((seqs, SUBLANES, c), F32), pltpu.VMEM((seqs, 1, LANES), F32),
                        pltpu.VMEM((seqs, SUBLANES, c), F32)],
        compiler_params=_cparams("parallel"),
        name="ssd_sample",
    )(xpad, *_ssd_param_args(p), p["expand"], cstate, h0)


def _memkv(mem, wk, wv):
    b, m, d = mem.shape
    c = wk.shape[1]
    nb = MEMKV_BATCHES if b % MEMKV_BATCHES == 0 else 1
    spec = pl.BlockSpec((nb, m, c), lambda i: (i, 0, 0))
    spec4 = pl.BlockSpec((nb, m, MEM_HEADS, c // MEM_HEADS), lambda i: (i, 0, 0, 0))
    return pl.pallas_call(
        _memkv_kernel,
        grid=(b // nb,),
        in_specs=[pl.BlockSpec((nb, m, d), lambda i: (i, 0, 0)), _full(wk.shape), _full(wv.shape)],
        out_specs=[spec4, spec4, spec, spec],
        out_shape=[jax.ShapeDtypeStruct((b, m, MEM_HEADS, c // MEM_HEADS), F32)] * 2
        + [jax.ShapeDtypeStruct((b, m, c), BF16)] * 2,
        compiler_params=_cparams("parallel"),
        name="mem_kv",
    )(mem, wk, wv)


def _xattn(x, xs_pad, wq, wg, k, v, ks, vs, tile):
    b, l, d = x.shape
    n = xs_pad.shape[0]
    c = wq.shape[1]
    m = k.shape[1]
    tile = min(tile, l)
    tiles_per_seq = l // tile
    n_tiles = b * tiles_per_seq
    seqs = n // n_tiles
    assert seqs * n_tiles == n
    kv_spec = pl.BlockSpec((1, m, c), lambda s: (s // tiles_per_seq, 0, 0))
    skv_spec = pl.BlockSpec((seqs,) + ks.shape[1:], lambda s: (s, 0, 0, 0))
    out, outs = pl.pallas_call(
        _xattn_kernel,
        grid=(n_tiles,),
        in_specs=[pl.BlockSpec((tile, d), lambda s: (s, 0)),
                  pl.BlockSpec((seqs, SUBLANES, d), lambda s: (s, 0, 0)),
                  wq.spec, wg.spec, kv_spec, kv_spec, skv_spec, skv_spec],
        out_specs=[pl.BlockSpec((tile, c), lambda s: (s, 0)),
                   pl.BlockSpec((seqs, SUBLANES, c), lambda s: (s, 0, 0))],
        out_shape=[jax.ShapeDtypeStruct((b * l, c), BF16), jax.ShapeDtypeStruct((n, SUBLANES, c), F32)],
        compiler_params=_cparams("parallel"),
        name="xattn",
    )(x.reshape(b * l, d), xs_pad, wq.array, wg.array, k, v, ks, vs)
    return out.reshape(b, l, c), outs


def _merge(rg, ssd, xa, x, w_out, ln_g, ln_b, tile, name):
    n, d = x.shape
    c = rg.shape[1]
    tile = min(tile, n)
    row = lambda w: pl.BlockSpec((tile, w), lambda i: (i, 0))
    return pl.pallas_call(
        functools.partial(_merge_kernel, sub_rows=min(MERGE_SUB_ROWS, tile)),
        grid=(n // tile,),
        in_specs=[row(c), row(c), row(c), row(d), _full(w_out.shape), _full(ln_g.shape), _full(ln_b.shape)],
        out_specs=row(d),
        out_shape=jax.ShapeDtypeStruct((n, d), F32),
        compiler_params=_cparams("parallel"),
        name=name,
    )(rg, ssd, xa, x, w_out, ln_g, ln_b)


class _Cols(NamedTuple):
    array: jax.Array
    width: int
    index: int

    @property
    def shape(self):
        return (self.array.shape[0], self.width)

    @property
    def spec(self):
        return pl.BlockSpec(self.shape, lambda *_, i=self.index: (0, i))


def _weight_prep_kernel(in_blk_ref, out_blk_ref, shift_ref, valid_ref, nxt_blk_ref, a_ref, b_ref, wo_ref,
                        o_ref, oo_ref, *, shift_rows):
    del in_blk_ref, out_blk_ref, nxt_blk_ref
    i = pl.program_id(0)
    rows = a_ref.shape[0]
    keep = lax.broadcasted_iota(jnp.int32, a_ref.shape, 0) < valid_ref[i]

    @pl.when(shift_ref[i] == 0)
    def _():
        o_ref[...] = jnp.where(keep, a_ref[...], 0.0).T.astype(BF16)

    @pl.when(shift_ref[i] != 0)
    def _():
        blk = jnp.concatenate([a_ref[shift_rows:, :], b_ref[:shift_rows, :]], axis=0)
        o_ref[...] = jnp.where(keep, blk, 0.0).T.astype(BF16)

    oo_ref[...] = wo_ref[...].astype(BF16)


def _weight_prep(w_in_t, w_out, bounds, placement):
    n_rows, k = w_in_t.shape
    blk = PREP_BLOCK
    in_blk, out_blk, shift, valid = [], [], [], []
    shifts = {lo % blk for lo, _ in bounds} - {0}
    assert len(shifts) <= 1
    shift_rows = shifts.pop() if shifts else SUBLANES
    assert shift_rows % SUBLANES == 0
    for (lo, hi), place in zip(bounds, placement):
        for j in range(-(-(hi - lo) // blk)):
            in_blk.append((lo + j * blk) // blk)
            out_blk.append(place + j)
            shift.append(lo % blk)
            valid.append(min(blk, hi - lo - j * blk))
    n_cols = (max(out_blk) + 1) * blk
    for gap in sorted(set(range(n_cols // blk)) - set(out_blk)):
        in_blk.append(0)
        out_blk.append(gap)
        shift.append(0)
        valid.append(0)
    steps = len(in_blk)
    last_in = -(-n_rows // blk) - 1
    oo_rows = -(-(-(-w_out.shape[0] // steps)) // SUBLANES) * SUBLANES
    oo_steps = -(-w_out.shape[0] // oo_rows)
    assert oo_steps <= steps
    nxt_blk = [min(a + 1, last_in) if sh else 0 for a, sh in zip(in_blk, shift)]
    tables = [jnp.asarray(np.asarray(t, np.int32)) for t in (in_blk, out_blk, shift, valid, nxt_blk)]
    wo_spec = pl.BlockSpec((oo_rows, w_out.shape[1]), lambda i, *_: (jnp.minimum(i, oo_steps - 1), 0))
    return pl.pallas_call(
        functools.partial(_weight_prep_kernel, shift_rows=shift_rows),
        grid_spec=pltpu.PrefetchScalarGridSpec(
            num_scalar_prefetch=5,
            grid=(steps,),
            in_specs=[pl.BlockSpec((blk, k), lambda i, ib, ob, sh, va, nb: (ib[i], 0)),
                      pl.BlockSpec((blk, k), lambda i, ib, ob, sh, va, nb: (nb[i], 0)),
                      wo_spec],
            out_specs=[pl.BlockSpec((k, blk), lambda i, ib, ob, sh, va, nb: (0, ob[i])), wo_spec],
        ),
        out_shape=[jax.ShapeDtypeStruct((k, n_cols), BF16), jax.ShapeDtypeStruct(w_out.shape, BF16)],
        compiler_params=_cparams("arbitrary"),
        name="weight_prep",
    )(*tables, w_in_t, w_in_t, w_out)


def _layer_params(w_in, rg_conv_w, rg_conv_b, w_rg_a, b_rg_a, w_rg_i, b_rg_i, rg_lambda, ssd_conv_w,
                  ssd_conv_b, ssd_dt_bias, ssd_a_log, ssd_d, ssd_norm_g, w_out, ln_g, ln_b):
    d_rg = rg_conv_w.shape[1]
    d_conv = ssd_conv_w.shape[1]
    d_ssd = ssd_norm_g.shape[0]
    heads = ssd_d.shape[0]
    sizes = (d_rg, d_rg, d_conv, d_ssd, heads)
    offs = [0]
    for s in sizes:
        offs.append(offs[-1] + s)
    d_xa = (w_in.shape[1] - offs[-1]) // 2
    offs += [offs[-1] + d_xa, offs[-1] + 2 * d_xa]
    blocks = lambda w: -(-w // PREP_BLOCK)
    assert d_rg == d_ssd == d_xa and blocks(d_conv) * PREP_BLOCK <= 2 * d_rg and heads <= LANES
    unit = blocks(d_rg)
    place = dict(xbc=0, z=2 * unit, rg_x=3 * unit, rg_g=4 * unit, xa_q=5 * unit, xa_g=6 * unit, dt=7 * unit)
    order = ("rg_x", "rg_g", "xbc", "z", "dt", "xa_q", "xa_g")
    wall, wo = _weight_prep(jnp.swapaxes(w_in, 0, 1), w_out, list(zip(offs[:-1], offs[1:])),
                            [place[k] for k in order])
    col = lambda key, width: _Cols(wall, width, place[key] * PREP_BLOCK // width)
    wx, wg, wxbc, wz = col("rg_x", d_rg), col("rg_g", d_rg), col("xbc", d_conv), col("z", d_ssd)
    wdt, wq, wxg = col("dt", LANES), col("xa_q", d_xa), col("xa_g", d_xa)
    row = lambda v: v.reshape(1, -1).astype(F32)
    pad_lanes = lambda v: jnp.pad(v, ((0, 0), (0, LANES - v.shape[1])))
    rg = dict(wx=wx, wg=wg, cw=rg_conv_w, cb=row(rg_conv_b),
              wgate=jnp.concatenate([w_rg_a, w_rg_i], axis=2).astype(BF16),
              ba=row(b_rg_a), bi=row(b_rg_i), lam=row(rg_lambda))
    head_of_channel = np.arange(d_ssd) // SSD_HEAD_DIM
    ssd = dict(wxbc=wxbc, wz=wz, wdt=wdt,
               cw=ssd_conv_w, cb=row(ssd_conv_b), dtb=pad_lanes(row(ssd_dt_bias)),
               alog=pad_lanes(row(ssd_a_log)), dexp=row(jnp.repeat(ssd_d, SSD_HEAD_DIM)), ng=row(ssd_norm_g),
               expand=jnp.asarray(np.arange(LANES)[:, None] == head_of_channel[None, :], dtype=BF16))
    xa = dict(wq=wq, wg=wxg)
    merge = dict(w=wo, g=row(ln_g), b=row(ln_b))
    return rg, ssd, xa, merge


PROMPT_RG_TILE = 256
PROMPT_XA_TILE = 512
SSD_TILE = 512
PREP_BLOCK = 512
MEMKV_BATCHES = 2
MERGE_TILE = 256
MERGE_SUB_ROWS = 256
SAMPLE_RG_SEQS = 32
SAMPLE_SSD_SEQS = 16


def kernel(x_prompt, x_sample, mem_prompt, state_rg_conv, state_rg_h, state_ssd_conv, state_ssd_h,
           cache_mem_k, cache_mem_v, w_in, rg_conv_w, rg_conv_b, w_rg_a, b_rg_a, w_rg_i, b_rg_i,
           rg_lambda, ssd_conv_w, ssd_conv_b, ssd_dt_bias, ssd_a_log, ssd_d, ssd_norm_g, w_mem_k,
           w_mem_v, w_out, ln_g, ln_b):
    assert w_in.shape[0] == DEPTH
    bp, lp, d = x_prompt.shape
    bs, ls, _ = x_sample.shape
    heads = ssd_d.shape[1]
    outs = {k: [] for k in ("rgc_p", "rgh_p", "sc_p", "sh_p", "mk_p", "mv_p", "rgc_s", "rgh_s", "sc_s", "sh_s")}
    yp, ys = x_prompt, x_sample
    pad_rows = lambda v, before, after: jnp.pad(v, ((0, 0), (before, after), (0, 0)))
    for l in range(DEPTH):
        rg, ssd, xa, merge = _layer_params(
            w_in[l], rg_conv_w[l], rg_conv_b[l], w_rg_a[l], b_rg_a[l], w_rg_i[l], b_rg_i[l], rg_lambda[l],
            ssd_conv_w[l], ssd_conv_b[l], ssd_dt_bias[l], ssd_a_log[l], ssd_d[l], ssd_norm_g[l],
            w_out[l], ln_g[l], ln_b[l])
        xs_pad = pad_rows(ys, 0, SAMPLE_PAD - ls)
        mk, mv, mkb, mvb = _memkv(mem_prompt, w_mem_k[l].astype(BF16), w_mem_v[l].astype(BF16))
        rg_o, rgc, rgh = _rg_prompt(yp, rg, PROMPT_RG_TILE)
        xa_o, xa_s = _xattn(yp, xs_pad, xa["wq"], xa["wg"], mkb, mvb, cache_mem_k[l], cache_mem_v[l],
                            PROMPT_XA_TILE)
        yp, sc, sh = _ssd_merge_prompt(yp, ssd, rg_o, xa_o, merge)
        outs["rgc_p"].append(rgc)
        outs["rgh_p"].append(rgh.reshape(bp, -1))
        outs["sc_p"].append(sc)
        outs["sh_p"].append(sh.reshape(bp, heads, SSD_HEAD_DIM, SSD_STATE))
        outs["mk_p"].append(mk)
        outs["mv_p"].append(mv)
        tail = SAMPLE_PAD - (CONV_W - 1)
        rg_o, rgc, rgh = _rg_sample(xs_pad, rg, pad_rows(state_rg_conv[l], tail, 0),
                                    state_rg_h[l][:, None, :], SAMPLE_RG_SEQS, ls)
        ssd_o, sc, sh = _ssd_sample(xs_pad, ssd, pad_rows(state_ssd_conv[l], tail, 0),
                                    state_ssd_h[l].reshape(bs, heads * SSD_HEAD_DIM, SSD_STATE),
                                    SAMPLE_SSD_SEQS, ls)
        flat = lambda v: v.reshape(bs * SAMPLE_PAD, v.shape[-1])
        ys_pad = _merge(flat(rg_o), flat(ssd_o), flat(xa_s), flat(xs_pad), merge["w"], merge["g"], merge["b"],
                        MERGE_TILE, "merge_sample").reshape(bs, SAMPLE_PAD, d)
        ys = ys_pad[:, :ls, :]
        outs["rgc_s"].append(rgc)
        outs["rgh_s"].append(rgh.reshape(bs, -1))
        outs["sc_s"].append(sc)
        outs["sh_s"].append(sh.reshape(bs, heads, SSD_HEAD_DIM, SSD_STATE))
    st = lambda k: jnp.stack(outs[k])
    return (yp, ys, st("rgc_p"), st("rgh_p"), st("sc_p"), st("sh_p"), st("mk_p"), st("mv_p"),
            st("rgc_s"), st("rgh_s"), st("sc_s"), st("sh_s"))
```

```python
import functools
from typing import NamedTuple

import jax
import jax.numpy as jnp
import numpy as np
from jax import lax
from jax.experimental import pallas as pl
from jax.experimental.pallas import tpu as pltpu

F32 = jnp.float32
BF16 = jnp.bfloat16

SUBLANES = 8
LANES = 128
MXU_WIDTH = 256
VMEM_LIMIT_BYTES = 56 * 1024 * 1024

RG_C = 8.0
CONV_W = 4
RG_BLOCKS = 8
SSD_HEAD_DIM = 64
SSD_GROUPS = 2
SSD_STATE = 128
SSD_CHUNK = 128
MEM_HEADS = 4
LN_EPS = 1e-5
RMS_EPS = 1e-5
DEPTH = 1
DEEPNORM_ALPHA = (2 * DEPTH) ** 0.25
LOG2_E = 1.4426950408889634
SAMPLE_PAD = SUBLANES


def _cparams(*sem):
    return pltpu.CompilerParams(dimension_semantics=sem, vmem_limit_bytes=VMEM_LIMIT_BYTES)


def _mm(a, b):
    return jnp.dot(a.astype(BF16), b.astype(BF16), preferred_element_type=F32)


def _mm_nt(a, b):
    return lax.dot_general(a.astype(BF16), b.astype(BF16), (((1,), (1,)), ((), ())),
                           preferred_element_type=F32)


def _mm_tn(a, b):
    return lax.dot_general(a.astype(BF16), b.astype(BF16), (((0,), (0,)), ((), ())),
                           preferred_element_type=F32)


def _exp_neg(x):
    return jnp.exp2(x * (-LOG2_E))


def _sigmoid(x):
    return 1.0 / (1.0 + _exp_neg(x))


def _silu(x):
    return x * _sigmoid(x)


def _softplus(x):
    return jnp.maximum(x, 0.0) + jnp.log(1.0 + jnp.exp(-jnp.abs(x)))


def _causal_conv_tiles(x3, p3, w, b):
    row = lax.broadcasted_iota(jnp.int32, x3.shape, 1)
    y = x3 * w[CONV_W - 1:CONV_W][None]
    for s in range(1, CONV_W):
        shifted = jnp.where(row >= s, pltpu.roll(x3, s, axis=1), pltpu.roll(p3, s, axis=1))
        y = y + shifted * w[CONV_W - 1 - s:CONV_W - s][None]
    return y + b[None]


def _scan_in_tiles(a3, b3):
    row = lax.broadcasted_iota(jnp.int32, a3.shape, 1)
    s = 1
    while s < SUBLANES:
        keep = row >= s
        a_sh = jnp.where(keep, pltpu.roll(a3, s, axis=1), 1.0)
        b_sh = jnp.where(keep, pltpu.roll(b3, s, axis=1), 0.0)
        b3 = a3 * b_sh + b3
        a3 = a3 * a_sh
        s *= 2
    return a3, b3


def _rg_gates(u, wgate, ba, bi, lam):
    pre = _mm(u, wgate)
    r = _sigmoid(pre[:, :LANES] + ba)
    i = _sigmoid(pre[:, LANES:] + bi)
    neg_log_a = r * (RG_C * _softplus(-lam))
    a = _exp_neg(neg_log_a)
    v = jnp.tanh(neg_log_a) * (1.0 + a * a)
    mult = jnp.where(v > 0.0, v * lax.rsqrt(v), 0.0)
    return a, mult * (i * u)


def _time_strided_conv(x3, tail3, w, b):
    slabs = x3.shape[0]
    row = lax.broadcasted_iota(jnp.int32, tail3.shape, 1)
    wrapped = jnp.where(row >= 1, pltpu.roll(x3[slabs - (CONV_W - 1):], 1, axis=1), pltpu.roll(tail3, 1, axis=1))
    y = x3 * w[CONV_W - 1:CONV_W][None]
    for s in range(1, CONV_W):
        shifted = jnp.concatenate([wrapped[CONV_W - 1 - s:], x3[:slabs - s]], axis=0)
        y = y + shifted * w[CONV_W - 1 - s:CONV_W - s][None]
    return y + b[None]


def _rg_prompt_kernel(x_ref, perm_ref, permt_ref, wx_ref, wg_ref, cw_ref, cb_ref, wgate_ref, ba_ref,
                      bi_ref, lam_ref, out_ref, conv_ref, hlast_ref, tail_scr, h_scr, proj_scr, outp_scr, *,
                      tiles_per_seq):
    s = pl.program_id(0)

    @pl.when(s == 0)
    def _():
        proj_scr[...] = jnp.zeros_like(proj_scr)

    @pl.when(jnp.logical_or(s == 0, s % tiles_per_seq == 1 % tiles_per_seq))
    def _():
        tail_scr[...] = jnp.zeros_like(tail_scr)
        h_scr[...] = jnp.zeros_like(h_scr)

    rows = x_ref.shape[0]
    slabs = rows // SUBLANES
    width = wx_ref.shape[1]
    per_group = MXU_WIDTH // LANES
    row = lax.broadcasted_iota(jnp.int32, (SUBLANES, LANES), 0)
    xb = jnp.dot(perm_ref[...], x_ref[...].astype(BF16), preferred_element_type=F32).astype(BF16)

    for cg in range(width // MXU_WIDTH):
        gs = slice(cg * MXU_WIDTH, (cg + 1) * MXU_WIDTH)
        for kk in range(per_group):
            k = cg * per_group + kk
            ks = slice(k * LANES, (k + 1) * LANES)
            x3 = proj_scr[:, ks].reshape(slabs, SUBLANES, LANES)
            tail3 = tail_scr[:, :, ks]
            last3 = x3[slabs - (CONV_W - 1):]
            tail_scr[:, :, ks] = last3
            for i in range(CONV_W - 1):
                conv_ref[0, i:i + 1, ks] = last3[i, SUBLANES - 1:, :]
            u3 = _time_strided_conv(x3, tail3, cw_ref[:, ks], cb_ref[:, ks])
            a, b = _rg_gates(u3.reshape(rows, LANES), wgate_ref[k], ba_ref[:, ks], bi_ref[:, ks], lam_ref[:, ks])
            a3 = a.reshape(slabs, SUBLANES, LANES)
            b3 = b.reshape(slabs, SUBLANES, LANES)
            h_loc, a_cum = [b3[0]], [a3[0]]
            for j in range(1, slabs):
                h_loc.append(a3[j] * h_loc[j - 1] + b3[j])
                a_cum.append(a3[j] * a_cum[j - 1])
            a_run, h_run = _scan_in_tiles(a_cum[-1][None], h_loc[-1][None])
            h_prev = h_scr[:, ks]
            h_end = h_run[0] + a_run[0] * h_prev
            h_in = jnp.where(row >= 1, pltpu.roll(h_end, 1, axis=0), h_prev)
            h_scr[:, ks] = h_end[SUBLANES - 1:]
            hlast_ref[0, :, ks] = h_end[SUBLANES - 1:]
            h = jnp.concatenate([h_loc[j] + a_cum[j] * h_in for j in range(slabs)], axis=0)
            gate = proj_scr[:, width + k * LANES:width + (k + 1) * LANES]
            outp_scr[:, ks] = (h * _silu(gate)).astype(BF16)
        proj_scr[:, gs] = jnp.dot(xb, wx_ref[:, gs], preferred_element_type=F32)
        proj_scr[:, width + cg * MXU_WIDTH:width + (cg + 1) * MXU_WIDTH] = jnp.dot(
            xb, wg_ref[:, gs], preferred_element_type=F32)

    out_ref[...] = jnp.dot(permt_ref[...], outp_scr[...], preferred_element_type=F32).astype(out_ref.dtype)


def _rg_sample_kernel(x_ref, wx_ref, wg_ref, cw_ref, cb_ref, wgate_ref, ba_ref, bi_ref, lam_ref,
                      cstate_ref, h0_ref, out_ref, conv_ref, hlast_ref, *, valid):
    seqs = x_ref.shape[0]
    rows = seqs * SUBLANES
    xb = x_ref[...].reshape(rows, x_ref.shape[2]).astype(BF16)
    rgx = jnp.dot(xb, wx_ref[...], preferred_element_type=F32)
    gate = jnp.dot(xb, wg_ref[...], preferred_element_type=F32)
    x3 = rgx.reshape(seqs, SUBLANES, rgx.shape[1])
    conv_ref[...] = x3[:, valid - (CONV_W - 1):valid, :]
    row = lax.broadcasted_iota(jnp.int32, (seqs, SUBLANES, LANES), 1)
    for k in range(RG_BLOCKS):
        ks = slice(k * LANES, (k + 1) * LANES)
        u3 = _causal_conv_tiles(x3[:, :, ks], cstate_ref[:, :, ks], cw_ref[:, ks], cb_ref[:, ks])
        u = u3.reshape(rows, LANES)
        a, b = _rg_gates(u, wgate_ref[k], ba_ref[:, ks], bi_ref[:, ks], lam_ref[:, ks])
        a3 = a.reshape(seqs, SUBLANES, LANES)
        b3 = b.reshape(seqs, SUBLANES, LANES)
        b3 = b3 + jnp.where(row == 0, a3 * h0_ref[:, :, ks], 0.0)
        _, h3 = _scan_in_tiles(a3, b3)
        hlast_ref[:, :, ks] = h3[:, valid - 1:valid, :]
        g3 = gate[:, ks].reshape(seqs, SUBLANES, LANES)
        out_ref[:, :, ks] = (h3 * _silu(g3)).astype(out_ref.dtype)


def _cumsum_rows(x):
    rows = x.shape[0]
    row = lax.broadcasted_iota(jnp.int32, x.shape, 0)
    s = 1
    while s < rows:
        x = x + jnp.where(row >= s, pltpu.roll(x, s, axis=0), 0.0)
        s *= 2
    return x


def _group_rmsnorm(y, gain):
    width = y.shape[-1] // SSD_GROUPS
    parts = []
    for g in range(SSD_GROUPS):
        yg = y[..., g * width:(g + 1) * width]
        ms = jnp.sum(yg * yg, axis=-1, keepdims=True) * (1.0 / width)
        parts.append(yg * lax.rsqrt(ms + RMS_EPS))
    return jnp.concatenate(parts, axis=-1) * gain


def _time_strided_cumsum(x):
    slabs = x.shape[0] // SUBLANES
    x3 = x.reshape(slabs, SUBLANES, x.shape[1])
    acc = [x3[0]]
    for j in range(1, slabs):
        acc.append(acc[j - 1] + x3[j])
    _, run = _scan_in_tiles(jnp.ones_like(acc[-1])[None], acc[-1][None])
    row = lax.broadcasted_iota(jnp.int32, run[0].shape, 0)
    before = jnp.where(row >= 1, pltpu.roll(run[0], 1, axis=0), 0.0)
    return jnp.concatenate([a + before for a in acc], axis=0)


def _ssd_merge_prompt_kernel(x_ref, perm_ref, permt_ref, wxbc_ref, wz_ref, wdt_ref, cw_ref, cb_ref, dtb_ref,
                             alog_ref, dexp_ref, ng_ref, rg_ref, xa_ref, xres_ref, wout_ref, lng_ref, lnb_ref,
                             y_ref, conv_ref, h_ref, tail_scr, proj_scr, xb_scr, h_scr, ssd_scr, xbc_scr, *,
                             tiles_per_seq, n_tiles):
    s = pl.program_id(0)

    @pl.when(s == 0)
    def _():
        proj_scr[...] = jnp.zeros_like(proj_scr)
        xb_scr[...] = jnp.zeros_like(xb_scr)
        ssd_scr[...] = jnp.zeros_like(ssd_scr)

    @pl.when(jnp.logical_or(s == 0, s % tiles_per_seq == 1 % tiles_per_seq))
    def _():
        tail_scr[...] = jnp.zeros_like(tail_scr)
        h_scr[...] = jnp.zeros_like(h_scr)

    q = SSD_CHUNK
    n_sub = x_ref.shape[0] // q
    slabs = q // SUBLANES
    d_ssd = wz_ref.shape[1]
    d_conv = wxbc_ref.shape[1]
    gn = SSD_GROUPS * SSD_STATE
    tail3 = tail_scr[...]
    for c in range(n_sub):
        x3 = proj_scr[c * q:(c + 1) * q, :d_conv].reshape(slabs, SUBLANES, d_conv)
        xbc_scr[c * q:(c + 1) * q, :] = _silu(_time_strided_conv(x3, tail3, cw_ref[...], cb_ref[...])).reshape(
            q, d_conv)
        tail3 = x3[slabs - (CONV_W - 1):]
    tail_scr[...] = tail3
    dtr_all = proj_scr[:, d_conv:]

    mix, row0 = None, 0
    for src in (rg_ref, ssd_scr, xa_ref):
        part = jnp.dot(src[...], wout_ref[row0:row0 + src.shape[1], :], preferred_element_type=F32)
        mix = part if mix is None else mix + part
        row0 += src.shape[1]
    res = DEEPNORM_ALPHA * xres_ref[...] + mix
    mu = jnp.mean(res, axis=-1, keepdims=True)
    cen = res - mu
    var = jnp.mean(cen * cen, axis=-1, keepdims=True)
    y_ref[...] = cen * lax.rsqrt(var + LN_EPS) * lng_ref[...] + lnb_ref[...]

    xin = x_ref[...].astype(BF16)
    xb_new = jnp.concatenate(
        [jnp.dot(perm_ref[...], xin[c * q:(c + 1) * q], preferred_element_type=F32).astype(BF16)
         for c in range(n_sub)], axis=0)
    z_all = jnp.dot(xb_scr[...], wz_ref[...], preferred_element_type=F32)
    xb_scr[...] = xb_new

    ii = lax.broadcasted_iota(jnp.int32, (q, q), 0)
    jj = lax.broadcasted_iota(jnp.int32, (q, q), 1)
    time_of = lambda r: (r % SUBLANES) * slabs + r // SUBLANES
    causal = time_of(ii) >= time_of(jj)
    lane = lax.broadcasted_iota(jnp.int32, (q, LANES), 1)
    lo = lane < SSD_HEAD_DIM
    srow = lax.broadcasted_iota(jnp.int32, (LANES, SSD_STATE), 0) < SSD_HEAD_DIM
    heads = d_ssd // SSD_HEAD_DIM
    pairs = heads // 2
    pairs_per_group = pairs // SSD_GROUPS

    projections = [(wxbc_ref, slice(g * MXU_WIDTH, (g + 1) * MXU_WIDTH), slice(g * MXU_WIDTH, (g + 1) * MXU_WIDTH))
                   for g in range(d_conv // MXU_WIDTH)]
    projections.append((wdt_ref, slice(None), slice(d_conv, None)))
    every = (n_sub * pairs) // len(projections)
    assert every >= 1
    issued = 0

    for c in range(n_sub):
        cs = slice(c * q, (c + 1) * q)
        sx = xbc_scr[cs, :d_ssd]
        bm = xbc_scr[cs, d_ssd:d_ssd + gn]
        cm = xbc_scr[cs, d_ssd + gn:]
        dt = _softplus(dtr_all[cs] + dtb_ref[...])
        da = dt * (-jnp.exp(alog_ref[...]))
        acum = _time_strided_cumsum(da) * LOG2_E
        alast = acum[q - 1:q, :]
        wgt = dt * jnp.exp2(alast - acum)
        tot = jnp.exp2(alast)
        acum_dt_t = (acum - jnp.log2(dt)).T
        cb = [_mm_nt(cm[:, g * SSD_STATE:(g + 1) * SSD_STATE], bm[:, g * SSD_STATE:(g + 1) * SSD_STATE])
              for g in range(SSD_GROUPS)]
        y_parts = []
        for g in range(SSD_GROUPS):
            ns = slice(g * SSD_STATE, (g + 1) * SSD_STATE)
            rows_g = slice(g * pairs_per_group * LANES, (g + 1) * pairs_per_group * LANES)
            h_in = h_scr[rows_g, :]
            y_off_g = _mm_nt(cm[:, ns], h_in)
            xws, decays = [], []
            for pl_ in range(pairs_per_group):
                pq = g * pairs_per_group + pl_
                ps = slice(pq * LANES, (pq + 1) * LANES)
                xq = sx[:, ps]
                ms, es, ws, ts = [], [], [], []
                for h in (2 * pq, 2 * pq + 1):
                    acol = jnp.broadcast_to(acum[:, h:h + 1], (q, q))
                    arow = jnp.broadcast_to(acum_dt_t[h:h + 1, :], (q, q))
                    decay_dt = jnp.exp2(jnp.where(causal, acol - arow, -jnp.inf))
                    ms.append((cb[g] * decay_dt).astype(BF16))
                    es.append(jnp.exp2(jnp.broadcast_to(acum[:, h:h + 1], (q, LANES))))
                    ws.append(jnp.broadcast_to(wgt[:, h:h + 1], (q, LANES)))
                    ts.append(jnp.broadcast_to(tot[:, h:h + 1], (LANES, SSD_STATE)))
                lhs = jnp.concatenate(ms, axis=1)
                rhs = jnp.concatenate([jnp.where(lo, xq, 0.0), jnp.where(lo, 0.0, xq)], axis=0)
                y_diag = _mm(lhs, rhs)
                y_off = y_off_g[:, pl_ * LANES:(pl_ + 1) * LANES] * jnp.where(lo, es[0], es[1])
                xws.append(xq * jnp.where(lo, ws[0], ws[1]))
                decays.append(jnp.where(srow, ts[0], ts[1]))
                y_parts.append(y_diag + y_off + dexp_ref[:, ps] * xq)
                if (c * pairs + pq + 1) % every == 0 and issued < len(projections):
                    w_ref, src, dst = projections[issued]
                    proj_scr[:, dst] = jnp.dot(xb_new, w_ref[:, src], preferred_element_type=F32)
                    issued += 1
            h_scr[rows_g, :] = (h_in * jnp.concatenate(decays, axis=0)
                                + _mm_tn(jnp.concatenate(xws, axis=1), bm[:, ns]))
        y = jnp.concatenate(y_parts, axis=1) * _silu(z_all[cs])
        y = _group_rmsnorm(y, ng_ref[...]).astype(BF16)
        ssd_scr[cs, :] = jnp.dot(permt_ref[...], y, preferred_element_type=F32).astype(BF16)
    assert issued == len(projections)

    @pl.when(jnp.logical_and(s % tiles_per_seq == 0, jnp.logical_and(s > 0, s <= n_tiles)))
    def _():
        h_ref[0] = h_scr[...]
        for i in range(CONV_W - 1):
            conv_ref[0, i:i + 1, :] = tail_scr[i, SUBLANES - 1:, :]


def _ssd_sample_kernel(x_ref, wxbc_ref, wz_ref, wdt_ref, cw_ref, cb_ref, dtb_ref, alog_ref, dexp_ref,
                       ng_ref, expand_ref, cstate_ref, h0_ref, out_ref, conv_ref, h_ref,
                       c_scr, b_scr, xw_scr, tot_scr, yoff_scr, *, valid):
    seqs = x_ref.shape[0]
    rows = seqs * SUBLANES
    d_ssd = wz_ref.shape[1]
    gn = SSD_GROUPS * SSD_STATE
    heads = d_ssd // SSD_HEAD_DIM
    hg = heads // SSD_GROUPS
    xb = x_ref[...].reshape(rows, x_ref.shape[2]).astype(BF16)
    xbc_raw = jnp.dot(xb, wxbc_ref[...], preferred_element_type=F32)
    z = jnp.dot(xb, wz_ref[...], preferred_element_type=F32)
    dtr = jnp.dot(xb, wdt_ref[...], preferred_element_type=F32)

    x3 = xbc_raw.reshape(seqs, SUBLANES, xbc_raw.shape[1])
    conv_ref[...] = x3[:, valid - (CONV_W - 1):valid, :]
    xbc3 = _silu(_causal_conv_tiles(x3, cstate_ref[...], cw_ref[...], cb_ref[...]))
    sx3 = xbc3[:, :, :d_ssd]
    b3 = xbc3[:, :, d_ssd:d_ssd + gn]
    c3 = xbc3[:, :, d_ssd + gn:]

    row = lax.broadcasted_iota(jnp.int32, (seqs, SUBLANES, LANES), 1)
    lane = lax.broadcasted_iota(jnp.int32, (seqs, SUBLANES, LANES), 2)
    dt3 = jnp.where(row < valid, _softplus(dtr + dtb_ref[...]).reshape(seqs, SUBLANES, LANES), 0.0)
    da3 = dt3 * (-jnp.exp(alog_ref[...]))[None]
    ones = jnp.ones_like(da3)
    _, acum3 = _scan_in_tiles(ones, da3)
    alast = acum3[:, SUBLANES - 1:, :]
    wgt3 = dt3 * jnp.exp(alast - acum3)
    tot_scr[...] = jnp.exp(alast)
    e3 = jnp.exp(acum3)

    coefs = []
    for u in range(valid):
        prod = c3 * b3[:, u:u + 1, :]
        cbu = [jnp.sum(prod[:, :, g * SSD_STATE:(g + 1) * SSD_STATE], axis=-1, keepdims=True)
               for g in range(SSD_GROUPS)]
        cb_heads = jnp.where(lane < hg, cbu[0], cbu[1])
        coefs.append(jnp.where(row >= u, cb_heads * jnp.exp(acum3 - acum3[:, u:u + 1, :]) * dt3[:, u:u + 1, :],
                               0.0))

    per_head = coefs + [wgt3, e3]
    stacked = jnp.concatenate([v.reshape(rows, LANES) for v in per_head], axis=0)
    expanded = None
    rest = stacked
    for _ in range(3):
        piece = rest.astype(BF16)
        rest = rest - piece.astype(F32)
        part = jnp.dot(piece, expand_ref[...], preferred_element_type=F32)
        expanded = part if expanded is None else expanded + part
    expanded = [expanded[i * rows:(i + 1) * rows].reshape(seqs, SUBLANES, d_ssd) for i in range(len(per_head))]
    y_diag = expanded[0] * sx3[:, 0:1, :]
    for u in range(1, valid):
        y_diag = y_diag + expanded[u] * sx3[:, u:u + 1, :]
    wgt_x, e_x = expanded[valid], expanded[valid + 1]

    c_scr[...] = c3
    b_scr[...] = b3
    xw_scr[...] = sx3 * wgt_x
    srow = lax.broadcasted_iota(jnp.int32, (2 * SSD_HEAD_DIM, SSD_STATE), 0) < SSD_HEAD_DIM

    def per_seq(s, carry):
        cs = c_scr[s]
        bs = b_scr[s]
        xws = xw_scr[s]
        tots = tot_scr[s]
        for g in range(SSD_GROUPS):
            gs = slice(g * SSD_STATE, (g + 1) * SSD_STATE)
            width = hg * SSD_HEAD_DIM
            cols = slice(g * width, (g + 1) * width)
            hin = h0_ref[s, cols, :]
            yoff_scr[s, :, cols] = _mm_nt(cs[:, gs], hin)
            upd = _mm_tn(xws[:, cols], bs[:, gs])
            for pq in range(hg // 2):
                h = g * hg + 2 * pq
                rs = slice(pq * LANES, (pq + 1) * LANES)
                t0 = jnp.broadcast_to(tots[0:1, h:h + 1], (LANES, SSD_STATE))
                t1 = jnp.broadcast_to(tots[0:1, h + 1:h + 2], (LANES, SSD_STATE))
                h_ref[s, g * width + pq * LANES:g * width + (pq + 1) * LANES, :] = (
                    hin[rs] * jnp.where(srow, t0, t1) + upd[rs])
        return carry

    lax.fori_loop(0, seqs, per_seq, 0)

    y = y_diag + yoff_scr[...] * e_x + dexp_ref[...][None] * sx3
    y = y * _silu(z.reshape(seqs, SUBLANES, d_ssd))
    out_ref[...] = _group_rmsnorm(y, ng_ref[...][None]).astype(out_ref.dtype)


def _memkv_kernel(mem_ref, wk_ref, wv_ref, k_ref, v_ref, kb_ref, vb_ref):
    nb, m, d = mem_ref.shape
    mb = mem_ref[...].reshape(nb * m, d).astype(BF16)
    for w_ref, o_ref, ob_ref in ((wk_ref, k_ref, kb_ref), (wv_ref, v_ref, vb_ref)):
        proj = jnp.dot(mb, w_ref[...], preferred_element_type=F32)
        for i in range(nb):
            o_ref[i] = proj[i * m:(i + 1) * m].reshape(o_ref.shape[1:])
        ob_ref[...] = proj.astype(BF16).reshape(ob_ref.shape)


def _softmax_terms(scores):
    p = jnp.exp(scores - jnp.max(scores, axis=-1, keepdims=True))
    return p, jnp.sum(p, axis=-1, keepdims=True)


def _prompt_attention_head(h, q_ref, rows, k_ref, v_ref, out_ref):
    c = k_ref.shape[2]
    d_head = c // MEM_HEADS
    hs = slice(h * d_head, (h + 1) * d_head)

    def scores():
        return _softmax_terms(_mm_nt(q_ref[:rows, hs], k_ref[0, :, hs]) * (d_head ** -0.5))

    def output(p, l):
        o = _mm(p, v_ref[0, :, hs]) * (1.0 / l)
        gate = q_ref[:rows, c + h * d_head:c + (h + 1) * d_head]
        out_ref[:, hs] = (o * _silu(gate)).astype(out_ref.dtype)

    return scores, output


def _sample_attention_seq(s, q_ref, rows, k_ref, v_ref, out_ref):
    _, n_mem, heads, d_head = k_ref.shape
    c = heads * d_head
    rs = slice(rows + s * SUBLANES, rows + (s + 1) * SUBLANES)

    def scores():
        shape = (heads * SUBLANES, n_mem * heads)
        same_head = (lax.broadcasted_iota(jnp.int32, shape, 0) // SUBLANES
                     == lax.broadcasted_iota(jnp.int32, shape, 1) % heads)
        qh = jnp.concatenate([q_ref[rs, h * d_head:(h + 1) * d_head] for h in range(heads)], axis=0)
        sc = _mm_nt(qh, k_ref[s].reshape(n_mem * heads, d_head)) * (d_head ** -0.5)
        return _softmax_terms(jnp.where(same_head, sc, -jnp.inf))

    def output(p, l):
        o = _mm(p, v_ref[s].reshape(n_mem * heads, d_head)) * (1.0 / l)
        o = jnp.concatenate([o[h * SUBLANES:(h + 1) * SUBLANES] for h in range(heads)], axis=1)
        out_ref[s] = (o * _silu(q_ref[rs, c:])).astype(out_ref.dtype)

    return scores, output


def _xattn_kernel(x_ref, xs_ref, wq_ref, wg_ref, k_ref, v_ref, ks_ref, vs_ref, out_ref, outs_ref):
    rows = x_ref.shape[0]
    seqs, _, d = xs_ref.shape
    xb = jnp.concatenate([x_ref[...], xs_ref[...].reshape(seqs * SUBLANES, d)], axis=0).astype(BF16)
    qg = jnp.concatenate([jnp.dot(xb, wq_ref[...], preferred_element_type=F32),
                          jnp.dot(xb, wg_ref[...], preferred_element_type=F32)], axis=1)
    items = [_prompt_attention_head(h, qg, rows, k_ref, v_ref, out_ref) for h in range(MEM_HEADS)]
    items += [_sample_attention_seq(i, qg, rows, ks_ref, vs_ref, outs_ref) for i in range(seqs)]
    for scores, output in items:
        output(*scores())


def _merge_kernel(rg_ref, ssd_ref, xa_ref, x_ref, w_ref, g_ref, b_ref, y_ref, *, sub_rows):
    d = rg_ref.shape[1]
    for r0 in range(0, x_ref.shape[0], sub_rows):
        rs = slice(r0, r0 + sub_rows)
        mix = (jnp.dot(rg_ref[rs, :].astype(BF16), w_ref[0:d, :], preferred_element_type=F32)
               + jnp.dot(ssd_ref[rs, :].astype(BF16), w_ref[d:2 * d, :], preferred_element_type=F32)
               + jnp.dot(xa_ref[rs, :].astype(BF16), w_ref[2 * d:3 * d, :], preferred_element_type=F32))
        res = DEEPNORM_ALPHA * x_ref[rs, :] + mix
        mu = jnp.mean(res, axis=-1, keepdims=True)
        cen = res - mu
        var = jnp.mean(cen * cen, axis=-1, keepdims=True)
        y_ref[rs, :] = cen * lax.rsqrt(var + LN_EPS) * g_ref[...] + b_ref[...]


def _full(shape):
    return pl.BlockSpec(shape, lambda *_: (0,) * len(shape))


def _time_stride_perm(rows):
    p = np.arange(rows)
    t = (p % SUBLANES) * (rows // SUBLANES) + p // SUBLANES
    perm = t[:, None] == np.arange(rows)[None, :]
    return jnp.asarray(perm, dtype=BF16), jnp.asarray(perm.T, dtype=BF16)


def _rg_prompt(x, p, tile):
    b, l, d = x.shape
    c = p["wx"].shape[1]
    tiles_per_seq = l // tile
    n_tiles = b * tiles_per_seq
    perm, perm_t = _time_stride_perm(tile)
    done = lambda s: jnp.maximum(s - 1, 0)
    out, conv, hlast = pl.pallas_call(
        functools.partial(_rg_prompt_kernel, tiles_per_seq=tiles_per_seq),
        grid=(n_tiles + 1,),
        in_specs=[pl.BlockSpec((tile, d), lambda s: (jnp.minimum(s, n_tiles - 1), 0)),
                  _full(perm.shape), _full(perm.shape), p["wx"].spec, p["wg"].spec,
                  _full(p["cw"].shape), _full(p["cb"].shape), _full(p["wgate"].shape), _full(p["ba"].shape),
                  _full(p["bi"].shape), _full(p["lam"].shape)],
        out_specs=[pl.BlockSpec((tile, c), lambda s: (done(s), 0)),
                   pl.BlockSpec((1, CONV_W - 1, c), lambda s: (done(s) // tiles_per_seq, 0, 0)),
                   pl.BlockSpec((1, 1, c), lambda s: (done(s) // tiles_per_seq, 0, 0))],
        out_shape=[jax.ShapeDtypeStruct((b * l, c), BF16),
                   jax.ShapeDtypeStruct((b, CONV_W - 1, c), F32),
                   jax.ShapeDtypeStruct((b, 1, c), F32)],
        scratch_shapes=[pltpu.VMEM((CONV_W - 1, SUBLANES, c), F32), pltpu.VMEM((1, c), F32),
                        pltpu.VMEM((tile, 2 * c), F32), pltpu.VMEM((tile, c), BF16)],
        compiler_params=_cparams("arbitrary"),
        name="rg_prompt",
    )(x.reshape(b * l, d), perm, perm_t, p["wx"].array, p["wg"].array, p["cw"], p["cb"], p["wgate"], p["ba"],
      p["bi"], p["lam"])
    return out.reshape(b, l, c), conv, hlast


def _rg_sample(xpad, p, cstate, h0, seqs, valid):
    n, _, d = xpad.shape
    c = p["wx"].shape[1]
    blk = lambda w: pl.BlockSpec((seqs, w[0], w[1]), lambda i: (i, 0, 0))
    return pl.pallas_call(
        functools.partial(_rg_sample_kernel, valid=valid),
        grid=(n // seqs,),
        in_specs=[blk((SUBLANES, d)), p["wx"].spec, p["wg"].spec, _full(p["cw"].shape),
                  _full(p["cb"].shape), _full(p["wgate"].shape), _full(p["ba"].shape),
                  _full(p["bi"].shape), _full(p["lam"].shape), blk((SUBLANES, c)), blk((1, c))],
        out_specs=[blk((SUBLANES, c)), blk((CONV_W - 1, c)), blk((1, c))],
        out_shape=[jax.ShapeDtypeStruct((n, SUBLANES, c), F32),
                   jax.ShapeDtypeStruct((n, CONV_W - 1, c), F32),
                   jax.ShapeDtypeStruct((n, 1, c), F32)],
        compiler_params=_cparams("parallel"),
        name="rg_sample",
    )(xpad, p["wx"].array, p["wg"].array, p["cw"], p["cb"], p["wgate"], p["ba"], p["bi"], p["lam"], cstate, h0)


_SSD_WEIGHTS = ("wxbc", "wz", "wdt")
_SSD_SMALL = ("cw", "cb", "dtb", "alog", "dexp", "ng")


def _ssd_param_specs(p):
    return [p[k].spec for k in _SSD_WEIGHTS] + [_full(p[k].shape) for k in _SSD_SMALL]


def _ssd_param_args(p):
    return [p[k].array for k in _SSD_WEIGHTS] + [p[k] for k in _SSD_SMALL]


def _ssd_merge_prompt(x, p, rg_out, xa_out, mp):
    b, l, d = x.shape
    cc = p["wxbc"].shape[1]
    c = p["wz"].shape[1]
    tile = min(SSD_TILE, l)
    tiles_per_seq = l // tile
    n_tiles = b * tiles_per_seq
    perm, perm_t = _time_stride_perm(SSD_CHUNK)
    flat = lambda v: v.reshape(b * l, v.shape[-1])
    clamp = lambda i: jnp.clip(i, 0, n_tiles - 1)
    rows = lambda w, back: pl.BlockSpec((tile, w), lambda s: (clamp(s - back), 0))
    state = lambda shape: pl.BlockSpec((1,) + shape, lambda s: (clamp(s - 1) // tiles_per_seq, 0, 0))
    y, conv, hstate = pl.pallas_call(
        functools.partial(_ssd_merge_prompt_kernel, tiles_per_seq=tiles_per_seq, n_tiles=n_tiles),
        grid=(n_tiles + 2,),
        in_specs=[rows(d, 0), _full(perm.shape), _full(perm.shape)] + _ssd_param_specs(p)
        + [rows(rg_out.shape[-1], 2), rows(xa_out.shape[-1], 2), rows(d, 2), _full(mp["w"].shape),
           _full(mp["g"].shape), _full(mp["b"].shape)],
        out_specs=[rows(d, 2), state((CONV_W - 1, cc)), state((c, SSD_STATE))],
        out_shape=[jax.ShapeDtypeStruct((b * l, d), F32),
                   jax.ShapeDtypeStruct((b, CONV_W - 1, cc), F32),
                   jax.ShapeDtypeStruct((b, c, SSD_STATE), F32)],
        scratch_shapes=[pltpu.VMEM((CONV_W - 1, SUBLANES, cc), F32),
                        pltpu.VMEM((tile, cc + LANES), F32), pltpu.VMEM((tile, d), BF16),
                        pltpu.VMEM((c, SSD_STATE), F32), pltpu.VMEM((tile, c), BF16),
                        pltpu.VMEM((tile, cc), F32)],
        compiler_params=_cparams("arbitrary"),
        name="ssd_merge_prompt",
    )(flat(x), perm, perm_t, *_ssd_param_args(p), flat(rg_out), flat(xa_out), flat(x), mp["w"], mp["g"], mp["b"])
    return y.reshape(b, l, d), conv, hstate


def _ssd_sample(xpad, p, cstate, h0, seqs, valid):
    n, _, d = xpad.shape
    cc = p["wxbc"].shape[1]
    c = p["wz"].shape[1]
    gn = SSD_GROUPS * SSD_STATE
    blk = lambda w: pl.BlockSpec((seqs, w[0], w[1]), lambda i: (i, 0, 0))
    return pl.pallas_call(
        functools.partial(_ssd_sample_kernel, valid=valid),
        grid=(n // seqs,),
        in_specs=[blk((SUBLANES, d))] + _ssd_param_specs(p)
        + [_full(p["expand"].shape), blk((SUBLANES, cc)), blk((c, SSD_STATE))],
        out_specs=[blk((SUBLANES, c)), blk((CONV_W - 1, cc)), blk((c, SSD_STATE))],
        out_shape=[jax.ShapeDtypeStruct((n, SUBLANES, c), F32),
                   jax.ShapeDtypeStruct((n, CONV_W - 1, cc), F32),
                   jax.ShapeDtypeStruct((n, c, SSD_STATE), F32)],
        scratch_shapes=[pltpu.VMEM((seqs, SUBLANES, gn), F32), pltpu.VMEM((seqs, SUBLANES, gn), F32),
                        pltpu.VMEM((seqs, SUBLANES, c), F32), pltpu.VMEM((seqs, 1, LANES), F32),
                        pltpu.VMEM((seqs, SUBLANES, c), F32)],
        compiler_params=_cparams("parallel"),
        name="ssd_sample",
    )(xpad, *_ssd_param_args(p), p["expand"], cstate, h0)


def _memkv(mem, wk, wv):
    b, m, d = mem.shape
    c = wk.shape[1]
    nb = MEMKV_BATCHES if b % MEMKV_BATCHES == 0 else 1
    spec = pl.BlockSpec((nb, m, c), lambda i: (i, 0, 0))
    spec4 = pl.BlockSpec((nb, m, MEM_HEADS, c // MEM_HEADS), lambda i: (i, 0, 0, 0))
    return pl.pallas_call(
        _memkv_kernel,
        grid=(b // nb,),
        in_specs=[pl.BlockSpec((nb, m, d), lambda i: (i, 0, 0)), _full(wk.shape), _full(wv.shape)],
        out_specs=[spec4, spec4, spec, spec],
        out_shape=[jax.ShapeDtypeStruct((b, m, MEM_HEADS, c // MEM_HEADS), F32)] * 2
        + [jax.ShapeDtypeStruct((b, m, c), BF16)] * 2,
        compiler_params=_cparams("parallel"),
        name="mem_kv",
    )(mem, wk, wv)


def _xattn(x, xs_pad, wq, wg, k, v, ks, vs, tile):
    b, l, d = x.shape
    n = xs_pad.shape[0]
    c = wq.shape[1]
    m = k.shape[1]
    tile = min(tile, l)
    tiles_per_seq = l // tile
    n_tiles = b * tiles_per_seq
    seqs = n // n_tiles
    assert seqs * n_tiles == n
    kv_spec = pl.BlockSpec((1, m, c), lambda s: (s // tiles_per_seq, 0, 0))
    skv_spec = pl.BlockSpec((seqs,) + ks.shape[1:], lambda s: (s, 0, 0, 0))
    out, outs = pl.pallas_call(
        _xattn_kernel,
        grid=(n_tiles,),
        in_specs=[pl.BlockSpec((tile, d), lambda s: (s, 0)),
                  pl.BlockSpec((seqs, SUBLANES, d), lambda s: (s, 0, 0)),
                  wq.spec, wg.spec, kv_spec, kv_spec, skv_spec, skv_spec],
        out_specs=[pl.BlockSpec((tile, c), lambda s: (s, 0)),
                   pl.BlockSpec((seqs, SUBLANES, c), lambda s: (s, 0, 0))],
        out_shape=[jax.ShapeDtypeStruct((b * l, c), BF16), jax.ShapeDtypeStruct((n, SUBLANES, c), F32)],
        compiler_params=_cparams("parallel"),
        name="xattn",
    )(x.reshape(b * l, d), xs_pad, wq.array, wg.array, k, v, ks, vs)
    return out.reshape(b, l, c), outs


def _merge(rg, ssd, xa, x, w_out, ln_g, ln_b, tile, name):
    n, d = x.shape
    c = rg.shape[1]
    tile = min(tile, n)
    row = lambda w: pl.BlockSpec((tile, w), lambda i: (i, 0))
    return pl.pallas_call(
        functools.partial(_merge_kernel, sub_rows=min(MERGE_SUB_ROWS, tile)),
        grid=(n // tile,),
        in_specs=[row(c), row(c), row(c), row(d), _full(w_out.shape), _full(ln_g.shape), _full(ln_b.shape)],
        out_specs=row(d),
        out_shape=jax.ShapeDtypeStruct((n, d), F32),
        compiler_params=_cparams("parallel"),
        name=name,
    )(rg, ssd, xa, x, w_out, ln_g, ln_b)


class _Cols(NamedTuple):
    array: jax.Array
    width: int
    index: int

    @property
    def shape(self):
        return (self.array.shape[0], self.width)

    @property
    def spec(self):
        return pl.BlockSpec(self.shape, lambda *_, i=self.index: (0, i))


def _weight_prep_kernel(in_blk_ref, out_blk_ref, shift_ref, valid_ref, nxt_blk_ref, a_ref, b_ref, wo_ref,
                        o_ref, oo_ref, *, shift_rows):
    del in_blk_ref, out_blk_ref, nxt_blk_ref
    i = pl.program_id(0)
    rows = a_ref.shape[0]
    keep = lax.broadcasted_iota(jnp.int32, a_ref.shape, 0) < valid_ref[i]

    @pl.when(shift_ref[i] == 0)
    def _():
        o_ref[...] = jnp.where(keep, a_ref[...], 0.0).T.astype(BF16)

    @pl.when(shift_ref[i] != 0)
    def _():
        blk = jnp.concatenate([a_ref[shift_rows:, :], b_ref[:shift_rows, :]], axis=0)
        o_ref[...] = jnp.where(keep, blk, 0.0).T.astype(BF16)

    oo_ref[...] = wo_ref[...].astype(BF16)


def _weight_prep(w_in_t, w_out, bounds, placement):
    n_rows, k = w_in_t.shape
    blk = PREP_BLOCK
    in_blk, out_blk, shift, valid = [], [], [], []
    shifts = {lo % blk for lo, _ in bounds} - {0}
    assert len(shifts) <= 1
    shift_rows = shifts.pop() if shifts else SUBLANES
    assert shift_rows % SUBLANES == 0
    for (lo, hi), place in zip(bounds, placement):
        for j in range(-(-(hi - lo) // blk)):
            in_blk.append((lo + j * blk) // blk)
            out_blk.append(place + j)
            shift.append(lo % blk)
            valid.append(min(blk, hi - lo - j * blk))
    n_cols = (max(out_blk) + 1) * blk
    for gap in sorted(set(range(n_cols // blk)) - set(out_blk)):
        in_blk.append(0)
        out_blk.append(gap)
        shift.append(0)
        valid.append(0)
    steps = len(in_blk)
    last_in = -(-n_rows // blk) - 1
    oo_rows = -(-(-(-w_out.shape[0] // steps)) // SUBLANES) * SUBLANES
    oo_steps = -(-w_out.shape[0] // oo_rows)
    assert oo_steps <= steps
    nxt_blk = [min(a + 1, last_in) if sh else 0 for a, sh in zip(in_blk, shift)]
    tables = [jnp.asarray(np.asarray(t, np.int32)) for t in (in_blk, out_blk, shift, valid, nxt_blk)]
    wo_spec = pl.BlockSpec((oo_rows, w_out.shape[1]), lambda i, *_: (jnp.minimum(i, oo_steps - 1), 0))
    return pl.pallas_call(
        functools.partial(_weight_prep_kernel, shift_rows=shift_rows),
        grid_spec=pltpu.PrefetchScalarGridSpec(
            num_scalar_prefetch=5,
            grid=(steps,),
            in_specs=[pl.BlockSpec((blk, k), lambda i, ib, ob, sh, va, nb: (ib[i], 0)),
                      pl.BlockSpec((blk, k), lambda i, ib, ob, sh, va, nb: (nb[i], 0)),
                      wo_spec],
            out_specs=[pl.BlockSpec((k, blk), lambda i, ib, ob, sh, va, nb: (0, ob[i])), wo_spec],
        ),
        out_shape=[jax.ShapeDtypeStruct((k, n_cols), BF16), jax.ShapeDtypeStruct(w_out.shape, BF16)],
        compiler_params=_cparams("arbitrary"),
        name="weight_prep",
    )(*tables, w_in_t, w_in_t, w_out)


def _layer_params(w_in, rg_conv_w, rg_conv_b, w_rg_a, b_rg_a, w_rg_i, b_rg_i, rg_lambda, ssd_conv_w,
                  ssd_conv_b, ssd_dt_bias, ssd_a_log, ssd_d, ssd_norm_g, w_out, ln_g, ln_b):
    d_rg = rg_conv_w.shape[1]
    d_conv = ssd_conv_w.shape[1]
    d_ssd = ssd_norm_g.shape[0]
    heads = ssd_d.shape[0]
    sizes = (d_rg, d_rg, d_conv, d_ssd, heads)
    offs = [0]
    for s in sizes:
        offs.append(offs[-1] + s)
    d_xa = (w_in.shape[1] - offs[-1]) // 2
    offs += [offs[-1] + d_xa, offs[-1] + 2 * d_xa]
    blocks = lambda w: -(-w // PREP_BLOCK)
    assert d_rg == d_ssd == d_xa and blocks(d_conv) * PREP_BLOCK <= 2 * d_rg and heads <= LANES
    unit = blocks(d_rg)
    place = dict(xbc=0, z=2 * unit, rg_x=3 * unit, rg_g=4 * unit, xa_q=5 * unit, xa_g=6 * unit, dt=7 * unit)
    order = ("rg_x", "rg_g", "xbc", "z", "dt", "xa_q", "xa_g")
    wall, wo = _weight_prep(jnp.swapaxes(w_in, 0, 1), w_out, list(zip(offs[:-1], offs[1:])),
                            [place[k] for k in order])
    col = lambda key, width: _Cols(wall, width, place[key] * PREP_BLOCK // width)
    wx, wg, wxbc, wz = col("rg_x", d_rg), col("rg_g", d_rg), col("xbc", d_conv), col("z", d_ssd)
    wdt, wq, wxg = col("dt", LANES), col("xa_q", d_xa), col("xa_g", d_xa)
    row = lambda v: v.reshape(1, -1).astype(F32)
    pad_lanes = lambda v: jnp.pad(v, ((0, 0), (0, LANES - v.shape[1])))
    rg = dict(wx=wx, wg=wg, cw=rg_conv_w, cb=row(rg_conv_b),
              wgate=jnp.concatenate([w_rg_a, w_rg_i], axis=2).astype(BF16),
              ba=row(b_rg_a), bi=row(b_rg_i), lam=row(rg_lambda))
    head_of_channel = np.arange(d_ssd) // SSD_HEAD_DIM
    ssd = dict(wxbc=wxbc, wz=wz, wdt=wdt,
               cw=ssd_conv_w, cb=row(ssd_conv_b), dtb=pad_lanes(row(ssd_dt_bias)),
               alog=pad_lanes(row(ssd_a_log)), dexp=row(jnp.repeat(ssd_d, SSD_HEAD_DIM)), ng=row(ssd_norm_g),
               expand=jnp.asarray(np.arange(LANES)[:, None] == head_of_channel[None, :], dtype=BF16))
    xa = dict(wq=wq, wg=wxg)
    merge = dict(w=wo, g=row(ln_g), b=row(ln_b))
    return rg, ssd, xa, merge


PROMPT_RG_TILE = 256
PROMPT_XA_TILE = 512
SSD_TILE = 512
PREP_BLOCK = 512
MEMKV_BATCHES = 2
MERGE_TILE = 256
MERGE_SUB_ROWS = 256
SAMPLE_RG_SEQS = 32
SAMPLE_SSD_SEQS = 16


def kernel(x_prompt, x_sample, mem_prompt, state_rg_conv, state_rg_h, state_ssd_conv, state_ssd_h,
           cache_mem_k, cache_mem_v, w_in, rg_conv_w, rg_conv_b, w_rg_a, b_rg_a, w_rg_i, b_rg_i,
           rg_lambda, ssd_conv_w, ssd_conv_b, ssd_dt_bias, ssd_a_log, ssd_d, ssd_norm_g, w_mem_k,
           w_mem_v, w_out, ln_g, ln_b):
    assert w_in.shape[0] == DEPTH
    bp, lp, d = x_prompt.shape
    bs, ls, _ = x_sample.shape
    heads = ssd_d.shape[1]
    outs = {k: [] for k in ("rgc_p", "rgh_p", "sc_p", "sh_p", "mk_p", "mv_p", "rgc_s", "rgh_s", "sc_s", "sh_s")}
    yp, ys = x_prompt, x_sample
    pad_rows = lambda v, before, after: jnp.pad(v, ((0, 0), (before, after), (0, 0)))
    for l in range(DEPTH):
        rg, ssd, xa, merge = _layer_params(
            w_in[l], rg_conv_w[l], rg_conv_b[l], w_rg_a[l], b_rg_a[l], w_rg_i[l], b_rg_i[l], rg_lambda[l],
            ssd_conv_w[l], ssd_conv_b[l], ssd_dt_bias[l], ssd_a_log[l], ssd_d[l], ssd_norm_g[l],
            w_out[l], ln_g[l], ln_b[l])
        xs_pad = pad_rows(ys, 0, SAMPLE_PAD - ls)
        mk, mv, mkb, mvb = _memkv(mem_prompt, w_mem_k[l].astype(BF16), w_mem_v[l].astype(BF16))
        rg_o, rgc, rgh = _rg_prompt(yp, rg, PROMPT_RG_TILE)
        xa_o, xa_s = _xattn(yp, xs_pad, xa["wq"], xa["wg"], mkb, mvb, cache_mem_k[l], cache_mem_v[l],
                            PROMPT_XA_TILE)
        yp, sc, sh = _ssd_merge_prompt(yp, ssd, rg_o, xa_o, merge)
        outs["rgc_p"].append(rgc)
        outs["rgh_p"].append(rgh.reshape(bp, -1))
        outs["sc_p"].append(sc)
        outs["sh_p"].append(sh.reshape(bp, heads, SSD_HEAD_DIM, SSD_STATE))
        outs["mk_p"].append(mk)
        outs["mv_p"].append(mv)
        tail = SAMPLE_PAD - (CONV_W - 1)
        rg_o, rgc, rgh = _rg_sample(xs_pad, rg, pad_rows(state_rg_conv[l], tail, 0),
                                    state_rg_h[l][:, None, :], SAMPLE_RG_SEQS, ls)
        ssd_o, sc, sh = _ssd_sample(xs_pad, ssd, pad_rows(state_ssd_conv[l], tail, 0),
                                    state_ssd_h[l].reshape(bs, heads * SSD_HEAD_DIM, SSD_STATE),
                                    SAMPLE_SSD_SEQS, ls)
        flat = lambda v: v.reshape(bs * SAMPLE_PAD, v.shape[-1])
        ys_pad = _merge(flat(rg_o), flat(ssd_o), flat(xa_s), flat(xs_pad), merge["w"], merge["g"], merge["b"],
                        MERGE_TILE, "merge_sample").reshape(bs, SAMPLE_PAD, d)
        ys = ys_pad[:, :ls, :]
        outs["rgc_s"].append(rgc)
        outs["rgh_s"].append(rgh.reshape(bs, -1))
        outs["sc_s"].append(sc)
        outs["sh_s"].append(sh.reshape(bs, heads, SSD_HEAD_DIM, SSD_STATE))
    st = lambda k: jnp.stack(outs[k])
    return (yp, ys, st("rgc_p"), st("rgh_p"), st("sc_p"), st("sh_p"), st("mk_p"), st("mv_p"),
            st("rgc_s"), st("rgh_s"), st("sc_s"), st("sh_s"))
```

```python
import functools
from typing import NamedTuple

import jax
import jax.numpy as jnp
import numpy as np
from jax import lax
from jax.experimental import pallas as pl
from jax.experimental.pallas import tpu as pltpu

F32 = jnp.float32
BF16 = jnp.bfloat16

SUBLANES = 8
LANES = 128
MXU_WIDTH = 256
VMEM_LIMIT_BYTES = 56 * 1024 * 1024

RG_C = 8.0
CONV_W = 4
RG_BLOCKS = 8
SSD_HEAD_DIM = 64
SSD_GROUPS = 2
SSD_STATE = 128
SSD_CHUNK = 128
MEM_HEADS = 4
LN_EPS = 1e-5
RMS_EPS = 1e-5
DEPTH = 1
DEEPNORM_ALPHA = (2 * DEPTH) ** 0.25
LOG2_E = 1.4426950408889634
SAMPLE_PAD = SUBLANES


def _cparams(*sem):
    return pltpu.CompilerParams(dimension_semantics=sem, vmem_limit_bytes=VMEM_LIMIT_BYTES)


def _mm(a, b):
    return jnp.dot(a.astype(BF16), b.astype(BF16), preferred_element_type=F32)


def _mm_nt(a, b):
    return lax.dot_general(a.astype(BF16), b.astype(BF16), (((1,), (1,)), ((), ())),
                           preferred_element_type=F32)


def _mm_tn(a, b):
    return lax.dot_general(a.astype(BF16), b.astype(BF16), (((0,), (0,)), ((), ())),
                           preferred_element_type=F32)


def _exp_neg(x):
    return jnp.exp2(x * (-LOG2_E))


def _sigmoid(x):
    return 1.0 / (1.0 + _exp_neg(x))


def _silu(x):
    return x * _sigmoid(x)


def _softplus(x):
    return jnp.maximum(x, 0.0) + jnp.log(1.0 + jnp.exp(-jnp.abs(x)))


def _causal_conv_tiles(x3, p3, w, b):
    row = lax.broadcasted_iota(jnp.int32, x3.shape, 1)
    y = x3 * w[CONV_W - 1:CONV_W][None]
    for s in range(1, CONV_W):
        shifted = jnp.where(row >= s, pltpu.roll(x3, s, axis=1), pltpu.roll(p3, s, axis=1))
        y = y + shifted * w[CONV_W - 1 - s:CONV_W - s][None]
    return y + b[None]


def _scan_in_tiles(a3, b3):
    row = lax.broadcasted_iota(jnp.int32, a3.shape, 1)
    s = 1
    while s < SUBLANES:
        keep = row >= s
        a_sh = jnp.where(keep, pltpu.roll(a3, s, axis=1), 1.0)
        b_sh = jnp.where(keep, pltpu.roll(b3, s, axis=1), 0.0)
        b3 = a3 * b_sh + b3
        a3 = a3 * a_sh
        s *= 2
    return a3, b3


def _rg_gates(u, wgate, ba, bi, lam):
    pre = _mm(u, wgate)
    r = _sigmoid(pre[:, :LANES] + ba)
    i = _sigmoid(pre[:, LANES:] + bi)
    neg_log_a = r * (RG_C * _softplus(-lam))
    a = _exp_neg(neg_log_a)
    v = jnp.tanh(neg_log_a) * (1.0 + a * a)
    mult = jnp.where(v > 0.0, v * lax.rsqrt(v), 0.0)
    return a, mult * (i * u)


def _time_strided_conv(x3, tail3, w, b):
    slabs = x3.shape[0]
    row = lax.broadcasted_iota(jnp.int32, tail3.shape, 1)
    wrapped = jnp.where(row >= 1, pltpu.roll(x3[slabs - (CONV_W - 1):], 1, axis=1), pltpu.roll(tail3, 1, axis=1))
    y = x3 * w[CONV_W - 1:CONV_W][None]
    for s in range(1, CONV_W):
        shifted = jnp.concatenate([wrapped[CONV_W - 1 - s:], x3[:slabs - s]], axis=0)
        y = y + shifted * w[CONV_W - 1 - s:CONV_W - s][None]
    return y + b[None]


def _rg_prompt_kernel(x_ref, perm_ref, permt_ref, wx_ref, wg_ref, cw_ref, cb_ref, wgate_ref, ba_ref,
                      bi_ref, lam_ref, out_ref, conv_ref, hlast_ref, tail_scr, h_scr, proj_scr, outp_scr, *,
                      tiles_per_seq):
    s = pl.program_id(0)

    @pl.when(s == 0)
    def _():
        proj_scr[...] = jnp.zeros_like(proj_scr)

    @pl.when(jnp.logical_or(s == 0, s % tiles_per_seq == 1 % tiles_per_seq))
    def _():
        tail_scr[...] = jnp.zeros_like(tail_scr)
        h_scr[...] = jnp.zeros_like(h_scr)

    rows = x_ref.shape[0]
    slabs = rows // SUBLANES
    width = wx_ref.shape[1]
    per_group = MXU_WIDTH // LANES
    row = lax.broadcasted_iota(jnp.int32, (SUBLANES, LANES), 0)
    xb = jnp.dot(perm_ref[...], x_ref[...].astype(BF16), preferred_element_type=F32).astype(BF16)

    for cg in range(width // MXU_WIDTH):
        gs = slice(cg * MXU_WIDTH, (cg + 1) * MXU_WIDTH)
        for kk in range(per_group):
            k = cg * per_group + kk
            ks = slice(k * LANES, (k + 1) * LANES)
            x3 = proj_scr[:, ks].reshape(slabs, SUBLANES, LANES)
            tail3 = tail_scr[:, :, ks]
            last3 = x3[slabs - (CONV_W - 1):]
            tail_scr[:, :, ks] = last3
            for i in range(CONV_W - 1):
                conv_ref[0, i:i + 1, ks] = last3[i, SUBLANES - 1:, :]
            u3 = _time_strided_conv(x3, tail3, cw_ref[:, ks], cb_ref[:, ks])
            a, b = _rg_gates(u3.reshape(rows, LANES), wgate_ref[k], ba_ref[:, ks], bi_ref[:, ks], lam_ref[:, ks])
            a3 = a.reshape(slabs, SUBLANES, LANES)
            b3 = b.reshape(slabs, SUBLANES, LANES)
            h_loc, a_cum = [b3[0]], [a3[0]]
            for j in range(1, slabs):
                h_loc.append(a3[j] * h_loc[j - 1] + b3[j])
                a_cum.append(a3[j] * a_cum[j - 1])
            a_run, h_run = _scan_in_tiles(a_cum[-1][None], h_loc[-1][None])
            h_prev = h_scr[:, ks]
            h_end = h_run[0] + a_run[0] * h_prev
            h_in = jnp.where(row >= 1, pltpu.roll(h_end, 1, axis=0), h_prev)
            h_scr[:, ks] = h_end[SUBLANES - 1:]
            hlast_ref[0, :, ks] = h_end[SUBLANES - 1:]
            h = jnp.concatenate([h_loc[j] + a_cum[j] * h_in for j in range(slabs)], axis=0)
            gate = proj_scr[:, width + k * LANES:width + (k + 1) * LANES]
            outp_scr[:, ks] = (h * _silu(gate)).astype(BF16)
        proj_scr[:, gs] = jnp.dot(xb, wx_ref[:, gs], preferred_element_type=F32)
        proj_scr[:, width + cg * MXU_WIDTH:width + (cg + 1) * MXU_WIDTH] = jnp.dot(
            xb, wg_ref[:, gs], preferred_element_type=F32)

    out_ref[...] = jnp.dot(permt_ref[...], outp_scr[...], preferred_element_type=F32).astype(out_ref.dtype)


def _rg_sample_kernel(x_ref, wx_ref, wg_ref, cw_ref, cb_ref, wgate_ref, ba_ref, bi_ref, lam_ref,
                      cstate_ref, h0_ref, out_ref, conv_ref, hlast_ref, *, valid):
    seqs = x_ref.shape[0]
    rows = seqs * SUBLANES
    xb = x_ref[...].reshape(rows, x_ref.shape[2]).astype(BF16)
    rgx = jnp.dot(xb, wx_ref[...], preferred_element_type=F32)
    gate = jnp.dot(xb, wg_ref[...], preferred_element_type=F32)
    x3 = rgx.reshape(seqs, SUBLANES, rgx.shape[1])
    conv_ref[...] = x3[:, valid - (CONV_W - 1):valid, :]
    row = lax.broadcasted_iota(jnp.int32, (seqs, SUBLANES, LANES), 1)
    for k in range(RG_BLOCKS):
        ks = slice(k * LANES, (k + 1) * LANES)
        u3 = _causal_conv_tiles(x3[:, :, ks], cstate_ref[:, :, ks], cw_ref[:, ks], cb_ref[:, ks])
        u = u3.reshape(rows, LANES)
        a, b = _rg_gates(u, wgate_ref[k], ba_ref[:, ks], bi_ref[:, ks], lam_ref[:, ks])
        a3 = a.reshape(seqs, SUBLANES, LANES)
        b3 = b.reshape(seqs, SUBLANES, LANES)
        b3 = b3 + jnp.where(row == 0, a3 * h0_ref[:, :, ks], 0.0)
        _, h3 = _scan_in_tiles(a3, b3)
        hlast_ref[:, :, ks] = h3[:, valid - 1:valid, :]
        g3 = gate[:, ks].reshape(seqs, SUBLANES, LANES)
        out_ref[:, :, ks] = (h3 * _silu(g3)).astype(out_ref.dtype)


def _group_rmsnorm(y, gain):
    width = y.shape[-1] // SSD_GROUPS
    parts = []
    for g in range(SSD_GROUPS):
        yg = y[..., g * width:(g + 1) * width]
        ms = jnp.sum(yg * yg, axis=-1, keepdims=True) * (1.0 / width)
        parts.append(yg * lax.rsqrt(ms + RMS_EPS))
    return jnp.concatenate(parts, axis=-1) * gain


def _time_strided_cumsum(x):
    slabs = x.shape[0] // SUBLANES
    x3 = x.reshape(slabs, SUBLANES, x.shape[1])
    acc = [x3[0]]
    for j in range(1, slabs):
        acc.append(acc[j - 1] + x3[j])
    _, run = _scan_in_tiles(jnp.ones_like(acc[-1])[None], acc[-1][None])
    row = lax.broadcasted_iota(jnp.int32, run[0].shape, 0)
    before = jnp.where(row >= 1, pltpu.roll(run[0], 1, axis=0), 0.0)
    return jnp.concatenate([a + before for a in acc], axis=0)


def _ssd_merge_prompt_kernel(x_ref, perm_ref, permt_ref, wxbc_ref, wz_ref, wdt_ref, cw_ref, cb_ref, dtb_ref,
                             alog_ref, dexp_ref, ng_ref, rg_ref, xa_ref, xres_ref, wout_ref, lng_ref, lnb_ref,
                             y_ref, conv_ref, h_ref, tail_scr, proj_scr, xb_scr, h_scr, ssd_scr, xbc_scr, *,
                             tiles_per_seq, n_tiles):
    s = pl.program_id(0)

    @pl.when(s == 0)
    def _():
        proj_scr[...] = jnp.zeros_like(proj_scr)
        xb_scr[...] = jnp.zeros_like(xb_scr)
        ssd_scr[...] = jnp.zeros_like(ssd_scr)

    @pl.when(jnp.logical_or(s == 0, s % tiles_per_seq == 1 % tiles_per_seq))
    def _():
        tail_scr[...] = jnp.zeros_like(tail_scr)
        h_scr[...] = jnp.zeros_like(h_scr)

    q = SSD_CHUNK
    n_sub = x_ref.shape[0] // q
    slabs = q // SUBLANES
    d_ssd = wz_ref.shape[1]
    d_conv = wxbc_ref.shape[1]
    gn = SSD_GROUPS * SSD_STATE
    tail3 = tail_scr[...]
    for c in range(n_sub):
        x3 = proj_scr[c * q:(c + 1) * q, :d_conv].reshape(slabs, SUBLANES, d_conv)
        xbc_scr[c * q:(c + 1) * q, :] = _silu(_time_strided_conv(x3, tail3, cw_ref[...], cb_ref[...])).reshape(
            q, d_conv)
        tail3 = x3[slabs - (CONV_W - 1):]
    tail_scr[...] = tail3
    dtr_all = proj_scr[:, d_conv:]

    mix, row0 = None, 0
    for src in (rg_ref, ssd_scr, xa_ref):
        part = jnp.dot(src[...], wout_ref[row0:row0 + src.shape[1], :], preferred_element_type=F32)
        mix = part if mix is None else mix + part
        row0 += src.shape[1]
    res = DEEPNORM_ALPHA * xres_ref[...] + mix
    mu = jnp.mean(res, axis=-1, keepdims=True)
    cen = res - mu
    var = jnp.mean(cen * cen, axis=-1, keepdims=True)
    y_ref[...] = cen * lax.rsqrt(var + LN_EPS) * lng_ref[...] + lnb_ref[...]

    xin = x_ref[...].astype(BF16)
    xb_new = jnp.concatenate(
        [jnp.dot(perm_ref[...], xin[c * q:(c + 1) * q], preferred_element_type=F32).astype(BF16)
         for c in range(n_sub)], axis=0)
    z_all = jnp.dot(xb_scr[...], wz_ref[...], preferred_element_type=F32)
    xb_scr[...] = xb_new

    ii = lax.broadcasted_iota(jnp.int32, (q, q), 0)
    jj = lax.broadcasted_iota(jnp.int32, (q, q), 1)
    time_of = lambda r: (r % SUBLANES) * slabs + r // SUBLANES
    causal = time_of(ii) >= time_of(jj)
    lane = lax.broadcasted_iota(jnp.int32, (q, LANES), 1)
    lo = lane < SSD_HEAD_DIM
    srow = lax.broadcasted_iota(jnp.int32, (LANES, SSD_STATE), 0) < SSD_HEAD_DIM
    heads = d_ssd // SSD_HEAD_DIM
    pairs = heads // 2
    pairs_per_group = pairs // SSD_GROUPS

    projections = [(wxbc_ref, slice(g * MXU_WIDTH, (g + 1) * MXU_WIDTH), slice(g * MXU_WIDTH, (g + 1) * MXU_WIDTH))
                   for g in range(d_conv // MXU_WIDTH)]
    projections.append((wdt_ref, slice(None), slice(d_conv, None)))
    every = (n_sub * pairs) // len(projections)
    assert every >= 1
    issued = 0

    for c in range(n_sub):
        cs = slice(c * q, (c + 1) * q)
        sx = xbc_scr[cs, :d_ssd]
        bm = xbc_scr[cs, d_ssd:d_ssd + gn]
        cm = xbc_scr[cs, d_ssd + gn:]
        dt = _softplus(dtr_all[cs] + dtb_ref[...])
        da = dt * (-jnp.exp(alog_ref[...]))
        acum = _time_strided_cumsum(da) * LOG2_E
        alast = acum[q - 1:q, :]
        wgt = dt * jnp.exp2(alast - acum)
        tot = jnp.exp2(alast)
        acum_dt_t = (acum - jnp.log2(dt)).T
        cb = [_mm_nt(cm[:, g * SSD_STATE:(g + 1) * SSD_STATE], bm[:, g * SSD_STATE:(g + 1) * SSD_STATE])
              for g in range(SSD_GROUPS)]
        y_parts = []
        for g in range(SSD_GROUPS):
            ns = slice(g * SSD_STATE, (g + 1) * SSD_STATE)
            rows_g = slice(g * pairs_per_group * LANES, (g + 1) * pairs_per_group * LANES)
            h_in = h_scr[rows_g, :]
            y_off_g = _mm_nt(cm[:, ns], h_in)
            xws, decays = [], []
            for pl_ in range(pairs_per_group):
                pq = g * pairs_per_group + pl_
                ps = slice(pq * LANES, (pq + 1) * LANES)
                xq = sx[:, ps]
                ms, es, ws, ts = [], [], [], []
                for h in (2 * pq, 2 * pq + 1):
                    acol = jnp.broadcast_to(acum[:, h:h + 1], (q, q))
                    arow = jnp.broadcast_to(acum_dt_t[h:h + 1, :], (q, q))
                    decay_dt = jnp.exp2(jnp.where(causal, acol - arow, -jnp.inf))
                    ms.append((cb[g] * decay_dt).astype(BF16))
                    es.append(jnp.exp2(jnp.broadcast_to(acum[:, h:h + 1], (q, LANES))))
                    ws.append(jnp.broadcast_to(wgt[:, h:h + 1], (q, LANES)))
                    ts.append(jnp.broadcast_to(tot[:, h:h + 1], (LANES, SSD_STATE)))
                lhs = jnp.concatenate(ms, axis=1)
                rhs = jnp.concatenate([jnp.where(lo, xq, 0.0), jnp.where(lo, 0.0, xq)], axis=0)
                y_diag = _mm(lhs, rhs)
                y_off = y_off_g[:, pl_ * LANES:(pl_ + 1) * LANES] * jnp.where(lo, es[0], es[1])
                xws.append(xq * jnp.where(lo, ws[0], ws[1]))
                decays.append(jnp.where(srow, ts[0], ts[1]))
                y_parts.append(y_diag + y_off + dexp_ref[:, ps] * xq)
                if (c * pairs + pq + 1) % every == 0 and issued < len(projections):
                    w_ref, src, dst = projections[issued]
                    proj_scr[:, dst] = jnp.dot(xb_new, w_ref[:, src], preferred_element_type=F32)
                    issued += 1
            h_scr[rows_g, :] = (h_in * jnp.concatenate(decays, axis=0)
                                + _mm_tn(jnp.concatenate(xws, axis=1), bm[:, ns]))
        y = jnp.concatenate(y_parts, axis=1) * _silu(z_all[cs])
        y = _group_rmsnorm(y, ng_ref[...]).astype(BF16)
        ssd_scr[cs, :] = jnp.dot(permt_ref[...], y, preferred_element_type=F32).astype(BF16)
    assert issued == len(projections)

    @pl.when(jnp.logical_and(s % tiles_per_seq == 0, jnp.logical_and(s > 0, s <= n_tiles)))
    def _():
        h_ref[0] = h_scr[...]
        for i in range(CONV_W - 1):
            conv_ref[0, i:i + 1, :] = tail_scr[i, SUBLANES - 1:, :]


def _ssd_sample_kernel(x_ref, wxbc_ref, wz_ref, wdt_ref, cw_ref, cb_ref, dtb_ref, alog_ref, dexp_ref,
                       ng_ref, expand_ref, cstate_ref, h0_ref, out_ref, conv_ref, h_ref,
                       c_scr, b_scr, xw_scr, tot_scr, yoff_scr, *, valid):
    seqs = x_ref.shape[0]
    rows = seqs * SUBLANES
    d_ssd = wz_ref.shape[1]
    gn = SSD_GROUPS * SSD_STATE
    heads = d_ssd // SSD_HEAD_DIM
    hg = heads // SSD_GROUPS
    xb = x_ref[...].reshape(rows, x_ref.shape[2]).astype(BF16)
    xbc_raw = jnp.dot(xb, wxbc_ref[...], preferred_element_type=F32)
    z = jnp.dot(xb, wz_ref[...], preferred_element_type=F32)
    dtr = jnp.dot(xb, wdt_ref[...], preferred_element_type=F32)

    x3 = xbc_raw.reshape(seqs, SUBLANES, xbc_raw.shape[1])
    conv_ref[...] = x3[:, valid - (CONV_W - 1):valid, :]
    xbc3 = _silu(_causal_conv_tiles(x3, cstate_ref[...], cw_ref[...], cb_ref[...]))
    sx3 = xbc3[:, :, :d_ssd]
    b3 = xbc3[:, :, d_ssd:d_ssd + gn]
    c3 = xbc3[:, :, d_ssd + gn:]

    row = lax.broadcasted_iota(jnp.int32, (seqs, SUBLANES, LANES), 1)
    lane = lax.broadcasted_iota(jnp.int32, (seqs, SUBLANES, LANES), 2)
    dt3 = jnp.where(row < valid, _softplus(dtr + dtb_ref[...]).reshape(seqs, SUBLANES, LANES), 0.0)
    da3 = dt3 * (-jnp.exp(alog_ref[...]))[None]
    ones = jnp.ones_like(da3)
    _, acum3 = _scan_in_tiles(ones, da3)
    alast = acum3[:, SUBLANES - 1:, :]
    wgt3 = dt3 * jnp.exp(alast - acum3)
    tot_scr[...] = jnp.exp(alast)
    e3 = jnp.exp(acum3)

    coefs = []
    for u in range(valid):
        prod = c3 * b3[:, u:u + 1, :]
        cbu = [jnp.sum(prod[:, :, g * SSD_STATE:(g + 1) * SSD_STATE], axis=-1, keepdims=True)
               for g in range(SSD_GROUPS)]
        cb_heads = jnp.where(lane < hg, cbu[0], cbu[1])
        coefs.append(jnp.where(row >= u, cb_heads * jnp.exp(acum3 - acum3[:, u:u + 1, :]) * dt3[:, u:u + 1, :],
                               0.0))

    per_head = coefs + [wgt3, e3]
    stacked = jnp.concatenate([v.reshape(rows, LANES) for v in per_head], axis=0)
    expanded = None
    rest = stacked
    for _ in range(3):
        piece = rest.astype(BF16)
        rest = rest - piece.astype(F32)
        part = jnp.dot(piece, expand_ref[...], preferred_element_type=F32)
        expanded = part if expanded is None else expanded + part
    expanded = [expanded[i * rows:(i + 1) * rows].reshape(seqs, SUBLANES, d_ssd) for i in range(len(per_head))]
    y_diag = expanded[0] * sx3[:, 0:1, :]
    for u in range(1, valid):
        y_diag = y_diag + expanded[u] * sx3[:, u:u + 1, :]
    wgt_x, e_x = expanded[valid], expanded[valid + 1]

    c_scr[...] = c3
    b_scr[...] = b3
    xw_scr[...] = sx3 * wgt_x
    srow = lax.broadcasted_iota(jnp.int32, (2 * SSD_HEAD_DIM, SSD_STATE), 0) < SSD_HEAD_DIM

    def per_seq(s, carry):
        cs = c_scr[s]
        bs = b_scr[s]
        xws = xw_scr[s]
        tots = tot_scr[s]
        for g in range(SSD_GROUPS):
            gs = slice(g * SSD_STATE, (g + 1) * SSD_STATE)
            width = hg * SSD_HEAD_DIM
            cols = slice(g * width, (g + 1) * width)
            hin = h0_ref[s, cols, :]
            yoff_scr[s, :, cols] = _mm_nt(cs[:, gs], hin)
            upd = _mm_tn(xws[:, cols], bs[:, gs])
            for pq in range(hg // 2):
                h = g * hg + 2 * pq
                rs = slice(pq * LANES, (pq + 1) * LANES)
                t0 = jnp.broadcast_to(tots[0:1, h:h + 1], (LANES, SSD_STATE))
                t1 = jnp.broadcast_to(tots[0:1, h + 1:h + 2], (LANES, SSD_STATE))
                h_ref[s, g * width + pq * LANES:g * width + (pq + 1) * LANES, :] = (
                    hin[rs] * jnp.where(srow, t0, t1) + upd[rs])
        return carry

    lax.fori_loop(0, seqs, per_seq, 0)

    y = y_diag + yoff_scr[...] * e_x + dexp_ref[...][None] * sx3
    y = y * _silu(z.reshape(seqs, SUBLANES, d_ssd))
    out_ref[...] = _group_rmsnorm(y, ng_ref[...][None]).astype(out_ref.dtype)


def _memkv_kernel(mem_ref, wk_ref, wv_ref, k_ref, v_ref, kb_ref, vb_ref):
    nb, m, d = mem_ref.shape
    mb = mem_ref[...].reshape(nb * m, d).astype(BF16)
    for w_ref, o_ref, ob_ref in ((wk_ref, k_ref, kb_ref), (wv_ref, v_ref, vb_ref)):
        proj = jnp.dot(mb, w_ref[...], preferred_element_type=F32)
        for i in range(nb):
            o_ref[i] = proj[i * m:(i + 1) * m].reshape(o_ref.shape[1:])
        ob_ref[...] = proj.astype(BF16).reshape(ob_ref.shape)


def _softmax_terms(scores):
    p = jnp.exp(scores - jnp.max(scores, axis=-1, keepdims=True))
    return p, jnp.sum(p, axis=-1, keepdims=True)


def _prompt_attention_head(h, q_ref, rows, k_ref, v_ref, out_ref):
    c = k_ref.shape[2]
    d_head = c // MEM_HEADS
    hs = slice(h * d_head, (h + 1) * d_head)

    def scores():
        return _softmax_terms(_mm_nt(q_ref[:rows, hs], k_ref[0, :, hs]) * (d_head ** -0.5))

    def output(p, l):
        o = _mm(p, v_ref[0, :, hs]) * (1.0 / l)
        gate = q_ref[:rows, c + h * d_head:c + (h + 1) * d_head]
        out_ref[:, hs] = (o * _silu(gate)).astype(out_ref.dtype)

    return scores, output


def _sample_attention_seq(s, q_ref, rows, k_ref, v_ref, out_ref):
    _, n_mem, heads, d_head = k_ref.shape
    c = heads * d_head
    rs = slice(rows + s * SUBLANES, rows + (s + 1) * SUBLANES)

    def scores():
        shape = (heads * SUBLANES, n_mem * heads)
        same_head = (lax.broadcasted_iota(jnp.int32, shape, 0) // SUBLANES
                     == lax.broadcasted_iota(jnp.int32, shape, 1) % heads)
        qh = jnp.concatenate([q_ref[rs, h * d_head:(h + 1) * d_head] for h in range(heads)], axis=0)
        sc = _mm_nt(qh, k_ref[s].reshape(n_mem * heads, d_head)) * (d_head ** -0.5)
        return _softmax_terms(jnp.where(same_head, sc, -jnp.inf))

    def output(p, l):
        o = _mm(p, v_ref[s].reshape(n_mem * heads, d_head)) * (1.0 / l)
        o = jnp.concatenate([o[h * SUBLANES:(h + 1) * SUBLANES] for h in range(heads)], axis=1)
        out_ref[s] = (o * _silu(q_ref[rs, c:])).astype(out_ref.dtype)

    return scores, output


def _xattn_kernel(x_ref, xs_ref, wq_ref, wg_ref, k_ref, v_ref, ks_ref, vs_ref, out_ref, outs_ref):
    rows = x_ref.shape[0]
    seqs, _, d = xs_ref.shape
    xb = jnp.concatenate([x_ref[...], xs_ref[...].reshape(seqs * SUBLANES, d)], axis=0).astype(BF16)
    qg = jnp.concatenate([jnp.dot(xb, wq_ref[...], preferred_element_type=F32),
                          jnp.dot(xb, wg_ref[...], preferred_element_type=F32)], axis=1)
    items = [_prompt_attention_head(h, qg, rows, k_ref, v_ref, out_ref) for h in range(MEM_HEADS)]
    items += [_sample_attention_seq(i, qg, rows, ks_ref, vs_ref, outs_ref) for i in range(seqs)]
    for scores, output in items:
        output(*scores())


def _merge_kernel(rg_ref, ssd_ref, xa_ref, x_ref, w_ref, g_ref, b_ref, y_ref, *, sub_rows):
    d = rg_ref.shape[1]
    for r0 in range(0, x_ref.shape[0], sub_rows):
        rs = slice(r0, r0 + sub_rows)
        mix = (jnp.dot(rg_ref[rs, :].astype(BF16), w_ref[0:d, :], preferred_element_type=F32)
               + jnp.dot(ssd_ref[rs, :].astype(BF16), w_ref[d:2 * d, :], preferred_element_type=F32)
               + jnp.dot(xa_ref[rs, :].astype(BF16), w_ref[2 * d:3 * d, :], preferred_element_type=F32))
        res = DEEPNORM_ALPHA * x_ref[rs, :] + mix
        mu = jnp.mean(res, axis=-1, keepdims=True)
        cen = res - mu
        var = jnp.mean(cen * cen, axis=-1, keepdims=True)
        y_ref[rs, :] = cen * lax.rsqrt(var + LN_EPS) * g_ref[...] + b_ref[...]


def _full(shape):
    return pl.BlockSpec(shape, lambda *_: (0,) * len(shape))


def _time_stride_perm(rows):
    p = np.arange(rows)
    t = (p % SUBLANES) * (rows // SUBLANES) + p // SUBLANES
    perm = t[:, None] == np.arange(rows)[None, :]
    return jnp.asarray(perm, dtype=BF16), jnp.asarray(perm.T, dtype=BF16)


def _rg_prompt(x, p, tile):
    b, l, d = x.shape
    c = p["wx"].shape[1]
    tiles_per_seq = l // tile
    n_tiles = b * tiles_per_seq
    perm, perm_t = _time_stride_perm(tile)
    done = lambda s: jnp.maximum(s - 1, 0)
    out, conv, hlast = pl.pallas_call(
        functools.partial(_rg_prompt_kernel, tiles_per_seq=tiles_per_seq),
        grid=(n_tiles + 1,),
        in_specs=[pl.BlockSpec((tile, d), lambda s: (jnp.minimum(s, n_tiles - 1), 0)),
                  _full(perm.shape), _full(perm.shape), p["wx"].spec, p["wg"].spec,
                  _full(p["cw"].shape), _full(p["cb"].shape), _full(p["wgate"].shape), _full(p["ba"].shape),
                  _full(p["bi"].shape), _full(p["lam"].shape)],
        out_specs=[pl.BlockSpec((tile, c), lambda s: (done(s), 0)),
                   pl.BlockSpec((1, CONV_W - 1, c), lambda s: (done(s) // tiles_per_seq, 0, 0)),
                   pl.BlockSpec((1, 1, c), lambda s: (done(s) // tiles_per_seq, 0, 0))],
        out_shape=[jax.ShapeDtypeStruct((b * l, c), BF16),
                   jax.ShapeDtypeStruct((b, CONV_W - 1, c), F32),
                   jax.ShapeDtypeStruct((b, 1, c), F32)],
        scratch_shapes=[pltpu.VMEM((CONV_W - 1, SUBLANES, c), F32), pltpu.VMEM((1, c), F32),
                        pltpu.VMEM((tile, 2 * c), F32), pltpu.VMEM((tile, c), BF16)],
        compiler_params=_cparams("arbitrary"),
        name="rg_prompt",
    )(x.reshape(b * l, d), perm, perm_t, p["wx"].array, p["wg"].array, p["cw"], p["cb"], p["wgate"], p["ba"],
      p["bi"], p["lam"])
    return out.reshape(b, l, c), conv, hlast


def _rg_sample(xpad, p, cstate, h0, seqs, valid):
    n, _, d = xpad.shape
    c = p["wx"].shape[1]
    blk = lambda w: pl.BlockSpec((seqs, w[0], w[1]), lambda i: (i, 0, 0))
    return pl.pallas_call(
        functools.partial(_rg_sample_kernel, valid=valid),
        grid=(n // seqs,),
        in_specs=[blk((SUBLANES, d)), p["wx"].spec, p["wg"].spec, _full(p["cw"].shape),
                  _full(p["cb"].shape), _full(p["wgate"].shape), _full(p["ba"].shape),
                  _full(p["bi"].shape), _full(p["lam"].shape), blk((SUBLANES, c)), blk((1, c))],
        out_specs=[blk((SUBLANES, c)), blk((CONV_W - 1, c)), blk((1, c))],
        out_shape=[jax.ShapeDtypeStruct((n, SUBLANES, c), F32),
                   jax.ShapeDtypeStruct((n, CONV_W - 1, c), F32),
                   jax.ShapeDtypeStruct((n, 1, c), F32)],
        compiler_params=_cparams("parallel"),
        name="rg_sample",
    )(xpad, p["wx"].array, p["wg"].array, p["cw"], p["cb"], p["wgate"], p["ba"], p["bi"], p["lam"], cstate, h0)


_SSD_WEIGHTS = ("wxbc", "wz", "wdt")
_SSD_SMALL = ("cw", "cb", "dtb", "alog", "dexp", "ng")


def _ssd_param_specs(p):
    return [p[k].spec for k in _SSD_WEIGHTS] + [_full(p[k].shape) for k in _SSD_SMALL]


def _ssd_param_args(p):
    return [p[k].array for k in _SSD_WEIGHTS] + [p[k] for k in _SSD_SMALL]


def _ssd_merge_prompt(x, p, rg_out, xa_out, mp):
    b, l, d = x.shape
    cc = p["wxbc"].shape[1]
    c = p["wz"].shape[1]
    tile = min(SSD_TILE, l)
    tiles_per_seq = l // tile
    n_tiles = b * tiles_per_seq
    perm, perm_t = _time_stride_perm(SSD_CHUNK)
    flat = lambda v: v.reshape(b * l, v.shape[-1])
    clamp = lambda i: jnp.clip(i, 0, n_tiles - 1)
    rows = lambda w, back: pl.BlockSpec((tile, w), lambda s: (clamp(s - back), 0))
    state = lambda shape: pl.BlockSpec((1,) + shape, lambda s: (clamp(s - 1) // tiles_per_seq, 0, 0))
    y, conv, hstate = pl.pallas_call(
        functools.partial(_ssd_merge_prompt_kernel, tiles_per_seq=tiles_per_seq, n_tiles=n_tiles),
        grid=(n_tiles + 2,),
        in_specs=[rows(d, 0), _full(perm.shape), _full(perm.shape)] + _ssd_param_specs(p)
        + [rows(rg_out.shape[-1], 2), rows(xa_out.shape[-1], 2), rows(d, 2), _full(mp["w"].shape),
           _full(mp["g"].shape), _full(mp["b"].shape)],
        out_specs=[rows(d, 2), state((CONV_W - 1, cc)), state((c, SSD_STATE))],
        out_shape=[jax.ShapeDtypeStruct((b * l, d), F32),
                   jax.ShapeDtypeStruct((b, CONV_W - 1, cc), F32),
                   jax.ShapeDtypeStruct((b, c, SSD_STATE), F32)],
        scratch_shapes=[pltpu.VMEM((CONV_W - 1, SUBLANES, cc), F32),
                        pltpu.VMEM((tile, cc + LANES), F32), pltpu.VMEM((tile, d), BF16),
                        pltpu.VMEM((c, SSD_STATE), F32), pltpu.VMEM((tile, c), BF16),
                        pltpu.VMEM((tile, cc), F32)],
        compiler_params=_cparams("arbitrary"),
        name="ssd_merge_prompt",
    )(flat(x), perm, perm_t, *_ssd_param_args(p), flat(rg_out), flat(xa_out), flat(x), mp["w"], mp["g"], mp["b"])
    return y.reshape(b, l, d), conv, hstate


def _ssd_sample(xpad, p, cstate, h0, seqs, valid):
    n, _, d = xpad.shape
    cc = p["wxbc"].shape[1]
    c = p["wz"].shape[1]
    gn = SSD_GROUPS * SSD_STATE
    blk = lambda w: pl.BlockSpec((seqs, w[0], w[1]), lambda i: (i, 0, 0))
    return pl.pallas_call(
        functools.partial(_ssd_sample_kernel, valid=valid),
        grid=(n // seqs,),
        in_specs=[blk((SUBLANES, d))] + _ssd_param_specs(p)
        + [_full(p["expand"].shape), blk((SUBLANES, cc)), blk((c, SSD_STATE))],
        out_specs=[blk((SUBLANES, c)), blk((CONV_W - 1, cc)), blk((c, SSD_STATE))],
        out_shape=[jax.ShapeDtypeStruct((n, SUBLANES, c), F32),
                   jax.ShapeDtypeStruct((n, CONV_W - 1, cc), F32),
                   jax.ShapeDtypeStruct((n, c, SSD_STATE), F32)],
        scratch_shapes=[pltpu.VMEM((seqs, SUBLANES, gn), F32), pltpu.VMEM((seqs, SUBLANES, gn), F32),
                        pltpu.VMEM((seqs, SUBLANES, c), F32), pltpu.VMEM((seqs, 1, LANES), F32),
                        pltpu.VMEM((seqs, SUBLANES, c), F32)],
        compiler_params=_cparams("parallel"),
        name="ssd_sample",
    )(xpad, *_ssd_param_args(p), p["expand"], cstate, h0)


def _memkv(mem, wk, wv):
    b, m, d = mem.shape
    c = wk.shape[1]
    nb = MEMKV_BATCHES if b % MEMKV_BATCHES == 0 else 1
    spec = pl.BlockSpec((nb, m, c), lambda i: (i, 0, 0))
    spec4 = pl.BlockSpec((nb, m, MEM_HEADS, c // MEM_HEADS), lambda i: (i, 0, 0, 0))
    return pl.pallas_call(
        _memkv_kernel,
        grid=(b // nb,),
        in_specs=[pl.BlockSpec((nb, m, d), lambda i: (i, 0, 0)), _full(wk.shape), _full(wv.shape)],
        out_specs=[spec4, spec4, spec, spec],
        out_shape=[jax.ShapeDtypeStruct((b, m, MEM_HEADS, c // MEM_HEADS), F32)] * 2
        + [jax.ShapeDtypeStruct((b, m, c), BF16)] * 2,
        compiler_params=_cparams("parallel"),
        name="mem_kv",
    )(mem, wk, wv)


def _xattn(x, xs_pad, wq, wg, k, v, ks, vs, tile):
    b, l, d = x.shape
    n = xs_pad.shape[0]
    c = wq.shape[1]
    m = k.shape[1]
    tile = min(tile, l)
    tiles_per_seq = l // tile
    n_tiles = b * tiles_per_seq
    seqs = n // n_tiles
    assert seqs * n_tiles == n
    kv_spec = pl.BlockSpec((1, m, c), lambda s: (s // tiles_per_seq, 0, 0))
    skv_spec = pl.BlockSpec((seqs,) + ks.shape[1:], lambda s: (s, 0, 0, 0))
    out, outs = pl.pallas_call(
        _xattn_kernel,
        grid=(n_tiles,),
        in_specs=[pl.BlockSpec((tile, d), lambda s: (s, 0)),
                  pl.BlockSpec((seqs, SUBLANES, d), lambda s: (s, 0, 0)),
                  wq.spec, wg.spec, kv_spec, kv_spec, skv_spec, skv_spec],
        out_specs=[pl.BlockSpec((tile, c), lambda s: (s, 0)),
                   pl.BlockSpec((seqs, SUBLANES, c), lambda s: (s, 0, 0))],
        out_shape=[jax.ShapeDtypeStruct((b * l, c), BF16), jax.ShapeDtypeStruct((n, SUBLANES, c), F32)],
        compiler_params=_cparams("parallel"),
        name="xattn",
    )(x.reshape(b * l, d), xs_pad, wq.array, wg.array, k, v, ks, vs)
    return out.reshape(b, l, c), outs


def _merge(rg, ssd, xa, x, w_out, ln_g, ln_b, tile, name):
    n, d = x.shape
    c = rg.shape[1]
    tile = min(tile, n)
    row = lambda w: pl.BlockSpec((tile, w), lambda i: (i, 0))
    return pl.pallas_call(
        functools.partial(_merge_kernel, sub_rows=min(MERGE_SUB_ROWS, tile)),
        grid=(n // tile,),
        in_specs=[row(c), row(c), row(c), row(d), _full(w_out.shape), _full(ln_g.shape), _full(ln_b.shape)],
        out_specs=row(d),
        out_shape=jax.ShapeDtypeStruct((n, d), F32),
        compiler_params=_cparams("parallel"),
        name=name,
    )(rg, ssd, xa, x, w_out, ln_g, ln_b)


class _Cols(NamedTuple):
    array: jax.Array
    width: int
    index: int

    @property
    def shape(self):
        return (self.array.shape[0], self.width)

    @property
    def spec(self):
        return pl.BlockSpec(self.shape, lambda *_, i=self.index: (0, i))


def _weight_prep_kernel(in_blk_ref, out_blk_ref, shift_ref, valid_ref, nxt_blk_ref, a_ref, b_ref, wo_ref,
                        o_ref, oo_ref, *, shift_rows):
    del in_blk_ref, out_blk_ref, nxt_blk_ref
    i = pl.program_id(0)
    rows = a_ref.shape[0]
    keep = lax.broadcasted_iota(jnp.int32, a_ref.shape, 0) < valid_ref[i]

    @pl.when(shift_ref[i] == 0)
    def _():
        o_ref[...] = jnp.where(keep, a_ref[...], 0.0).T.astype(BF16)

    @pl.when(shift_ref[i] != 0)
    def _():
        blk = jnp.concatenate([a_ref[shift_rows:, :], b_ref[:shift_rows, :]], axis=0)
        o_ref[...] = jnp.where(keep, blk, 0.0).T.astype(BF16)

    oo_ref[...] = wo_ref[...].astype(BF16)


def _weight_prep(w_in_t, w_out, bounds, placement):
    n_rows, k = w_in_t.shape
    blk = PREP_BLOCK
    in_blk, out_blk, shift, valid = [], [], [], []
    shifts = {lo % blk for lo, _ in bounds} - {0}
    assert len(shifts) <= 1
    shift_rows = shifts.pop() if shifts else SUBLANES
    assert shift_rows % SUBLANES == 0
    for (lo, hi), place in zip(bounds, placement):
        for j in range(-(-(hi - lo) // blk)):
            in_blk.append((lo + j * blk) // blk)
            out_blk.append(place + j)
            shift.append(lo % blk)
            valid.append(min(blk, hi - lo - j * blk))
    n_cols = (max(out_blk) + 1) * blk
    for gap in sorted(set(range(n_cols // blk)) - set(out_blk)):
        in_blk.append(0)
        out_blk.append(gap)
        shift.append(0)
        valid.append(0)
    steps = len(in_blk)
    last_in = -(-n_rows // blk) - 1
    oo_rows = -(-(-(-w_out.shape[0] // steps)) // SUBLANES) * SUBLANES
    oo_steps = -(-w_out.shape[0] // oo_rows)
    assert oo_steps <= steps
    nxt_blk = [min(a + 1, last_in) if sh else 0 for a, sh in zip(in_blk, shift)]
    tables = [jnp.asarray(np.asarray(t, np.int32)) for t in (in_blk, out_blk, shift, valid, nxt_blk)]
    wo_spec = pl.BlockSpec((oo_rows, w_out.shape[1]), lambda i, *_: (jnp.minimum(i, oo_steps - 1), 0))
    return pl.pallas_call(
        functools.partial(_weight_prep_kernel, shift_rows=shift_rows),
        grid_spec=pltpu.PrefetchScalarGridSpec(
            num_scalar_prefetch=5,
            grid=(steps,),
            in_specs=[pl.BlockSpec((blk, k), lambda i, ib, ob, sh, va, nb: (ib[i], 0)),
                      pl.BlockSpec((blk, k), lambda i, ib, ob, sh, va, nb: (nb[i], 0)),
                      wo_spec],
            out_specs=[pl.BlockSpec((k, blk), lambda i, ib, ob, sh, va, nb: (0, ob[i])), wo_spec],
        ),
        out_shape=[jax.ShapeDtypeStruct((k, n_cols), BF16), jax.ShapeDtypeStruct(w_out.shape, BF16)],
        compiler_params=_cparams("arbitrary"),
        name="weight_prep",
    )(*tables, w_in_t, w_in_t, w_out)


def _layer_params(w_in, rg_conv_w, rg_conv_b, w_rg_a, b_rg_a, w_rg_i, b_rg_i, rg_lambda, ssd_conv_w,
                  ssd_conv_b, ssd_dt_bias, ssd_a_log, ssd_d, ssd_norm_g, w_out, ln_g, ln_b):
    d_rg = rg_conv_w.shape[1]
    d_conv = ssd_conv_w.shape[1]
    d_ssd = ssd_norm_g.shape[0]
    heads = ssd_d.shape[0]
    sizes = (d_rg, d_rg, d_conv, d_ssd, heads)
    offs = [0]
    for s in sizes:
        offs.append(offs[-1] + s)
    d_xa = (w_in.shape[1] - offs[-1]) // 2
    offs += [offs[-1] + d_xa, offs[-1] + 2 * d_xa]
    blocks = lambda w: -(-w // PREP_BLOCK)
    assert d_rg == d_ssd == d_xa and blocks(d_conv) * PREP_BLOCK <= 2 * d_rg and heads <= LANES
    unit = blocks(d_rg)
    place = dict(xbc=0, z=2 * unit, rg_x=3 * unit, rg_g=4 * unit, xa_q=5 * unit, xa_g=6 * unit, dt=7 * unit)
    order = ("rg_x", "rg_g", "xbc", "z", "dt", "xa_q", "xa_g")
    wall, wo = _weight_prep(jnp.swapaxes(w_in, 0, 1), w_out, list(zip(offs[:-1], offs[1:])),
                            [place[k] for k in order])
    col = lambda key, width: _Cols(wall, width, place[key] * PREP_BLOCK // width)
    wx, wg, wxbc, wz = col("rg_x", d_rg), col("rg_g", d_rg), col("xbc", d_conv), col("z", d_ssd)
    wdt, wq, wxg = col("dt", LANES), col("xa_q", d_xa), col("xa_g", d_xa)
    row = lambda v: v.reshape(1, -1).astype(F32)
    pad_lanes = lambda v: jnp.pad(v, ((0, 0), (0, LANES - v.shape[1])))
    rg = dict(wx=wx, wg=wg, cw=rg_conv_w, cb=row(rg_conv_b),
              wgate=jnp.concatenate([w_rg_a, w_rg_i], axis=2).astype(BF16),
              ba=row(b_rg_a), bi=row(b_rg_i), lam=row(rg_lambda))
    head_of_channel = np.arange(d_ssd) // SSD_HEAD_DIM
    ssd = dict(wxbc=wxbc, wz=wz, wdt=wdt,
               cw=ssd_conv_w, cb=row(ssd_conv_b), dtb=pad_lanes(row(ssd_dt_bias)),
               alog=pad_lanes(row(ssd_a_log)), dexp=row(jnp.repeat(ssd_d, SSD_HEAD_DIM)), ng=row(ssd_norm_g),
               expand=jnp.asarray(np.arange(LANES)[:, None] == head_of_channel[None, :], dtype=BF16))
    xa = dict(wq=wq, wg=wxg)
    merge = dict(w=wo, g=row(ln_g), b=row(ln_b))
    return rg, ssd, xa, merge


PROMPT_RG_TILE = 256
PROMPT_XA_TILE = 512
SSD_TILE = 512
PREP_BLOCK = 512
MEMKV_BATCHES = 2
MERGE_TILE = 256
MERGE_SUB_ROWS = 256
SAMPLE_RG_SEQS = 32
SAMPLE_SSD_SEQS = 16


def kernel(x_prompt, x_sample, mem_prompt, state_rg_conv, state_rg_h, state_ssd_conv, state_ssd_h,
           cache_mem_k, cache_mem_v, w_in, rg_conv_w, rg_conv_b, w_rg_a, b_rg_a, w_rg_i, b_rg_i,
           rg_lambda, ssd_conv_w, ssd_conv_b, ssd_dt_bias, ssd_a_log, ssd_d, ssd_norm_g, w_mem_k,
           w_mem_v, w_out, ln_g, ln_b):
    assert w_in.shape[0] == DEPTH
    bp, lp, d = x_prompt.shape
    bs, ls, _ = x_sample.shape
    heads = ssd_d.shape[1]
    outs = {k: [] for k in ("rgc_p", "rgh_p", "sc_p", "sh_p", "mk_p", "mv_p", "rgc_s", "rgh_s", "sc_s", "sh_s")}
    yp, ys = x_prompt, x_sample
    pad_rows = lambda v, before, after: jnp.pad(v, ((0, 0), (before, after), (0, 0)))
    for l in range(DEPTH):
        rg, ssd, xa, merge = _layer_params(
            w_in[l], rg_conv_w[l], rg_conv_b[l], w_rg_a[l], b_rg_a[l], w_rg_i[l], b_rg_i[l], rg_lambda[l],
            ssd_conv_w[l], ssd_conv_b[l], ssd_dt_bias[l], ssd_a_log[l], ssd_d[l], ssd_norm_g[l],
            w_out[l], ln_g[l], ln_b[l])
        xs_pad = pad_rows(ys, 0, SAMPLE_PAD - ls)
        mk, mv, mkb, mvb = _memkv(mem_prompt, w_mem_k[l].astype(BF16), w_mem_v[l].astype(BF16))
        rg_o, rgc, rgh = _rg_prompt(yp, rg, PROMPT_RG_TILE)
        xa_o, xa_s = _xattn(yp, xs_pad, xa["wq"], xa["wg"], mkb, mvb, cache_mem_k[l], cache_mem_v[l],
                            PROMPT_XA_TILE)
        yp, sc, sh = _ssd_merge_prompt(yp, ssd, rg_o, xa_o, merge)
        outs["rgc_p"].append(rgc)
        outs["rgh_p"].append(rgh.reshape(bp, -1))
        outs["sc_p"].append(sc)
        outs["sh_p"].append(sh.reshape(bp, heads, SSD_HEAD_DIM, SSD_STATE))
        outs["mk_p"].append(mk)
        outs["mv_p"].append(mv)
        tail = SAMPLE_PAD - (CONV_W - 1)
        rg_o, rgc, rgh = _rg_sample(xs_pad, rg, pad_rows(state_rg_conv[l], tail, 0),
                                    state_rg_h[l][:, None, :], SAMPLE_RG_SEQS, ls)
        ssd_o, sc, sh = _ssd_sample(xs_pad, ssd, pad_rows(state_ssd_conv[l], tail, 0),
                                    state_ssd_h[l].reshape(bs, heads * SSD_HEAD_DIM, SSD_STATE),
                                    SAMPLE_SSD_SEQS, ls)
        flat = lambda v: v.reshape(bs * SAMPLE_PAD, v.shape[-1])
        ys_pad = _merge(flat(rg_o), flat(ssd_o), flat(xa_s), flat(xs_pad), merge["w"], merge["g"], merge["b"],
                        MERGE_TILE, "merge_sample").reshape(bs, SAMPLE_PAD, d)
        ys = ys_pad[:, :ls, :]
        outs["rgc_s"].append(rgc)
        outs["rgh_s"].append(rgh.reshape(bs, -1))
        outs["sc_s"].append(sc)
        outs["sh_s"].append(sh.reshape(bs, heads, SSD_HEAD_DIM, SSD_STATE))
    st = lambda k: jnp.stack(outs[k])
    return (yp, ys, st("rgc_p"), st("rgh_p"), st("sc_p"), st("sh_p"), st("mk_p"), st("mv_p"),
            st("rgc_s"), st("rgh_s"), st("sc_s"), st("sh_s"))
```

```python
import functools
from typing import NamedTuple

import jax
import jax.numpy as jnp
import numpy as np
from jax import lax
from jax.experimental import pallas as pl
from jax.experimental.pallas import tpu as pltpu

F32 = jnp.float32
BF16 = jnp.bfloat16

SUBLANES = 8
LANES = 128
MXU_WIDTH = 256
VMEM_LIMIT_BYTES = 56 * 1024 * 1024

RG_C = 8.0
CONV_W = 4
RG_BLOCKS = 8
SSD_HEAD_DIM = 64
SSD_GROUPS = 2
SSD_STATE = 128
SSD_CHUNK = 128
MEM_HEADS = 4
LN_EPS = 1e-5
RMS_EPS = 1e-5
DEPTH = 1
DEEPNORM_ALPHA = (2 * DEPTH) ** 0.25
LOG2_E = 1.4426950408889634
SAMPLE_PAD = SUBLANES


def _cparams(*sem):
    return pltpu.CompilerParams(dimension_semantics=sem, vmem_limit_bytes=VMEM_LIMIT_BYTES)


def _mm(a, b):
    return jnp.dot(a.astype(BF16), b.astype(BF16), preferred_element_type=F32)


def _mm_nt(a, b):
    return lax.dot_general(a.astype(BF16), b.astype(BF16), (((1,), (1,)), ((), ())),
                           preferred_element_type=F32)


def _mm_tn(a, b):
    return lax.dot_general(a.astype(BF16), b.astype(BF16), (((0,), (0,)), ((), ())),
                           preferred_element_type=F32)


def _exp_neg(x):
    return jnp.exp2(x * (-LOG2_E))


def _sigmoid(x):
    return 1.0 / (1.0 + _exp_neg(x))


def _silu(x):
    return x * _sigmoid(x)


def _softplus(x):
    return jnp.maximum(x, 0.0) + jnp.log(1.0 + jnp.exp(-jnp.abs(x)))


def _causal_conv_tiles(x3, p3, w, b):
    row = lax.broadcasted_iota(jnp.int32, x3.shape, 1)
    y = x3 * w[CONV_W - 1:CONV_W][None]
    for s in range(1, CONV_W):
        shifted = jnp.where(row >= s, pltpu.roll(x3, s, axis=1), pltpu.roll(p3, s, axis=1))
        y = y + shifted * w[CONV_W - 1 - s:CONV_W - s][None]
    return y + b[None]


def _scan_in_tiles(a3, b3):
    row = lax.broadcasted_iota(jnp.int32, a3.shape, 1)
    s = 1
    while s < SUBLANES:
        keep = row >= s
        a_sh = jnp.where(keep, pltpu.roll(a3, s, axis=1), 1.0)
        b_sh = jnp.where(keep, pltpu.roll(b3, s, axis=1), 0.0)
        b3 = a3 * b_sh + b3
        a3 = a3 * a_sh
        s *= 2
    return a3, b3


def _rg_gates(u, wgate, ba, bi, lam):
    pre = _mm(u, wgate)
    r = _sigmoid(pre[:, :LANES] + ba)
    i = _sigmoid(pre[:, LANES:] + bi)
    neg_log_a = r * (RG_C * _softplus(-lam))
    a = _exp_neg(neg_log_a)
    v = jnp.tanh(neg_log_a) * (1.0 + a * a)
    mult = jnp.where(v > 0.0, v * lax.rsqrt(v), 0.0)
    return a, mult * (i * u)


def _time_strided_conv(x3, tail3, w, b):
    slabs = x3.shape[0]
    row = lax.broadcasted_iota(jnp.int32, tail3.shape, 1)
    wrapped = jnp.where(row >= 1, pltpu.roll(x3[slabs - (CONV_W - 1):], 1, axis=1), pltpu.roll(tail3, 1, axis=1))
    y = x3 * w[CONV_W - 1:CONV_W][None]
    for s in range(1, CONV_W):
        shifted = jnp.concatenate([wrapped[CONV_W - 1 - s:], x3[:slabs - s]], axis=0)
        y = y + shifted * w[CONV_W - 1 - s:CONV_W - s][None]
    return y + b[None]


def _rg_prompt_kernel(x_ref, perm_ref, permt_ref, wx_ref, wg_ref, cw_ref, cb_ref, wgate_ref, ba_ref,
                      bi_ref, lam_ref, out_ref, conv_ref, hlast_ref, tail_scr, h_scr, proj_scr, outp_scr, *,
                      tiles_per_seq):
    s = pl.program_id(0)

    @pl.when(s == 0)
    def _():
        proj_scr[...] = jnp.zeros_like(proj_scr)

    @pl.when(jnp.logical_or(s == 0, s % tiles_per_seq == 1 % tiles_per_seq))
    def _():
        tail_scr[...] = jnp.zeros_like(tail_scr)
        h_scr[...] = jnp.zeros_like(h_scr)

    rows = x_ref.shape[0]
    slabs = rows // SUBLANES
    width = wx_ref.shape[1]
    per_group = MXU_WIDTH // LANES
    row = lax.broadcasted_iota(jnp.int32, (SUBLANES, LANES), 0)
    xb = jnp.dot(perm_ref[...], x_ref[...].astype(BF16), preferred_element_type=F32).astype(BF16)

    for cg in range(width // MXU_WIDTH):
        gs = slice(cg * MXU_WIDTH, (cg + 1) * MXU_WIDTH)
        for kk in range(per_group):
            k = cg * per_group + kk
            ks = slice(k * LANES, (k + 1) * LANES)
            x3 = proj_scr[:, ks].reshape(slabs, SUBLANES, LANES)
            tail3 = tail_scr[:, :, ks]
            last3 = x3[slabs - (CONV_W - 1):]
            tail_scr[:, :, ks] = last3
            for i in range(CONV_W - 1):
                conv_ref[0, i:i + 1, ks] = last3[i, SUBLANES - 1:, :]
            u3 = _time_strided_conv(x3, tail3, cw_ref[:, ks], cb_ref[:, ks])
            a, b = _rg_gates(u3.reshape(rows, LANES), wgate_ref[k], ba_ref[:, ks], bi_ref[:, ks], lam_ref[:, ks])
            a3 = a.reshape(slabs, SUBLANES, LANES)
            b3 = b.reshape(slabs, SUBLANES, LANES)
            h_loc, a_cum = [b3[0]], [a3[0]]
            for j in range(1, slabs):
                h_loc.append(a3[j] * h_loc[j - 1] + b3[j])
                a_cum.append(a3[j] * a_cum[j - 1])
            a_run, h_run = _scan_in_tiles(a_cum[-1][None], h_loc[-1][None])
            h_prev = h_scr[:, ks]
            h_end = h_run[0] + a_run[0] * h_prev
            h_in = jnp.where(row >= 1, pltpu.roll(h_end, 1, axis=0), h_prev)
            h_scr[:, ks] = h_end[SUBLANES - 1:]
            hlast_ref[0, :, ks] = h_end[SUBLANES - 1:]
            h = jnp.concatenate([h_loc[j] + a_cum[j] * h_in for j in range(slabs)], axis=0)
            gate = proj_scr[:, width + k * LANES:width + (k + 1) * LANES]
            outp_scr[:, ks] = (h * _silu(gate)).astype(BF16)
        proj_scr[:, gs] = jnp.dot(xb, wx_ref[:, gs], preferred_element_type=F32)
        proj_scr[:, width + cg * MXU_WIDTH:width + (cg + 1) * MXU_WIDTH] = jnp.dot(
            xb, wg_ref[:, gs], preferred_element_type=F32)

    out_ref[...] = jnp.dot(permt_ref[...], outp_scr[...], preferred_element_type=F32).astype(out_ref.dtype)


def _rg_sample_kernel(x_ref, wx_ref, wg_ref, cw_ref, cb_ref, wgate_ref, ba_ref, bi_ref, lam_ref,
                      cstate_ref, h0_ref, out_ref, conv_ref, hlast_ref, *, valid):
    seqs = x_ref.shape[0]
    rows = seqs * SUBLANES
    xb = x_ref[...].reshape(rows, x_ref.shape[2]).astype(BF16)
    rgx = jnp.dot(xb, wx_ref[...], preferred_element_type=F32)
    gate = jnp.dot(xb, wg_ref[...], preferred_element_type=F32)
    x3 = rgx.reshape(seqs, SUBLANES, rgx.shape[1])
    conv_ref[...] = x3[:, valid - (CONV_W - 1):valid, :]
    row = lax.broadcasted_iota(jnp.int32, (seqs, SUBLANES, LANES), 1)
    for k in range(RG_BLOCKS):
        ks = slice(k * LANES, (k + 1) * LANES)
        u3 = _causal_conv_tiles(x3[:, :, ks], cstate_ref[:, :, ks], cw_ref[:, ks], cb_ref[:, ks])
        u = u3.reshape(rows, LANES)
        a, b = _rg_gates(u, wgate_ref[k], ba_ref[:, ks], bi_ref[:, ks], lam_ref[:, ks])
        a3 = a.reshape(seqs, SUBLANES, LANES)
        b3 = b.reshape(seqs, SUBLANES, LANES)
        b3 = b3 + jnp.where(row == 0, a3 * h0_ref[:, :, ks], 0.0)
        _, h3 = _scan_in_tiles(a3, b3)
        hlast_ref[:, :, ks] = h3[:, valid - 1:valid, :]
        g3 = gate[:, ks].reshape(seqs, SUBLANES, LANES)
        out_ref[:, :, ks] = (h3 * _silu(g3)).astype(out_ref.dtype)


def _group_rmsnorm(y, gain):
    width = y.shape[-1] // SSD_GROUPS
    parts = []
    for g in range(SSD_GROUPS):
        yg = y[..., g * width:(g + 1) * width]
        ms = jnp.sum(yg * yg, axis=-1, keepdims=True) * (1.0 / width)
        parts.append(yg * lax.rsqrt(ms + RMS_EPS))
    return jnp.concatenate(parts, axis=-1) * gain


def _time_strided_cumsum(x):
    slabs = x.shape[0] // SUBLANES
    x3 = x.reshape(slabs, SUBLANES, x.shape[1])
    acc = [x3[0]]
    for j in range(1, slabs):
        acc.append(acc[j - 1] + x3[j])
    _, run = _scan_in_tiles(jnp.ones_like(acc[-1])[None], acc[-1][None])
    row = lax.broadcasted_iota(jnp.int32, run[0].shape, 0)
    before = jnp.where(row >= 1, pltpu.roll(run[0], 1, axis=0), 0.0)
    return jnp.concatenate([a + before for a in acc], axis=0)


def _ssd_merge_prompt_kernel(x_ref, perm_ref, permt_ref, wxbc_ref, wz_ref, wdt_ref, cw_ref, cb_ref, dtb_ref,
                             alog_ref, dexp_ref, ng_ref, rg_ref, xa_ref, xres_ref, wout_ref, lng_ref, lnb_ref,
                             y_ref, conv_ref, h_ref, tail_scr, proj_scr, xb_scr, h_scr, ssd_scr, xbc_scr, *,
                             tiles_per_seq, n_tiles):
    s = pl.program_id(0)

    @pl.when(s == 0)
    def _():
        proj_scr[...] = jnp.zeros_like(proj_scr)
        xb_scr[...] = jnp.zeros_like(xb_scr)
        ssd_scr[...] = jnp.zeros_like(ssd_scr)

    @pl.when(jnp.logical_or(s == 0, s % tiles_per_seq == 1 % tiles_per_seq))
    def _():
        tail_scr[...] = jnp.zeros_like(tail_scr)
        h_scr[...] = jnp.zeros_like(h_scr)

    q = SSD_CHUNK
    n_sub = x_ref.shape[0] // q
    slabs = q // SUBLANES
    d_ssd = wz_ref.shape[1]
    d_conv = wxbc_ref.shape[1]
    gn = SSD_GROUPS * SSD_STATE
    tail3 = tail_scr[...]
    for c in range(n_sub):
        x3 = proj_scr[c * q:(c + 1) * q, :d_conv].reshape(slabs, SUBLANES, d_conv)
        xbc_scr[c * q:(c + 1) * q, :] = _silu(_time_strided_conv(x3, tail3, cw_ref[...], cb_ref[...])).reshape(
            q, d_conv)
        tail3 = x3[slabs - (CONV_W - 1):]
    tail_scr[...] = tail3
    dtr_all = proj_scr[:, d_conv:]

    mix, row0 = None, 0
    for src in (rg_ref, ssd_scr, xa_ref):
        part = jnp.dot(src[...], wout_ref[row0:row0 + src.shape[1], :], preferred_element_type=F32)
        mix = part if mix is None else mix + part
        row0 += src.shape[1]
    res = DEEPNORM_ALPHA * xres_ref[...] + mix
    mu = jnp.mean(res, axis=-1, keepdims=True)
    cen = res - mu
    var = jnp.mean(cen * cen, axis=-1, keepdims=True)
    y_ref[...] = cen * lax.rsqrt(var + LN_EPS) * lng_ref[...] + lnb_ref[...]

    xin = x_ref[...].astype(BF16)
    xb_new = jnp.concatenate(
        [jnp.dot(perm_ref[...], xin[c * q:(c + 1) * q], preferred_element_type=F32).astype(BF16)
         for c in range(n_sub)], axis=0)
    z_all = jnp.dot(xb_scr[...], wz_ref[...], preferred_element_type=F32)
    xb_scr[...] = xb_new

    ii = lax.broadcasted_iota(jnp.int32, (q, q), 0)
    jj = lax.broadcasted_iota(jnp.int32, (q, q), 1)
    time_of = lambda r: (r % SUBLANES) * slabs + r // SUBLANES
    causal = time_of(ii) >= time_of(jj)
    lane = lax.broadcasted_iota(jnp.int32, (q, LANES), 1)
    lo = lane < SSD_HEAD_DIM
    srow = lax.broadcasted_iota(jnp.int32, (LANES, SSD_STATE), 0) < SSD_HEAD_DIM
    heads = d_ssd // SSD_HEAD_DIM
    pairs = heads // 2
    pairs_per_group = pairs // SSD_GROUPS

    projections = [(wxbc_ref, slice(g * MXU_WIDTH, (g + 1) * MXU_WIDTH), slice(g * MXU_WIDTH, (g + 1) * MXU_WIDTH))
                   for g in range(d_conv // MXU_WIDTH)]
    projections.append((wdt_ref, slice(None), slice(d_conv, None)))
    every = (n_sub * pairs) // len(projections)
    assert every >= 1
    issued = 0

    for c in range(n_sub):
        cs = slice(c * q, (c + 1) * q)
        sx = xbc_scr[cs, :d_ssd]
        bm = xbc_scr[cs, d_ssd:d_ssd + gn]
        cm = xbc_scr[cs, d_ssd + gn:]
        dt = _softplus(dtr_all[cs] + dtb_ref[...])
        da = dt * (-jnp.exp(alog_ref[...]))
        acum = _time_strided_cumsum(da) * LOG2_E
        alast = acum[q - 1:q, :]
        wgt = dt * jnp.exp2(alast - acum)
        tot = jnp.exp2(alast)
        acum_dt_t = (acum - jnp.log2(dt)).T
        cb = [_mm_nt(cm[:, g * SSD_STATE:(g + 1) * SSD_STATE], bm[:, g * SSD_STATE:(g + 1) * SSD_STATE])
              for g in range(SSD_GROUPS)]
        y_parts = []
        for g in range(SSD_GROUPS):
            ns = slice(g * SSD_STATE, (g + 1) * SSD_STATE)
            rows_g = slice(g * pairs_per_group * LANES, (g + 1) * pairs_per_group * LANES)
            h_in = h_scr[rows_g, :]
            y_off_g = _mm_nt(cm[:, ns], h_in)
            xws, decays = [], []
            for pl_ in range(pairs_per_group):
                pq = g * pairs_per_group + pl_
                ps = slice(pq * LANES, (pq + 1) * LANES)
                xq = sx[:, ps]
                ms, es, ws, ts = [], [], [], []
                for h in (2 * pq, 2 * pq + 1):
                    acol = jnp.broadcast_to(acum[:, h:h + 1], (q, q))
                    arow = jnp.broadcast_to(acum_dt_t[h:h + 1, :], (q, q))
                    decay_dt = jnp.exp2(jnp.where(causal, acol - arow, -jnp.inf))
                    ms.append((cb[g] * decay_dt).astype(BF16))
                    es.append(jnp.exp2(jnp.broadcast_to(acum[:, h:h + 1], (q, LANES))))
                    ws.append(jnp.broadcast_to(wgt[:, h:h + 1], (q, LANES)))
                    ts.append(jnp.broadcast_to(tot[:, h:h + 1], (LANES, SSD_STATE)))
                lhs = jnp.concatenate(ms, axis=1)
                rhs = jnp.concatenate([jnp.where(lo, xq, 0.0), jnp.where(lo, 0.0, xq)], axis=0)
                y_diag = _mm(lhs, rhs)
                y_off = y_off_g[:, pl_ * LANES:(pl_ + 1) * LANES] * jnp.where(lo, es[0], es[1])
                xws.append(xq * jnp.where(lo, ws[0], ws[1]))
                decays.append(jnp.where(srow, ts[0], ts[1]))
                y_parts.append(y_diag + y_off + dexp_ref[:, ps] * xq)
                if (c * pairs + pq + 1) % every == 0 and issued < len(projections):
                    w_ref, src, dst = projections[issued]
                    proj_scr[:, dst] = jnp.dot(xb_new, w_ref[:, src], preferred_element_type=F32)
                    issued += 1
            h_scr[rows_g, :] = (h_in * jnp.concatenate(decays, axis=0)
                                + _mm_tn(jnp.concatenate(xws, axis=1), bm[:, ns]))
        y = jnp.concatenate(y_parts, axis=1) * _silu(z_all[cs])
        y = _group_rmsnorm(y, ng_ref[...]).astype(BF16)
        ssd_scr[cs, :] = jnp.dot(permt_ref[...], y, preferred_element_type=F32).astype(BF16)
    assert issued == len(projections)

    @pl.when(jnp.logical_and(s % tiles_per_seq == 0, jnp.logical_and(s > 0, s <= n_tiles)))
    def _():
        h_ref[0] = h_scr[...]
        for i in range(CONV_W - 1):
            conv_ref[0, i:i + 1, :] = tail_scr[i, SUBLANES - 1:, :]


def _ssd_sample_kernel(x_ref, wxbc_ref, wz_ref, wdt_ref, cw_ref, cb_ref, dtb_ref, alog_ref, dexp_ref,
                       ng_ref, expand_ref, cstate_ref, h0_ref, out_ref, conv_ref, h_ref,
                       c_scr, b_scr, xw_scr, tot_scr, yoff_scr, *, valid):
    seqs = x_ref.shape[0]
    rows = seqs * SUBLANES
    d_ssd = wz_ref.shape[1]
    gn = SSD_GROUPS * SSD_STATE
    heads = d_ssd // SSD_HEAD_DIM
    hg = heads // SSD_GROUPS
    xb = x_ref[...].reshape(rows, x_ref.shape[2]).astype(BF16)
    xbc_raw = jnp.dot(xb, wxbc_ref[...], preferred_element_type=F32)
    z = jnp.dot(xb, wz_ref[...], preferred_element_type=F32)
    dtr = jnp.dot(xb, wdt_ref[...], preferred_element_type=F32)

    x3 = xbc_raw.reshape(seqs, SUBLANES, xbc_raw.shape[1])
    conv_ref[...] = x3[:, valid - (CONV_W - 1):valid, :]
    xbc3 = _silu(_causal_conv_tiles(x3, cstate_ref[...], cw_ref[...], cb_ref[...]))
    sx3 = xbc3[:, :, :d_ssd]
    b3 = xbc3[:, :, d_ssd:d_ssd + gn]
    c3 = xbc3[:, :, d_ssd + gn:]

    row = lax.broadcasted_iota(jnp.int32, (seqs, SUBLANES, LANES), 1)
    lane = lax.broadcasted_iota(jnp.int32, (seqs, SUBLANES, LANES), 2)
    dt3 = jnp.where(row < valid, _softplus(dtr + dtb_ref[...]).reshape(seqs, SUBLANES, LANES), 0.0)
    da3 = dt3 * (-jnp.exp(alog_ref[...]))[None]
    ones = jnp.ones_like(da3)
    _, acum3 = _scan_in_tiles(ones, da3)
    alast = acum3[:, SUBLANES - 1:, :]
    wgt3 = dt3 * jnp.exp(alast - acum3)
    tot_scr[...] = jnp.exp(alast)
    e3 = jnp.exp(acum3)

    coefs = []
    for u in range(valid):
        prod = c3 * b3[:, u:u + 1, :]
        cbu = [jnp.sum(prod[:, :, g * SSD_STATE:(g + 1) * SSD_STATE], axis=-1, keepdims=True)
               for g in range(SSD_GROUPS)]
        cb_heads = jnp.where(lane < hg, cbu[0], cbu[1])
        coefs.append(jnp.where(row >= u, cb_heads * jnp.exp(acum3 - acum3[:, u:u + 1, :]) * dt3[:, u:u + 1, :],
                               0.0))

    per_head = coefs + [wgt3, e3]
    stacked = jnp.concatenate([v.reshape(rows, LANES) for v in per_head], axis=0)
    expanded = None
    rest = stacked
    for _ in range(3):
        piece = rest.astype(BF16)
        rest = rest - piece.astype(F32)
        part = jnp.dot(piece, expand_ref[...], preferred_element_type=F32)
        expanded = part if expanded is None else expanded + part
    expanded = [expanded[i * rows:(i + 1) * rows].reshape(seqs, SUBLANES, d_ssd) for i in range(len(per_head))]
    y_diag = expanded[0] * sx3[:, 0:1, :]
    for u in range(1, valid):
        y_diag = y_diag + expanded[u] * sx3[:, u:u + 1, :]
    wgt_x, e_x = expanded[valid], expanded[valid + 1]

    c_scr[...] = c3
    b_scr[...] = b3
    xw_scr[...] = sx3 * wgt_x
    srow = lax.broadcasted_iota(jnp.int32, (2 * SSD_HEAD_DIM, SSD_STATE), 0) < SSD_HEAD_DIM

    def per_seq(s, carry):
        cs = c_scr[s]
        bs = b_scr[s]
        xws = xw_scr[s]
        tots = tot_scr[s]
        for g in range(SSD_GROUPS):
            gs = slice(g * SSD_STATE, (g + 1) * SSD_STATE)
            width = hg * SSD_HEAD_DIM
            cols = slice(g * width, (g + 1) * width)
            hin = h0_ref[s, cols, :]
            yoff_scr[s, :, cols] = _mm_nt(cs[:, gs], hin)
            upd = _mm_tn(xws[:, cols], bs[:, gs])
            for pq in range(hg // 2):
                h = g * hg + 2 * pq
                rs = slice(pq * LANES, (pq + 1) * LANES)
                t0 = jnp.broadcast_to(tots[0:1, h:h + 1], (LANES, SSD_STATE))
                t1 = jnp.broadcast_to(tots[0:1, h + 1:h + 2], (LANES, SSD_STATE))
                h_ref[s, g * width + pq * LANES:g * width + (pq + 1) * LANES, :] = (
                    hin[rs] * jnp.where(srow, t0, t1) + upd[rs])
        return carry

    lax.fori_loop(0, seqs, per_seq, 0)

    y = y_diag + yoff_scr[...] * e_x + dexp_ref[...][None] * sx3
    y = y * _silu(z.reshape(seqs, SUBLANES, d_ssd))
    out_ref[...] = _group_rmsnorm(y, ng_ref[...][None]).astype(out_ref.dtype)


def _memkv_kernel(mem_ref, wk_ref, wv_ref, k_ref, v_ref, kb_ref, vb_ref):
    nb, m, d = mem_ref.shape
    mb = mem_ref[...].reshape(nb * m, d).astype(BF16)
    for w_ref, o_ref, ob_ref in ((wk_ref, k_ref, kb_ref), (wv_ref, v_ref, vb_ref)):
        proj = jnp.dot(mb, w_ref[...], preferred_element_type=F32)
        for i in range(nb):
            o_ref[i] = proj[i * m:(i + 1) * m].reshape(o_ref.shape[1:])
        ob_ref[...] = proj.astype(BF16).reshape(ob_ref.shape)


def _softmax_terms(scores):
    p = jnp.exp(scores - jnp.max(scores, axis=-1, keepdims=True))
    return p, jnp.sum(p, axis=-1, keepdims=True)


def _prompt_attention_head(h, q_ref, rows, k_ref, v_ref, out_ref):
    c = k_ref.shape[2]
    d_head = c // MEM_HEADS
    hs = slice(h * d_head, (h + 1) * d_head)

    def scores():
        return _softmax_terms(_mm_nt(q_ref[:rows, hs], k_ref[0, :, hs]) * (d_head ** -0.5))

    def output(p, l):
        o = _mm(p, v_ref[0, :, hs]) * (1.0 / l)
        gate = q_ref[:rows, c + h * d_head:c + (h + 1) * d_head]
        out_ref[:, hs] = (o * _silu(gate)).astype(out_ref.dtype)

    return scores, output


def _sample_attention_seq(s, q_ref, rows, k_ref, v_ref, out_ref):
    _, n_mem, heads, d_head = k_ref.shape
    c = heads * d_head
    rs = slice(rows + s * SUBLANES, rows + (s + 1) * SUBLANES)

    def scores():
        shape = (heads * SUBLANES, n_mem * heads)
        same_head = (lax.broadcasted_iota(jnp.int32, shape, 0) // SUBLANES
                     == lax.broadcasted_iota(jnp.int32, shape, 1) % heads)
        qh = jnp.concatenate([q_ref[rs, h * d_head:(h + 1) * d_head] for h in range(heads)], axis=0)
        sc = _mm_nt(qh, k_ref[s].reshape(n_mem * heads, d_head)) * (d_head ** -0.5)
        return _softmax_terms(jnp.where(same_head, sc, -jnp.inf))

    def output(p, l):
        o = _mm(p, v_ref[s].reshape(n_mem * heads, d_head)) * (1.0 / l)
        o = jnp.concatenate([o[h * SUBLANES:(h + 1) * SUBLANES] for h in range(heads)], axis=1)
        out_ref[s] = (o * _silu(q_ref[rs, c:])).astype(out_ref.dtype)

    return scores, output


def _xattn_kernel(x_ref, xs_ref, wq_ref, wg_ref, k_ref, v_ref, ks_ref, vs_ref, out_ref, outs_ref):
    rows = x_ref.shape[0]
    seqs, _, d = xs_ref.shape
    xb = jnp.concatenate([x_ref[...], xs_ref[...].reshape(seqs * SUBLANES, d)], axis=0).astype(BF16)
    qg = jnp.concatenate([jnp.dot(xb, wq_ref[...], preferred_element_type=F32),
                          jnp.dot(xb, wg_ref[...], preferred_element_type=F32)], axis=1)
    items = [_prompt_attention_head(h, qg, rows, k_ref, v_ref, out_ref) for h in range(MEM_HEADS)]
    items += [_sample_attention_seq(i, qg, rows, ks_ref, vs_ref, outs_ref) for i in range(seqs)]
    pending = None
    for scores, output in items:
        terms = scores()
        if pending is not None:
            pending[0](*pending[1])
        pending = (output, terms)
    pending[0](*pending[1])


def _merge_kernel(rg_ref, ssd_ref, xa_ref, x_ref, w_ref, g_ref, b_ref, y_ref, *, sub_rows):
    d = rg_ref.shape[1]
    for r0 in range(0, x_ref.shape[0], sub_rows):
        rs = slice(r0, r0 + sub_rows)
        mix = (jnp.dot(rg_ref[rs, :].astype(BF16), w_ref[0:d, :], preferred_element_type=F32)
               + jnp.dot(ssd_ref[rs, :].astype(BF16), w_ref[d:2 * d, :], preferred_element_type=F32)
               + jnp.dot(xa_ref[rs, :].astype(BF16), w_ref[2 * d:3 * d, :], preferred_element_type=F32))
        res = DEEPNORM_ALPHA * x_ref[rs, :] + mix
        mu = jnp.mean(res, axis=-1, keepdims=True)
        cen = res - mu
        var = jnp.mean(cen * cen, axis=-1, keepdims=True)
        y_ref[rs, :] = cen * lax.rsqrt(var + LN_EPS) * g_ref[...] + b_ref[...]


def _full(shape):
    return pl.BlockSpec(shape, lambda *_: (0,) * len(shape))


def _time_stride_perm(rows):
    p = np.arange(rows)
    t = (p % SUBLANES) * (rows // SUBLANES) + p // SUBLANES
    perm = t[:, None] == np.arange(rows)[None, :]
    return jnp.asarray(perm, dtype=BF16), jnp.asarray(perm.T, dtype=BF16)


def _rg_prompt(x, p, tile):
    b, l, d = x.shape
    c = p["wx"].shape[1]
    tiles_per_seq = l // tile
    n_tiles = b * tiles_per_seq
    perm, perm_t = _time_stride_perm(tile)
    done = lambda s: jnp.maximum(s - 1, 0)
    out, conv, hlast = pl.pallas_call(
        functools.partial(_rg_prompt_kernel, tiles_per_seq=tiles_per_seq),
        grid=(n_tiles + 1,),
        in_specs=[pl.BlockSpec((tile, d), lambda s: (jnp.minimum(s, n_tiles - 1), 0)),
                  _full(perm.shape), _full(perm.shape), p["wx"].spec, p["wg"].spec,
                  _full(p["cw"].shape), _full(p["cb"].shape), _full(p["wgate"].shape), _full(p["ba"].shape),
                  _full(p["bi"].shape), _full(p["lam"].shape)],
        out_specs=[pl.BlockSpec((tile, c), lambda s: (done(s), 0)),
                   pl.BlockSpec((1, CONV_W - 1, c), lambda s: (done(s) // tiles_per_seq, 0, 0)),
                   pl.BlockSpec((1, 1, c), lambda s: (done(s) // tiles_per_seq, 0, 0))],
        out_shape=[jax.ShapeDtypeStruct((b * l, c), BF16),
                   jax.ShapeDtypeStruct((b, CONV_W - 1, c), F32),
                   jax.ShapeDtypeStruct((b, 1, c), F32)],
        scratch_shapes=[pltpu.VMEM((CONV_W - 1, SUBLANES, c), F32), pltpu.VMEM((1, c), F32),
                        pltpu.VMEM((tile, 2 * c), F32), pltpu.VMEM((tile, c), BF16)],
        compiler_params=_cparams("arbitrary"),
        name="rg_prompt",
    )(x.reshape(b * l, d), perm, perm_t, p["wx"].array, p["wg"].array, p["cw"], p["cb"], p["wgate"], p["ba"],
      p["bi"], p["lam"])
    return out.reshape(b, l, c), conv, hlast


def _rg_sample(xpad, p, cstate, h0, seqs, valid):
    n, _, d = xpad.shape
    c = p["wx"].shape[1]
    blk = lambda w: pl.BlockSpec((seqs, w[0], w[1]), lambda i: (i, 0, 0))
    return pl.pallas_call(
        functools.partial(_rg_sample_kernel, valid=valid),
        grid=(n // seqs,),
        in_specs=[blk((SUBLANES, d)), p["wx"].spec, p["wg"].spec, _full(p["cw"].shape),
                  _full(p["cb"].shape), _full(p["wgate"].shape), _full(p["ba"].shape),
                  _full(p["bi"].shape), _full(p["lam"].shape), blk((SUBLANES, c)), blk((1, c))],
        out_specs=[blk((SUBLANES, c)), blk((CONV_W - 1, c)), blk((1, c))],
        out_shape=[jax.ShapeDtypeStruct((n, SUBLANES, c), F32),
                   jax.ShapeDtypeStruct((n, CONV_W - 1, c), F32),
                   jax.ShapeDtypeStruct((n, 1, c), F32)],
        compiler_params=_cparams("parallel"),
        name="rg_sample",
    )(xpad, p["wx"].array, p["wg"].array, p["cw"], p["cb"], p["wgate"], p["ba"], p["bi"], p["lam"], cstate, h0)


_SSD_WEIGHTS = ("wxbc", "wz", "wdt")
_SSD_SMALL = ("cw", "cb", "dtb", "alog", "dexp", "ng")


def _ssd_param_specs(p):
    return [p[k].spec for k in _SSD_WEIGHTS] + [_full(p[k].shape) for k in _SSD_SMALL]


def _ssd_param_args(p):
    return [p[k].array for k in _SSD_WEIGHTS] + [p[k] for k in _SSD_SMALL]


def _ssd_merge_prompt(x, p, rg_out, xa_out, mp):
    b, l, d = x.shape
    cc = p["wxbc"].shape[1]
    c = p["wz"].shape[1]
    tile = min(SSD_TILE, l)
    tiles_per_seq = l // tile
    n_tiles = b * tiles_per_seq
    perm, perm_t = _time_stride_perm(SSD_CHUNK)
    flat = lambda v: v.reshape(b * l, v.shape[-1])
    clamp = lambda i: jnp.clip(i, 0, n_tiles - 1)
    rows = lambda w, back: pl.BlockSpec((tile, w), lambda s: (clamp(s - back), 0))
    state = lambda shape: pl.BlockSpec((1,) + shape, lambda s: (clamp(s - 1) // tiles_per_seq, 0, 0))
    y, conv, hstate = pl.pallas_call(
        functools.partial(_ssd_merge_prompt_kernel, tiles_per_seq=tiles_per_seq, n_tiles=n_tiles),
        grid=(n_tiles + 2,),
        in_specs=[rows(d, 0), _full(perm.shape), _full(perm.shape)] + _ssd_param_specs(p)
        + [rows(rg_out.shape[-1], 2), rows(xa_out.shape[-1], 2), rows(d, 2), _full(mp["w"].shape),
           _full(mp["g"].shape), _full(mp["b"].shape)],
        out_specs=[rows(d, 2), state((CONV_W - 1, cc)), state((c, SSD_STATE))],
        out_shape=[jax.ShapeDtypeStruct((b * l, d), F32),
                   jax.ShapeDtypeStruct((b, CONV_W - 1, cc), F32),
                   jax.ShapeDtypeStruct((b, c, SSD_STATE), F32)],
        scratch_shapes=[pltpu.VMEM((CONV_W - 1, SUBLANES, cc), F32),
                        pltpu.VMEM((tile, cc + LANES), F32), pltpu.VMEM((tile, d), BF16),
                        pltpu.VMEM((c, SSD_STATE), F32), pltpu.VMEM((tile, c), BF16),
                        pltpu.VMEM((tile, cc), F32)],
        compiler_params=_cparams("arbitrary"),
        name="ssd_merge_prompt",
    )(flat(x), perm, perm_t, *_ssd_param_args(p), flat(rg_out), flat(xa_out), flat(x), mp["w"], mp["g"], mp["b"])
    return y.reshape(b, l, d), conv, hstate


def _ssd_sample(xpad, p, cstate, h0, seqs, valid):
    n, _, d = xpad.shape
    cc = p["wxbc"].shape[1]
    c = p["wz"].shape[1]
    gn = SSD_GROUPS * SSD_STATE
    blk = lambda w: pl.BlockSpec((seqs, w[0], w[1]), lambda i: (i, 0, 0))
    return pl.pallas_call(
        functools.partial(_ssd_sample_kernel, valid=valid),
        grid=(n // seqs,),
        in_specs=[blk((SUBLANES, d))] + _ssd_param_specs(p)
        + [_full(p["expand"].shape), blk((SUBLANES, cc)), blk((c, SSD_STATE))],
        out_specs=[blk((SUBLANES, c)), blk((CONV_W - 1, cc)), blk((c, SSD_STATE))],
        out_shape=[jax.ShapeDtypeStruct((n, SUBLANES, c), F32),
                   jax.ShapeDtypeStruct((n, CONV_W - 1, cc), F32),
                   jax.ShapeDtypeStruct((n, c, SSD_STATE), F32)],
        scratch_shapes=[pltpu.VMEM((seqs, SUBLANES, gn), F32), pltpu.VMEM((seqs, SUBLANES, gn), F32),
                        pltpu.VMEM((seqs, SUBLANES, c), F32), pltpu.VMEM((seqs, 1, LANES), F32),
                        pltpu.VMEM((seqs, SUBLANES, c), F32)],
        compiler_params=_cparams("parallel"),
        name="ssd_sample",
    )(xpad, *_ssd_param_args(p), p["expand"], cstate, h0)


def _memkv(mem, wk, wv):
    b, m, d = mem.shape
    c = wk.shape[1]
    nb = MEMKV_BATCHES if b % MEMKV_BATCHES == 0 else 1
    spec = pl.BlockSpec((nb, m, c), lambda i: (i, 0, 0))
    spec4 = pl.BlockSpec((nb, m, MEM_HEADS, c // MEM_HEADS), lambda i: (i, 0, 0, 0))
    return pl.pallas_call(
        _memkv_kernel,
        grid=(b // nb,),
        in_specs=[pl.BlockSpec((nb, m, d), lambda i: (i, 0, 0)), _full(wk.shape), _full(wv.shape)],
        out_specs=[spec4, spec4, spec, spec],
        out_shape=[jax.ShapeDtypeStruct((b, m, MEM_HEADS, c // MEM_HEADS), F32)] * 2
        + [jax.ShapeDtypeStruct((b, m, c), BF16)] * 2,
        compiler_params=_cparams("parallel"),
        name="mem_kv",
    )(mem, wk, wv)


def _xattn(x, xs_pad, wq, wg, k, v, ks, vs, tile):
    b, l, d = x.shape
    n = xs_pad.shape[0]
    c = wq.shape[1]
    m = k.shape[1]
    tile = min(tile, l)
    tiles_per_seq = l // tile
    n_tiles = b * tiles_per_seq
    seqs = n // n_tiles
    assert seqs * n_tiles == n
    kv_spec = pl.BlockSpec((1, m, c), lambda s: (s // tiles_per_seq, 0, 0))
    skv_spec = pl.BlockSpec((seqs,) + ks.shape[1:], lambda s: (s, 0, 0, 0))
    out, outs = pl.pallas_call(
        _xattn_kernel,
        grid=(n_tiles,),
        in_specs=[pl.BlockSpec((tile, d), lambda s: (s, 0)),
                  pl.BlockSpec((seqs, SUBLANES, d), lambda s: (s, 0, 0)),
                  wq.spec, wg.spec, kv_spec, kv_spec, skv_spec, skv_spec],
        out_specs=[pl.BlockSpec((tile, c), lambda s: (s, 0)),
                   pl.BlockSpec((seqs, SUBLANES, c), lambda s: (s, 0, 0))],
        out_shape=[jax.ShapeDtypeStruct((b * l, c), BF16), jax.ShapeDtypeStruct((n, SUBLANES, c), F32)],
        compiler_params=_cparams("parallel"),
        name="xattn",
    )(x.reshape(b * l, d), xs_pad, wq.array, wg.array, k, v, ks, vs)
    return out.reshape(b, l, c), outs


def _merge(rg, ssd, xa, x, w_out, ln_g, ln_b, tile, name):
    n, d = x.shape
    c = rg.shape[1]
    tile = min(tile, n)
    row = lambda w: pl.BlockSpec((tile, w), lambda i: (i, 0))
    return pl.pallas_call(
        functools.partial(_merge_kernel, sub_rows=min(MERGE_SUB_ROWS, tile)),
        grid=(n // tile,),
        in_specs=[row(c), row(c), row(c), row(d), _full(w_out.shape), _full(ln_g.shape), _full(ln_b.shape)],
        out_specs=row(d),
        out_shape=jax.ShapeDtypeStruct((n, d), F32),
        compiler_params=_cparams("parallel"),
        name=name,
    )(rg, ssd, xa, x, w_out, ln_g, ln_b)


class _Cols(NamedTuple):
    array: jax.Array
    width: int
    index: int

    @property
    def shape(self):
        return (self.array.shape[0], self.width)

    @property
    def spec(self):
        return pl.BlockSpec(self.shape, lambda *_, i=self.index: (0, i))


def _weight_prep_kernel(in_blk_ref, out_blk_ref, shift_ref, valid_ref, nxt_blk_ref, a_ref, b_ref, wo_ref,
                        o_ref, oo_ref, *, shift_rows):
    del in_blk_ref, out_blk_ref, nxt_blk_ref
    i = pl.program_id(0)
    rows = a_ref.shape[0]
    keep = lax.broadcasted_iota(jnp.int32, a_ref.shape, 0) < valid_ref[i]

    @pl.when(shift_ref[i] == 0)
    def _():
        o_ref[...] = jnp.where(keep, a_ref[...], 0.0).T.astype(BF16)

    @pl.when(shift_ref[i] != 0)
    def _():
        blk = jnp.concatenate([a_ref[shift_rows:, :], b_ref[:shift_rows, :]], axis=0)
        o_ref[...] = jnp.where(keep, blk, 0.0).T.astype(BF16)

    oo_ref[...] = wo_ref[...].astype(BF16)


def _weight_prep(w_in_t, w_out, bounds, placement):
    n_rows, k = w_in_t.shape
    blk = PREP_BLOCK
    in_blk, out_blk, shift, valid = [], [], [], []
    shifts = {lo % blk for lo, _ in bounds} - {0}
    assert len(shifts) <= 1
    shift_rows = shifts.pop() if shifts else SUBLANES
    assert shift_rows % SUBLANES == 0
    for (lo, hi), place in zip(bounds, placement):
        for j in range(-(-(hi - lo) // blk)):
            in_blk.append((lo + j * blk) // blk)
            out_blk.append(place + j)
            shift.append(lo % blk)
            valid.append(min(blk, hi - lo - j * blk))
    n_cols = (max(out_blk) + 1) * blk
    for gap in sorted(set(range(n_cols // blk)) - set(out_blk)):
        in_blk.append(0)
        out_blk.append(gap)
        shift.append(0)
        valid.append(0)
    steps = len(in_blk)
    last_in = -(-n_rows // blk) - 1
    oo_rows = -(-(-(-w_out.shape[0] // steps)) // SUBLANES) * SUBLANES
    oo_steps = -(-w_out.shape[0] // oo_rows)
    assert oo_steps <= steps
    nxt_blk = [min(a + 1, last_in) if sh else 0 for a, sh in zip(in_blk, shift)]
    tables = [jnp.asarray(np.asarray(t, np.int32)) for t in (in_blk, out_blk, shift, valid, nxt_blk)]
    wo_spec = pl.BlockSpec((oo_rows, w_out.shape[1]), lambda i, *_: (jnp.minimum(i, oo_steps - 1), 0))
    return pl.pallas_call(
        functools.partial(_weight_prep_kernel, shift_rows=shift_rows),
        grid_spec=pltpu.PrefetchScalarGridSpec(
            num_scalar_prefetch=5,
            grid=(steps,),
            in_specs=[pl.BlockSpec((blk, k), lambda i, ib, ob, sh, va, nb: (ib[i], 0)),
                      pl.BlockSpec((blk, k), lambda i, ib, ob, sh, va, nb: (nb[i], 0)),
                      wo_spec],
            out_specs=[pl.BlockSpec((k, blk), lambda i, ib, ob, sh, va, nb: (0, ob[i])), wo_spec],
        ),
        out_shape=[jax.ShapeDtypeStruct((k, n_cols), BF16), jax.ShapeDtypeStruct(w_out.shape, BF16)],
        compiler_params=_cparams("arbitrary"),
        name="weight_prep",
    )(*tables, w_in_t, w_in_t, w_out)


def _layer_params(w_in, rg_conv_w, rg_conv_b, w_rg_a, b_rg_a, w_rg_i, b_rg_i, rg_lambda, ssd_conv_w,
                  ssd_conv_b, ssd_dt_bias, ssd_a_log, ssd_d, ssd_norm_g, w_out, ln_g, ln_b):
    d_rg = rg_conv_w.shape[1]
    d_conv = ssd_conv_w.shape[1]
    d_ssd = ssd_norm_g.shape[0]
    heads = ssd_d.shape[0]
    sizes = (d_rg, d_rg, d_conv, d_ssd, heads)
    offs = [0]
    for s in sizes:
        offs.append(offs[-1] + s)
    d_xa = (w_in.shape[1] - offs[-1]) // 2
    offs += [offs[-1] + d_xa, offs[-1] + 2 * d_xa]
    blocks = lambda w: -(-w // PREP_BLOCK)
    assert d_rg == d_ssd == d_xa and blocks(d_conv) * PREP_BLOCK <= 2 * d_rg and heads <= LANES
    unit = blocks(d_rg)
    place = dict(xbc=0, z=2 * unit, rg_x=3 * unit, rg_g=4 * unit, xa_q=5 * unit, xa_g=6 * unit, dt=7 * unit)
    order = ("rg_x", "rg_g", "xbc", "z", "dt", "xa_q", "xa_g")
    wall, wo = _weight_prep(jnp.swapaxes(w_in, 0, 1), w_out, list(zip(offs[:-1], offs[1:])),
                            [place[k] for k in order])
    col = lambda key, width: _Cols(wall, width, place[key] * PREP_BLOCK // width)
    wx, wg, wxbc, wz = col("rg_x", d_rg), col("rg_g", d_rg), col("xbc", d_conv), col("z", d_ssd)
    wdt, wq, wxg = col("dt", LANES), col("xa_q", d_xa), col("xa_g", d_xa)
    row = lambda v: v.reshape(1, -1).astype(F32)
    pad_lanes = lambda v: jnp.pad(v, ((0, 0), (0, LANES - v.shape[1])))
    rg = dict(wx=wx, wg=wg, cw=rg_conv_w, cb=row(rg_conv_b),
              wgate=jnp.concatenate([w_rg_a, w_rg_i], axis=2).astype(BF16),
              ba=row(b_rg_a), bi=row(b_rg_i), lam=row(rg_lambda))
    head_of_channel = np.arange(d_ssd) // SSD_HEAD_DIM
    ssd = dict(wxbc=wxbc, wz=wz, wdt=wdt,
               cw=ssd_conv_w, cb=row(ssd_conv_b), dtb=pad_lanes(row(ssd_dt_bias)),
               alog=pad_lanes(row(ssd_a_log)), dexp=row(jnp.repeat(ssd_d, SSD_HEAD_DIM)), ng=row(ssd_norm_g),
               expand=jnp.asarray(np.arange(LANES)[:, None] == head_of_channel[None, :], dtype=BF16))
    xa = dict(wq=wq, wg=wxg)
    merge = dict(w=wo, g=row(ln_g), b=row(ln_b))
    return rg, ssd, xa, merge


PROMPT_RG_TILE = 256
PROMPT_XA_TILE = 512
SSD_TILE = 512
PREP_BLOCK = 512
MEMKV_BATCHES = 2
MERGE_TILE = 256
MERGE_SUB_ROWS = 256
SAMPLE_RG_SEQS = 32
SAMPLE_SSD_SEQS = 16


def kernel(x_prompt, x_sample, mem_prompt, state_rg_conv, state_rg_h, state_ssd_conv, state_ssd_h,
           cache_mem_k, cache_mem_v, w_in, rg_conv_w, rg_conv_b, w_rg_a, b_rg_a, w_rg_i, b_rg_i,
           rg_lambda, ssd_conv_w, ssd_conv_b, ssd_dt_bias, ssd_a_log, ssd_d, ssd_norm_g, w_mem_k,
           w_mem_v, w_out, ln_g, ln_b):
    assert w_in.shape[0] == DEPTH
    bp, lp, d = x_prompt.shape
    bs, ls, _ = x_sample.shape
    heads = ssd_d.shape[1]
    outs = {k: [] for k in ("rgc_p", "rgh_p", "sc_p", "sh_p", "mk_p", "mv_p", "rgc_s", "rgh_s", "sc_s", "sh_s")}
    yp, ys = x_prompt, x_sample
    pad_rows = lambda v, before, after: jnp.pad(v, ((0, 0), (before, after), (0, 0)))
    for l in range(DEPTH):
        rg, ssd, xa, merge = _layer_params(
            w_in[l], rg_conv_w[l], rg_conv_b[l], w_rg_a[l], b_rg_a[l], w_rg_i[l], b_rg_i[l], rg_lambda[l],
            ssd_conv_w[l], ssd_conv_b[l], ssd_dt_bias[l], ssd_a_log[l], ssd_d[l], ssd_norm_g[l],
            w_out[l], ln_g[l], ln_b[l])
        xs_pad = pad_rows(ys, 0, SAMPLE_PAD - ls)
        mk, mv, mkb, mvb = _memkv(mem_prompt, w_mem_k[l].astype(BF16), w_mem_v[l].astype(BF16))
        rg_o, rgc, rgh = _rg_prompt(yp, rg, PROMPT_RG_TILE)
        xa_o, xa_s = _xattn(yp, xs_pad, xa["wq"], xa["wg"], mkb, mvb, cache_mem_k[l], cache_mem_v[l],
                            PROMPT_XA_TILE)
        yp, sc, sh = _ssd_merge_prompt(yp, ssd, rg_o, xa_o, merge)
        outs["rgc_p"].append(rgc)
        outs["rgh_p"].append(rgh.reshape(bp, -1))
        outs["sc_p"].append(sc)
        outs["sh_p"].append(sh.reshape(bp, heads, SSD_HEAD_DIM, SSD_STATE))
        outs["mk_p"].append(mk)
        outs["mv_p"].append(mv)
        tail = SAMPLE_PAD - (CONV_W - 1)
        rg_o, rgc, rgh = _rg_sample(xs_pad, rg, pad_rows(state_rg_conv[l], tail, 0),
                                    state_rg_h[l][:, None, :], SAMPLE_RG_SEQS, ls)
        ssd_o, sc, sh = _ssd_sample(xs_pad, ssd, pad_rows(state_ssd_conv[l], tail, 0),
                                    state_ssd_h[l].reshape(bs, heads * SSD_HEAD_DIM, SSD_STATE),
                                    SAMPLE_SSD_SEQS, ls)
        flat = lambda v: v.reshape(bs * SAMPLE_PAD, v.shape[-1])
        ys_pad = _merge(flat(rg_o), flat(ssd_o), flat(xa_s), flat(xs_pad), merge["w"], merge["g"], merge["b"],
                        MERGE_TILE, "merge_sample").reshape(bs, SAMPLE_PAD, d)
        ys = ys_pad[:, :ls, :]
        outs["rgc_s"].append(rgc)
        outs["rgh_s"].append(rgh.reshape(bs, -1))
        outs["sc_s"].append(sc)
        outs["sh_s"].append(sh.reshape(bs, heads, SSD_HEAD_DIM, SSD_STATE))
    st = lambda k: jnp.stack(outs[k])
    return (yp, ys, st("rgc_p"), st("rgh_p"), st("sc_p"), st("sh_p"), st("mk_p"), st("mv_p"),
            st("rgc_s"), st("rgh_s"), st("sc_s"), st("sh_s"))
```

```python
import functools
from typing import NamedTuple

import jax
import jax.numpy as jnp
import numpy as np
from jax import lax
from jax.experimental import pallas as pl
from jax.experimental.pallas import tpu as pltpu

F32 = jnp.float32
BF16 = jnp.bfloat16

SUBLANES = 8
LANES = 128
MXU_WIDTH = 256
VMEM_LIMIT_BYTES = 56 * 1024 * 1024

RG_C = 8.0
CONV_W = 4
RG_BLOCKS = 8
SSD_HEAD_DIM = 64
SSD_GROUPS = 2
SSD_STATE = 128
SSD_CHUNK = 128
MEM_HEADS = 4
LN_EPS = 1e-5
RMS_EPS = 1e-5
DEPTH = 1
DEEPNORM_ALPHA = (2 * DEPTH) ** 0.25
LOG2_E = 1.4426950408889634
SAMPLE_PAD = SUBLANES


def _cparams(*sem):
    return pltpu.CompilerParams(dimension_semantics=sem, vmem_limit_bytes=VMEM_LIMIT_BYTES)


def _mm(a, b):
    return jnp.dot(a.astype(BF16), b.astype(BF16), preferred_element_type=F32)


def _mm_nt(a, b):
    return lax.dot_general(a.astype(BF16), b.astype(BF16), (((1,), (1,)), ((), ())),
                           preferred_element_type=F32)


def _mm_tn(a, b):
    return lax.dot_general(a.astype(BF16), b.astype(BF16), (((0,), (0,)), ((), ())),
                           preferred_element_type=F32)


def _exp_neg(x):
    return jnp.exp2(x * (-LOG2_E))


def _sigmoid(x):
    return 1.0 / (1.0 + _exp_neg(x))


def _silu(x):
    return x * _sigmoid(x)


def _softplus(x):
    return jnp.maximum(x, 0.0) + jnp.log(1.0 + jnp.exp(-jnp.abs(x)))


def _causal_conv_tiles(x3, p3, w, b):
    row = lax.broadcasted_iota(jnp.int32, x3.shape, 1)
    y = x3 * w[CONV_W - 1:CONV_W][None]
    for s in range(1, CONV_W):
        shifted = jnp.where(row >= s, pltpu.roll(x3, s, axis=1), pltpu.roll(p3, s, axis=1))
        y = y + shifted * w[CONV_W - 1 - s:CONV_W - s][None]
    return y + b[None]


def _scan_in_tiles(a3, b3):
    row = lax.broadcasted_iota(jnp.int32, a3.shape, 1)
    s = 1
    while s < SUBLANES:
        keep = row >= s
        a_sh = jnp.where(keep, pltpu.roll(a3, s, axis=1), 1.0)
        b_sh = jnp.where(keep, pltpu.roll(b3, s, axis=1), 0.0)
        b3 = a3 * b_sh + b3
        a3 = a3 * a_sh
        s *= 2
    return a3, b3


def _rg_gates(u, wgate, ba, bi, lam):
    pre = _mm(u, wgate)
    r = _sigmoid(pre[:, :LANES] + ba)
    i = _sigmoid(pre[:, LANES:] + bi)
    neg_log_a = r * (RG_C * _softplus(-lam))
    a = _exp_neg(neg_log_a)
    v = jnp.tanh(neg_log_a) * (1.0 + a * a)
    mult = jnp.where(v > 0.0, v * lax.rsqrt(v), 0.0)
    return a, mult * (i * u)


def _time_strided_conv(x3, tail3, w, b):
    slabs = x3.shape[0]
    row = lax.broadcasted_iota(jnp.int32, tail3.shape, 1)
    wrapped = jnp.where(row >= 1, pltpu.roll(x3[slabs - (CONV_W - 1):], 1, axis=1), pltpu.roll(tail3, 1, axis=1))
    y = x3 * w[CONV_W - 1:CONV_W][None]
    for s in range(1, CONV_W):
        shifted = jnp.concatenate([wrapped[CONV_W - 1 - s:], x3[:slabs - s]], axis=0)
        y = y + shifted * w[CONV_W - 1 - s:CONV_W - s][None]
    return y + b[None]


def _rg_prompt_kernel(x_ref, perm_ref, permt_ref, wx_ref, wg_ref, cw_ref, cb_ref, wgate_ref, ba_ref,
                      bi_ref, lam_ref, out_ref, conv_ref, hlast_ref, tail_scr, h_scr, proj_scr, outp_scr, *,
                      tiles_per_seq):
    s = pl.program_id(0)

    @pl.when(s == 0)
    def _():
        proj_scr[...] = jnp.zeros_like(proj_scr)

    @pl.when(jnp.logical_or(s == 0, s % tiles_per_seq == 1 % tiles_per_seq))
    def _():
        tail_scr[...] = jnp.zeros_like(tail_scr)
        h_scr[...] = jnp.zeros_like(h_scr)

    rows = x_ref.shape[0]
    slabs = rows // SUBLANES
    width = wx_ref.shape[1]
    per_group = MXU_WIDTH // LANES
    row = lax.broadcasted_iota(jnp.int32, (SUBLANES, LANES), 0)
    xb = jnp.dot(perm_ref[...], x_ref[...].astype(BF16), preferred_element_type=F32).astype(BF16)

    for cg in range(width // MXU_WIDTH):
        gs = slice(cg * MXU_WIDTH, (cg + 1) * MXU_WIDTH)
        for kk in range(per_group):
            k = cg * per_group + kk
            ks = slice(k * LANES, (k + 1) * LANES)
            x3 = proj_scr[:, ks].reshape(slabs, SUBLANES, LANES)
            tail3 = tail_scr[:, :, ks]
            last3 = x3[slabs - (CONV_W - 1):]
            tail_scr[:, :, ks] = last3
            for i in range(CONV_W - 1):
                conv_ref[0, i:i + 1, ks] = last3[i, SUBLANES - 1:, :]
            u3 = _time_strided_conv(x3, tail3, cw_ref[:, ks], cb_ref[:, ks])
            a, b = _rg_gates(u3.reshape(rows, LANES), wgate_ref[k], ba_ref[:, ks], bi_ref[:, ks], lam_ref[:, ks])
            a3 = a.reshape(slabs, SUBLANES, LANES)
            b3 = b.reshape(slabs, SUBLANES, LANES)
            h_loc, a_cum = [b3[0]], [a3[0]]
            for j in range(1, slabs):
                h_loc.append(a3[j] * h_loc[j - 1] + b3[j])
                a_cum.append(a3[j] * a_cum[j - 1])
            a_run, h_run = _scan_in_tiles(a_cum[-1][None], h_loc[-1][None])
            h_prev = h_scr[:, ks]
            h_end = h_run[0] + a_run[0] * h_prev
            h_in = jnp.where(row >= 1, pltpu.roll(h_end, 1, axis=0), h_prev)
            h_scr[:, ks] = h_end[SUBLANES - 1:]
            hlast_ref[0, :, ks] = h_end[SUBLANES - 1:]
            h = jnp.concatenate([h_loc[j] + a_cum[j] * h_in for j in range(slabs)], axis=0)
            gate = proj_scr[:, width + k * LANES:width + (k + 1) * LANES]
            outp_scr[:, ks] = (h * _silu(gate)).astype(BF16)
        proj_scr[:, gs] = jnp.dot(xb, wx_ref[:, gs], preferred_element_type=F32)
        proj_scr[:, width + cg * MXU_WIDTH:width + (cg + 1) * MXU_WIDTH] = jnp.dot(
            xb, wg_ref[:, gs], preferred_element_type=F32)

    out_ref[...] = jnp.dot(permt_ref[...], outp_scr[...], preferred_element_type=F32).astype(out_ref.dtype)


def _rg_sample_kernel(x_ref, wx_ref, wg_ref, cw_ref, cb_ref, wgate_ref, ba_ref, bi_ref, lam_ref,
                      cstate_ref, h0_ref, out_ref, conv_ref, hlast_ref, *, valid):
    seqs = x_ref.shape[0]
    rows = seqs * SUBLANES
    xb = x_ref[...].reshape(rows, x_ref.shape[2]).astype(BF16)
    rgx = jnp.dot(xb, wx_ref[...], preferred_element_type=F32)
    gate = jnp.dot(xb, wg_ref[...], preferred_element_type=F32)
    x3 = rgx.reshape(seqs, SUBLANES, rgx.shape[1])
    conv_ref[...] = x3[:, valid - (CONV_W - 1):valid, :]
    row = lax.broadcasted_iota(jnp.int32, (seqs, SUBLANES, LANES), 1)
    for k in range(RG_BLOCKS):
        ks = slice(k * LANES, (k + 1) * LANES)
        u3 = _causal_conv_tiles(x3[:, :, ks], cstate_ref[:, :, ks], cw_ref[:, ks], cb_ref[:, ks])
        u = u3.reshape(rows, LANES)
        a, b = _rg_gates(u, wgate_ref[k], ba_ref[:, ks], bi_ref[:, ks], lam_ref[:, ks])
        a3 = a.reshape(seqs, SUBLANES, LANES)
        b3 = b.reshape(seqs, SUBLANES, LANES)
        b3 = b3 + jnp.where(row == 0, a3 * h0_ref[:, :, ks], 0.0)
        _, h3 = _scan_in_tiles(a3, b3)
        hlast_ref[:, :, ks] = h3[:, valid - 1:valid, :]
        g3 = gate[:, ks].reshape(seqs, SUBLANES, LANES)
        out_ref[:, :, ks] = (h3 * _silu(g3)).astype(out_ref.dtype)


def _group_rmsnorm(y, gain):
    width = y.shape[-1] // SSD_GROUPS
    parts = []
    for g in range(SSD_GROUPS):
        yg = y[..., g * width:(g + 1) * width]
        ms = jnp.sum(yg * yg, axis=-1, keepdims=True) * (1.0 / width)
        parts.append(yg * lax.rsqrt(ms + RMS_EPS))
    return jnp.concatenate(parts, axis=-1) * gain


def _time_strided_cumsum(x):
    slabs = x.shape[0] // SUBLANES
    x3 = x.reshape(slabs, SUBLANES, x.shape[1])
    acc = [x3[0]]
    for j in range(1, slabs):
        acc.append(acc[j - 1] + x3[j])
    _, run = _scan_in_tiles(jnp.ones_like(acc[-1])[None], acc[-1][None])
    row = lax.broadcasted_iota(jnp.int32, run[0].shape, 0)
    before = jnp.where(row >= 1, pltpu.roll(run[0], 1, axis=0), 0.0)
    return jnp.concatenate([a + before for a in acc], axis=0)


def _ssd_merge_prompt_kernel(x_ref, perm_ref, permt_ref, wxbc_ref, wz_ref, wdt_ref, cw_ref, cb_ref, dtb_ref,
                             alog_ref, dexp_ref, ng_ref, rg_ref, xa_ref, xres_ref, wout_ref, lng_ref, lnb_ref,
                             y_ref, conv_ref, h_ref, tail_scr, proj_scr, xb_scr, h_scr, ssd_scr, xbc_scr, *,
                             tiles_per_seq, n_tiles):
    s = pl.program_id(0)

    @pl.when(s == 0)
    def _():
        proj_scr[...] = jnp.zeros_like(proj_scr)
        xb_scr[...] = jnp.zeros_like(xb_scr)
        ssd_scr[...] = jnp.zeros_like(ssd_scr)

    @pl.when(jnp.logical_or(s == 0, s % tiles_per_seq == 1 % tiles_per_seq))
    def _():
        tail_scr[...] = jnp.zeros_like(tail_scr)
        h_scr[...] = jnp.zeros_like(h_scr)

    q = SSD_CHUNK
    n_sub = x_ref.shape[0] // q
    slabs = q // SUBLANES
    d_ssd = wz_ref.shape[1]
    d_conv = wxbc_ref.shape[1]
    gn = SSD_GROUPS * SSD_STATE
    tail3 = tail_scr[...]
    for c in range(n_sub):
        x3 = proj_scr[c * q:(c + 1) * q, :d_conv].reshape(slabs, SUBLANES, d_conv)
        xbc_scr[c * q:(c + 1) * q, :] = _silu(_time_strided_conv(x3, tail3, cw_ref[...], cb_ref[...])).reshape(
            q, d_conv)
        tail3 = x3[slabs - (CONV_W - 1):]
    tail_scr[...] = tail3
    dtr_all = proj_scr[:, d_conv:]

    mix, row0 = None, 0
    for src in (rg_ref, ssd_scr, xa_ref):
        part = jnp.dot(src[...], wout_ref[row0:row0 + src.shape[1], :], preferred_element_type=F32)
        mix = part if mix is None else mix + part
        row0 += src.shape[1]
    res = DEEPNORM_ALPHA * xres_ref[...] + mix
    mu = jnp.mean(res, axis=-1, keepdims=True)
    cen = res - mu
    var = jnp.mean(cen * cen, axis=-1, keepdims=True)
    y_ref[...] = cen * lax.rsqrt(var + LN_EPS) * lng_ref[...] + lnb_ref[...]

    xin = x_ref[...].astype(BF16)
    xb_new = jnp.concatenate(
        [jnp.dot(perm_ref[...], xin[c * q:(c + 1) * q], preferred_element_type=F32).astype(BF16)
         for c in range(n_sub)], axis=0)
    z_all = jnp.dot(xb_scr[...], wz_ref[...], preferred_element_type=F32)
    xb_scr[...] = xb_new

    ii = lax.broadcasted_iota(jnp.int32, (q, q), 0)
    jj = lax.broadcasted_iota(jnp.int32, (q, q), 1)
    time_of = lambda r: (r % SUBLANES) * slabs + r // SUBLANES
    causal = time_of(ii) >= time_of(jj)
    lane = lax.broadcasted_iota(jnp.int32, (q, LANES), 1)
    lo = lane < SSD_HEAD_DIM
    srow = lax.broadcasted_iota(jnp.int32, (LANES, SSD_STATE), 0) < SSD_HEAD_DIM
    heads = d_ssd // SSD_HEAD_DIM
    pairs = heads // 2
    pairs_per_group = pairs // SSD_GROUPS

    projections = [(wxbc_ref, slice(g * MXU_WIDTH, (g + 1) * MXU_WIDTH), slice(g * MXU_WIDTH, (g + 1) * MXU_WIDTH))
                   for g in range(d_conv // MXU_WIDTH)]
    projections.append((wdt_ref, slice(None), slice(d_conv, None)))
    every = (n_sub * pairs) // len(projections)
    assert every >= 1
    issued = 0

    for c in range(n_sub):
        cs = slice(c * q, (c + 1) * q)
        sx = xbc_scr[cs, :d_ssd]
        bm = xbc_scr[cs, d_ssd:d_ssd + gn]
        cm = xbc_scr[cs, d_ssd + gn:]
        dt = _softplus(dtr_all[cs] + dtb_ref[...])
        da = dt * (-jnp.exp(alog_ref[...]))
        acum = _time_strided_cumsum(da) * LOG2_E
        alast = acum[q - 1:q, :]
        wgt = dt * jnp.exp2(alast - acum)
        tot = jnp.exp2(alast)
        acum_dt_t = (acum - jnp.log2(dt)).T
        cb = [_mm_nt(cm[:, g * SSD_STATE:(g + 1) * SSD_STATE], bm[:, g * SSD_STATE:(g + 1) * SSD_STATE])
              for g in range(SSD_GROUPS)]
        y_parts = []
        for g in range(SSD_GROUPS):
            ns = slice(g * SSD_STATE, (g + 1) * SSD_STATE)
            rows_g = slice(g * pairs_per_group * LANES, (g + 1) * pairs_per_group * LANES)
            h_in = h_scr[rows_g, :]
            y_off_g = _mm_nt(cm[:, ns], h_in)
            xws, decays = [], []
            for pl_ in range(pairs_per_group):
                pq = g * pairs_per_group + pl_
                ps = slice(pq * LANES, (pq + 1) * LANES)
                xq = sx[:, ps]
                ms, es, ws, ts = [], [], [], []
                for h in (2 * pq, 2 * pq + 1):
                    acol = jnp.broadcast_to(acum[:, h:h + 1], (q, q))
                    arow = jnp.broadcast_to(acum_dt_t[h:h + 1, :], (q, q))
                    decay_dt = jnp.exp2(jnp.where(causal, acol - arow, -jnp.inf))
                    ms.append((cb[g] * decay_dt).astype(BF16))
                    es.append(jnp.exp2(jnp.broadcast_to(acum[:, h:h + 1], (q, LANES))))
                    ws.append(jnp.broadcast_to(wgt[:, h:h + 1], (q, LANES)))
                    ts.append(jnp.broadcast_to(tot[:, h:h + 1], (LANES, SSD_STATE)))
                lhs = jnp.concatenate(ms, axis=1)
                rhs = jnp.concatenate([jnp.where(lo, xq, 0.0), jnp.where(lo, 0.0, xq)], axis=0)
                y_diag = _mm(lhs, rhs)
                y_off = y_off_g[:, pl_ * LANES:(pl_ + 1) * LANES] * jnp.where(lo, es[0], es[1])
                xws.append(xq * jnp.where(lo, ws[0], ws[1]))
                decays.append(jnp.where(srow, ts[0], ts[1]))
                y_parts.append(y_diag + y_off + dexp_ref[:, ps] * xq)
                if (c * pairs + pq + 1) % every == 0 and issued < len(projections):
                    w_ref, src, dst = projections[issued]
                    proj_scr[:, dst] = jnp.dot(xb_new, w_ref[:, src], preferred_element_type=F32)
                    issued += 1
            h_scr[rows_g, :] = (h_in * jnp.concatenate(decays, axis=0)
                                + _mm_tn(jnp.concatenate(xws, axis=1), bm[:, ns]))
        y = jnp.concatenate(y_parts, axis=1) * _silu(z_all[cs])
        y = _group_rmsnorm(y, ng_ref[...]).astype(BF16)
        ssd_scr[cs, :] = jnp.dot(permt_ref[...], y, preferred_element_type=F32).astype(BF16)
    assert issued == len(projections)

    @pl.when(jnp.logical_and(s % tiles_per_seq == 0, jnp.logical_and(s > 0, s <= n_tiles)))
    def _():
        h_ref[0] = h_scr[...]
        for i in range(CONV_W - 1):
            conv_ref[0, i:i + 1, :] = tail_scr[i, SUBLANES - 1:, :]


def _ssd_sample_kernel(x_ref, wxbc_ref, wz_ref, wdt_ref, cw_ref, cb_ref, dtb_ref, alog_ref, dexp_ref,
                       ng_ref, expand_ref, cstate_ref, h0_ref, out_ref, conv_ref, h_ref,
                       c_scr, b_scr, xw_scr, tot_scr, yoff_scr, *, valid):
    seqs = x_ref.shape[0]
    rows = seqs * SUBLANES
    d_ssd = wz_ref.shape[1]
    gn = SSD_GROUPS * SSD_STATE
    heads = d_ssd // SSD_HEAD_DIM
    hg = heads // SSD_GROUPS
    xb = x_ref[...].reshape(rows, x_ref.shape[2]).astype(BF16)
    xbc_raw = jnp.dot(xb, wxbc_ref[...], preferred_element_type=F32)
    z = jnp.dot(xb, wz_ref[...], preferred_element_type=F32)
    dtr = jnp.dot(xb, wdt_ref[...], preferred_element_type=F32)

    x3 = xbc_raw.reshape(seqs, SUBLANES, xbc_raw.shape[1])
    conv_ref[...] = x3[:, valid - (CONV_W - 1):valid, :]
    xbc3 = _silu(_causal_conv_tiles(x3, cstate_ref[...], cw_ref[...], cb_ref[...]))
    sx3 = xbc3[:, :, :d_ssd]
    b3 = xbc3[:, :, d_ssd:d_ssd + gn]
    c3 = xbc3[:, :, d_ssd + gn:]

    row = lax.broadcasted_iota(jnp.int32, (seqs, SUBLANES, LANES), 1)
    lane = lax.broadcasted_iota(jnp.int32, (seqs, SUBLANES, LANES), 2)
    dt3 = jnp.where(row < valid, _softplus(dtr + dtb_ref[...]).reshape(seqs, SUBLANES, LANES), 0.0)
    da3 = dt3 * (-jnp.exp(alog_ref[...]))[None]
    ones = jnp.ones_like(da3)
    _, acum3 = _scan_in_tiles(ones, da3)
    alast = acum3[:, SUBLANES - 1:, :]
    wgt3 = dt3 * jnp.exp(alast - acum3)
    tot_scr[...] = jnp.exp(alast)
    e3 = jnp.exp(acum3)

    coefs = []
    for u in range(valid):
        prod = c3 * b3[:, u:u + 1, :]
        cbu = [jnp.sum(prod[:, :, g * SSD_STATE:(g + 1) * SSD_STATE], axis=-1, keepdims=True)
               for g in range(SSD_GROUPS)]
        cb_heads = jnp.where(lane < hg, cbu[0], cbu[1])
        coefs.append(jnp.where(row >= u, cb_heads * jnp.exp(acum3 - acum3[:, u:u + 1, :]) * dt3[:, u:u + 1, :],
                               0.0))

    per_head = coefs + [wgt3, e3]
    stacked = jnp.concatenate([v.reshape(rows, LANES) for v in per_head], axis=0)
    expanded = None
    rest = stacked
    for _ in range(3):
        piece = rest.astype(BF16)
        rest = rest - piece.astype(F32)
        part = jnp.dot(piece, expand_ref[...], preferred_element_type=F32)
        expanded = part if expanded is None else expanded + part
    expanded = [expanded[i * rows:(i + 1) * rows].reshape(seqs, SUBLANES, d_ssd) for i in range(len(per_head))]
    y_diag = expanded[0] * sx3[:, 0:1, :]
    for u in range(1, valid):
        y_diag = y_diag + expanded[u] * sx3[:, u:u + 1, :]
    wgt_x, e_x = expanded[valid], expanded[valid + 1]

    c_scr[...] = c3
    b_scr[...] = b3
    xw_scr[...] = sx3 * wgt_x
    srow = lax.broadcasted_iota(jnp.int32, (2 * SSD_HEAD_DIM, SSD_STATE), 0) < SSD_HEAD_DIM

    def per_seq(s, carry):
        cs = c_scr[s]
        bs = b_scr[s]
        xws = xw_scr[s]
        tots = tot_scr[s]
        for g in range(SSD_GROUPS):
            gs = slice(g * SSD_STATE, (g + 1) * SSD_STATE)
            width = hg * SSD_HEAD_DIM
            cols = slice(g * width, (g + 1) * width)
            hin = h0_ref[s, cols, :]
            yoff_scr[s, :, cols] = _mm_nt(cs[:, gs], hin)
            upd = _mm_tn(xws[:, cols], bs[:, gs])
            for pq in range(hg // 2):
                h = g * hg + 2 * pq
                rs = slice(pq * LANES, (pq + 1) * LANES)
                t0 = jnp.broadcast_to(tots[0:1, h:h + 1], (LANES, SSD_STATE))
                t1 = jnp.broadcast_to(tots[0:1, h + 1:h + 2], (LANES, SSD_STATE))
                h_ref[s, g * width + pq * LANES:g * width + (pq + 1) * LANES, :] = (
                    hin[rs] * jnp.where(srow, t0, t1) + upd[rs])
        return carry

    lax.fori_loop(0, seqs, per_seq, 0)

    y = y_diag + yoff_scr[...] * e_x + dexp_ref[...][None] * sx3
    y = y * _silu(z.reshape(seqs, SUBLANES, d_ssd))
    out_ref[...] = _group_rmsnorm(y, ng_ref[...][None]).astype(out_ref.dtype)


def _memkv_kernel(mem_ref, wk_ref, wv_ref, k_ref, v_ref, kb_ref, vb_ref):
    nb, m, d = mem_ref.shape
    mb = mem_ref[...].reshape(nb * m, d).astype(BF16)
    for w_ref, o_ref, ob_ref in ((wk_ref, k_ref, kb_ref), (wv_ref, v_ref, vb_ref)):
        proj = jnp.dot(mb, w_ref[...], preferred_element_type=F32)
        for i in range(nb):
            o_ref[i] = proj[i * m:(i + 1) * m].reshape(o_ref.shape[1:])
        ob_ref[...] = proj.astype(BF16).reshape(ob_ref.shape)


def _softmax_terms(scores):
    p = jnp.exp(scores - jnp.max(scores, axis=-1, keepdims=True))
    return p, jnp.sum(p, axis=-1, keepdims=True)


def _prompt_attention_head(h, q_ref, rows, k_ref, v_ref, out_ref):
    c = k_ref.shape[2]
    d_head = c // MEM_HEADS
    hs = slice(h * d_head, (h + 1) * d_head)

    def scores():
        return _softmax_terms(_mm_nt(q_ref[:rows, hs], k_ref[0, :, hs]) * (d_head ** -0.5))

    def output(p, l):
        o = _mm(p, v_ref[0, :, hs]) * (1.0 / l)
        gate = q_ref[:rows, c + h * d_head:c + (h + 1) * d_head]
        out_ref[:, hs] = (o * _silu(gate)).astype(out_ref.dtype)

    return scores, output


def _sample_attention_seq(s, q_ref, rows, k_ref, v_ref, out_ref):
    _, n_mem, heads, d_head = k_ref.shape
    c = heads * d_head
    rs = slice(rows + s * SUBLANES, rows + (s + 1) * SUBLANES)

    def scores():
        shape = (heads * SUBLANES, n_mem * heads)
        same_head = (lax.broadcasted_iota(jnp.int32, shape, 0) // SUBLANES
                     == lax.broadcasted_iota(jnp.int32, shape, 1) % heads)
        qh = jnp.concatenate([q_ref[rs, h * d_head:(h + 1) * d_head] for h in range(heads)], axis=0)
        sc = _mm_nt(qh, k_ref[s].reshape(n_mem * heads, d_head)) * (d_head ** -0.5)
        return _softmax_terms(jnp.where(same_head, sc, -jnp.inf))

    def output(p, l):
        o = _mm(p, v_ref[s].reshape(n_mem * heads, d_head)) * (1.0 / l)
        o = jnp.concatenate([o[h * SUBLANES:(h + 1) * SUBLANES] for h in range(heads)], axis=1)
        out_ref[s] = (o * _silu(q_ref[rs, c:])).astype(out_ref.dtype)

    return scores, output


def _xattn_kernel(x_ref, xs_ref, wq_ref, wg_ref, k_ref, v_ref, ks_ref, vs_ref, out_ref, outs_ref):
    rows = x_ref.shape[0]
    seqs, _, d = xs_ref.shape
    xb = jnp.concatenate([x_ref[...], xs_ref[...].reshape(seqs * SUBLANES, d)], axis=0).astype(BF16)
    qg = jnp.concatenate([jnp.dot(xb, wq_ref[...], preferred_element_type=F32),
                          jnp.dot(xb, wg_ref[...], preferred_element_type=F32)], axis=1)
    items = [_sample_attention_seq(i, qg, rows, ks_ref, vs_ref, outs_ref) for i in range(seqs)]
    items += [_prompt_attention_head(h, qg, rows, k_ref, v_ref, out_ref) for h in range(MEM_HEADS)]
    pending = None
    for scores, output in items:
        terms = scores()
        if pending is not None:
            pending[0](*pending[1])
        pending = (output, terms)
    pending[0](*pending[1])


def _merge_kernel(rg_ref, ssd_ref, xa_ref, x_ref, w_ref, g_ref, b_ref, y_ref, *, sub_rows):
    d = rg_ref.shape[1]
    for r0 in range(0, x_ref.shape[0], sub_rows):
        rs = slice(r0, r0 + sub_rows)
        mix = (jnp.dot(rg_ref[rs, :].astype(BF16), w_ref[0:d, :], preferred_element_type=F32)
               + jnp.dot(ssd_ref[rs, :].astype(BF16), w_ref[d:2 * d, :], preferred_element_type=F32)
               + jnp.dot(xa_ref[rs, :].astype(BF16), w_ref[2 * d:3 * d, :], preferred_element_type=F32))
        res = DEEPNORM_ALPHA * x_ref[rs, :] + mix
        mu = jnp.mean(res, axis=-1, keepdims=True)
        cen = res - mu
        var = jnp.mean(cen * cen, axis=-1, keepdims=True)
        y_ref[rs, :] = cen * lax.rsqrt(var + LN_EPS) * g_ref[...] + b_ref[...]


def _full(shape):
    return pl.BlockSpec(shape, lambda *_: (0,) * len(shape))


def _time_stride_perm(rows):
    p = np.arange(rows)
    t = (p % SUBLANES) * (rows // SUBLANES) + p // SUBLANES
    perm = t[:, None] == np.arange(rows)[None, :]
    return jnp.asarray(perm, dtype=BF16), jnp.asarray(perm.T, dtype=BF16)


def _rg_prompt(x, p, tile):
    b, l, d = x.shape
    c = p["wx"].shape[1]
    tiles_per_seq = l // tile
    n_tiles = b * tiles_per_seq
    perm, perm_t = _time_stride_perm(tile)
    done = lambda s: jnp.maximum(s - 1, 0)
    out, conv, hlast = pl.pallas_call(
        functools.partial(_rg_prompt_kernel, tiles_per_seq=tiles_per_seq),
        grid=(n_tiles + 1,),
        in_specs=[pl.BlockSpec((tile, d), lambda s: (jnp.minimum(s, n_tiles - 1), 0)),
                  _full(perm.shape), _full(perm.shape), p["wx"].spec, p["wg"].spec,
                  _full(p["cw"].shape), _full(p["cb"].shape), _full(p["wgate"].shape), _full(p["ba"].shape),
                  _full(p["bi"].shape), _full(p["lam"].shape)],
        out_specs=[pl.BlockSpec((tile, c), lambda s: (done(s), 0)),
                   pl.BlockSpec((1, CONV_W - 1, c), lambda s: (done(s) // tiles_per_seq, 0, 0)),
                   pl.BlockSpec((1, 1, c), lambda s: (done(s) // tiles_per_seq, 0, 0))],
        out_shape=[jax.ShapeDtypeStruct((b * l, c), BF16),
                   jax.ShapeDtypeStruct((b, CONV_W - 1, c), F32),
                   jax.ShapeDtypeStruct((b, 1, c), F32)],
        scratch_shapes=[pltpu.VMEM((CONV_W - 1, SUBLANES, c), F32), pltpu.VMEM((1, c), F32),
                        pltpu.VMEM((tile, 2 * c), F32), pltpu.VMEM((tile, c), BF16)],
        compiler_params=_cparams("arbitrary"),
        name="rg_prompt",
    )(x.reshape(b * l, d), perm, perm_t, p["wx"].array, p["wg"].array, p["cw"], p["cb"], p["wgate"], p["ba"],
      p["bi"], p["lam"])
    return out.reshape(b, l, c), conv, hlast


def _rg_sample(xpad, p, cstate, h0, seqs, valid):
    n, _, d = xpad.shape
    c = p["wx"].shape[1]
    blk = lambda w: pl.BlockSpec((seqs, w[0], w[1]), lambda i: (i, 0, 0))
    return pl.pallas_call(
        functools.partial(_rg_sample_kernel, valid=valid),
        grid=(n // seqs,),
        in_specs=[blk((SUBLANES, d)), p["wx"].spec, p["wg"].spec, _full(p["cw"].shape),
                  _full(p["cb"].shape), _full(p["wgate"].shape), _full(p["ba"].shape),
                  _full(p["bi"].shape), _full(p["lam"].shape), blk((SUBLANES, c)), blk((1, c))],
        out_specs=[blk((SUBLANES, c)), blk((CONV_W - 1, c)), blk((1, c))],
        out_shape=[jax.ShapeDtypeStruct((n, SUBLANES, c), F32),
                   jax.ShapeDtypeStruct((n, CONV_W - 1, c), F32),
                   jax.ShapeDtypeStruct((n, 1, c), F32)],
        compiler_params=_cparams("parallel"),
        name="rg_sample",
    )(xpad, p["wx"].array, p["wg"].array, p["cw"], p["cb"], p["wgate"], p["ba"], p["bi"], p["lam"], cstate, h0)


_SSD_WEIGHTS = ("wxbc", "wz", "wdt")
_SSD_SMALL = ("cw", "cb", "dtb", "alog", "dexp", "ng")


def _ssd_param_specs(p):
    return [p[k].spec for k in _SSD_WEIGHTS] + [_full(p[k].shape) for k in _SSD_SMALL]


def _ssd_param_args(p):
    return [p[k].array for k in _SSD_WEIGHTS] + [p[k] for k in _SSD_SMALL]


def _ssd_merge_prompt(x, p, rg_out, xa_out, mp):
    b, l, d = x.shape
    cc = p["wxbc"].shape[1]
    c = p["wz"].shape[1]
    tile = min(SSD_TILE, l)
    tiles_per_seq = l // tile
    n_tiles = b * tiles_per_seq
    perm, perm_t = _time_stride_perm(SSD_CHUNK)
    flat = lambda v: v.reshape(b * l, v.shape[-1])
    clamp = lambda i: jnp.clip(i, 0, n_tiles - 1)
    rows = lambda w, back: pl.BlockSpec((tile, w), lambda s: (clamp(s - back), 0))
    state = lambda shape: pl.BlockSpec((1,) + shape, lambda s: (clamp(s - 1) // tiles_per_seq, 0, 0))
    y, conv, hstate = pl.pallas_call(
        functools.partial(_ssd_merge_prompt_kernel, tiles_per_seq=tiles_per_seq, n_tiles=n_tiles),
        grid=(n_tiles + 2,),
        in_specs=[rows(d, 0), _full(perm.shape), _full(perm.shape)] + _ssd_param_specs(p)
        + [rows(rg_out.shape[-1], 2), rows(xa_out.shape[-1], 2), rows(d, 2), _full(mp["w"].shape),
           _full(mp["g"].shape), _full(mp["b"].shape)],
        out_specs=[rows(d, 2), state((CONV_W - 1, cc)), state((c, SSD_STATE))],
        out_shape=[jax.ShapeDtypeStruct((b * l, d), F32),
                   jax.ShapeDtypeStruct((b, CONV_W - 1, cc), F32),
                   jax.ShapeDtypeStruct((b, c, SSD_STATE), F32)],
        scratch_shapes=[pltpu.VMEM((CONV_W - 1, SUBLANES, cc), F32),
                        pltpu.VMEM((tile, cc + LANES), F32), pltpu.VMEM((tile, d), BF16),
                        pltpu.VMEM((c, SSD_STATE), F32), pltpu.VMEM((tile, c), BF16),
                        pltpu.VMEM((tile, cc), F32)],
        compiler_params=_cparams("arbitrary"),
        name="ssd_merge_prompt",
    )(flat(x), perm, perm_t, *_ssd_param_args(p), flat(rg_out), flat(xa_out), flat(x), mp["w"], mp["g"], mp["b"])
    return y.reshape(b, l, d), conv, hstate


def _ssd_sample(xpad, p, cstate, h0, seqs, valid):
    n, _, d = xpad.shape
    cc = p["wxbc"].shape[1]
    c = p["wz"].shape[1]
    gn = SSD_GROUPS * SSD_STATE
    blk = lambda w: pl.BlockSpec((seqs, w[0], w[1]), lambda i: (i, 0, 0))
    return pl.pallas_call(
        functools.partial(_ssd_sample_kernel, valid=valid),
        grid=(n // seqs,),
        in_specs=[blk((SUBLANES, d))] + _ssd_param_specs(p)
        + [_full(p["expand"].shape), blk((SUBLANES, cc)), blk((c, SSD_STATE))],
        out_specs=[blk((SUBLANES, c)), blk((CONV_W - 1, cc)), blk((c, SSD_STATE))],
        out_shape=[jax.ShapeDtypeStruct((n, SUBLANES, c), F32),
                   jax.ShapeDtypeStruct((n, CONV_W - 1, cc), F32),
                   jax.ShapeDtypeStruct((n, c, SSD_STATE), F32)],
        scratch_shapes=[pltpu.VMEM((seqs, SUBLANES, gn), F32), pltpu.VMEM((seqs, SUBLANES, gn), F32),
                        pltpu.VMEM((seqs, SUBLANES, c), F32), pltpu.VMEM((seqs, 1, LANES), F32),
                        pltpu.VMEM((seqs, SUBLANES, c), F32)],
        compiler_params=_cparams("parallel"),
        name="ssd_sample",
    )(xpad, *_ssd_param_args(p), p["expand"], cstate, h0)


def _memkv(mem, wk, wv):
    b, m, d = mem.shape
    c = wk.shape[1]
    nb = MEMKV_BATCHES if b % MEMKV_BATCHES == 0 else 1
    spec = pl.BlockSpec((nb, m, c), lambda i: (i, 0, 0))
    spec4 = pl.BlockSpec((nb, m, MEM_HEADS, c // MEM_HEADS), lambda i: (i, 0, 0, 0))
    return pl.pallas_call(
        _memkv_kernel,
        grid=(b // nb,),
        in_specs=[pl.BlockSpec((nb, m, d), lambda i: (i, 0, 0)), _full(wk.shape), _full(wv.shape)],
        out_specs=[spec4, spec4, spec, spec],
        out_shape=[jax.ShapeDtypeStruct((b, m, MEM_HEADS, c // MEM_HEADS), F32)] * 2
        + [jax.ShapeDtypeStruct((b, m, c), BF16)] * 2,
        compiler_params=_cparams("parallel"),
        name="mem_kv",
    )(mem, wk, wv)


def _xattn(x, xs_pad, wq, wg, k, v, ks, vs, tile):
    b, l, d = x.shape
    n = xs_pad.shape[0]
    c = wq.shape[1]
    m = k.shape[1]
    tile = min(tile, l)
    tiles_per_seq = l // tile
    n_tiles = b * tiles_per_seq
    seqs = n // n_tiles
    assert seqs * n_tiles == n
    kv_spec = pl.BlockSpec((1, m, c), lambda s: (s // tiles_per_seq, 0, 0))
    skv_spec = pl.BlockSpec((seqs,) + ks.shape[1:], lambda s: (s, 0, 0, 0))
    out, outs = pl.pallas_call(
        _xattn_kernel,
        grid=(n_tiles,),
        in_specs=[pl.BlockSpec((tile, d), lambda s: (s, 0)),
                  pl.BlockSpec((seqs, SUBLANES, d), lambda s: (s, 0, 0)),
                  wq.spec, wg.spec, kv_spec, kv_spec, skv_spec, skv_spec],
        out_specs=[pl.BlockSpec((tile, c), lambda s: (s, 0)),
                   pl.BlockSpec((seqs, SUBLANES, c), lambda s: (s, 0, 0))],
        out_shape=[jax.ShapeDtypeStruct((b * l, c), BF16), jax.ShapeDtypeStruct((n, SUBLANES, c), F32)],
        compiler_params=_cparams("parallel"),
        name="xattn",
    )(x.reshape(b * l, d), xs_pad, wq.array, wg.array, k, v, ks, vs)
    return out.reshape(b, l, c), outs


def _merge(rg, ssd, xa, x, w_out, ln_g, ln_b, tile, name):
    n, d = x.shape
    c = rg.shape[1]
    tile = min(tile, n)
    row = lambda w: pl.BlockSpec((tile, w), lambda i: (i, 0))
    return pl.pallas_call(
        functools.partial(_merge_kernel, sub_rows=min(MERGE_SUB_ROWS, tile)),
        grid=(n // tile,),
        in_specs=[row(c), row(c), row(c), row(d), _full(w_out.shape), _full(ln_g.shape), _full(ln_b.shape)],
        out_specs=row(d),
        out_shape=jax.ShapeDtypeStruct((n, d), F32),
        compiler_params=_cparams("parallel"),
        name=name,
    )(rg, ssd, xa, x, w_out, ln_g, ln_b)


class _Cols(NamedTuple):
    array: jax.Array
    width: int
    index: int

    @property
    def shape(self):
        return (self.array.shape[0], self.width)

    @property
    def spec(self):
        return pl.BlockSpec(self.shape, lambda *_, i=self.index: (0, i))


def _weight_prep_kernel(in_blk_ref, out_blk_ref, shift_ref, valid_ref, nxt_blk_ref, a_ref, b_ref, wo_ref,
                        o_ref, oo_ref, *, shift_rows):
    del in_blk_ref, out_blk_ref, nxt_blk_ref
    i = pl.program_id(0)
    rows = a_ref.shape[0]
    keep = lax.broadcasted_iota(jnp.int32, a_ref.shape, 0) < valid_ref[i]

    @pl.when(shift_ref[i] == 0)
    def _():
        o_ref[...] = jnp.where(keep, a_ref[...], 0.0).T.astype(BF16)

    @pl.when(shift_ref[i] != 0)
    def _():
        blk = jnp.concatenate([a_ref[shift_rows:, :], b_ref[:shift_rows, :]], axis=0)
        o_ref[...] = jnp.where(keep, blk, 0.0).T.astype(BF16)

    oo_ref[...] = wo_ref[...].astype(BF16)


def _weight_prep(w_in_t, w_out, bounds, placement):
    n_rows, k = w_in_t.shape
    blk = PREP_BLOCK
    in_blk, out_blk, shift, valid = [], [], [], []
    shifts = {lo % blk for lo, _ in bounds} - {0}
    assert len(shifts) <= 1
    shift_rows = shifts.pop() if shifts else SUBLANES
    assert shift_rows % SUBLANES == 0
    for (lo, hi), place in zip(bounds, placement):
        for j in range(-(-(hi - lo) // blk)):
            in_blk.append((lo + j * blk) // blk)
            out_blk.append(place + j)
            shift.append(lo % blk)
            valid.append(min(blk, hi - lo - j * blk))
    n_cols = (max(out_blk) + 1) * blk
    for gap in sorted(set(range(n_cols // blk)) - set(out_blk)):
        in_blk.append(0)
        out_blk.append(gap)
        shift.append(0)
        valid.append(0)
    steps = len(in_blk)
    last_in = -(-n_rows // blk) - 1
    oo_rows = -(-(-(-w_out.shape[0] // steps)) // SUBLANES) * SUBLANES
    oo_steps = -(-w_out.shape[0] // oo_rows)
    assert oo_steps <= steps
    nxt_blk = [min(a + 1, last_in) if sh else 0 for a, sh in zip(in_blk, shift)]
    tables = [jnp.asarray(np.asarray(t, np.int32)) for t in (in_blk, out_blk, shift, valid, nxt_blk)]
    wo_spec = pl.BlockSpec((oo_rows, w_out.shape[1]), lambda i, *_: (jnp.minimum(i, oo_steps - 1), 0))
    return pl.pallas_call(
        functools.partial(_weight_prep_kernel, shift_rows=shift_rows),
        grid_spec=pltpu.PrefetchScalarGridSpec(
            num_scalar_prefetch=5,
            grid=(steps,),
            in_specs=[pl.BlockSpec((blk, k), lambda i, ib, ob, sh, va, nb: (ib[i], 0)),
                      pl.BlockSpec((blk, k), lambda i, ib, ob, sh, va, nb: (nb[i], 0)),
                      wo_spec],
            out_specs=[pl.BlockSpec((k, blk), lambda i, ib, ob, sh, va, nb: (0, ob[i])), wo_spec],
        ),
        out_shape=[jax.ShapeDtypeStruct((k, n_cols), BF16), jax.ShapeDtypeStruct(w_out.shape, BF16)],
        compiler_params=_cparams("arbitrary"),
        name="weight_prep",
    )(*tables, w_in_t, w_in_t, w_out)


def _layer_params(w_in, rg_conv_w, rg_conv_b, w_rg_a, b_rg_a, w_rg_i, b_rg_i, rg_lambda, ssd_conv_w,
                  ssd_conv_b, ssd_dt_bias, ssd_a_log, ssd_d, ssd_norm_g, w_out, ln_g, ln_b):
    d_rg = rg_conv_w.shape[1]
    d_conv = ssd_conv_w.shape[1]
    d_ssd = ssd_norm_g.shape[0]
    heads = ssd_d.shape[0]
    sizes = (d_rg, d_rg, d_conv, d_ssd, heads)
    offs = [0]
    for s in sizes:
        offs.append(offs[-1] + s)
    d_xa = (w_in.shape[1] - offs[-1]) // 2
    offs += [offs[-1] + d_xa, offs[-1] + 2 * d_xa]
    blocks = lambda w: -(-w // PREP_BLOCK)
    assert d_rg == d_ssd == d_xa and blocks(d_conv) * PREP_BLOCK <= 2 * d_rg and heads <= LANES
    unit = blocks(d_rg)
    place = dict(xbc=0, z=2 * unit, rg_x=3 * unit, rg_g=4 * unit, xa_q=5 * unit, xa_g=6 * unit, dt=7 * unit)
    order = ("rg_x", "rg_g", "xbc", "z", "dt", "xa_q", "xa_g")
    wall, wo = _weight_prep(jnp.swapaxes(w_in, 0, 1), w_out, list(zip(offs[:-1], offs[1:])),
                            [place[k] for k in order])
    col = lambda key, width: _Cols(wall, width, place[key] * PREP_BLOCK // width)
    wx, wg, wxbc, wz = col("rg_x", d_rg), col("rg_g", d_rg), col("xbc", d_conv), col("z", d_ssd)
    wdt, wq, wxg = col("dt", LANES), col("xa_q", d_xa), col("xa_g", d_xa)
    row = lambda v: v.reshape(1, -1).astype(F32)
    pad_lanes = lambda v: jnp.pad(v, ((0, 0), (0, LANES - v.shape[1])))
    rg = dict(wx=wx, wg=wg, cw=rg_conv_w, cb=row(rg_conv_b),
              wgate=jnp.concatenate([w_rg_a, w_rg_i], axis=2).astype(BF16),
              ba=row(b_rg_a), bi=row(b_rg_i), lam=row(rg_lambda))
    head_of_channel = np.arange(d_ssd) // SSD_HEAD_DIM
    ssd = dict(wxbc=wxbc, wz=wz, wdt=wdt,
               cw=ssd_conv_w, cb=row(ssd_conv_b), dtb=pad_lanes(row(ssd_dt_bias)),
               alog=pad_lanes(row(ssd_a_log)), dexp=row(jnp.repeat(ssd_d, SSD_HEAD_DIM)), ng=row(ssd_norm_g),
               expand=jnp.asarray(np.arange(LANES)[:, None] == head_of_channel[None, :], dtype=BF16))
    xa = dict(wq=wq, wg=wxg)
    merge = dict(w=wo, g=row(ln_g), b=row(ln_b))
    return rg, ssd, xa, merge


PROMPT_RG_TILE = 256
PROMPT_XA_TILE = 512
SSD_TILE = 512
PREP_BLOCK = 512
MEMKV_BATCHES = 2
MERGE_TILE = 256
MERGE_SUB_ROWS = 256
SAMPLE_RG_SEQS = 32
SAMPLE_SSD_SEQS = 16


def kernel(x_prompt, x_sample, mem_prompt, state_rg_conv, state_rg_h, state_ssd_conv, state_ssd_h,
           cache_mem_k, cache_mem_v, w_in, rg_conv_w, rg_conv_b, w_rg_a, b_rg_a, w_rg_i, b_rg_i,
           rg_lambda, ssd_conv_w, ssd_conv_b, ssd_dt_bias, ssd_a_log, ssd_d, ssd_norm_g, w_mem_k,
           w_mem_v, w_out, ln_g, ln_b):
    assert w_in.shape[0] == DEPTH
    bp, lp, d = x_prompt.shape
    bs, ls, _ = x_sample.shape
    heads = ssd_d.shape[1]
    outs = {k: [] for k in ("rgc_p", "rgh_p", "sc_p", "sh_p", "mk_p", "mv_p", "rgc_s", "rgh_s", "sc_s", "sh_s")}
    yp, ys = x_prompt, x_sample
    pad_rows = lambda v, before, after: jnp.pad(v, ((0, 0), (before, after), (0, 0)))
    for l in range(DEPTH):
        rg, ssd, xa, merge = _layer_params(
            w_in[l], rg_conv_w[l], rg_conv_b[l], w_rg_a[l], b_rg_a[l], w_rg_i[l], b_rg_i[l], rg_lambda[l],
            ssd_conv_w[l], ssd_conv_b[l], ssd_dt_bias[l], ssd_a_log[l], ssd_d[l], ssd_norm_g[l],
            w_out[l], ln_g[l], ln_b[l])
        xs_pad = pad_rows(ys, 0, SAMPLE_PAD - ls)
        mk, mv, mkb, mvb = _memkv(mem_prompt, w_mem_k[l].astype(BF16), w_mem_v[l].astype(BF16))
        rg_o, rgc, rgh = _rg_prompt(yp, rg, PROMPT_RG_TILE)
        xa_o, xa_s = _xattn(yp, xs_pad, xa["wq"], xa["wg"], mkb, mvb, cache_mem_k[l], cache_mem_v[l],
                            PROMPT_XA_TILE)
        yp, sc, sh = _ssd_merge_prompt(yp, ssd, rg_o, xa_o, merge)
        outs["rgc_p"].append(rgc)
        outs["rgh_p"].append(rgh.reshape(bp, -1))
        outs["sc_p"].append(sc)
        outs["sh_p"].append(sh.reshape(bp, heads, SSD_HEAD_DIM, SSD_STATE))
        outs["mk_p"].append(mk)
        outs["mv_p"].append(mv)
        tail = SAMPLE_PAD - (CONV_W - 1)
        rg_o, rgc, rgh = _rg_sample(xs_pad, rg, pad_rows(state_rg_conv[l], tail, 0),
                                    state_rg_h[l][:, None, :], SAMPLE_RG_SEQS, ls)
        ssd_o, sc, sh = _ssd_sample(xs_pad, ssd, pad_rows(state_ssd_conv[l], tail, 0),
                                    state_ssd_h[l].reshape(bs, heads * SSD_HEAD_DIM, SSD_STATE),
                                    SAMPLE_SSD_SEQS, ls)
        flat = lambda v: v.reshape(bs * SAMPLE_PAD, v.shape[-1])
        ys_pad = _merge(flat(rg_o), flat(ssd_o), flat(xa_s), flat(xs_pad), merge["w"], merge["g"], merge["b"],
                        MERGE_TILE, "merge_sample").reshape(bs, SAMPLE_PAD, d)
        ys = ys_pad[:, :ls, :]
        outs["rgc_s"].append(rgc)
        outs["rgh_s"].append(rgh.reshape(bs, -1))
        outs["sc_s"].append(sc)
        outs["sh_s"].append(sh.reshape(bs, heads, SSD_HEAD_DIM, SSD_STATE))
    st = lambda k: jnp.stack(outs[k])
    return (yp, ys, st("rgc_p"), st("rgh_p"), st("sc_p"), st("sh_p"), st("mk_p"), st("mv_p"),
            st("rgc_s"), st("rgh_s"), st("sc_s"), st("sh_s"))
```
